```python
import math
import jax, jax.numpy as jnp
from jax import lax
import numpy as np

D_MODEL = 1024
BATCH = 8
SEQ = 2048
DEPTH = 2

EPS = 1e-6
HEAD_DIM = 64
BRANCH_W = 512
N_BRANCH = 4
HG_HEADS = 8
HG_DK = 64
HG_DV = 64
HG_CHUNK = 64
MLA_HEADS = 8
MLA_Q_RANK = 256
MLA_KV_RANK = 128
MLA_NOPE = 64
MLA_ROPE = 32
MLA_V = 64
ROPE_BASE = 10000.0
MAX_POS_OFFSET = 4096
DIFF_HEADS = 4
DIFF_QK = 64
DIFF_V = 128
SWA_Q_HEADS = 8
SWA_KV_HEADS = 2
SWA_WINDOW = 128
Q_BLOCK = 128
N_EXPERTS = 16
N_GROUPS = 4
TOP_K = 2
D_FF_EXPERT = 256

IN_SPLITS = (
    HG_HEADS * HG_DK, HG_HEADS * HG_DK, HG_HEADS * HG_DV, HG_HEADS * HG_DV,
    MLA_Q_RANK, MLA_KV_RANK, MLA_ROPE,
    DIFF_HEADS * 2 * DIFF_QK, DIFF_HEADS * 2 * DIFF_QK, DIFF_HEADS * DIFF_V,
    SWA_Q_HEADS * HEAD_DIM, SWA_KV_HEADS * HEAD_DIM, SWA_KV_HEADS * HEAD_DIM,
    N_BRANCH * D_MODEL,
)
N_IN = sum(IN_SPLITS)
SPLIT_POINTS = tuple(sum(IN_SPLITS[:i + 1]) for i in range(len(IN_SPLITS) - 1))

kernel_name = 'hybrid_gated_hgrn2_mla_diff_swa_moe'


def rmsnorm(x, gain):
    xf = x.astype(jnp.float32)
    xf = xf * lax.rsqrt(jnp.mean(xf * xf, axis=-1, keepdims=True) + EPS)
    return xf.astype(x.dtype) * gain


def modulate(h, shift, scale):
    return h * (1 + scale[:, None, :]) + shift[:, None, :]


def apply_rope(x, cos, sin):
    half = x.shape[-1] // 2
    x1, x2 = x[..., :half], x[..., half:]
    return jnp.concatenate([x1 * cos - x2 * sin, x1 * sin + x2 * cos], axis=-1)


def causal_probs(q_blk, k, q_start, scale):
    s = jnp.einsum('bqhd,bkhd->bhqk', q_blk, k).astype(jnp.float32) * scale
    q_pos = q_start + jnp.arange(q_blk.shape[1])
    k_pos = jnp.arange(k.shape[1])
    s = jnp.where(k_pos[None, :] <= q_pos[:, None], s, -jnp.inf)
    return jax.nn.softmax(s, axis=-1)


def sweep_query_blocks(body, q):
    B, S = q.shape[:2]
    nb = S // Q_BLOCK
    qb = jnp.moveaxis(q.reshape((B, nb, Q_BLOCK) + q.shape[2:]), 1, 0)
    starts = jnp.arange(nb, dtype=jnp.int32) * Q_BLOCK
    out = lax.map(lambda a: body(a[0], a[1]), (qb, starts))
    out = jnp.moveaxis(out, 0, 1)
    return out.reshape((B, S) + out.shape[3:])


def hgrn2_mixer(q_raw, f_raw, i_raw, g_raw, lb, o_gain):
    B, S, _ = q_raw.shape
    nc = S // HG_CHUNK
    f32 = jnp.float32
    lbf = lb.astype(f32)
    logf = jnp.logaddexp(jnp.log(lbf), jnp.log1p(-lbf) + jax.nn.log_sigmoid(f_raw.astype(f32)))
    k = -jnp.expm1(logf)
    q = jax.nn.silu(q_raw.astype(f32))
    v = i_raw.astype(f32)

    def to_chunks(t, d):
        t = t.reshape(B, nc, HG_CHUNK, HG_HEADS, d)
        return jnp.transpose(t, (1, 0, 3, 2, 4))

    qc, kc, vc, lfc = to_chunks(q, HG_DK), to_chunks(k, HG_DK), to_chunks(v, HG_DV), to_chunks(logf, HG_DK)
    tri = jnp.tril(jnp.ones((HG_CHUNK, HG_CHUNK), dtype=bool))[None, None, :, :, None]

    def step(state, inp):
        qt, kt, vt, lf = inp
        b = jnp.cumsum(lf, axis=2)
        o_inter = jnp.einsum('bhck,bhkv->bhcv', qt * jnp.exp(b), state)
        diff = b[:, :, :, None, :] - b[:, :, None, :, :]
        decay = jnp.exp(jnp.where(tri, diff, -jnp.inf))
        attn = jnp.einsum('bhtk,bhtsk,bhsk->bhts', qt, decay, kt)
        o_intra = jnp.einsum('bhts,bhsv->bhtv', attn, vt)
        b_last = b[:, :, -1:, :]
        new_state = jnp.exp(b_last[:, :, 0, :])[..., None] * state + \
            jnp.einsum('bhck,bhcv->bhkv', kt * jnp.exp(b_last - b), vt)
        return new_state, o_inter + o_intra

    state0 = jnp.zeros((B, HG_HEADS, HG_DK, HG_DV), f32)
    _, o = lax.scan(step, state0, (qc, kc, vc, lfc))
    o = jnp.transpose(o, (1, 0, 3, 2, 4)).reshape(B, S, HG_HEADS, HG_DV)
    o = rmsnorm(o, o_gain).astype(g_raw.dtype).reshape(B, S, HG_HEADS * HG_DV)
    return o * jax.nn.silu(g_raw)


def mla_mixer(c_q, c_kv, k_rope, q_norm_g, kv_norm_g, w_uq, w_ukv, qk_g, cos, sin):
    B, S, _ = c_q.shape
    q = (rmsnorm(c_q, q_norm_g) @ w_uq).reshape(B, S, MLA_HEADS, MLA_NOPE + MLA_ROPE)
    kv = (rmsnorm(c_kv, kv_norm_g) @ w_ukv).reshape(B, S, MLA_HEADS, MLA_NOPE + MLA_V)
    k_nope, v = kv[..., :MLA_NOPE], kv[..., MLA_NOPE:]
    k = jnp.concatenate([k_nope, jnp.broadcast_to(k_rope[:, :, None, :], (B, S, MLA_HEADS, MLA_ROPE))], axis=-1)
    q = rmsnorm(q, qk_g[0])
    k = rmsnorm(k, qk_g[1])
    q = jnp.concatenate([q[..., :MLA_NOPE], apply_rope(q[..., MLA_NOPE:], cos, sin)], axis=-1)
    k = jnp.concatenate([k[..., :MLA_NOPE], apply_rope(k[..., MLA_NOPE:], cos, sin)], axis=-1)
    scale = (MLA_NOPE + MLA_ROPE) ** -0.5

    def body(q_blk, start):
        p = causal_probs(q_blk, k, start, scale)
        return jnp.einsum('bhqk,bkhd->bqhd', p.astype(v.dtype), v)

    o = sweep_query_blocks(body, q)
    return o.reshape(B, S, MLA_HEADS * MLA_V)


def diff_mixer(q, k, v, qk_g, lam_p, o_g, lam_init):
    B, S, _ = q.shape
    q = rmsnorm(q.reshape(B, S, DIFF_HEADS, 2, DIFF_QK), qk_g[0])
    k = rmsnorm(k.reshape(B, S, DIFF_HEADS, 2, DIFF_QK), qk_g[1])
    v = v.reshape(B, S, DIFF_HEADS, DIFF_V)
    k1, k2 = k[:, :, :, 0], k[:, :, :, 1]
    lp = lam_p.astype(jnp.float32)
    lam = jnp.exp(jnp.sum(lp[0] * lp[1])) - jnp.exp(jnp.sum(lp[2] * lp[3])) + lam_init
    scale = DIFF_QK ** -0.5

    def body(q_blk, start):
        p1 = causal_probs(q_blk[:, :, :, 0], k1, start, scale)
        p2 = causal_probs(q_blk[:, :, :, 1], k2, start, scale)
        return jnp.einsum('bhqk,bkhd->bqhd', (p1 - lam * p2).astype(v.dtype), v)

    o = sweep_query_blocks(body, q)
    o = rmsnorm(o, o_g) * (1.0 - lam_init)
    return o.reshape(B, S, DIFF_HEADS * DIFF_V)


def swa_mixer(q, k, v, qk_g, sinks):
    B, S, _ = q.shape
    G = SWA_Q_HEADS // SWA_KV_HEADS
    W = SWA_WINDOW
    nb = S // W
    q = rmsnorm(q.reshape(B, S, SWA_KV_HEADS, G, HEAD_DIM), qk_g[0])
    k = rmsnorm(k.reshape(B, S, SWA_KV_HEADS, HEAD_DIM), qk_g[1])
    v = v.reshape(B, S, SWA_KV_HEADS, HEAD_DIM)

    def band(t):
        tp = jnp.pad(t, ((0, 0), (W, 0), (0, 0), (0, 0)))
        tb = tp.reshape(B, nb + 1, W, SWA_KV_HEADS, HEAD_DIM)
        return jnp.concatenate([tb[:, :-1], tb[:, 1:]], axis=2)

    kw, vw = band(k), band(v)
    qb = q.reshape(B, nb, W, SWA_KV_HEADS, G, HEAD_DIM)
    s = jnp.einsum('bnqhgd,bnkhd->bnhgqk', qb, kw).astype(jnp.float32) * HEAD_DIM ** -0.5
    qi = jnp.arange(W)[:, None]
    kj = jnp.arange(2 * W)[None, :]
    delta = qi + W - kj
    blk = jnp.arange(nb)[:, None, None]
    valid = (delta >= 0) & (delta < W) & (blk * W + kj[None] - W >= 0)
    s = jnp.where(valid[None, :, None, None], s, -jnp.inf)
    sink = jnp.broadcast_to(sinks.astype(jnp.float32).reshape(1, 1, SWA_KV_HEADS, G, 1, 1), s.shape[:-1] + (1,))
    p = jax.nn.softmax(jnp.concatenate([s, sink], axis=-1), axis=-1)[..., :-1]
    o = jnp.einsum('bnhgqk,bnkhd->bnqhgd', p.astype(v.dtype), vw)
    return o.reshape(B, S, SWA_Q_HEADS * HEAD_DIM)


def moe(h, router_w, router_bias, w_gate, w_up, w_down):
    B, S, _ = h.shape
    f32 = jnp.float32
    per = N_EXPERTS // N_GROUPS
    scores = jax.nn.sigmoid(jnp.einsum('bsd,de->bse', h, router_w).astype(f32))
    sel = scores + router_bias.astype(f32)
    grp = lax.top_k(sel.reshape(B, S, N_GROUPS, per), TOP_K)[0].sum(-1)
    g_idx = jnp.argmax(grp, axis=-1)
    in_grp = (jnp.arange(N_EXPERTS) // per)[None, None, :] == g_idx[..., None]
    _, e_idx = lax.top_k(jnp.where(in_grp, sel, -jnp.inf), TOP_K)
    w = jnp.take_along_axis(scores, e_idx, axis=-1)
    w = w / jnp.sum(w, axis=-1, keepdims=True)
    combine = jnp.sum(jax.nn.one_hot(e_idx, N_EXPERTS, dtype=f32) * w[..., None], axis=-2)
    hid = jax.nn.silu(jnp.einsum('bsd,edf->bsef', h, w_gate)) * jnp.einsum('bsd,edf->bsef', h, w_up)
    return jnp.einsum('bsef,bse,efd->bsd', hid, combine.astype(h.dtype), w_down)


def setup_inputs(seed: int = 0) -> dict:
    key = jax.random.key(seed)
    ks = jax.random.split(key, 32)
    f32 = jnp.float32
    L, D = DEPTH, D_MODEL

    def nrm(k, shape, scale):
        return jax.random.normal(k, shape, f32) * scale

    def gain(k, shape):
        return 1.0 + 0.02 * jax.random.normal(k, shape, f32)

    offs = jax.random.randint(ks[2], (BATCH, 1), 0, MAX_POS_OFFSET, dtype=jnp.int32)
    return {
        'x': nrm(ks[0], (BATCH, SEQ, D), 1.0),
        'c': nrm(ks[1], (BATCH, D), 1.0),
        'positions': offs + jnp.arange(SEQ, dtype=jnp.int32)[None, :],
        'ada_w': nrm(ks[3], (L, D, 6 * D), 0.5 * D ** -0.5),
        'ada_b': nrm(ks[4], (L, 6 * D), 0.02),
        'norm_mix': gain(ks[5], (L, D)),
        'norm_ffn': gain(ks[6], (L, D)),
        'w_in': nrm(ks[7], (L, D, N_IN), D ** -0.5),
        'hg_lb_logits': nrm(ks[8], (L, HG_HEADS * HG_DK), 1.0),
        'hg_onorm': gain(ks[9], (L, HG_DV)),
        'mla_q_norm': gain(ks[10], (L, MLA_Q_RANK)),
        'mla_kv_norm': gain(ks[11], (L, MLA_KV_RANK)),
        'mla_w_uq': nrm(ks[12], (L, MLA_Q_RANK, MLA_HEADS * (MLA_NOPE + MLA_ROPE)), MLA_Q_RANK ** -0.5),
        'mla_w_ukv': nrm(ks[13], (L, MLA_KV_RANK, MLA_HEADS * (MLA_NOPE + MLA_V)), MLA_KV_RANK ** -0.5),
        'mla_qk_norm': gain(ks[14], (L, 2, MLA_NOPE + MLA_ROPE)),
        'diff_qk_norm': gain(ks[15], (L, 2, DIFF_QK)),
        'diff_lam': nrm(ks[16], (L, 4, DIFF_QK), 0.1),
        'diff_onorm': gain(ks[17], (L, DIFF_V)),
        'swa_qk_norm': gain(ks[18], (L, 2, HEAD_DIM)),
        'swa_sinks': nrm(ks[19], (L, SWA_Q_HEADS), 1.0),
        'w_branch': nrm(ks[20], (L, N_BRANCH, BRANCH_W, D), BRANCH_W ** -0.5),
        'w_out': nrm(ks[21], (L, D, D), D ** -0.5),
        'router_w': nrm(ks[22], (D, N_EXPERTS), D ** -0.5),
        'router_bias': nrm(ks[23], (N_EXPERTS,), 0.01),
        'moe_w_gate': nrm(ks[24], (L, N_EXPERTS, D, D_FF_EXPERT), D ** -0.5),
        'moe_w_up': nrm(ks[25], (L, N_EXPERTS, D, D_FF_EXPERT), D ** -0.5),
        'moe_w_down': nrm(ks[26], (L, N_EXPERTS, D_FF_EXPERT, D), D_FF_EXPERT ** -0.5),
    }


def reference(x, c, positions, ada_w, ada_b, norm_mix, norm_ffn, w_in, hg_lb_logits, hg_onorm,
              mla_q_norm, mla_kv_norm, mla_w_uq, mla_w_ukv, mla_qk_norm, diff_qk_norm, diff_lam,
              diff_onorm, swa_qk_norm, swa_sinks, w_branch, w_out, router_w, router_bias,
              moe_w_gate, moe_w_up, moe_w_down):
    B, S, D = x.shape
    f32 = jnp.float32
    inv_freq = ROPE_BASE ** (-jnp.arange(0, MLA_ROPE, 2, dtype=f32) / MLA_ROPE)
    ang = positions.astype(f32)[..., None] * inv_freq
    cos = jnp.cos(ang)[:, :, None, :].astype(x.dtype)
    sin = jnp.sin(ang)[:, :, None, :].astype(x.dtype)
    lb_all = jnp.cumsum(jax.nn.softmax(hg_lb_logits.astype(f32), axis=0), axis=0)
    lb_all = lb_all - lb_all[0]
    c_act = jax.nn.silu(c)

    for l in range(DEPTH):
        mod = c_act @ ada_w[l] + ada_b[l]
        sh1, sc1, gt1, sh2, sc2, gt2 = jnp.split(mod, 6, axis=-1)
        h = modulate(rmsnorm(x, norm_mix[l]), sh1, sc1)
        proj = h @ w_in[l]
        (hq, hf, hi, hg, mcq, mckv, mkr, dq, dk, dv, sq, sk, sv, gates) = jnp.split(proj, SPLIT_POINTS, axis=-1)
        y_a = hgrn2_mixer(hq, hf, hi, hg, lb_all[l], hg_onorm[l])
        y_b = mla_mixer(mcq, mckv, mkr, mla_q_norm[l], mla_kv_norm[l], mla_w_uq[l], mla_w_ukv[l],
                        mla_qk_norm[l], cos, sin)
        lam_init = 0.8 - 0.6 * math.exp(-0.3 * l)
        y_c = diff_mixer(dq, dk, dv, diff_qk_norm[l], diff_lam[l], diff_onorm[l], lam_init)
        y_d = swa_mixer(sq, sk, sv, swa_qk_norm[l], swa_sinks[l])
        ys = jnp.stack([y_a, y_b, y_c, y_d], axis=2)
        branch = jnp.einsum('bsnw,nwd->bsnd', ys, w_branch[l])
        gate = jax.nn.sigmoid(gates.reshape(B, S, N_BRANCH, D))
        merged = jnp.sum(gate * branch, axis=2)
        x = x + gt1[:, None, :] * (merged @ w_out[l])
        h2 = modulate(rmsnorm(x, norm_ffn[l]), sh2, sc2)
        x = x + gt2[:, None, :] * moe(h2, router_w, router_bias, moe_w_gate[l], moe_w_up[l], moe_w_down[l])
    return x
```

```python
import functools
import math

import jax
import jax.numpy as jnp
from jax import lax
from jax.experimental import pallas as pl
from jax.experimental.pallas import tpu as pltpu

F32 = jnp.float32
BF16 = jnp.bfloat16

D_MODEL = 1024
DEPTH = 2
EPS = 1e-6
N_BRANCH = 4
HG_HEADS = 8
HG_DK = 64
HG_W = HG_HEADS * HG_DK
HG_SUB = 16
MLA_HEADS = 8
MLA_Q_RANK = 256
MLA_KV_RANK = 128
MLA_NOPE = 64
MLA_ROPE = 32
MLA_V = 64
ROPE_BASE = 10000.0
DIFF_HEADS = 4
DIFF_QK = 64
DIFF_V = 128
SWA_Q_HEADS = 8
SWA_KV_HEADS = 2
SWA_WINDOW = 128
HEAD_DIM = 64
N_EXPERTS = 16
N_GROUPS = 4
D_FF_EXPERT = 256
IN_SPLITS = (512, 512, 512, 512, 256, 128, 32, 512, 512, 512, 512, 128, 128, 4096)

LANE = 128
NEG = -1e30
VMEM_LIMIT = 56 * 1024 * 1024


def _cp(sem, vmem=VMEM_LIMIT):
    return pltpu.CompilerParams(dimension_semantics=sem, vmem_limit_bytes=vmem)


def _nt(a, b):
    return lax.dot_general(a, b, (((1,), (1,)), ((), ())), preferred_element_type=F32)


def _tn(a, b):
    return lax.dot_general(a, b, (((0,), (0,)), ((), ())), preferred_element_type=F32)


def _dot(a, b):
    return jnp.dot(a, b, preferred_element_type=F32)


def _split2(x):
    hi = x.astype(BF16)
    lo = (x - hi.astype(F32)).astype(BF16)
    return hi, lo


def _split3(x):
    hi = x.astype(BF16)
    r = x - hi.astype(F32)
    mid = r.astype(BF16)
    lo = (r - mid.astype(F32)).astype(BF16)
    return hi, mid, lo


def _seg_id(idx, seg):
    shift = seg.bit_length() - 1
    assert 1 << shift == seg
    return lax.shift_right_logical(idx, shift)


def _same_seg(n, seg):
    r = lax.broadcasted_iota(jnp.int32, (n, n), 0)
    c = lax.broadcasted_iota(jnp.int32, (n, n), 1)
    return _seg_id(r, seg) == _seg_id(c, seg)


def _seg_ones(n, seg):
    return _same_seg(n, seg).astype(BF16)


def _seg_mean_sq(x, seg):
    n = x.shape[-1]
    hi, lo = _split2(x * x)
    ones = _seg_ones(n, seg)
    return (_dot(hi, ones) + _dot(lo, ones)) * (1.0 / seg)


def _silu(x):
    return x * jax.nn.sigmoid(x)


def _mod_kernel(c_ref, w_ref, b_ref, o_ref):
    c = c_ref[...]
    o_ref[0] = jnp.dot(_silu(c), w_ref[0], preferred_element_type=F32,
                       precision=lax.Precision.HIGHEST) + b_ref[0]


def _modulation(c, ada_w, ada_b):
    nl, d, n6 = ada_w.shape
    b = c.shape[0]
    tn = 1536
    return pl.pallas_call(
        _mod_kernel,
        grid=(nl, n6 // tn),
        in_specs=[pl.BlockSpec((b, d), lambda l, j: (0, 0)),
                  pl.BlockSpec((1, d, tn), lambda l, j: (l, 0, j)),
                  pl.BlockSpec((1, 1, tn), lambda l, j: (l, 0, j))],
        out_specs=pl.BlockSpec((1, b, tn), lambda l, j: (l, 0, j)),
        out_shape=jax.ShapeDtypeStruct((nl, b, n6), F32),
        compiler_params=_cp(("arbitrary", "arbitrary")),
        name="modulation",
    )(c, ada_w, ada_b.reshape(nl, 1, n6))


W1_COLS = 4864


def _inproj_kernel(x_ref, g_ref, sh_ref, sc_ref, w_ref,
                   ohg_ref, ohf_ref, omla_ref, odiff_ref, oswa_ref, oh_ref):
    x = x_ref[...]
    ms = jnp.mean(x * x, axis=-1, keepdims=True)
    h = x * lax.rsqrt(ms + EPS) * g_ref[...]
    h = h * (1.0 + sc_ref[0]) + sh_ref[0]
    hb = h.astype(BF16)
    oh_ref[...] = hb

    def proj(lo, hi):
        return _dot(hb, w_ref[:, lo:hi])

    for k in range(3):
        ohg_ref[:, 512 * k:512 * (k + 1)] = proj(512 * k, 512 * (k + 1)).astype(BF16)
    ohf_ref[...] = proj(1536, 2048)
    omla_ref[...] = proj(2048, 2560).astype(BF16)
    for k in range(3):
        odiff_ref[:, 512 * k:512 * (k + 1)] = proj(2560 + 512 * k, 3072 + 512 * k).astype(BF16)
    oswa_ref[:, 0:512] = proj(4096, 4608).astype(BF16)
    oswa_ref[:, 512:768] = proj(4608, 4864).astype(BF16)


def _inproj(x2, gain, sh, sc, w1, seq):
    t, d = x2.shape
    tm = 512
    tpb = seq // tm
    row = lambda i: (i, 0)
    per_b = lambda i: (i // tpb, 0, 0)
    outs = [(1536, BF16), (512, F32), (512, BF16), (1536, BF16), (768, BF16), (d, BF16)]
    return pl.pallas_call(
        _inproj_kernel,
        grid=(t // tm,),
        in_specs=[pl.BlockSpec((tm, d), row),
                  pl.BlockSpec((1, d), lambda i: (0, 0)),
                  pl.BlockSpec((1, 1, d), per_b),
                  pl.BlockSpec((1, 1, d), per_b),
                  pl.BlockSpec((d, W1_COLS), lambda i: (0, 0))],
        out_specs=[pl.BlockSpec((tm, w), row) for w, _ in outs],
        out_shape=[jax.ShapeDtypeStruct((t, w), dt) for w, dt in outs],
        compiler_params=_cp(("arbitrary",)),
        name="inproj",
    )(x2, gain, sh, sc, w1)


def _hgrn_kernel(q_ref, i_ref, g_ref, f_ref, loglb_ref, log1mlb_ref, og_ref, o_ref,
                 st_ref, c_s, qs_s, kk_s, qe_s, ke_s, dec_s, od_s, t_s, a_s):
    rows_blk = q_ref.shape[0]
    n_sub = rows_blk // HG_SUB

    @pl.when(pl.program_id(1) == 0)
    def _():
        st_ref[...] = jnp.zeros_like(st_ref)

    fr = f_ref[...]
    ls = jnp.minimum(fr, 0.0) - jnp.log1p(jnp.exp(-jnp.abs(fr)))
    a = loglb_ref[...]
    c2 = log1mlb_ref[...] + ls
    lf = jnp.maximum(a, c2) + jnp.log1p(jnp.exp(-jnp.abs(a - c2)))

    r = lax.broadcasted_iota(jnp.int32, (rows_blk, rows_blk), 0)
    cc = lax.broadcasted_iota(jnp.int32, (rows_blk, rows_blk), 1)
    same = _same_seg(rows_blk, HG_SUB)
    tri = (same & (cc <= r)).astype(BF16)
    blk = same.astype(BF16)
    parts = _split3(lf)
    c = _dot(tri, parts[0]) + _dot(tri, parts[1]) + _dot(tri, parts[2])
    tot = _dot(blk, parts[0]) + _dot(blk, parts[1]) + _dot(blk, parts[2])

    qs = _silu(q_ref[...].astype(F32))
    kk = 1.0 - jnp.exp(lf)
    c_s[...] = c
    qs_s[...] = qs
    kk_s[...] = kk
    qe_s[...] = (qs * jnp.exp(c)).astype(BF16)
    ke_s[...] = (kk * jnp.exp(tot - c)).astype(BF16)
    dec_s[...] = jnp.exp(tot)

    same_head = _same_seg(LANE, HG_DK)
    head_mask = same_head.astype(F32)
    head_ones = same_head.astype(BF16)
    trow = lax.broadcasted_iota(jnp.int32, (HG_SUB, HG_W), 0)

    def body(i, carry):
        r0 = pl.multiple_of(i * HG_SUB, HG_SUB)
        rows = pl.ds(r0, HG_SUB)
        c_i = c_s[rows, :]
        qs_i = qs_s[rows, :]
        kk_i = kk_s[rows, :]
        v_i = i_ref[rows, :].astype(F32)
        for s in range(HG_SUB):
            e = jnp.exp(jnp.where(trow >= s, c_i - c_i[s:s + 1, :], NEG))
            t_s[s * HG_SUB:(s + 1) * HG_SUB, :] = (e * qs_i * kk_i[s:s + 1, :]).astype(BF16)
        for j in range(HG_W // LANE):
            cols = slice(LANE * j, LANE * (j + 1))
            a_s[:, cols] = _dot(t_s[:, cols], head_ones)
        acc = jnp.zeros((HG_SUB, HG_W), F32)
        for s in range(HG_SUB):
            acc = acc + a_s[s * HG_SUB:(s + 1) * HG_SUB, :] * v_i[s:s + 1, :]
        for j in range(HG_W // LANE):
            cols = slice(LANE * j, LANE * (j + 1))
            st = st_ref[j]
            o_int = _nt(qe_s[rows, cols], st.astype(BF16))
            upd = _tn(i_ref[rows, cols], ke_s[rows, cols])
            st_ref[j] = st * dec_s[pl.ds(r0, 1), cols] + upd * head_mask
            od_s[rows, cols] = acc[:, cols] + o_int
        return carry

    lax.fori_loop(0, n_sub, body, 0)

    o = od_s[...]
    ms = _seg_mean_sq(o, HG_DK)
    on = o * lax.rsqrt(ms + EPS) * og_ref[...]
    o_ref[...] = (on * _silu(g_ref[...].astype(F32))).astype(BF16)


def _hgrn(hg3, hf, loglb, log1mlb, ogain, batch, seq):
    t = hf.shape[0]
    rb = 256
    nb = seq // rb
    blk = lambda k: pl.BlockSpec((rb, HG_W), lambda b, n, k=k: (b * nb + n, k))
    vec = pl.BlockSpec((1, HG_W), lambda b, n: (0, 0))
    return pl.pallas_call(
        _hgrn_kernel,
        grid=(batch, nb),
        in_specs=[blk(0), blk(1), blk(2), blk(0), vec, vec, vec],
        out_specs=blk(0),
        out_shape=jax.ShapeDtypeStruct((t, HG_W), BF16),
        scratch_shapes=[pltpu.VMEM((HG_W // LANE, LANE, LANE), F32),
                        pltpu.VMEM((rb, HG_W), F32), pltpu.VMEM((rb, HG_W), F32),
                        pltpu.VMEM((rb, HG_W), F32), pltpu.VMEM((rb, HG_W), BF16),
                        pltpu.VMEM((rb, HG_W), BF16), pltpu.VMEM((rb, HG_W), F32),
                        pltpu.VMEM((rb, HG_W), F32),
                        pltpu.VMEM((HG_SUB * HG_SUB, HG_W), BF16),
                        pltpu.VMEM((HG_SUB * HG_SUB, HG_W), F32)],
        compiler_params=_cp(("arbitrary", "arbitrary")),
        name="hgrn2",
    )(hg3, hg3, hg3, hf, loglb, log1mlb, ogain)


def _prep_kernel(mla_ref, diff_ref, swa_ref, cos_ref, sin_ref,
                 qng_ref, kvg_ref, wqa_ref, wqb_ref, wka_ref, wkb_ref, wv_ref,
                 gq_ref, gqs_ref, gk_ref, gks_ref, dgq_ref, dgk_ref, sgq_ref, sgk_ref,
                 qm_ref, km_ref, vm_ref, qd_ref, kd_ref, qs_ref, ks_ref, vs_ref):
    blk = mla_ref[...].astype(F32)
    cq = blk[:, :MLA_Q_RANK]
    rest = blk[:, MLA_Q_RANK:]
    cqn = cq * lax.rsqrt(jnp.mean(cq * cq, axis=-1, keepdims=True) + EPS) * qng_ref[...]
    lane = lax.broadcasted_iota(jnp.int32, rest.shape, 1)
    is_kv = lane < MLA_KV_RANK
    ms_kv = jnp.sum(jnp.where(is_kv, rest * rest, 0.0), axis=-1, keepdims=True) * (1.0 / MLA_KV_RANK)
    restn = jnp.where(is_kv, rest * lax.rsqrt(ms_kv + EPS) * kvg_ref[...], rest)
    cqb = cqn.astype(BF16)
    rb = restn.astype(BF16)
    qa = _dot(cqb, wqa_ref[...])
    qb = _dot(cqb, wqb_ref[...])
    ka = _dot(rb, wka_ref[...])
    kb = _dot(rb, wkb_ref[...])
    vm_ref[...] = _dot(rb, wv_ref[...]).astype(BF16)
    cosf = cos_ref[...]
    sinf = sin_ref[...]
    cq_t = cosf * gq_ref[...]
    sq_t = sinf * gqs_ref[...]
    ck_t = cosf * gk_ref[...]
    sk_t = sinf * gks_ref[...]
    inv_n = 1.0 / (MLA_NOPE + MLA_ROPE)
    scale = (MLA_NOPE + MLA_ROPE) ** -0.5
    for h in range(MLA_HEADS):
        cols = slice(LANE * h, LANE * (h + 1))
        x = qa[:, cols]
        rinv = lax.rsqrt(jnp.sum(x * x, axis=-1, keepdims=True) * inv_n + EPS)
        qm_ref[:, cols] = ((x * cq_t + qb[:, cols] * sq_t) * (rinv * scale)).astype(BF16)
        y = ka[:, cols]
        rinv = lax.rsqrt(jnp.sum(y * y, axis=-1, keepdims=True) * inv_n + EPS)
        km_ref[:, cols] = ((y * ck_t + kb[:, cols] * sk_t) * rinv).astype(BF16)

    def seg_norm(x, gain, scale):
        return x * lax.rsqrt(_seg_mean_sq(x, HEAD_DIM) + EPS) * (gain * scale)

    dq = diff_ref[:, 0:512].astype(F32)
    dk = diff_ref[:, 512:1024].astype(F32)
    qd_ref[...] = seg_norm(dq, dgq_ref[...], DIFF_QK ** -0.5).astype(BF16)
    kd_ref[...] = seg_norm(dk, dgk_ref[...], 1.0).astype(BF16)

    sq = swa_ref[:, 0:512].astype(F32)
    qs_ref[...] = seg_norm(sq, sgq_ref[...], HEAD_DIM ** -0.5).astype(BF16)
    skv = swa_ref[:, 512:768].astype(F32)
    kn = seg_norm(skv[:, :LANE], sgk_ref[...], 1.0)
    vv = skv[:, LANE:]
    low = lax.broadcasted_iota(jnp.int32, kn.shape, 1) < HEAD_DIM

    def dup(x):
        sw = pltpu.roll(x, HEAD_DIM, 1)
        return jnp.where(low, x, sw), jnp.where(low, sw, x)

    k0, k1 = dup(kn)
    v0, v1 = dup(vv)
    ks_ref[:, :LANE] = k0.astype(BF16)
    ks_ref[:, LANE:] = k1.astype(BF16)
    vs_ref[:, :LANE] = v0.astype(BF16)
    vs_ref[:, LANE:] = v1.astype(BF16)


def _prep(mla, diff, swa, cosf, sinf, p):
    t = mla.shape[0]
    tm = 512
    row = lambda i: (i, 0)
    full = lambda a: pl.BlockSpec(a.shape, lambda i: (0,) * a.ndim)
    consts = [p["qng"], p["kvg"], p["wqa"], p["wqb"], p["wka"], p["wkb"], p["wv"],
              p["gq"], p["gqs"], p["gk"], p["gks"], p["dgq"], p["dgk"], p["sgq"], p["sgk"]]
    outs = [1024, 1024, 512, 512, 512, 512, 256, 256]
    return pl.pallas_call(
        _prep_kernel,
        grid=(t // tm,),
        in_specs=[pl.BlockSpec((tm, 512), row),
                  pl.BlockSpec((tm, 1024), row),
                  pl.BlockSpec((tm, 768), row),
                  pl.BlockSpec((tm, LANE), row),
                  pl.BlockSpec((tm, LANE), row)] + [full(a) for a in consts],
        out_specs=[pl.BlockSpec((tm, w), row) for w in outs],
        out_shape=[jax.ShapeDtypeStruct((t, w), BF16) for w in outs],
        compiler_params=_cp(("arbitrary",)),
        name="attn_prep",
    )(mla, diff, swa, cosf, sinf, *consts)


def _online(s, m, l):
    mn = jnp.maximum(m, jnp.max(s, axis=-1, keepdims=True))
    p = jnp.exp(s - mn)
    alpha = jnp.exp(m - mn)
    return p, alpha, mn, alpha * l + jnp.sum(p, axis=-1, keepdims=True)


def _mla_attn_kernel(q_ref, k_ref, v_ref, o_ref):
    tq = q_ref.shape[0]
    qi = pl.program_id(1)
    row = lax.broadcasted_iota(jnp.int32, (tq, tq), 0)
    col = lax.broadcasted_iota(jnp.int32, (tq, tq), 1)
    causal = col <= row
    low = lax.broadcasted_iota(jnp.int32, (tq, LANE), 1) < MLA_V

    for j in range(MLA_HEADS // 2):
        c0 = slice(2 * LANE * j, 2 * LANE * j + LANE)
        c1 = slice(2 * LANE * j + LANE, 2 * LANE * (j + 1))
        cv = slice(LANE * j, LANE * (j + 1))
        q0 = q_ref[:, c0]
        q1 = q_ref[:, c1]

        def step(rows, carry, masked):
            m0, l0, m1, l1, acc = carry
            s0 = _nt(q0, k_ref[rows, c0])
            s1 = _nt(q1, k_ref[rows, c1])
            if masked:
                s0 = jnp.where(causal, s0, NEG)
                s1 = jnp.where(causal, s1, NEG)
            p0, a0, m0, l0 = _online(s0, m0, l0)
            p1, a1, m1, l1 = _online(s1, m1, l1)
            vt = v_ref[rows, cv]
            pv0 = _dot(p0.astype(BF16), vt)
            pv1 = _dot(p1.astype(BF16), vt)
            acc = jnp.where(low, a0 * acc + pv0, a1 * acc + pv1)
            return m0, l0, m1, l1, acc

        def body(ki, carry):
            return step(pl.ds(pl.multiple_of(ki * tq, tq), tq), carry, False)

        neg = jnp.full((tq, 1), NEG, F32)
        zero = jnp.zeros((tq, 1), F32)
        carry = lax.fori_loop(0, qi, body, (neg, zero, neg, zero, jnp.zeros((tq, LANE), F32)))
        m0, l0, m1, l1, acc = step(pl.ds(pl.multiple_of(qi * tq, tq), tq), carry, True)
        o_ref[:, cv] = (acc * jnp.where(low, 1.0 / l0, 1.0 / l1)).astype(BF16)


def _mla_attn(qm, km, vm, batch, seq):
    t = qm.shape[0]
    tq = 256
    nq = seq // tq
    return pl.pallas_call(
        _mla_attn_kernel,
        grid=(batch, nq),
        in_specs=[pl.BlockSpec((tq, 1024), lambda b, i: (b * nq + i, 0)),
                  pl.BlockSpec((seq, 1024), lambda b, i: (b, 0)),
                  pl.BlockSpec((seq, 512), lambda b, i: (b, 0))],
        out_specs=pl.BlockSpec((tq, 512), lambda b, i: (b * nq + i, 0)),
        out_shape=jax.ShapeDtypeStruct((t, 512), BF16),
        compiler_params=_cp(("arbitrary", "arbitrary")),
        name="mla_attn",
    )(qm, km, vm)


def _diff_attn_kernel(q_ref, k_ref, v_ref, lam_ref, og_ref, o_ref, *, lam_init):
    tq = q_ref.shape[0]
    qi = pl.program_id(1)
    row = lax.broadcasted_iota(jnp.int32, (tq, tq), 0)
    col = lax.broadcasted_iota(jnp.int32, (tq, tq), 1)
    causal = col <= row
    low = lax.broadcasted_iota(jnp.int32, (tq, LANE), 1) < DIFF_QK
    lp = lam_ref[...]
    lam = (jnp.exp(jnp.sum(lp[0:1] * lp[1:2], axis=-1, keepdims=True))
           - jnp.exp(jnp.sum(lp[2:3] * lp[3:4], axis=-1, keepdims=True)) + lam_init)

    for h in range(DIFF_HEADS):
        cols = slice(LANE * h, LANE * (h + 1))
        qt = q_ref[:, cols]
        q1 = jnp.where(low, qt, jnp.zeros_like(qt))
        q2 = jnp.where(low, jnp.zeros_like(qt), qt)

        def step(rows, carry, masked):
            m1, l1, a1, m2, l2, a2 = carry
            kt = k_ref[rows, cols]
            s1 = _nt(q1, kt)
            s2 = _nt(q2, kt)
            if masked:
                s1 = jnp.where(causal, s1, NEG)
                s2 = jnp.where(causal, s2, NEG)
            p1, al1, m1, l1 = _online(s1, m1, l1)
            p2, al2, m2, l2 = _online(s2, m2, l2)
            vt = v_ref[rows, cols]
            a1 = al1 * a1 + _dot(p1.astype(BF16), vt)
            a2 = al2 * a2 + _dot(p2.astype(BF16), vt)
            return m1, l1, a1, m2, l2, a2

        def body(ki, carry):
            return step(pl.ds(pl.multiple_of(ki * tq, tq), tq), carry, False)

        neg = jnp.full((tq, 1), NEG, F32)
        zero = jnp.zeros((tq, 1), F32)
        zacc = jnp.zeros((tq, LANE), F32)
        carry = lax.fori_loop(0, qi, body, (neg, zero, zacc, neg, zero, zacc))
        m1, l1, a1, m2, l2, a2 = step(pl.ds(pl.multiple_of(qi * tq, tq), tq), carry, True)
        o = a1 / l1 - lam * (a2 / l2)
        on = o * lax.rsqrt(jnp.mean(o * o, axis=-1, keepdims=True) + EPS) * og_ref[...]
        o_ref[:, cols] = (on * (1.0 - lam_init)).astype(BF16)


def _diff_attn(qd, kd, diff, lam_p, og, lam_init, batch, seq):
    t = qd.shape[0]
    tq = 256
    nq = seq // tq
    return pl.pallas_call(
        functools.partial(_diff_attn_kernel, lam_init=lam_init),
        grid=(batch, nq),
        in_specs=[pl.BlockSpec((tq, 512), lambda b, i: (b * nq + i, 0)),
                  pl.BlockSpec((seq, 512), lambda b, i: (b, 0)),
                  pl.BlockSpec((seq, 512), lambda b, i: (b, 2)),
                  pl.BlockSpec(lam_p.shape, lambda b, i: (0, 0)),
                  pl.BlockSpec(og.shape, lambda b, i: (0, 0))],
        out_specs=pl.BlockSpec((tq, 512), lambda b, i: (b * nq + i, 0)),
        out_shape=jax.ShapeDtypeStruct((t, 512), BF16),
        compiler_params=_cp(("arbitrary", "arbitrary")),
        name="diff_attn",
    )(qd, kd, diff, lam_p, og)


def _swa_kernel(q_ref, kp_ref, kc_ref, vp_ref, vc_ref, sink_ref, o_ref):
    w = q_ref.shape[0]
    n = pl.program_id(1)
    row = lax.broadcasted_iota(jnp.int32, (w, w), 0)
    col = lax.broadcasted_iota(jnp.int32, (w, w), 1)
    cur_ok = col <= row
    prev_ok = (col > row) & (n > 0)
    low = lax.broadcasted_iota(jnp.int32, (w, LANE), 1) < HEAD_DIM

    for j in range(SWA_Q_HEADS // 2):
        cols = slice(LANE * j, LANE * (j + 1))
        kv = (2 * j) // (SWA_Q_HEADS // SWA_KV_HEADS)
        kcols = slice(LANE * kv, LANE * (kv + 1))
        qt = q_ref[:, cols]
        kp = kp_ref[:, kcols]
        kc = kc_ref[:, kcols]
        vp = vp_ref[:, kcols]
        vc = vc_ref[:, kcols]
        outs = []
        for half in range(2):
            qh = jnp.where(low, qt, jnp.zeros_like(qt)) if half == 0 else jnp.where(low, jnp.zeros_like(qt), qt)
            sink = sink_ref[2 * j + half:2 * j + half + 1, 0:1]
            sp = jnp.where(prev_ok, _nt(qh, kp), NEG)
            sc = jnp.where(cur_ok, _nt(qh, kc), NEG)
            m = jnp.maximum(jnp.maximum(jnp.max(sp, axis=-1, keepdims=True),
                                        jnp.max(sc, axis=-1, keepdims=True)), sink)
            pp = jnp.exp(sp - m)
            pc = jnp.exp(sc - m)
            den = (jnp.sum(pp, axis=-1, keepdims=True) + jnp.sum(pc, axis=-1, keepdims=True)
                   + jnp.exp(sink - m))
            outs.append((_dot(pp.astype(BF16), vp) + _dot(pc.astype(BF16), vc)) / den)
        o_ref[:, cols] = jnp.where(low, outs[0], outs[1]).astype(BF16)


def _swa(qs, ks, vs, sinks, batch, seq):
    t = qs.shape[0]
    w = SWA_WINDOW
    nb = seq // w
    cur = lambda b, n: (b * nb + n, 0)
    prev = lambda b, n: (b * nb + jnp.maximum(n - 1, 0), 0)
    return pl.pallas_call(
        _swa_kernel,
        grid=(batch, nb),
        in_specs=[pl.BlockSpec((w, 512), cur),
                  pl.BlockSpec((w, 256), prev), pl.BlockSpec((w, 256), cur),
                  pl.BlockSpec((w, 256), prev), pl.BlockSpec((w, 256), cur),
                  pl.BlockSpec(sinks.shape, lambda b, n: (0, 0))],
        out_specs=pl.BlockSpec((w, 512), cur),
        out_shape=jax.ShapeDtypeStruct((t, 512), BF16),
        compiler_params=_cp(("arbitrary", "arbitrary")),
        name="swa_attn",
    )(qs, ks, ks, vs, vs, sinks)


def _merge_kernel(h_ref, ya_ref, yb_ref, yc_ref, yd_ref, x_ref, gt1_ref, wg_ref, wb_ref, wo_ref,
                  g2_ref, sh2_ref, sc2_ref, rw_ref, rb_ref, xo_ref, h2_ref, comb_ref):
    h = h_ref[...]
    d = x_ref.shape[1]
    merged = None
    for b, y_ref in enumerate((ya_ref, yb_ref, yc_ref, yd_ref)):
        gate = jax.nn.sigmoid(_dot(h, wg_ref[:, d * b:d * (b + 1)]))
        term = gate * _dot(y_ref[...], wb_ref[b])
        merged = term if merged is None else merged + term
    xn = x_ref[...] + gt1_ref[0] * _dot(merged.astype(BF16), wo_ref[...])
    xo_ref[...] = xn
    ms = jnp.mean(xn * xn, axis=-1, keepdims=True)
    h2 = xn * lax.rsqrt(ms + EPS) * g2_ref[...]
    h2 = h2 * (1.0 + sc2_ref[0]) + sh2_ref[0]
    h2_ref[...] = h2.astype(BF16)

    hh, hm, _ = _split3(h2)
    wh, wm = _split2(rw_ref[...])
    logits = _dot(hh, wh) + _dot(hm, wh) + _dot(hh, wm)
    lt = logits.T
    scores = jax.nn.sigmoid(lt[0:N_EXPERTS, :])
    sel = scores + rb_ref[...]
    per = N_EXPERTS // N_GROUPS
    srow = [sel[e:e + 1, :] for e in range(N_EXPERTS)]
    gsum = []
    for g in range(N_GROUPS):
        a, b_, c, e_ = srow[per * g:per * (g + 1)]
        gsum.append(jnp.maximum(jnp.maximum(jnp.maximum(a + b_, a + c), jnp.maximum(a + e_, b_ + c)),
                                jnp.maximum(b_ + e_, c + e_)))
    best = jnp.maximum(jnp.maximum(gsum[0], gsum[1]), jnp.maximum(gsum[2], gsum[3]))
    taken = None
    rows = []
    for g in range(N_GROUPS):
        hit = gsum[g] == best
        pick = hit if taken is None else hit & jnp.logical_not(taken)
        taken = hit if taken is None else taken | hit
        for e in range(per * g, per * (g + 1)):
            rank = jnp.zeros_like(best)
            for o in range(per * g, per * (g + 1)):
                if o == e:
                    continue
                ahead = (srow[o] > srow[e]) | ((srow[o] == srow[e]) & (o < e))
                rank = rank + ahead.astype(F32)
            rows.append(jnp.where(pick & (rank < 1.5), scores[e:e + 1, :], 0.0))
    wsum = rows[0]
    for r_ in rows[1:]:
        wsum = wsum + r_
    inv = 1.0 / wsum
    rid = lax.broadcasted_iota(jnp.int32, scores.shape, 0)
    comb_e = jnp.zeros_like(scores)
    for e, r_ in enumerate(rows):
        comb_e = jnp.where(rid == e, r_ * inv, comb_e)
    comb_t = jnp.concatenate([comb_e, jnp.zeros((LANE - N_EXPERTS, lt.shape[1]), F32)], axis=0)
    comb_ref[...] = comb_t.T


def _merge(h, ys, x2, gt1, wg, wb, wo, g2, sh2, sc2, rw, rb, seq):
    t, d = x2.shape
    tm = 512
    tpb = seq // tm
    row = lambda i: (i, 0)
    per_b = lambda i: (i // tpb, 0, 0)
    c2 = lambda i: (0, 0)
    return pl.pallas_call(
        _merge_kernel,
        grid=(t // tm,),
        in_specs=[pl.BlockSpec((tm, d), row)] + [pl.BlockSpec((tm, 512), row)] * 4
                 + [pl.BlockSpec((tm, d), row), pl.BlockSpec((1, 1, d), per_b),
                    pl.BlockSpec(wg.shape, c2), pl.BlockSpec(wb.shape, lambda i: (0, 0, 0)),
                    pl.BlockSpec(wo.shape, c2), pl.BlockSpec((1, d), c2),
                    pl.BlockSpec((1, 1, d), per_b), pl.BlockSpec((1, 1, d), per_b),
                    pl.BlockSpec(rw.shape, c2), pl.BlockSpec(rb.shape, c2)],
        out_specs=[pl.BlockSpec((tm, d), row), pl.BlockSpec((tm, d), row), pl.BlockSpec((tm, LANE), row)],
        out_shape=[jax.ShapeDtypeStruct((t, d), F32), jax.ShapeDtypeStruct((t, d), BF16),
                   jax.ShapeDtypeStruct((t, LANE), F32)],
        compiler_params=_cp(("arbitrary",)),
        name="merge_router",
    )(h, *ys, x2, gt1, wg, wb, wo, g2, sh2, sc2, rw, rb)


def _moe_kernel(h2_ref, comb_ref, x_ref, gt2_ref, wg_ref, wu_ref, wd_ref, o_ref, acc_ref):
    e = pl.program_id(1)

    @pl.when(e == 0)
    def _():
        acc_ref[...] = jnp.zeros_like(acc_ref)

    h2 = h2_ref[...]
    hid = _silu(_dot(h2, wg_ref[0])) * _dot(h2, wu_ref[0])
    comb = comb_ref[...]
    lane = lax.broadcasted_iota(jnp.int32, comb.shape, 1)
    ce = jnp.sum(jnp.where(lane == e, comb, 0.0), axis=-1, keepdims=True)
    acc_ref[...] += _dot((hid * ce).astype(BF16), wd_ref[0])

    @pl.when(e == pl.num_programs(1) - 1)
    def _():
        o_ref[...] = x_ref[...] + gt2_ref[0] * acc_ref[...]


def _moe(h2, comb, x2, gt2, wg, wu, wd, seq):
    t, d = x2.shape
    ne, _, f = wg.shape
    tm = 1024
    tpb = seq // tm
    row = lambda i, e: (i, 0)
    return pl.pallas_call(
        _moe_kernel,
        grid=(t // tm, ne),
        in_specs=[pl.BlockSpec((tm, d), row), pl.BlockSpec((tm, LANE), row), pl.BlockSpec((tm, d), row),
                  pl.BlockSpec((1, 1, d), lambda i, e: (i // tpb, 0, 0)),
                  pl.BlockSpec((1, d, f), lambda i, e: (e, 0, 0)),
                  pl.BlockSpec((1, d, f), lambda i, e: (e, 0, 0)),
                  pl.BlockSpec((1, f, d), lambda i, e: (e, 0, 0))],
        out_specs=pl.BlockSpec((tm, d), row),
        out_shape=jax.ShapeDtypeStruct((t, d), F32),
        scratch_shapes=[pltpu.VMEM((tm, d), F32)],
        compiler_params=_cp(("arbitrary", "arbitrary")),
        name="moe",
    )(h2, comb, x2, gt2, wg, wu, wd)


def _split_cols(w):
    out, o = [], 0
    for n in IN_SPLITS:
        out.append(w[:, o:o + n])
        o += n
    return out


def _layer_params(l, w_in, hg_onorm, mla_q_norm, mla_kv_norm, mla_w_uq, mla_w_ukv, mla_qk_norm,
                  diff_qk_norm, swa_qk_norm, swa_sinks, lb_all):
    (hq, hf, hi, hgate, mcq, mckv, mkr, dq, dk, dv, sq, sk, sv, gates) = _split_cols(w_in[l])
    d = w_in.shape[1]
    w1 = jnp.concatenate([hq, hi, hgate, hf, mcq, mckv, mkr, jnp.zeros((d, 96), F32),
                          dq, dk, dv, sq, sk, sv], axis=1).astype(BF16)
    p = {"w1": w1, "wg": gates.astype(BF16)}

    lb = lb_all[l]
    p["loglb"] = jnp.log(lb)[None, :]
    p["log1mlb"] = jnp.log1p(-lb)[None, :]
    p["ogain"] = jnp.tile(hg_onorm[l], HG_HEADS)[None, :]

    hd = MLA_NOPE + MLA_ROPE
    half = MLA_ROPE // 2
    wq = mla_w_uq[l].reshape(MLA_Q_RANK, MLA_HEADS, hd)
    z = lambda r, n: jnp.zeros((r, MLA_HEADS, n), F32)
    nope, rope = wq[:, :, :MLA_NOPE], wq[:, :, MLA_NOPE:]
    p["wqa"] = jnp.concatenate([nope, rope, z(MLA_Q_RANK, 32)], -1).reshape(MLA_Q_RANK, -1).astype(BF16)
    p["wqb"] = jnp.concatenate([z(MLA_Q_RANK, MLA_NOPE), rope[:, :, half:], rope[:, :, :half],
                                z(MLA_Q_RANK, 32)], -1).reshape(MLA_Q_RANK, -1).astype(BF16)
    wkv = mla_w_ukv[l].reshape(MLA_KV_RANK, MLA_HEADS, MLA_NOPE + MLA_V)
    knope, vproj = wkv[:, :, :MLA_NOPE], wkv[:, :, MLA_NOPE:]
    eye = jnp.eye(MLA_ROPE, dtype=F32)
    swap = jnp.concatenate([eye[:, half:], eye[:, :half]], axis=1)
    place = lambda m: jnp.broadcast_to(
        jnp.concatenate([jnp.zeros((MLA_ROPE, MLA_NOPE), F32), m, jnp.zeros((MLA_ROPE, 32), F32)], -1)[:, None, :],
        (MLA_ROPE, MLA_HEADS, LANE))
    pad_rows = 256 - MLA_KV_RANK - MLA_ROPE
    p["wka"] = jnp.concatenate([jnp.concatenate([knope, z(MLA_KV_RANK, 64)], -1), place(eye),
                                z(pad_rows, LANE)], 0).reshape(256, -1).astype(BF16)
    p["wkb"] = jnp.concatenate([z(MLA_KV_RANK, LANE), place(swap), z(pad_rows, LANE)], 0
                               ).reshape(256, -1).astype(BF16)
    p["wv"] = jnp.concatenate([vproj.reshape(MLA_KV_RANK, -1),
                               jnp.zeros((256 - MLA_KV_RANK, MLA_HEADS * MLA_V), F32)], 0).astype(BF16)
    p["qng"] = mla_q_norm[l][None, :]
    p["kvg"] = jnp.concatenate([mla_kv_norm[l], jnp.ones((256 - MLA_KV_RANK,), F32)])[None, :]

    def rope_gains(g):
        base = jnp.concatenate([g, jnp.zeros((LANE - hd,), F32)])
        part = jnp.concatenate([jnp.zeros((MLA_NOPE,), F32), g[MLA_NOPE + half:], g[MLA_NOPE:MLA_NOPE + half],
                                jnp.zeros((LANE - hd,), F32)])
        return base[None, :], part[None, :]

    p["gq"], p["gqs"] = rope_gains(mla_qk_norm[l, 0])
    p["gk"], p["gks"] = rope_gains(mla_qk_norm[l, 1])
    p["dgq"] = jnp.tile(diff_qk_norm[l, 0], 8)[None, :]
    p["dgk"] = jnp.tile(diff_qk_norm[l, 1], 8)[None, :]
    p["sgq"] = jnp.tile(swa_qk_norm[l, 0], 8)[None, :]
    p["sgk"] = jnp.tile(swa_qk_norm[l, 1], 2)[None, :]
    p["sinks"] = jnp.broadcast_to(swa_sinks[l][:, None], (SWA_Q_HEADS, LANE))
    return p


def _rope_tables(positions):
    inv_freq = ROPE_BASE ** (-jnp.arange(0, MLA_ROPE, 2, dtype=F32) / MLA_ROPE)
    ang = positions.astype(F32).reshape(-1)[:, None] * inv_freq
    cos, sin = jnp.cos(ang), jnp.sin(ang)
    t = ang.shape[0]
    cosf = jnp.concatenate([jnp.ones((t, MLA_NOPE), F32), cos, cos, jnp.zeros((t, 32), F32)], axis=1)
    sinf = jnp.concatenate([jnp.zeros((t, MLA_NOPE), F32), -sin, sin, jnp.zeros((t, 32), F32)], axis=1)
    return cosf, sinf


def kernel(x, c, positions, ada_w, ada_b, norm_mix, norm_ffn, w_in, hg_lb_logits, hg_onorm, mla_q_norm, mla_kv_norm, mla_w_uq, mla_w_ukv, mla_qk_norm, diff_qk_norm, diff_lam, diff_onorm, swa_qk_norm, swa_sinks, w_branch, w_out, router_w, router_bias, moe_w_gate, moe_w_up, moe_w_down):
    batch, seq, d = x.shape
    x2 = x.reshape(batch * seq, d)
    cosf, sinf = _rope_tables(positions)
    lb_all = jnp.cumsum(jax.nn.softmax(hg_lb_logits.astype(F32), axis=0), axis=0)
    lb_all = lb_all - lb_all[0]
    mod = _modulation(c, ada_w, ada_b)
    rw = jnp.concatenate([router_w, jnp.zeros((d, LANE - N_EXPERTS), F32)], axis=1)
    rb = router_bias.astype(F32)[:, None]

    for l in range(DEPTH):
        sh1, sc1, gt1, sh2, sc2, gt2 = [mod[l, :, d * k:d * (k + 1)][:, None, :] for k in range(6)]
        p = _layer_params(l, w_in, hg_onorm, mla_q_norm, mla_kv_norm, mla_w_uq, mla_w_ukv, mla_qk_norm,
                          diff_qk_norm, swa_qk_norm, swa_sinks, lb_all)
        hg3, hf, mla, diff, swa, h = _inproj(x2, norm_mix[l][None, :], sh1, sc1, p["w1"], seq)
        y_a = _hgrn(hg3, hf, p["loglb"], p["log1mlb"], p["ogain"], batch, seq)
        qm, km, vm, qd, kd, qs, ks, vs = _prep(mla, diff, swa, cosf, sinf, p)
        y_b = _mla_attn(qm, km, vm, batch, seq)
        lam_init = 0.8 - 0.6 * math.exp(-0.3 * l)
        y_c = _diff_attn(qd, kd, diff, diff_lam[l], diff_onorm[l][None, :], lam_init, batch, seq)
        y_d = _swa(qs, ks, vs, p["sinks"], batch, seq)
        x2, h2, comb = _merge(h, (y_a, y_b, y_c, y_d), x2, gt1, p["wg"], w_branch[l].astype(BF16),
                              w_out[l].astype(BF16), norm_ffn[l][None, :], sh2, sc2, rw, rb, seq)
        x2 = _moe(h2, comb, x2, gt2, moe_w_gate[l].astype(BF16), moe_w_up[l].astype(BF16),
                  moe_w_down[l].astype(BF16), seq)
    return x2.reshape(batch, seq, d)
```

```python
import functools
import math

import jax
import jax.numpy as jnp
from jax import lax
from jax.experimental import pallas as pl
from jax.experimental.pallas import tpu as pltpu

F32 = jnp.float32
BF16 = jnp.bfloat16

D_MODEL = 1024
DEPTH = 2
EPS = 1e-6
N_BRANCH = 4
HG_HEADS = 8
HG_DK = 64
HG_W = HG_HEADS * HG_DK
HG_SUB = 16
MLA_HEADS = 8
MLA_Q_RANK = 256
MLA_KV_RANK = 128
MLA_NOPE = 64
MLA_ROPE = 32
MLA_V = 64
ROPE_BASE = 10000.0
DIFF_HEADS = 4
DIFF_QK = 64
DIFF_V = 128
SWA_Q_HEADS = 8
SWA_KV_HEADS = 2
SWA_WINDOW = 128
HEAD_DIM = 64
N_EXPERTS = 16
N_GROUPS = 4
D_FF_EXPERT = 256
IN_SPLITS = (512, 512, 512, 512, 256, 128, 32, 512, 512, 512, 512, 128, 128, 4096)

LANE = 128
ATT_BLK = 256
SWA_QB = 4
LOG2E = 1.4426950408889634
NEG = -1e30
VMEM_LIMIT = 56 * 1024 * 1024


def _cp(sem, vmem=VMEM_LIMIT):
    return pltpu.CompilerParams(dimension_semantics=sem, vmem_limit_bytes=vmem)


def _nt(a, b):
    return lax.dot_general(a, b, (((1,), (1,)), ((), ())), preferred_element_type=F32)


def _tn(a, b):
    return lax.dot_general(a, b, (((0,), (0,)), ((), ())), preferred_element_type=F32)


def _dot(a, b):
    return jnp.dot(a, b, preferred_element_type=F32)


def _split2(x):
    hi = x.astype(BF16)
    lo = (x - hi.astype(F32)).astype(BF16)
    return hi, lo


def _split3(x):
    hi = x.astype(BF16)
    r = x - hi.astype(F32)
    mid = r.astype(BF16)
    lo = (r - mid.astype(F32)).astype(BF16)
    return hi, mid, lo


def _seg_id(idx, seg):
    shift = seg.bit_length() - 1
    assert 1 << shift == seg
    return lax.shift_right_logical(idx, shift)


def _same_seg(n, seg):
    r = lax.broadcasted_iota(jnp.int32, (n, n), 0)
    c = lax.broadcasted_iota(jnp.int32, (n, n), 1)
    return _seg_id(r, seg) == _seg_id(c, seg)


def _seg_ones(n, seg):
    return _same_seg(n, seg).astype(BF16)


def _seg_mean_sq(x, seg):
    n = x.shape[-1]
    hi, lo = _split2(x * x)
    ones = _seg_ones(n, seg)
    return (_dot(hi, ones) + _dot(lo, ones)) * (1.0 / seg)


def _silu(x):
    return x * jax.nn.sigmoid(x)


def _mod_kernel(c_ref, w_ref, b_ref, o_ref):
    c = c_ref[...]
    o_ref[0] = jnp.dot(_silu(c), w_ref[0], preferred_element_type=F32,
                       precision=lax.Precision.HIGHEST) + b_ref[0]


def _modulation(c, ada_w, ada_b):
    nl, d, n6 = ada_w.shape
    b = c.shape[0]
    tn = 1536
    return pl.pallas_call(
        _mod_kernel,
        grid=(nl, n6 // tn),
        in_specs=[pl.BlockSpec((b, d), lambda l, j: (0, 0)),
                  pl.BlockSpec((1, d, tn), lambda l, j: (l, 0, j)),
                  pl.BlockSpec((1, 1, tn), lambda l, j: (l, 0, j))],
        out_specs=pl.BlockSpec((1, b, tn), lambda l, j: (l, 0, j)),
        out_shape=jax.ShapeDtypeStruct((nl, b, n6), F32),
        compiler_params=_cp(("arbitrary", "arbitrary")),
        name="modulation",
    )(c, ada_w, ada_b.reshape(nl, 1, n6))


W1_COLS = 4864


def _inproj_kernel(x_ref, g_ref, sh_ref, sc_ref, w_ref,
                   ohg_ref, ohf_ref, omla_ref, odiff_ref, oswa_ref, oh_ref):
    x = x_ref[...]
    ms = jnp.mean(x * x, axis=-1, keepdims=True)
    h = x * lax.rsqrt(ms + EPS) * g_ref[...]
    h = h * (1.0 + sc_ref[0]) + sh_ref[0]
    hb = h.astype(BF16)
    oh_ref[...] = hb

    def proj(lo, hi):
        return _dot(hb, w_ref[:, lo:hi])

    for k in range(3):
        ohg_ref[:, 512 * k:512 * (k + 1)] = proj(512 * k, 512 * (k + 1)).astype(BF16)
    ohf_ref[...] = proj(1536, 2048)
    omla_ref[...] = proj(2048, 2560).astype(BF16)
    for k in range(3):
        odiff_ref[:, 512 * k:512 * (k + 1)] = proj(2560 + 512 * k, 3072 + 512 * k).astype(BF16)
    oswa_ref[:, 0:512] = proj(4096, 4608).astype(BF16)
    oswa_ref[:, 512:768] = proj(4608, 4864).astype(BF16)


def _inproj(x2, gain, sh, sc, w1, seq):
    t, d = x2.shape
    tm = 512
    tpb = seq // tm
    row = lambda i: (i, 0)
    per_b = lambda i: (i // tpb, 0, 0)
    outs = [(1536, BF16), (512, F32), (512, BF16), (1536, BF16), (768, BF16), (d, BF16)]
    return pl.pallas_call(
        _inproj_kernel,
        grid=(t // tm,),
        in_specs=[pl.BlockSpec((tm, d), row),
                  pl.BlockSpec((1, d), lambda i: (0, 0)),
                  pl.BlockSpec((1, 1, d), per_b),
                  pl.BlockSpec((1, 1, d), per_b),
                  pl.BlockSpec((d, W1_COLS), lambda i: (0, 0))],
        out_specs=[pl.BlockSpec((tm, w), row) for w, _ in outs],
        out_shape=[jax.ShapeDtypeStruct((t, w), dt) for w, dt in outs],
        compiler_params=_cp(("arbitrary",)),
        name="inproj",
    )(x2, gain, sh, sc, w1)


def _hgrn_kernel(q_ref, i_ref, g_ref, f_ref, loglb_ref, log1mlb_ref, og_ref, o_ref,
                 st_ref, c_s, qs_s, kk_s, qe_s, ke_s, dec_s, od_s, t_s, a_s):
    rows_blk = q_ref.shape[0]
    n_sub = rows_blk // HG_SUB

    @pl.when(pl.program_id(1) == 0)
    def _():
        st_ref[...] = jnp.zeros_like(st_ref)

    fr = f_ref[...]
    ls = jnp.minimum(fr, 0.0) - jnp.log1p(jnp.exp(-jnp.abs(fr)))
    a = loglb_ref[...]
    c2 = log1mlb_ref[...] + ls
    lf = jnp.maximum(a, c2) + jnp.log1p(jnp.exp(-jnp.abs(a - c2)))

    r = lax.broadcasted_iota(jnp.int32, (rows_blk, rows_blk), 0)
    cc = lax.broadcasted_iota(jnp.int32, (rows_blk, rows_blk), 1)
    same = _same_seg(rows_blk, HG_SUB)
    tri = (same & (cc <= r)).astype(BF16)
    blk = same.astype(BF16)
    parts = _split3(lf)
    c = _dot(tri, parts[0]) + _dot(tri, parts[1]) + _dot(tri, parts[2])
    tot = _dot(blk, parts[0]) + _dot(blk, parts[1]) + _dot(blk, parts[2])

    qs = _silu(q_ref[...].astype(F32))
    kk = 1.0 - jnp.exp(lf)
    c_s[...] = c
    qs_s[...] = qs
    kk_s[...] = kk
    qe_s[...] = (qs * jnp.exp(c)).astype(BF16)
    ke_s[...] = (kk * jnp.exp(tot - c)).astype(BF16)
    dec_s[...] = jnp.exp(tot)

    same_head = _same_seg(LANE, HG_DK)
    head_mask = same_head.astype(F32)
    head_ones = same_head.astype(BF16)
    trow = lax.broadcasted_iota(jnp.int32, (HG_SUB, HG_W), 0)

    def body(i, carry):
        r0 = pl.multiple_of(i * HG_SUB, HG_SUB)
        rows = pl.ds(r0, HG_SUB)
        c_i = c_s[rows, :]
        qs_i = qs_s[rows, :]
        kk_i = kk_s[rows, :]
        v_i = i_ref[rows, :].astype(F32)
        for s in range(HG_SUB):
            e = jnp.exp(jnp.where(trow >= s, c_i - c_i[s:s + 1, :], NEG))
            t_s[s * HG_SUB:(s + 1) * HG_SUB, :] = (e * qs_i * kk_i[s:s + 1, :]).astype(BF16)
        for j in range(HG_W // LANE):
            cols = slice(LANE * j, LANE * (j + 1))
            a_s[:, cols] = _dot(t_s[:, cols], head_ones)
        acc = jnp.zeros((HG_SUB, HG_W), F32)
        for s in range(HG_SUB):
            acc = acc + a_s[s * HG_SUB:(s + 1) * HG_SUB, :] * v_i[s:s + 1, :]
        for j in range(HG_W // LANE):
            cols = slice(LANE * j, LANE * (j + 1))
            st = st_ref[j]
            o_int = _nt(qe_s[rows, cols], st.astype(BF16))
            upd = _tn(i_ref[rows, cols], ke_s[rows, cols])
            st_ref[j] = st * dec_s[pl.ds(r0, 1), cols] + upd * head_mask
            od_s[rows, cols] = acc[:, cols] + o_int
        return carry

    lax.fori_loop(0, n_sub, body, 0)

    o = od_s[...]
    ms = _seg_mean_sq(o, HG_DK)
    on = o * lax.rsqrt(ms + EPS) * og_ref[...]
    o_ref[...] = (on * _silu(g_ref[...].astype(F32))).astype(BF16)


def _hgrn(hg3, hf, loglb, log1mlb, ogain, batch, seq):
    t = hf.shape[0]
    rb = 256
    nb = seq // rb
    blk = lambda k: pl.BlockSpec((rb, HG_W), lambda b, n, k=k: (b * nb + n, k))
    vec = pl.BlockSpec((1, HG_W), lambda b, n: (0, 0))
    return pl.pallas_call(
        _hgrn_kernel,
        grid=(batch, nb),
        in_specs=[blk(0), blk(1), blk(2), blk(0), vec, vec, vec],
        out_specs=blk(0),
        out_shape=jax.ShapeDtypeStruct((t, HG_W), BF16),
        scratch_shapes=[pltpu.VMEM((HG_W // LANE, LANE, LANE), F32),
                        pltpu.VMEM((rb, HG_W), F32), pltpu.VMEM((rb, HG_W), F32),
                        pltpu.VMEM((rb, HG_W), F32), pltpu.VMEM((rb, HG_W), BF16),
                        pltpu.VMEM((rb, HG_W), BF16), pltpu.VMEM((rb, HG_W), F32),
                        pltpu.VMEM((rb, HG_W), F32),
                        pltpu.VMEM((HG_SUB * HG_SUB, HG_W), BF16),
                        pltpu.VMEM((HG_SUB * HG_SUB, HG_W), F32)],
        compiler_params=_cp(("arbitrary", "arbitrary")),
        name="hgrn2",
    )(hg3, hg3, hg3, hf, loglb, log1mlb, ogain)


def _store_transposed_blocks(out_ref, v):
    blk = out_ref.shape[2]
    for u in range(out_ref.shape[0]):
        out_ref[u] = v[u * blk:(u + 1) * blk, :].T.astype(BF16)


def _prep_kernel(mla_ref, diff_ref, swa_ref, cos_ref, sin_ref,
                 qng_ref, kvg_ref, wqa_ref, wqb_ref, wka_ref, wkb_ref, wv_ref,
                 gq_ref, gqs_ref, gk_ref, gks_ref, dgq_ref, dgk_ref, sgq_ref, sgk_ref,
                 qm_ref, km_ref, vmt_ref, qd_ref, kd_ref, vdt_ref, qs_ref, ks_ref, vst_ref):
    blk = mla_ref[...].astype(F32)
    cq = blk[:, :MLA_Q_RANK]
    rest = blk[:, MLA_Q_RANK:]
    cqn = cq * lax.rsqrt(jnp.mean(cq * cq, axis=-1, keepdims=True) + EPS) * qng_ref[...]
    lane = lax.broadcasted_iota(jnp.int32, rest.shape, 1)
    is_kv = lane < MLA_KV_RANK
    ms_kv = jnp.sum(jnp.where(is_kv, rest * rest, 0.0), axis=-1, keepdims=True) * (1.0 / MLA_KV_RANK)
    restn = jnp.where(is_kv, rest * lax.rsqrt(ms_kv + EPS) * kvg_ref[...], rest)
    cqb = cqn.astype(BF16)
    rb = restn.astype(BF16)
    qa = _dot(cqb, wqa_ref[...])
    qb = _dot(cqb, wqb_ref[...])
    ka = _dot(rb, wka_ref[...])
    kb = _dot(rb, wkb_ref[...])
    _store_transposed_blocks(vmt_ref, _dot(rb, wv_ref[...]))
    cosf = cos_ref[...]
    sinf = sin_ref[...]
    cq_t = cosf * gq_ref[...]
    sq_t = sinf * gqs_ref[...]
    ck_t = cosf * gk_ref[...]
    sk_t = sinf * gks_ref[...]
    inv_n = 1.0 / (MLA_NOPE + MLA_ROPE)
    scale = (MLA_NOPE + MLA_ROPE) ** -0.5 * LOG2E
    for h in range(MLA_HEADS):
        cols = slice(LANE * h, LANE * (h + 1))
        x = qa[:, cols]
        rinv = lax.rsqrt(jnp.sum(x * x, axis=-1, keepdims=True) * inv_n + EPS)
        qm_ref[:, cols] = ((x * cq_t + qb[:, cols] * sq_t) * (rinv * scale)).astype(BF16)
        y = ka[:, cols]
        rinv = lax.rsqrt(jnp.sum(y * y, axis=-1, keepdims=True) * inv_n + EPS)
        km_ref[:, cols] = ((y * ck_t + kb[:, cols] * sk_t) * rinv).astype(BF16)

    def seg_norm(x, gain, scale):
        return x * lax.rsqrt(_seg_mean_sq(x, HEAD_DIM) + EPS) * (gain * scale)

    dq = diff_ref[:, 0:512].astype(F32)
    dk = diff_ref[:, 512:1024].astype(F32)
    qd_ref[...] = seg_norm(dq, dgq_ref[...], DIFF_QK ** -0.5 * LOG2E).astype(BF16)
    kd_ref[...] = seg_norm(dk, dgk_ref[...], 1.0).astype(BF16)
    _store_transposed_blocks(vdt_ref, diff_ref[:, 1024:1536].astype(F32))

    sq = swa_ref[:, 0:512].astype(F32)
    qs_ref[...] = seg_norm(sq, sgq_ref[...], HEAD_DIM ** -0.5 * LOG2E).astype(BF16)
    skv = swa_ref[:, 512:768].astype(F32)
    kn = seg_norm(skv[:, :LANE], sgk_ref[...], 1.0)
    low = lax.broadcasted_iota(jnp.int32, kn.shape, 1) < HEAD_DIM
    sw = pltpu.roll(kn, HEAD_DIM, 1)
    ks_ref[:, :LANE] = jnp.where(low, kn, sw).astype(BF16)
    ks_ref[:, LANE:] = jnp.where(low, sw, kn).astype(BF16)
    _store_transposed_blocks(vst_ref, skv[:, LANE:])


def _prep(mla, diff, swa, cosf, sinf, p):
    t = mla.shape[0]
    tm = 512
    row = lambda i: (i, 0)
    full = lambda a: pl.BlockSpec(a.shape, lambda i: (0,) * a.ndim)
    consts = [p["qng"], p["kvg"], p["wqa"], p["wqb"], p["wka"], p["wkb"], p["wv"],
              p["gq"], p["gqs"], p["gk"], p["gks"], p["dgq"], p["dgk"], p["sgq"], p["sgk"]]
    def rows_out(w):
        return pl.BlockSpec((tm, w), row), jax.ShapeDtypeStruct((t, w), BF16)

    def transposed_out(n, blk):
        return (pl.BlockSpec((tm // blk, n, blk), lambda i: (i, 0, 0)),
                jax.ShapeDtypeStruct((t // blk, n, blk), BF16))

    outs = [rows_out(1024), rows_out(1024), transposed_out(512, ATT_BLK),
            rows_out(512), rows_out(512), transposed_out(512, ATT_BLK),
            rows_out(512), rows_out(256), transposed_out(LANE, SWA_WINDOW)]
    return pl.pallas_call(
        _prep_kernel,
        grid=(t // tm,),
        in_specs=[pl.BlockSpec((tm, 512), row),
                  pl.BlockSpec((tm, 1536), row),
                  pl.BlockSpec((tm, 768), row),
                  pl.BlockSpec((tm, LANE), row),
                  pl.BlockSpec((tm, LANE), row)] + [full(a) for a in consts],
        out_specs=[o[0] for o in outs],
        out_shape=[o[1] for o in outs],
        compiler_params=_cp(("arbitrary",)),
        name="attn_prep",
    )(mla, diff, swa, cosf, sinf, *consts)


def _causal_t(blk):
    key = lax.broadcasted_iota(jnp.int32, (blk, blk), 0)
    qry = lax.broadcasted_iota(jnp.int32, (blk, blk), 1)
    return key <= qry


def _two_pass_attention(n_sets, score_fn, value_fn, s_scr, acc_scr, blk):
    qi = pl.program_id(1)
    causal = _causal_t(blk)

    def scores(ki, m, masked):
        out = []
        for i in range(n_sets):
            s = score_fn(i, ki)
            if masked:
                s = jnp.where(causal, s, NEG)
            s_scr[i, ki] = s
            out.append(jnp.maximum(m[i], jnp.max(s, axis=0, keepdims=True)))
        return tuple(out)

    m = tuple(jnp.full((1, blk), NEG, F32) for _ in range(n_sets))
    m = lax.fori_loop(0, qi, lambda ki, c: scores(ki, c, False), m)
    m = scores(qi, m, True)

    acc_scr[...] = jnp.zeros_like(acc_scr)

    def accumulate(ki, l):
        out = []
        for i in range(n_sets):
            p = jnp.exp2(s_scr[i, ki] - m[i])
            out.append(l[i] + jnp.sum(p, axis=0, keepdims=True))
            acc_scr[i] += _dot(value_fn(i, ki), p.astype(BF16))
        return tuple(out)

    zero = tuple(jnp.zeros((1, blk), F32) for _ in range(n_sets))
    return lax.fori_loop(0, qi + 1, accumulate, zero)


def _mla_attn_kernel(q_ref, k_ref, vt_ref, o_ref, s_scr, acc_scr):
    blk = q_ref.shape[0]

    def score_fn(h, ki):
        rows = pl.ds(pl.multiple_of(ki * blk, blk), blk)
        cols = slice(LANE * h, LANE * (h + 1))
        return _nt(k_ref[rows, cols], q_ref[:, cols])

    def value_fn(h, ki):
        return vt_ref[ki, MLA_V * h:MLA_V * (h + 1), :]

    l = _two_pass_attention(MLA_HEADS, score_fn, value_fn, s_scr, acc_scr, blk)
    for j in range(MLA_HEADS // 2):
        o_t = jnp.concatenate([acc_scr[2 * j] / l[2 * j], acc_scr[2 * j + 1] / l[2 * j + 1]], axis=0)
        o_ref[:, LANE * j:LANE * (j + 1)] = o_t.T.astype(BF16)


def _mla_attn(qm, km, vmt, batch, seq):
    t = qm.shape[0]
    nq = seq // ATT_BLK
    return pl.pallas_call(
        _mla_attn_kernel,
        grid=(batch, nq),
        in_specs=[pl.BlockSpec((ATT_BLK, 1024), lambda b, i: (b * nq + i, 0)),
                  pl.BlockSpec((seq, 1024), lambda b, i: (b, 0)),
                  pl.BlockSpec((nq, 512, ATT_BLK), lambda b, i: (b, 0, 0))],
        out_specs=pl.BlockSpec((ATT_BLK, 512), lambda b, i: (b * nq + i, 0)),
        out_shape=jax.ShapeDtypeStruct((t, 512), BF16),
        scratch_shapes=[pltpu.VMEM((MLA_HEADS, nq, ATT_BLK, ATT_BLK), F32),
                        pltpu.VMEM((MLA_HEADS, MLA_V, ATT_BLK), F32)],
        compiler_params=_cp(("arbitrary", "arbitrary")),
        name="mla_attn",
    )(qm, km, vmt)


def _diff_attn_kernel(q_ref, k_ref, vt_ref, lam_ref, og_ref, o_ref, s_scr, acc_scr, qm_scr, *, lam_init):
    blk = q_ref.shape[0]
    low = lax.broadcasted_iota(jnp.int32, (blk, LANE), 1) < DIFF_QK
    lp = lam_ref[...]
    lam = (jnp.exp(jnp.sum(lp[0:1] * lp[1:2], axis=-1, keepdims=True))
           - jnp.exp(jnp.sum(lp[2:3] * lp[3:4], axis=-1, keepdims=True)) + lam_init)

    for h in range(DIFF_HEADS):
        qt = q_ref[:, LANE * h:LANE * (h + 1)]
        zero = jnp.zeros_like(qt)
        qm_scr[2 * h] = jnp.where(low, qt, zero)
        qm_scr[2 * h + 1] = jnp.where(low, zero, qt)

    def score_fn(i, ki):
        rows = pl.ds(pl.multiple_of(ki * blk, blk), blk)
        h = i // 2
        return _nt(k_ref[rows, LANE * h:LANE * (h + 1)], qm_scr[i])

    def value_fn(i, ki):
        h = i // 2
        return vt_ref[ki, DIFF_V * h:DIFF_V * (h + 1), :]

    l = _two_pass_attention(2 * DIFF_HEADS, score_fn, value_fn, s_scr, acc_scr, blk)
    for h in range(DIFF_HEADS):
        o_t = acc_scr[2 * h] / l[2 * h] - lam * (acc_scr[2 * h + 1] / l[2 * h + 1])
        on_t = o_t * lax.rsqrt(jnp.mean(o_t * o_t, axis=0, keepdims=True) + EPS)
        o_ref[:, LANE * h:LANE * (h + 1)] = (on_t.T * (og_ref[...] * (1.0 - lam_init))).astype(BF16)


def _diff_attn(qd, kd, vdt, lam_p, og, lam_init, batch, seq):
    t = qd.shape[0]
    nq = seq // ATT_BLK
    return pl.pallas_call(
        functools.partial(_diff_attn_kernel, lam_init=lam_init),
        grid=(batch, nq),
        in_specs=[pl.BlockSpec((ATT_BLK, 512), lambda b, i: (b * nq + i, 0)),
                  pl.BlockSpec((seq, 512), lambda b, i: (b, 0)),
                  pl.BlockSpec((nq, 512, ATT_BLK), lambda b, i: (b, 0, 0)),
                  pl.BlockSpec(lam_p.shape, lambda b, i: (0, 0)),
                  pl.BlockSpec(og.shape, lambda b, i: (0, 0))],
        out_specs=pl.BlockSpec((ATT_BLK, 512), lambda b, i: (b * nq + i, 0)),
        out_shape=jax.ShapeDtypeStruct((t, 512), BF16),
        scratch_shapes=[pltpu.VMEM((2 * DIFF_HEADS, nq, ATT_BLK, ATT_BLK), F32),
                        pltpu.VMEM((2 * DIFF_HEADS, DIFF_V, ATT_BLK), F32),
                        pltpu.VMEM((2 * DIFF_HEADS, ATT_BLK, LANE), BF16)],
        compiler_params=_cp(("arbitrary", "arbitrary")),
        name="diff_attn",
    )(qd, kd, vdt, lam_p, og)


def _swa_kernel(q_ref, kp_ref, kc_ref, vtp_ref, vtc_ref, sink_ref, o_ref):
    w = SWA_WINDOW
    grp = SWA_Q_HEADS // SWA_KV_HEADS
    n = pl.program_id(1)
    key = lax.broadcasted_iota(jnp.int32, (2 * w, grp * w), 0)
    qry = lax.broadcasted_iota(jnp.int32, (2 * w, grp * w), 1) & (w - 1)
    cur_ok = (key >= w) & (key - w <= qry)
    prev_ok = (key < w) & (key > qry)
    low = lax.broadcasted_iota(jnp.int32, (w, LANE), 1) < HEAD_DIM

    for t in range(q_ref.shape[0] // w):
        rows = slice(t * w, (t + 1) * w)
        if t == 0:
            kp, vtp = kp_ref[...], vtp_ref[0]
            valid = cur_ok | (prev_ok & (n > 0))
        else:
            kp, vtp = kc_ref[(t - 1) * w:t * w, :], vtc_ref[t - 1]
            valid = cur_ok | prev_ok
        kc, vtc = kc_ref[rows, :], vtc_ref[t]
        for kv in range(SWA_KV_HEADS):
            kcols = slice(LANE * kv, LANE * (kv + 1))
            vrows = slice(HEAD_DIM * kv, HEAD_DIM * (kv + 1))
            k_win = jnp.concatenate([kp[:, kcols], kc[:, kcols]], axis=0)
            parts = []
            for u in range(2):
                qt = q_ref[rows, LANE * (2 * kv + u):LANE * (2 * kv + u + 1)]
                zero = jnp.zeros_like(qt)
                parts += [jnp.where(low, qt, zero), jnp.where(low, zero, qt)]
            s = jnp.where(valid, _nt(k_win, jnp.concatenate(parts, axis=0)), NEG)
            sink = sink_ref[:, grp * w * kv:grp * w * (kv + 1)]
            m = jnp.maximum(jnp.max(s, axis=0, keepdims=True), sink)
            p = jnp.exp2(s - m)
            den = jnp.sum(p, axis=0, keepdims=True) + jnp.exp2(sink - m)
            vt_win = jnp.concatenate([vtp[vrows, :], vtc[vrows, :]], axis=1)
            o_t = _dot(vt_win, p.astype(BF16)) / den
            for u in range(2):
                pair = jnp.concatenate([o_t[:, 2 * u * w:(2 * u + 1) * w],
                                        o_t[:, (2 * u + 1) * w:(2 * u + 2) * w]], axis=0)
                o_ref[rows, LANE * (2 * kv + u):LANE * (2 * kv + u + 1)] = pair.T.astype(BF16)


def _swa(qs, ks, vst, sink_row, batch, seq):
    t = qs.shape[0]
    w = SWA_WINDOW
    nb = seq // w
    ns = nb // SWA_QB
    cur = lambda b, n: (b * ns + n, 0)
    cur3 = lambda b, n: (b * ns + n, 0, 0)
    prev = lambda b, n: (b * nb + jnp.maximum(n * SWA_QB - 1, 0), 0)
    prev3 = lambda b, n: (b * nb + jnp.maximum(n * SWA_QB - 1, 0), 0, 0)
    return pl.pallas_call(
        _swa_kernel,
        grid=(batch, ns),
        in_specs=[pl.BlockSpec((SWA_QB * w, 512), cur),
                  pl.BlockSpec((w, 256), prev), pl.BlockSpec((SWA_QB * w, 256), cur),
                  pl.BlockSpec((1, LANE, w), prev3), pl.BlockSpec((SWA_QB, LANE, w), cur3),
                  pl.BlockSpec(sink_row.shape, lambda b, n: (0, 0))],
        out_specs=pl.BlockSpec((SWA_QB * w, 512), cur),
        out_shape=jax.ShapeDtypeStruct((t, 512), BF16),
        compiler_params=_cp(("arbitrary", "arbitrary")),
        name="swa_attn",
    )(qs, ks, ks, vst, vst, sink_row)


def _merge_kernel(h_ref, ya_ref, yb_ref, yc_ref, yd_ref, x_ref, gt1_ref, wg_ref, wb_ref, wo_ref,
                  g2_ref, sh2_ref, sc2_ref, rw_ref, rb_ref, xo_ref, h2_ref, comb_ref):
    h = h_ref[...]
    d = x_ref.shape[1]
    merged = None
    for b, y_ref in enumerate((ya_ref, yb_ref, yc_ref, yd_ref)):
        gate = jax.nn.sigmoid(_dot(h, wg_ref[:, d * b:d * (b + 1)]))
        term = gate * _dot(y_ref[...], wb_ref[b])
        merged = term if merged is None else merged + term
    xn = x_ref[...] + gt1_ref[0] * _dot(merged.astype(BF16), wo_ref[...])
    xo_ref[...] = xn
    ms = jnp.mean(xn * xn, axis=-1, keepdims=True)
    h2 = xn * lax.rsqrt(ms + EPS) * g2_ref[...]
    h2 = h2 * (1.0 + sc2_ref[0]) + sh2_ref[0]
    h2_ref[...] = h2.astype(BF16)

    hh, hm, _ = _split3(h2)
    wh, wm = _split2(rw_ref[...])
    logits = _dot(hh, wh) + _dot(hm, wh) + _dot(hh, wm)
    lt = logits.T
    scores = jax.nn.sigmoid(lt[0:N_EXPERTS, :])
    sel = scores + rb_ref[...]
    per = N_EXPERTS // N_GROUPS
    srow = [sel[e:e + 1, :] for e in range(N_EXPERTS)]
    gsum = []
    for g in range(N_GROUPS):
        a, b_, c, e_ = srow[per * g:per * (g + 1)]
        gsum.append(jnp.maximum(jnp.maximum(jnp.maximum(a + b_, a + c), jnp.maximum(a + e_, b_ + c)),
                                jnp.maximum(b_ + e_, c + e_)))
    best = jnp.maximum(jnp.maximum(gsum[0], gsum[1]), jnp.maximum(gsum[2], gsum[3]))
    taken = None
    rows = []
    for g in range(N_GROUPS):
        hit = gsum[g] == best
        pick = hit if taken is None else hit & jnp.logical_not(taken)
        taken = hit if taken is None else taken | hit
        for e in range(per * g, per * (g + 1)):
            rank = jnp.zeros_like(best)
            for o in range(per * g, per * (g + 1)):
                if o == e:
                    continue
                ahead = (srow[o] > srow[e]) | ((srow[o] == srow[e]) & (o < e))
                rank = rank + ahead.astype(F32)
            rows.append(jnp.where(pick & (rank < 1.5), scores[e:e + 1, :], 0.0))
    wsum = rows[0]
    for r_ in rows[1:]:
        wsum = wsum + r_
    inv = 1.0 / wsum
    rid = lax.broadcasted_iota(jnp.int32, scores.shape, 0)
    comb_e = jnp.zeros_like(scores)
    for e, r_ in enumerate(rows):
        comb_e = jnp.where(rid == e, r_ * inv, comb_e)
    comb_t = jnp.concatenate([comb_e, jnp.zeros((LANE - N_EXPERTS, lt.shape[1]), F32)], axis=0)
    comb_ref[...] = comb_t.T


def _merge(h, ys, x2, gt1, wg, wb, wo, g2, sh2, sc2, rw, rb, seq):
    t, d = x2.shape
    tm = 512
    tpb = seq // tm
    row = lambda i: (i, 0)
    per_b = lambda i: (i // tpb, 0, 0)
    c2 = lambda i: (0, 0)
    return pl.pallas_call(
        _merge_kernel,
        grid=(t // tm,),
        in_specs=[pl.BlockSpec((tm, d), row)] + [pl.BlockSpec((tm, 512), row)] * 4
                 + [pl.BlockSpec((tm, d), row), pl.BlockSpec((1, 1, d), per_b),
                    pl.BlockSpec(wg.shape, c2), pl.BlockSpec(wb.shape, lambda i: (0, 0, 0)),
                    pl.BlockSpec(wo.shape, c2), pl.BlockSpec((1, d), c2),
                    pl.BlockSpec((1, 1, d), per_b), pl.BlockSpec((1, 1, d), per_b),
                    pl.BlockSpec(rw.shape, c2), pl.BlockSpec(rb.shape, c2)],
        out_specs=[pl.BlockSpec((tm, d), row), pl.BlockSpec((tm, d), row), pl.BlockSpec((tm, LANE), row)],
        out_shape=[jax.ShapeDtypeStruct((t, d), F32), jax.ShapeDtypeStruct((t, d), BF16),
                   jax.ShapeDtypeStruct((t, LANE), F32)],
        compiler_params=_cp(("arbitrary",)),
        name="merge_router",
    )(h, *ys, x2, gt1, wg, wb, wo, g2, sh2, sc2, rw, rb)


def _moe_kernel(h2_ref, comb_ref, x_ref, gt2_ref, wg_ref, wu_ref, wd_ref, o_ref, acc_ref):
    e = pl.program_id(1)

    @pl.when(e == 0)
    def _():
        acc_ref[...] = jnp.zeros_like(acc_ref)

    h2 = h2_ref[...]
    hid = _silu(_dot(h2, wg_ref[0])) * _dot(h2, wu_ref[0])
    comb = comb_ref[...]
    lane = lax.broadcasted_iota(jnp.int32, comb.shape, 1)
    ce = jnp.sum(jnp.where(lane == e, comb, 0.0), axis=-1, keepdims=True)
    acc_ref[...] += _dot((hid * ce).astype(BF16), wd_ref[0])

    @pl.when(e == pl.num_programs(1) - 1)
    def _():
        o_ref[...] = x_ref[...] + gt2_ref[0] * acc_ref[...]


def _moe(h2, comb, x2, gt2, wg, wu, wd, seq):
    t, d = x2.shape
    ne, _, f = wg.shape
    tm = 1024
    tpb = seq // tm
    row = lambda i, e: (i, 0)
    return pl.pallas_call(
        _moe_kernel,
        grid=(t // tm, ne),
        in_specs=[pl.BlockSpec((tm, d), row), pl.BlockSpec((tm, LANE), row), pl.BlockSpec((tm, d), row),
                  pl.BlockSpec((1, 1, d), lambda i, e: (i // tpb, 0, 0)),
                  pl.BlockSpec((1, d, f), lambda i, e: (e, 0, 0)),
                  pl.BlockSpec((1, d, f), lambda i, e: (e, 0, 0)),
                  pl.BlockSpec((1, f, d), lambda i, e: (e, 0, 0))],
        out_specs=pl.BlockSpec((tm, d), row),
        out_shape=jax.ShapeDtypeStruct((t, d), F32),
        scratch_shapes=[pltpu.VMEM((tm, d), F32)],
        compiler_params=_cp(("arbitrary", "arbitrary")),
        name="moe",
    )(h2, comb, x2, gt2, wg, wu, wd)


def _split_cols(w):
    out, o = [], 0
    for n in IN_SPLITS:
        out.append(w[:, o:o + n])
        o += n
    return out


def _layer_params(l, w_in, hg_onorm, mla_q_norm, mla_kv_norm, mla_w_uq, mla_w_ukv, mla_qk_norm,
                  diff_qk_norm, swa_qk_norm, swa_sinks, lb_all):
    (hq, hf, hi, hgate, mcq, mckv, mkr, dq, dk, dv, sq, sk, sv, gates) = _split_cols(w_in[l])
    d = w_in.shape[1]
    w1 = jnp.concatenate([hq, hi, hgate, hf, mcq, mckv, mkr, jnp.zeros((d, 96), F32),
                          dq, dk, dv, sq, sk, sv], axis=1).astype(BF16)
    p = {"w1": w1, "wg": gates.astype(BF16)}

    lb = lb_all[l]
    p["loglb"] = jnp.log(lb)[None, :]
    p["log1mlb"] = jnp.log1p(-lb)[None, :]
    p["ogain"] = jnp.tile(hg_onorm[l], HG_HEADS)[None, :]

    hd = MLA_NOPE + MLA_ROPE
    half = MLA_ROPE // 2
    wq = mla_w_uq[l].reshape(MLA_Q_RANK, MLA_HEADS, hd)
    z = lambda r, n: jnp.zeros((r, MLA_HEADS, n), F32)
    nope, rope = wq[:, :, :MLA_NOPE], wq[:, :, MLA_NOPE:]
    p["wqa"] = jnp.concatenate([nope, rope, z(MLA_Q_RANK, 32)], -1).reshape(MLA_Q_RANK, -1).astype(BF16)
    p["wqb"] = jnp.concatenate([z(MLA_Q_RANK, MLA_NOPE), rope[:, :, half:], rope[:, :, :half],
                                z(MLA_Q_RANK, 32)], -1).reshape(MLA_Q_RANK, -1).astype(BF16)
    wkv = mla_w_ukv[l].reshape(MLA_KV_RANK, MLA_HEADS, MLA_NOPE + MLA_V)
    knope, vproj = wkv[:, :, :MLA_NOPE], wkv[:, :, MLA_NOPE:]
    eye = jnp.eye(MLA_ROPE, dtype=F32)
    swap = jnp.concatenate([eye[:, half:], eye[:, :half]], axis=1)
    place = lambda m: jnp.broadcast_to(
        jnp.concatenate([jnp.zeros((MLA_ROPE, MLA_NOPE), F32), m, jnp.zeros((MLA_ROPE, 32), F32)], -1)[:, None, :],
        (MLA_ROPE, MLA_HEADS, LANE))
    pad_rows = 256 - MLA_KV_RANK - MLA_ROPE
    p["wka"] = jnp.concatenate([jnp.concatenate([knope, z(MLA_KV_RANK, 64)], -1), place(eye),
                                z(pad_rows, LANE)], 0).reshape(256, -1).astype(BF16)
    p["wkb"] = jnp.concatenate([z(MLA_KV_RANK, LANE), place(swap), z(pad_rows, LANE)], 0
                               ).reshape(256, -1).astype(BF16)
    p["wv"] = jnp.concatenate([vproj.reshape(MLA_KV_RANK, -1),
                               jnp.zeros((256 - MLA_KV_RANK, MLA_HEADS * MLA_V), F32)], 0).astype(BF16)
    p["qng"] = mla_q_norm[l][None, :]
    p["kvg"] = jnp.concatenate([mla_kv_norm[l], jnp.ones((256 - MLA_KV_RANK,), F32)])[None, :]

    def rope_gains(g):
        base = jnp.concatenate([g, jnp.zeros((LANE - hd,), F32)])
        part = jnp.concatenate([jnp.zeros((MLA_NOPE,), F32), g[MLA_NOPE + half:], g[MLA_NOPE:MLA_NOPE + half],
                                jnp.zeros((LANE - hd,), F32)])
        return base[None, :], part[None, :]

    p["gq"], p["gqs"] = rope_gains(mla_qk_norm[l, 0])
    p["gk"], p["gks"] = rope_gains(mla_qk_norm[l, 1])
    p["dgq"] = jnp.tile(diff_qk_norm[l, 0], 8)[None, :]
    p["dgk"] = jnp.tile(diff_qk_norm[l, 1], 8)[None, :]
    p["sgq"] = jnp.tile(swa_qk_norm[l, 0], 8)[None, :]
    p["sgk"] = jnp.tile(swa_qk_norm[l, 1], 2)[None, :]
    p["sinks"] = jnp.repeat(swa_sinks[l].astype(F32) * LOG2E, SWA_WINDOW)[None, :]
    return p


def _rope_tables(positions):
    inv_freq = ROPE_BASE ** (-jnp.arange(0, MLA_ROPE, 2, dtype=F32) / MLA_ROPE)
    ang = positions.astype(F32).reshape(-1)[:, None] * inv_freq
    cos, sin = jnp.cos(ang), jnp.sin(ang)
    t = ang.shape[0]
    cosf = jnp.concatenate([jnp.ones((t, MLA_NOPE), F32), cos, cos, jnp.zeros((t, 32), F32)], axis=1)
    sinf = jnp.concatenate([jnp.zeros((t, MLA_NOPE), F32), -sin, sin, jnp.zeros((t, 32), F32)], axis=1)
    return cosf, sinf


def kernel(x, c, positions, ada_w, ada_b, norm_mix, norm_ffn, w_in, hg_lb_logits, hg_onorm, mla_q_norm, mla_kv_norm, mla_w_uq, mla_w_ukv, mla_qk_norm, diff_qk_norm, diff_lam, diff_onorm, swa_qk_norm, swa_sinks, w_branch, w_out, router_w, router_bias, moe_w_gate, moe_w_up, moe_w_down):
    batch, seq, d = x.shape
    x2 = x.reshape(batch * seq, d)
    cosf, sinf = _rope_tables(positions)
    lb_all = jnp.cumsum(jax.nn.softmax(hg_lb_logits.astype(F32), axis=0), axis=0)
    lb_all = lb_all - lb_all[0]
    mod = _modulation(c, ada_w, ada_b)
    rw = jnp.concatenate([router_w, jnp.zeros((d, LANE - N_EXPERTS), F32)], axis=1)
    rb = router_bias.astype(F32)[:, None]

    for l in range(DEPTH):
        sh1, sc1, gt1, sh2, sc2, gt2 = [mod[l, :, d * k:d * (k + 1)][:, None, :] for k in range(6)]
        p = _layer_params(l, w_in, hg_onorm, mla_q_norm, mla_kv_norm, mla_w_uq, mla_w_ukv, mla_qk_norm,
                          diff_qk_norm, swa_qk_norm, swa_sinks, lb_all)
        hg3, hf, mla, diff, swa, h = _inproj(x2, norm_mix[l][None, :], sh1, sc1, p["w1"], seq)
        y_a = _hgrn(hg3, hf, p["loglb"], p["log1mlb"], p["ogain"], batch, seq)
        qm, km, vmt, qd, kd, vdt, qs, ks, vst = _prep(mla, diff, swa, cosf, sinf, p)
        y_b = _mla_attn(qm, km, vmt, batch, seq)
        lam_init = 0.8 - 0.6 * math.exp(-0.3 * l)
        y_c = _diff_attn(qd, kd, vdt, diff_lam[l], diff_onorm[l][None, :], lam_init, batch, seq)
        y_d = _swa(qs, ks, vst, p["sinks"], batch, seq)
        x2, h2, comb = _merge(h, (y_a, y_b, y_c, y_d), x2, gt1, p["wg"], w_branch[l].astype(BF16),
                              w_out[l].astype(BF16), norm_ffn[l][None, :], sh2, sc2, rw, rb, seq)
        x2 = _moe(h2, comb, x2, gt2, moe_w_gate[l].astype(BF16), moe_w_up[l].astype(BF16),
                  moe_w_down[l].astype(BF16), seq)
    return x2.reshape(batch, seq, d)
```

```python
import functools
import math

import jax
import jax.numpy as jnp
from jax import lax
from jax.experimental import pallas as pl
from jax.experimental.pallas import tpu as pltpu

F32 = jnp.float32
BF16 = jnp.bfloat16

D_MODEL = 1024
DEPTH = 2
EPS = 1e-6
N_BRANCH = 4
HG_HEADS = 8
HG_DK = 64
HG_W = HG_HEADS * HG_DK
HG_SUB = 16
HG_CHUNK = 64
HG_SAFE_DECAY = 80.0
HG_DIAG_ROWS =(HG_SUB // 2) * HG_SUB + (HG_SUB // 2) ** 2
MLA_HEADS = 8
MLA_Q_RANK = 256
MLA_KV_RANK = 128
MLA_NOPE = 64
MLA_ROPE = 32
MLA_V = 64
ROPE_BASE = 10000.0
DIFF_HEADS = 4
DIFF_QK = 64
DIFF_V = 128
SWA_Q_HEADS = 8
SWA_KV_HEADS = 2
SWA_WINDOW = 128
HEAD_DIM = 64
N_EXPERTS = 16
N_GROUPS = 4
D_FF_EXPERT = 256
IN_SPLITS = (512, 512, 512, 512, 256, 128, 32, 512, 512, 512, 512, 128, 128, 4096)

LANE = 128
ATT_BLK = 256
SWA_QB = 4
LOG2E = 1.4426950408889634
NEG = -1e30
VMEM_LIMIT = 56 * 1024 * 1024


def _cp(sem, vmem=VMEM_LIMIT):
    return pltpu.CompilerParams(dimension_semantics=sem, vmem_limit_bytes=vmem)


def _nt(a, b):
    return lax.dot_general(a, b, (((1,), (1,)), ((), ())), preferred_element_type=F32)


def _tn(a, b):
    return lax.dot_general(a, b, (((0,), (0,)), ((), ())), preferred_element_type=F32)


def _dot(a, b):
    return jnp.dot(a, b, preferred_element_type=F32)


def _split2(x):
    hi = x.astype(BF16)
    lo = (x - hi.astype(F32)).astype(BF16)
    return hi, lo


def _split3(x):
    hi = x.astype(BF16)
    r = x - hi.astype(F32)
    mid = r.astype(BF16)
    lo = (r - mid.astype(F32)).astype(BF16)
    return hi, mid, lo


def _seg_id(idx, seg):
    shift = seg.bit_length() - 1
    assert 1 << shift == seg
    return lax.shift_right_logical(idx, shift)


def _same_seg(n, seg):
    r = lax.broadcasted_iota(jnp.int32, (n, n), 0)
    c = lax.broadcasted_iota(jnp.int32, (n, n), 1)
    return _seg_id(r, seg) == _seg_id(c, seg)


def _seg_ones(n, seg):
    return _same_seg(n, seg).astype(BF16)


def _seg_mean_sq(x, seg):
    n = x.shape[-1]
    hi, lo = _split2(x * x)
    ones = _seg_ones(n, seg)
    return (_dot(hi, ones) + _dot(lo, ones)) * (1.0 / seg)


def _silu(x):
    return x * jax.nn.sigmoid(x)


def _mod_kernel(c_ref, w_ref, b_ref, o_ref):
    c = c_ref[...]
    o_ref[0] = jnp.dot(_silu(c), w_ref[0], preferred_element_type=F32,
                       precision=lax.Precision.HIGHEST) + b_ref[0]


def _modulation(c, ada_w, ada_b):
    nl, d, n6 = ada_w.shape
    b = c.shape[0]
    tn = 1536
    return pl.pallas_call(
        _mod_kernel,
        grid=(nl, n6 // tn),
        in_specs=[pl.BlockSpec((b, d), lambda l, j: (0, 0)),
                  pl.BlockSpec((1, d, tn), lambda l, j: (l, 0, j)),
                  pl.BlockSpec((1, 1, tn), lambda l, j: (l, 0, j))],
        out_specs=pl.BlockSpec((1, b, tn), lambda l, j: (l, 0, j)),
        out_shape=jax.ShapeDtypeStruct((nl, b, n6), F32),
        compiler_params=_cp(("arbitrary", "arbitrary")),
        name="modulation",
    )(c, ada_w, ada_b.reshape(nl, 1, n6))


W1_COLS = 4864


def _inproj_kernel(x_ref, g_ref, sh_ref, sc_ref, w_ref,
                   ohg_ref, ohf_ref, omla_ref, odiff_ref, oswa_ref, oh_ref):
    x = x_ref[...]
    ms = jnp.mean(x * x, axis=-1, keepdims=True)
    h = x * lax.rsqrt(ms + EPS) * g_ref[...]
    h = h * (1.0 + sc_ref[0]) + sh_ref[0]
    hb = h.astype(BF16)
    oh_ref[...] = hb

    def proj(lo, hi):
        return _dot(hb, w_ref[:, lo:hi])

    for k in range(3):
        ohg_ref[:, 512 * k:512 * (k + 1)] = proj(512 * k, 512 * (k + 1)).astype(BF16)
    ohf_ref[...] = proj(1536, 2048)
    omla_ref[...] = proj(2048, 2560).astype(BF16)
    for k in range(3):
        odiff_ref[:, 512 * k:512 * (k + 1)] = proj(2560 + 512 * k, 3072 + 512 * k).astype(BF16)
    oswa_ref[:, 0:512] = proj(4096, 4608).astype(BF16)
    oswa_ref[:, 512:768] = proj(4608, 4864).astype(BF16)


def _inproj(x2, gain, sh, sc, w1, seq):
    t, d = x2.shape
    tm = 512
    tpb = seq // tm
    row = lambda i: (i, 0)
    per_b = lambda i: (i // tpb, 0, 0)
    outs = [(1536, BF16), (512, F32), (512, BF16), (1536, BF16), (768, BF16), (d, BF16)]
    return pl.pallas_call(
        _inproj_kernel,
        grid=(t // tm,),
        in_specs=[pl.BlockSpec((tm, d), row),
                  pl.BlockSpec((1, d), lambda i: (0, 0)),
                  pl.BlockSpec((1, 1, d), per_b),
                  pl.BlockSpec((1, 1, d), per_b),
                  pl.BlockSpec((d, W1_COLS), lambda i: (0, 0))],
        out_specs=[pl.BlockSpec((tm, w), row) for w, _ in outs],
        out_shape=[jax.ShapeDtypeStruct((t, w), dt) for w, dt in outs],
        compiler_params=_cp(("arbitrary",)),
        name="inproj",
    )(x2, gain, sh, sc, w1)


def _segment_cumsum(x, seg):
    n = x.shape[0]
    r = lax.broadcasted_iota(jnp.int32, (n, n), 0)
    cc = lax.broadcasted_iota(jnp.int32, (n, n), 1)
    same = _same_seg(n, seg)
    tri = (same & (cc <= r)).astype(BF16)
    blk = same.astype(BF16)
    parts = _split3(x)
    c = _dot(tri, parts[0]) + _dot(tri, parts[1]) + _dot(tri, parts[2])
    tot = _dot(blk, parts[0]) + _dot(blk, parts[1]) + _dot(blk, parts[2])
    return c, tot


def _hgrn_chunk_path(i_ref, st_ref, c_s, tot_s, qs_s, kk_s, qe_s, kd_s, ke_s, dec_s, od_s):
    rows_blk = c_s.shape[0]
    tot = tot_s[...]
    rel = c_s[...] - 0.5 * tot
    half_dec = jnp.exp(0.5 * tot)
    kd = kk_s[...] * jnp.exp(-rel)
    qe_s[...] = (qs_s[...] * jnp.exp(rel)).astype(BF16)
    kd_s[...] = kd.astype(BF16)
    ke_s[...] = (kd * half_dec).astype(BF16)
    dec_s[...] = half_dec

    row = lax.broadcasted_iota(jnp.int32, (2 * rows_blk, rows_blk), 0) & (rows_blk - 1)
    col = lax.broadcasted_iota(jnp.int32, (2 * rows_blk, rows_blk), 1)
    intra = (_seg_id(row, HG_CHUNK) == _seg_id(col, HG_CHUNK)) & (col <= row)
    low_q = lax.broadcasted_iota(jnp.int32, (rows_blk, LANE), 1) < HG_DK
    low_c = lax.broadcasted_iota(jnp.int32, (HG_CHUNK, LANE), 1) < HG_DK

    for j in range(HG_W // LANE):
        cols = slice(LANE * j, LANE * (j + 1))
        qe = qe_s[:, cols]
        zero = jnp.zeros_like(qe)
        q2 = jnp.concatenate([jnp.where(low_q, qe, zero), jnp.where(low_q, zero, qe)], axis=0)
        attn = jnp.where(intra, _nt(q2, kd_s[:, cols]), 0.0).astype(BF16)
        o2 = _dot(attn, i_ref[:, cols])
        for ch in range(rows_blk // HG_CHUNK):
            rows = slice(HG_CHUNK * ch, HG_CHUNK * (ch + 1))
            rows_hi = slice(rows_blk + HG_CHUNK * ch, rows_blk + HG_CHUNK * (ch + 1))
            st = st_ref[j]
            hd = dec_s[HG_CHUNK * ch:HG_CHUNK * ch + 1, cols]
            inter = _nt(jnp.concatenate([q2[rows], q2[rows_hi]], axis=0), (st * hd).astype(BF16))
            od_s[rows, cols] = jnp.where(low_c, o2[rows] + inter[:HG_CHUNK], o2[rows_hi] + inter[HG_CHUNK:])
            upd = _tn(i_ref[rows, cols], ke_s[rows, cols])
            st_ref[j] = st * (hd * hd) + upd


def _hgrn_exact_path(i_ref, st_ref, lf_s, c_s, qs_s, kk_s, qe_s, ke_s, dec_s, od_s, t_s, a_s):
    rows_blk = c_s.shape[0]
    n_sub = rows_blk // HG_SUB
    c, tot = _segment_cumsum(lf_s[...], HG_SUB)
    c_s[...] = c
    qe_s[...] = (qs_s[...] * jnp.exp(c)).astype(BF16)
    ke_s[...] = (kk_s[...] * jnp.exp(tot - c)).astype(BF16)
    dec_s[...] = jnp.exp(tot)

    same_head = _same_seg(LANE, HG_DK)
    head_mask = same_head.astype(F32)
    head_ones = same_head.astype(BF16)
    for j in range(HG_W // LANE):
        st_ref[j] = st_ref[j] * head_mask
    half = HG_SUB // 2
    trow = lax.broadcasted_iota(jnp.int32, (half, HG_W), 0)

    def body(i, carry):
        r0 = pl.multiple_of(i * HG_SUB, HG_SUB)
        rows = pl.ds(r0, HG_SUB)
        c_i = c_s[rows, :]
        qs_i = qs_s[rows, :]
        kk_i = kk_s[rows, :]
        v_i = i_ref[rows, :].astype(F32)
        c_lo, c_hi = c_i[:half], c_i[half:]
        q_lo, q_hi = qs_i[:half], qs_i[half:]
        for s in range(half):
            c_row, k_row = c_i[s:s + 1, :], kk_i[s:s + 1, :]
            e_lo = jnp.exp(jnp.where(trow >= s, c_lo - c_row, NEG))
            e_hi = jnp.exp(c_hi - c_row)
            both = jnp.concatenate([e_lo * q_lo, e_hi * q_hi], axis=0) * k_row
            t_s[s * HG_SUB:(s + 1) * HG_SUB, :] = both.astype(BF16)
        for s in range(half, HG_SUB, 2):
            pair = []
            for u in (s, s + 1):
                e_hi = jnp.exp(jnp.where(trow >= u - half, c_hi - c_i[u:u + 1, :], NEG))
                pair.append(e_hi * q_hi * kk_i[u:u + 1, :])
            base = half * HG_SUB + (s - half) * half
            t_s[base:base + HG_SUB, :] = jnp.concatenate(pair, axis=0).astype(BF16)
        for j in range(HG_W // LANE):
            cols = slice(LANE * j, LANE * (j + 1))
            a_s[:, cols] = _dot(t_s[:, cols], head_ones)
        acc_lo = jnp.zeros((half, HG_W), F32)
        acc_hi = jnp.zeros((half, HG_W), F32)
        for s in range(half):
            acc_lo = acc_lo + a_s[s * HG_SUB:s * HG_SUB + half, :] * v_i[s:s + 1, :]
            acc_hi = acc_hi + a_s[s * HG_SUB + half:(s + 1) * HG_SUB, :] * v_i[s:s + 1, :]
        for s in range(half, HG_SUB):
            base = half * HG_SUB + (s - half) * half
            acc_hi = acc_hi + a_s[base:base + half, :] * v_i[s:s + 1, :]
        acc = jnp.concatenate([acc_lo, acc_hi], axis=0)
        for j in range(HG_W // LANE):
            cols = slice(LANE * j, LANE * (j + 1))
            st = st_ref[j]
            o_int = _nt(qe_s[rows, cols], st.astype(BF16))
            upd = _tn(i_ref[rows, cols], ke_s[rows, cols])
            st_ref[j] = st * dec_s[pl.ds(r0, 1), cols] + upd * head_mask
            od_s[rows, cols] = acc[:, cols] + o_int
        return carry

    lax.fori_loop(0, n_sub, body, 0)


def _hgrn_kernel(q_ref, i_ref, g_ref, f_ref, loglb_ref, log1mlb_ref, og_ref, o_ref,
                 st_ref, lf_s, c_s, tot_s, qs_s, kk_s, qe_s, kd_s, ke_s, dec_s, od_s, t_s, a_s):
    @pl.when(pl.program_id(1) == 0)
    def _():
        st_ref[...] = jnp.zeros_like(st_ref)

    fr = f_ref[...]
    ls = jnp.minimum(fr, 0.0) - jnp.log(1.0 + jnp.exp(-jnp.abs(fr)))
    a = loglb_ref[...]
    c2 = log1mlb_ref[...] + ls
    lf = jnp.maximum(a, c2) + jnp.log(1.0 + jnp.exp(-jnp.abs(a - c2)))
    lf_s[...] = lf
    qs_s[...] = _silu(q_ref[...].astype(F32))
    kk_s[...] = 1.0 - jnp.exp(lf)
    c, tot = _segment_cumsum(lf, HG_CHUNK)
    c_s[...] = c
    tot_s[...] = tot
    safe = 0.5 * jnp.max(-tot) <= HG_SAFE_DECAY

    @pl.when(safe)
    def _():
        _hgrn_chunk_path(i_ref, st_ref, c_s, tot_s, qs_s, kk_s, qe_s, kd_s, ke_s, dec_s, od_s)

    @pl.when(jnp.logical_not(safe))
    def _():
        _hgrn_exact_path(i_ref, st_ref, lf_s, c_s, qs_s, kk_s, qe_s, ke_s, dec_s, od_s, t_s, a_s)

    o = od_s[...]
    ms = _seg_mean_sq(o, HG_DK)
    on = o * lax.rsqrt(ms + EPS) * og_ref[...]
    o_ref[...] = (on * _silu(g_ref[...].astype(F32))).astype(BF16)


def _hgrn(hg3, hf, loglb, log1mlb, ogain, batch, seq):
    t = hf.shape[0]
    rb = 256
    nb = seq // rb
    blk = lambda k: pl.BlockSpec((rb, HG_W), lambda b, n, k=k: (b * nb + n, k))
    vec = pl.BlockSpec((1, HG_W), lambda b, n: (0, 0))
    f32_blk = pltpu.VMEM((rb, HG_W), F32)
    bf16_blk = pltpu.VMEM((rb, HG_W), BF16)
    return pl.pallas_call(
        _hgrn_kernel,
        grid=(batch, nb),
        in_specs=[blk(0), blk(1), blk(2), blk(0), vec, vec, vec],
        out_specs=blk(0),
        out_shape=jax.ShapeDtypeStruct((t, HG_W), BF16),
        scratch_shapes=[pltpu.VMEM((HG_W // LANE, LANE, LANE), F32),
                        f32_blk, f32_blk, f32_blk, f32_blk, f32_blk,
                        bf16_blk, bf16_blk, bf16_blk,
                        f32_blk, f32_blk,
                        pltpu.VMEM((HG_DIAG_ROWS, HG_W), BF16),
                        pltpu.VMEM((HG_DIAG_ROWS, HG_W), F32)],
        compiler_params=_cp(("arbitrary", "arbitrary")),
        name="hgrn2",
    )(hg3, hg3, hg3, hf, loglb, log1mlb, ogain)


def _store_transposed_blocks(out_ref, v):
    blk = out_ref.shape[2]
    for u in range(out_ref.shape[0]):
        out_ref[u] = v[u * blk:(u + 1) * blk, :].T.astype(BF16)


def _prep_kernel(mla_ref, diff_ref, swa_ref, cos_ref, sin_ref,
                 qng_ref, kvg_ref, wqa_ref, wqb_ref, wka_ref, wkb_ref, wv_ref,
                 gq_ref, gqs_ref, gk_ref, gks_ref, dgq_ref, dgk_ref, sgq_ref, sgk_ref,
                 qm_ref, km_ref, vmt_ref, qd_ref, kd_ref, vdt_ref, qs_ref, ks_ref, vst_ref):
    blk = mla_ref[...].astype(F32)
    cq = blk[:, :MLA_Q_RANK]
    rest = blk[:, MLA_Q_RANK:]
    cqn = cq * lax.rsqrt(jnp.mean(cq * cq, axis=-1, keepdims=True) + EPS) * qng_ref[...]
    lane = lax.broadcasted_iota(jnp.int32, rest.shape, 1)
    is_kv = lane < MLA_KV_RANK
    ms_kv = jnp.sum(jnp.where(is_kv, rest * rest, 0.0), axis=-1, keepdims=True) * (1.0 / MLA_KV_RANK)
    restn = jnp.where(is_kv, rest * lax.rsqrt(ms_kv + EPS) * kvg_ref[...], rest)
    cqb = cqn.astype(BF16)
    rb = restn.astype(BF16)
    qa = _dot(cqb, wqa_ref[...])
    qb = _dot(cqb, wqb_ref[...])
    ka = _dot(rb, wka_ref[...])
    kb = _dot(rb, wkb_ref[...])
    _store_transposed_blocks(vmt_ref, _dot(rb, wv_ref[...]))
    cosf = cos_ref[...]
    sinf = sin_ref[...]
    cq_t = cosf * gq_ref[...]
    sq_t = sinf * gqs_ref[...]
    ck_t = cosf * gk_ref[...]
    sk_t = sinf * gks_ref[...]
    inv_n = 1.0 / (MLA_NOPE + MLA_ROPE)
    scale = (MLA_NOPE + MLA_ROPE) ** -0.5 * LOG2E
    for h in range(MLA_HEADS):
        cols = slice(LANE * h, LANE * (h + 1))
        x = qa[:, cols]
        rinv = lax.rsqrt(jnp.sum(x * x, axis=-1, keepdims=True) * inv_n + EPS)
        qm_ref[:, cols] = ((x * cq_t + qb[:, cols] * sq_t) * (rinv * scale)).astype(BF16)
        y = ka[:, cols]
        rinv = lax.rsqrt(jnp.sum(y * y, axis=-1, keepdims=True) * inv_n + EPS)
        km_ref[:, cols] = ((y * ck_t + kb[:, cols] * sk_t) * rinv).astype(BF16)

    def seg_norm(x, gain, scale):
        return x * lax.rsqrt(_seg_mean_sq(x, HEAD_DIM) + EPS) * (gain * scale)

    dq = diff_ref[:, 0:512].astype(F32)
    dk = diff_ref[:, 512:1024].astype(F32)
    qd_ref[...] = seg_norm(dq, dgq_ref[...], DIFF_QK ** -0.5 * LOG2E).astype(BF16)
    kd_ref[...] = seg_norm(dk, dgk_ref[...], 1.0).astype(BF16)
    _store_transposed_blocks(vdt_ref, diff_ref[:, 1024:1536].astype(F32))

    sq = swa_ref[:, 0:512].astype(F32)
    qs_ref[...] = seg_norm(sq, sgq_ref[...], HEAD_DIM ** -0.5 * LOG2E).astype(BF16)
    skv = swa_ref[:, 512:768].astype(F32)
    kn = seg_norm(skv[:, :LANE], sgk_ref[...], 1.0)
    low = lax.broadcasted_iota(jnp.int32, kn.shape, 1) < HEAD_DIM
    sw = pltpu.roll(kn, HEAD_DIM, 1)
    ks_ref[:, :LANE] = jnp.where(low, kn, sw).astype(BF16)
    ks_ref[:, LANE:] = jnp.where(low, sw, kn).astype(BF16)
    _store_transposed_blocks(vst_ref, skv[:, LANE:])


def _prep(mla, diff, swa, cosf, sinf, p):
    t = mla.shape[0]
    tm = 512
    row = lambda i: (i, 0)
    full = lambda a: pl.BlockSpec(a.shape, lambda i: (0,) * a.ndim)
    consts = [p["qng"], p["kvg"], p["wqa"], p["wqb"], p["wka"], p["wkb"], p["wv"],
              p["gq"], p["gqs"], p["gk"], p["gks"], p["dgq"], p["dgk"], p["sgq"], p["sgk"]]
    def rows_out(w):
        return pl.BlockSpec((tm, w), row), jax.ShapeDtypeStruct((t, w), BF16)

    def transposed_out(n, blk):
        return (pl.BlockSpec((tm // blk, n, blk), lambda i: (i, 0, 0)),
                jax.ShapeDtypeStruct((t // blk, n, blk), BF16))

    outs = [rows_out(1024), rows_out(1024), transposed_out(512, ATT_BLK),
            rows_out(512), rows_out(512), transposed_out(512, ATT_BLK),
            rows_out(512), rows_out(256), transposed_out(LANE, SWA_WINDOW)]
    return pl.pallas_call(
        _prep_kernel,
        grid=(t // tm,),
        in_specs=[pl.BlockSpec((tm, 512), row),
                  pl.BlockSpec((tm, 1536), row),
                  pl.BlockSpec((tm, 768), row),
                  pl.BlockSpec((tm, LANE), row),
                  pl.BlockSpec((tm, LANE), row)] + [full(a) for a in consts],
        out_specs=[o[0] for o in outs],
        out_shape=[o[1] for o in outs],
        compiler_params=_cp(("arbitrary",)),
        name="attn_prep",
    )(mla, diff, swa, cosf, sinf, *consts)


def _causal_t(blk):
    key = lax.broadcasted_iota(jnp.int32, (blk, blk), 0)
    qry = lax.broadcasted_iota(jnp.int32, (blk, blk), 1)
    return key <= qry


def _two_pass_attention(n_sets, score_fn, value_fn, s_scr, acc_scr, blk):
    qi = pl.program_id(1)
    causal = _causal_t(blk)

    def scores(ki, m, masked):
        out = []
        for i in range(n_sets):
            s = score_fn(i, ki)
            if masked:
                s = jnp.where(causal, s, NEG)
            s_scr[i, ki] = s
            out.append(jnp.maximum(m[i], jnp.max(s, axis=0, keepdims=True)))
        return tuple(out)

    m = tuple(jnp.full((1, blk), NEG, F32) for _ in range(n_sets))
    m = lax.fori_loop(0, qi, lambda ki, c: scores(ki, c, False), m)
    m = scores(qi, m, True)

    acc_scr[...] = jnp.zeros_like(acc_scr)

    def accumulate(ki, l):
        out = []
        for i in range(n_sets):
            p = jnp.exp2(s_scr[i, ki] - m[i])
            out.append(l[i] + jnp.sum(p, axis=0, keepdims=True))
            acc_scr[i] += _dot(value_fn(i, ki), p.astype(BF16))
        return tuple(out)

    zero = tuple(jnp.zeros((1, blk), F32) for _ in range(n_sets))
    return lax.fori_loop(0, qi + 1, accumulate, zero)


def _mla_attn_kernel(q_ref, k_ref, vt_ref, o_ref, s_scr, acc_scr):
    blk = q_ref.shape[0]

    def score_fn(h, ki):
        rows = pl.ds(pl.multiple_of(ki * blk, blk), blk)
        cols = slice(LANE * h, LANE * (h + 1))
        return _nt(k_ref[rows, cols], q_ref[:, cols])

    def value_fn(h, ki):
        return vt_ref[ki, MLA_V * h:MLA_V * (h + 1), :]

    l = _two_pass_attention(MLA_HEADS, score_fn, value_fn, s_scr, acc_scr, blk)
    for j in range(MLA_HEADS // 2):
        o_t = jnp.concatenate([acc_scr[2 * j] / l[2 * j], acc_scr[2 * j + 1] / l[2 * j + 1]], axis=0)
        o_ref[:, LANE * j:LANE * (j + 1)] = o_t.T.astype(BF16)


def _mla_attn(qm, km, vmt, batch, seq):
    t = qm.shape[0]
    nq = seq // ATT_BLK
    return pl.pallas_call(
        _mla_attn_kernel,
        grid=(batch, nq),
        in_specs=[pl.BlockSpec((ATT_BLK, 1024), lambda b, i: (b * nq + i, 0)),
                  pl.BlockSpec((seq, 1024), lambda b, i: (b, 0)),
                  pl.BlockSpec((nq, 512, ATT_BLK), lambda b, i: (b, 0, 0))],
        out_specs=pl.BlockSpec((ATT_BLK, 512), lambda b, i: (b * nq + i, 0)),
        out_shape=jax.ShapeDtypeStruct((t, 512), BF16),
        scratch_shapes=[pltpu.VMEM((MLA_HEADS, nq, ATT_BLK, ATT_BLK), F32),
                        pltpu.VMEM((MLA_HEADS, MLA_V, ATT_BLK), F32)],
        compiler_params=_cp(("arbitrary", "arbitrary")),
        name="mla_attn",
    )(qm, km, vmt)


def _diff_attn_kernel(q_ref, k_ref, vt_ref, lam_ref, og_ref, o_ref, s_scr, acc_scr, qm_scr, *, lam_init):
    blk = q_ref.shape[0]
    low = lax.broadcasted_iota(jnp.int32, (blk, LANE), 1) < DIFF_QK
    lp = lam_ref[...]
    lam = (jnp.exp(jnp.sum(lp[0:1] * lp[1:2], axis=-1, keepdims=True))
           - jnp.exp(jnp.sum(lp[2:3] * lp[3:4], axis=-1, keepdims=True)) + lam_init)

    for h in range(DIFF_HEADS):
        qt = q_ref[:, LANE * h:LANE * (h + 1)]
        zero = jnp.zeros_like(qt)
        qm_scr[2 * h] = jnp.where(low, qt, zero)
        qm_scr[2 * h + 1] = jnp.where(low, zero, qt)

    def score_fn(i, ki):
        rows = pl.ds(pl.multiple_of(ki * blk, blk), blk)
        h = i // 2
        return _nt(k_ref[rows, LANE * h:LANE * (h + 1)], qm_scr[i])

    def value_fn(i, ki):
        h = i // 2
        return vt_ref[ki, DIFF_V * h:DIFF_V * (h + 1), :]

    l = _two_pass_attention(2 * DIFF_HEADS, score_fn, value_fn, s_scr, acc_scr, blk)
    for h in range(DIFF_HEADS):
        o_t = acc_scr[2 * h] / l[2 * h] - lam * (acc_scr[2 * h + 1] / l[2 * h + 1])
        on_t = o_t * lax.rsqrt(jnp.mean(o_t * o_t, axis=0, keepdims=True) + EPS)
        o_ref[:, LANE * h:LANE * (h + 1)] = (on_t.T * (og_ref[...] * (1.0 - lam_init))).astype(BF16)


def _diff_attn(qd, kd, vdt, lam_p, og, lam_init, batch, seq):
    t = qd.shape[0]
    nq = seq // ATT_BLK
    return pl.pallas_call(
        functools.partial(_diff_attn_kernel, lam_init=lam_init),
        grid=(batch, nq),
        in_specs=[pl.BlockSpec((ATT_BLK, 512), lambda b, i: (b * nq + i, 0)),
                  pl.BlockSpec((seq, 512), lambda b, i: (b, 0)),
                  pl.BlockSpec((nq, 512, ATT_BLK), lambda b, i: (b, 0, 0)),
                  pl.BlockSpec(lam_p.shape, lambda b, i: (0, 0)),
                  pl.BlockSpec(og.shape, lambda b, i: (0, 0))],
        out_specs=pl.BlockSpec((ATT_BLK, 512), lambda b, i: (b * nq + i, 0)),
        out_shape=jax.ShapeDtypeStruct((t, 512), BF16),
        scratch_shapes=[pltpu.VMEM((2 * DIFF_HEADS, nq, ATT_BLK, ATT_BLK), F32),
                        pltpu.VMEM((2 * DIFF_HEADS, DIFF_V, ATT_BLK), F32),
                        pltpu.VMEM((2 * DIFF_HEADS, ATT_BLK, LANE), BF16)],
        compiler_params=_cp(("arbitrary", "arbitrary")),
        name="diff_attn",
    )(qd, kd, vdt, lam_p, og)


def _swa_kernel(q_ref, kp_ref, kc_ref, vtp_ref, vtc_ref, sink_ref, o_ref):
    w = SWA_WINDOW
    grp = SWA_Q_HEADS // SWA_KV_HEADS
    n = pl.program_id(1)
    key = lax.broadcasted_iota(jnp.int32, (2 * w, grp * w), 0)
    qry = lax.broadcasted_iota(jnp.int32, (2 * w, grp * w), 1) & (w - 1)
    cur_ok = (key >= w) & (key - w <= qry)
    prev_ok = (key < w) & (key > qry)
    low = lax.broadcasted_iota(jnp.int32, (w, LANE), 1) < HEAD_DIM

    for t in range(q_ref.shape[0] // w):
        rows = slice(t * w, (t + 1) * w)
        if t == 0:
            kp, vtp = kp_ref[...], vtp_ref[0]
            valid = cur_ok | (prev_ok & (n > 0))
        else:
            kp, vtp = kc_ref[(t - 1) * w:t * w, :], vtc_ref[t - 1]
            valid = cur_ok | prev_ok
        kc, vtc = kc_ref[rows, :], vtc_ref[t]
        for kv in range(SWA_KV_HEADS):
            kcols = slice(LANE * kv, LANE * (kv + 1))
            vrows = slice(HEAD_DIM * kv, HEAD_DIM * (kv + 1))
            k_win = jnp.concatenate([kp[:, kcols], kc[:, kcols]], axis=0)
            parts = []
            for u in range(2):
                qt = q_ref[rows, LANE * (2 * kv + u):LANE * (2 * kv + u + 1)]
                zero = jnp.zeros_like(qt)
                parts += [jnp.where(low, qt, zero), jnp.where(low, zero, qt)]
            s = jnp.where(valid, _nt(k_win, jnp.concatenate(parts, axis=0)), NEG)
            sink = sink_ref[:, grp * w * kv:grp * w * (kv + 1)]
            m = jnp.maximum(jnp.max(s, axis=0, keepdims=True), sink)
            p = jnp.exp2(s - m)
            den = jnp.sum(p, axis=0, keepdims=True) + jnp.exp2(sink - m)
            vt_win = jnp.concatenate([vtp[vrows, :], vtc[vrows, :]], axis=1)
            o_t = _dot(vt_win, p.astype(BF16)) / den
            for u in range(2):
                pair = jnp.concatenate([o_t[:, 2 * u * w:(2 * u + 1) * w],
                                        o_t[:, (2 * u + 1) * w:(2 * u + 2) * w]], axis=0)
                o_ref[rows, LANE * (2 * kv + u):LANE * (2 * kv + u + 1)] = pair.T.astype(BF16)


def _swa(qs, ks, vst, sink_row, batch, seq):
    t = qs.shape[0]
    w = SWA_WINDOW
    nb = seq // w
    ns = nb // SWA_QB
    cur = lambda b, n: (b * ns + n, 0)
    cur3 = lambda b, n: (b * ns + n, 0, 0)
    prev = lambda b, n: (b * nb + jnp.maximum(n * SWA_QB - 1, 0), 0)
    prev3 = lambda b, n: (b * nb + jnp.maximum(n * SWA_QB - 1, 0), 0, 0)
    return pl.pallas_call(
        _swa_kernel,
        grid=(batch, ns),
        in_specs=[pl.BlockSpec((SWA_QB * w, 512), cur),
                  pl.BlockSpec((w, 256), prev), pl.BlockSpec((SWA_QB * w, 256), cur),
                  pl.BlockSpec((1, LANE, w), prev3), pl.BlockSpec((SWA_QB, LANE, w), cur3),
                  pl.BlockSpec(sink_row.shape, lambda b, n: (0, 0))],
        out_specs=pl.BlockSpec((SWA_QB * w, 512), cur),
        out_shape=jax.ShapeDtypeStruct((t, 512), BF16),
        compiler_params=_cp(("arbitrary", "arbitrary")),
        name="swa_attn",
    )(qs, ks, ks, vst, vst, sink_row)


def _merge_kernel(h_ref, ya_ref, yb_ref, yc_ref, yd_ref, x_ref, gt1_ref, wg_ref, wb_ref, wo_ref,
                  g2_ref, sh2_ref, sc2_ref, rw_ref, rb_ref, xo_ref, h2_ref, comb_ref):
    h = h_ref[...]
    d = x_ref.shape[1]
    merged = None
    for b, y_ref in enumerate((ya_ref, yb_ref, yc_ref, yd_ref)):
        gate = jax.nn.sigmoid(_dot(h, wg_ref[:, d * b:d * (b + 1)]))
        term = gate * _dot(y_ref[...], wb_ref[b])
        merged = term if merged is None else merged + term
    xn = x_ref[...] + gt1_ref[0] * _dot(merged.astype(BF16), wo_ref[...])
    xo_ref[...] = xn
    ms = jnp.mean(xn * xn, axis=-1, keepdims=True)
    h2 = xn * lax.rsqrt(ms + EPS) * g2_ref[...]
    h2 = h2 * (1.0 + sc2_ref[0]) + sh2_ref[0]
    h2_ref[...] = h2.astype(BF16)

    hh, hm, _ = _split3(h2)
    wh, wm = _split2(rw_ref[...])
    logits = _dot(hh, wh) + _dot(hm, wh) + _dot(hh, wm)
    lt = logits.T
    scores = jax.nn.sigmoid(lt[0:N_EXPERTS, :])
    sel = scores + rb_ref[...]
    per = N_EXPERTS // N_GROUPS
    srow = [sel[e:e + 1, :] for e in range(N_EXPERTS)]
    gsum = []
    for g in range(N_GROUPS):
        a, b_, c, e_ = srow[per * g:per * (g + 1)]
        gsum.append(jnp.maximum(jnp.maximum(jnp.maximum(a + b_, a + c), jnp.maximum(a + e_, b_ + c)),
                                jnp.maximum(b_ + e_, c + e_)))
    best = jnp.maximum(jnp.maximum(gsum[0], gsum[1]), jnp.maximum(gsum[2], gsum[3]))
    taken = None
    rows = []
    for g in range(N_GROUPS):
        hit = gsum[g] == best
        pick = hit if taken is None else hit & jnp.logical_not(taken)
        taken = hit if taken is None else taken | hit
        for e in range(per * g, per * (g + 1)):
            rank = jnp.zeros_like(best)
            for o in range(per * g, per * (g + 1)):
                if o == e:
                    continue
                ahead = (srow[o] > srow[e]) | ((srow[o] == srow[e]) & (o < e))
                rank = rank + ahead.astype(F32)
            rows.append(jnp.where(pick & (rank < 1.5), scores[e:e + 1, :], 0.0))
    wsum = rows[0]
    for r_ in rows[1:]:
        wsum = wsum + r_
    inv = 1.0 / wsum
    rid = lax.broadcasted_iota(jnp.int32, scores.shape, 0)
    comb_e = jnp.zeros_like(scores)
    for e, r_ in enumerate(rows):
        comb_e = jnp.where(rid == e, r_ * inv, comb_e)
    comb_t = jnp.concatenate([comb_e, jnp.zeros((LANE - N_EXPERTS, lt.shape[1]), F32)], axis=0)
    comb_ref[...] = comb_t.T


def _merge(h, ys, x2, gt1, wg, wb, wo, g2, sh2, sc2, rw, rb, seq):
    t, d = x2.shape
    tm = 512
    tpb = seq // tm
    row = lambda i: (i, 0)
    per_b = lambda i: (i // tpb, 0, 0)
    c2 = lambda i: (0, 0)
    return pl.pallas_call(
        _merge_kernel,
        grid=(t // tm,),
        in_specs=[pl.BlockSpec((tm, d), row)] + [pl.BlockSpec((tm, 512), row)] * 4
                 + [pl.BlockSpec((tm, d), row), pl.BlockSpec((1, 1, d), per_b),
                    pl.BlockSpec(wg.shape, c2), pl.BlockSpec(wb.shape, lambda i: (0, 0, 0)),
                    pl.BlockSpec(wo.shape, c2), pl.BlockSpec((1, d), c2),
                    pl.BlockSpec((1, 1, d), per_b), pl.BlockSpec((1, 1, d), per_b),
                    pl.BlockSpec(rw.shape, c2), pl.BlockSpec(rb.shape, c2)],
        out_specs=[pl.BlockSpec((tm, d), row), pl.BlockSpec((tm, d), row), pl.BlockSpec((tm, LANE), row)],
        out_shape=[jax.ShapeDtypeStruct((t, d), F32), jax.ShapeDtypeStruct((t, d), BF16),
                   jax.ShapeDtypeStruct((t, LANE), F32)],
        compiler_params=_cp(("arbitrary",)),
        name="merge_router",
    )(h, *ys, x2, gt1, wg, wb, wo, g2, sh2, sc2, rw, rb)


def _moe_kernel(h2_ref, comb_ref, x_ref, gt2_ref, wg_ref, wu_ref, wd_ref, o_ref, acc_ref):
    e = pl.program_id(1)

    @pl.when(e == 0)
    def _():
        acc_ref[...] = jnp.zeros_like(acc_ref)

    h2 = h2_ref[...]
    hid = _silu(_dot(h2, wg_ref[0, 0].astype(BF16))) * _dot(h2, wu_ref[0, 0].astype(BF16))
    comb = comb_ref[...]
    lane = lax.broadcasted_iota(jnp.int32, comb.shape, 1)
    ce = jnp.sum(jnp.where(lane == e, comb, 0.0), axis=-1, keepdims=True)
    acc_ref[...] += _dot((hid * ce).astype(BF16), wd_ref[0, 0].astype(BF16))

    @pl.when(e == pl.num_programs(1) - 1)
    def _():
        o_ref[...] = x_ref[...] + gt2_ref[0] * acc_ref[...]


def _moe(h2, comb, x2, gt2, wg, wu, wd, layer, seq):
    t, d = x2.shape
    _, ne, _, f = wg.shape
    tm = 1024
    tpb = seq // tm
    row = lambda i, e: (i, 0)
    return pl.pallas_call(
        _moe_kernel,
        grid=(t // tm, ne),
        in_specs=[pl.BlockSpec((tm, d), row), pl.BlockSpec((tm, LANE), row), pl.BlockSpec((tm, d), row),
                  pl.BlockSpec((1, 1, d), lambda i, e: (i // tpb, 0, 0)),
                  pl.BlockSpec((1, 1, d, f), lambda i, e: (layer, e, 0, 0)),
                  pl.BlockSpec((1, 1, d, f), lambda i, e: (layer, e, 0, 0)),
                  pl.BlockSpec((1, 1, f, d), lambda i, e: (layer, e, 0, 0))],
        out_specs=pl.BlockSpec((tm, d), row),
        out_shape=jax.ShapeDtypeStruct((t, d), F32),
        scratch_shapes=[pltpu.VMEM((tm, d), F32)],
        compiler_params=_cp(("arbitrary", "arbitrary")),
        name="moe",
    )(h2, comb, x2, gt2, wg, wu, wd)


def _split_cols(w):
    out, o = [], 0
    for n in IN_SPLITS:
        out.append(w[:, o:o + n])
        o += n
    return out


def _layer_params(l, w_in, hg_onorm, mla_q_norm, mla_kv_norm, mla_w_uq, mla_w_ukv, mla_qk_norm,
                  diff_qk_norm, swa_qk_norm, swa_sinks, lb_all):
    (hq, hf, hi, hgate, mcq, mckv, mkr, dq, dk, dv, sq, sk, sv, gates) = _split_cols(w_in[l])
    d = w_in.shape[1]
    w1 = jnp.concatenate([hq, hi, hgate, hf, mcq, mckv, mkr, jnp.zeros((d, 96), F32),
                          dq, dk, dv, sq, sk, sv], axis=1).astype(BF16)
    p = {"w1": w1, "wg": gates.astype(BF16)}

    lb = lb_all[l]
    p["loglb"] = jnp.log(lb)[None, :]
    p["log1mlb"] = jnp.log1p(-lb)[None, :]
    p["ogain"] = jnp.tile(hg_onorm[l], HG_HEADS)[None, :]

    hd = MLA_NOPE + MLA_ROPE
    half = MLA_ROPE // 2
    wq = mla_w_uq[l].reshape(MLA_Q_RANK, MLA_HEADS, hd)
    z = lambda r, n: jnp.zeros((r, MLA_HEADS, n), F32)
    nope, rope = wq[:, :, :MLA_NOPE], wq[:, :, MLA_NOPE:]
    p["wqa"] = jnp.concatenate([nope, rope, z(MLA_Q_RANK, 32)], -1).reshape(MLA_Q_RANK, -1).astype(BF16)
    p["wqb"] = jnp.concatenate([z(MLA_Q_RANK, MLA_NOPE), rope[:, :, half:], rope[:, :, :half],
                                z(MLA_Q_RANK, 32)], -1).reshape(MLA_Q_RANK, -1).astype(BF16)
    wkv = mla_w_ukv[l].reshape(MLA_KV_RANK, MLA_HEADS, MLA_NOPE + MLA_V)
    knope, vproj = wkv[:, :, :MLA_NOPE], wkv[:, :, MLA_NOPE:]
    eye = jnp.eye(MLA_ROPE, dtype=F32)
    swap = jnp.concatenate([eye[:, half:], eye[:, :half]], axis=1)
    place = lambda m: jnp.broadcast_to(
        jnp.concatenate([jnp.zeros((MLA_ROPE, MLA_NOPE), F32), m, jnp.zeros((MLA_ROPE, 32), F32)], -1)[:, None, :],
        (MLA_ROPE, MLA_HEADS, LANE))
    pad_rows = 256 - MLA_KV_RANK - MLA_ROPE
    p["wka"] = jnp.concatenate([jnp.concatenate([knope, z(MLA_KV_RANK, 64)], -1), place(eye),
                                z(pad_rows, LANE)], 0).reshape(256, -1).astype(BF16)
    p["wkb"] = jnp.concatenate([z(MLA_KV_RANK, LANE), place(swap), z(pad_rows, LANE)], 0
                               ).reshape(256, -1).astype(BF16)
    p["wv"] = jnp.concatenate([vproj.reshape(MLA_KV_RANK, -1),
                               jnp.zeros((256 - MLA_KV_RANK, MLA_HEADS * MLA_V), F32)], 0).astype(BF16)
    p["qng"] = mla_q_norm[l][None, :]
    p["kvg"] = jnp.concatenate([mla_kv_norm[l], jnp.ones((256 - MLA_KV_RANK,), F32)])[None, :]

    def rope_gains(g):
        base = jnp.concatenate([g, jnp.zeros((LANE - hd,), F32)])
        part = jnp.concatenate([jnp.zeros((MLA_NOPE,), F32), g[MLA_NOPE + half:], g[MLA_NOPE:MLA_NOPE + half],
                                jnp.zeros((LANE - hd,), F32)])
        return base[None, :], part[None, :]

    p["gq"], p["gqs"] = rope_gains(mla_qk_norm[l, 0])
    p["gk"], p["gks"] = rope_gains(mla_qk_norm[l, 1])
    p["dgq"] = jnp.tile(diff_qk_norm[l, 0], 8)[None, :]
    p["dgk"] = jnp.tile(diff_qk_norm[l, 1], 8)[None, :]
    p["sgq"] = jnp.tile(swa_qk_norm[l, 0], 8)[None, :]
    p["sgk"] = jnp.tile(swa_qk_norm[l, 1], 2)[None, :]
    p["sinks"] = jnp.repeat(swa_sinks[l].astype(F32) * LOG2E, SWA_WINDOW)[None, :]
    return p


def _rope_tables(positions):
    inv_freq = ROPE_BASE ** (-jnp.arange(0, MLA_ROPE, 2, dtype=F32) / MLA_ROPE)
    ang = positions.astype(F32).reshape(-1)[:, None] * inv_freq
    cos, sin = jnp.cos(ang), jnp.sin(ang)
    t = ang.shape[0]
    cosf = jnp.concatenate([jnp.ones((t, MLA_NOPE), F32), cos, cos, jnp.zeros((t, 32), F32)], axis=1)
    sinf = jnp.concatenate([jnp.zeros((t, MLA_NOPE), F32), -sin, sin, jnp.zeros((t, 32), F32)], axis=1)
    return cosf, sinf


def kernel(x, c, positions, ada_w, ada_b, norm_mix, norm_ffn, w_in, hg_lb_logits, hg_onorm, mla_q_norm, mla_kv_norm, mla_w_uq, mla_w_ukv, mla_qk_norm, diff_qk_norm, diff_lam, diff_onorm, swa_qk_norm, swa_sinks, w_branch, w_out, router_w, router_bias, moe_w_gate, moe_w_up, moe_w_down):
    batch, seq, d = x.shape
    x2 = x.reshape(batch * seq, d)
    cosf, sinf = _rope_tables(positions)
    lb_all = jnp.cumsum(jax.nn.softmax(hg_lb_logits.astype(F32), axis=0), axis=0)
    lb_all = lb_all - lb_all[0]
    mod = _modulation(c, ada_w, ada_b)
    rw = jnp.concatenate([router_w, jnp.zeros((d, LANE - N_EXPERTS), F32)], axis=1)
    rb = router_bias.astype(F32)[:, None]

    for l in range(DEPTH):
        sh1, sc1, gt1, sh2, sc2, gt2 = [mod[l, :, d * k:d * (k + 1)][:, None, :] for k in range(6)]
        p = _layer_params(l, w_in, hg_onorm, mla_q_norm, mla_kv_norm, mla_w_uq, mla_w_ukv, mla_qk_norm,
                          diff_qk_norm, swa_qk_norm, swa_sinks, lb_all)
        hg3, hf, mla, diff, swa, h = _inproj(x2, norm_mix[l][None, :], sh1, sc1, p["w1"], seq)
        y_a = _hgrn(hg3, hf, p["loglb"], p["log1mlb"], p["ogain"], batch, seq)
        qm, km, vmt, qd, kd, vdt, qs, ks, vst = _prep(mla, diff, swa, cosf, sinf, p)
        y_b = _mla_attn(qm, km, vmt, batch, seq)
        lam_init = 0.8 - 0.6 * math.exp(-0.3 * l)
        y_c = _diff_attn(qd, kd, vdt, diff_lam[l], diff_onorm[l][None, :], lam_init, batch, seq)
        y_d = _swa(qs, ks, vst, p["sinks"], batch, seq)
        x2, h2, comb = _merge(h, (y_a, y_b, y_c, y_d), x2, gt1, p["wg"], w_branch[l].astype(BF16),
                              w_out[l].astype(BF16), norm_ffn[l][None, :], sh2, sc2, rw, rb, seq)
        x2 = _moe(h2, comb, x2, gt2, moe_w_gate, moe_w_up, moe_w_down, l, seq)
    return x2.reshape(batch, seq, d)
```

```python
import functools
import math

import jax
import jax.numpy as jnp
from jax import lax
from jax.experimental import pallas as pl
from jax.experimental.pallas import tpu as pltpu

F32 = jnp.float32
BF16 = jnp.bfloat16

D_MODEL = 1024
DEPTH = 2
EPS = 1e-6
N_BRANCH = 4
HG_HEADS = 8
HG_DK = 64
HG_W = HG_HEADS * HG_DK
HG_SUB = 16
HG_CHUNK = 64
HG_SAFE_DECAY = 80.0
HG_DIAG_ROWS =(HG_SUB // 2) * HG_SUB + (HG_SUB // 2) ** 2
MLA_HEADS = 8
MLA_Q_RANK = 256
MLA_KV_RANK = 128
MLA_NOPE = 64
MLA_ROPE = 32
MLA_V = 64
ROPE_BASE = 10000.0
DIFF_HEADS = 4
DIFF_QK = 64
DIFF_V = 128
SWA_Q_HEADS = 8
SWA_KV_HEADS = 2
SWA_WINDOW = 128
HEAD_DIM = 64
N_EXPERTS = 16
N_GROUPS = 4
D_FF_EXPERT = 256
IN_SPLITS = (512, 512, 512, 512, 256, 128, 32, 512, 512, 512, 512, 128, 128, 4096)

MOE_TILE = 512
MOE_ALIGN = 16
MOE_CHUNK = 160
MOE_SORT_ROWS = 768
MOE_DEST_LANE = N_EXPERTS
LANE = 128
ATT_BLK = 256
SWA_QB = 4
LOG2E = 1.4426950408889634
NEG = -1e30
VMEM_LIMIT = 56 * 1024 * 1024


def _cp(sem, vmem=VMEM_LIMIT):
    return pltpu.CompilerParams(dimension_semantics=sem, vmem_limit_bytes=vmem)


def _nt(a, b):
    return lax.dot_general(a, b, (((1,), (1,)), ((), ())), preferred_element_type=F32)


def _tn(a, b):
    return lax.dot_general(a, b, (((0,), (0,)), ((), ())), preferred_element_type=F32)


def _dot(a, b):
    return jnp.dot(a, b, preferred_element_type=F32)


def _split2(x):
    hi = x.astype(BF16)
    lo = (x - hi.astype(F32)).astype(BF16)
    return hi, lo


def _split3(x):
    hi = x.astype(BF16)
    r = x - hi.astype(F32)
    mid = r.astype(BF16)
    lo = (r - mid.astype(F32)).astype(BF16)
    return hi, mid, lo


def _seg_id(idx, seg):
    shift = seg.bit_length() - 1
    assert 1 << shift == seg
    return lax.shift_right_logical(idx, shift)


def _same_seg(n, seg):
    r = lax.broadcasted_iota(jnp.int32, (n, n), 0)
    c = lax.broadcasted_iota(jnp.int32, (n, n), 1)
    return _seg_id(r, seg) == _seg_id(c, seg)


def _seg_ones(n, seg):
    return _same_seg(n, seg).astype(BF16)


def _seg_mean_sq(x, seg):
    n = x.shape[-1]
    hi, lo = _split2(x * x)
    ones = _seg_ones(n, seg)
    return (_dot(hi, ones) + _dot(lo, ones)) * (1.0 / seg)


def _silu(x):
    return x * jax.nn.sigmoid(x)


def _mod_kernel(c_ref, w_ref, b_ref, o_ref):
    c = c_ref[...]
    o_ref[0] = jnp.dot(_silu(c), w_ref[0], preferred_element_type=F32,
                       precision=lax.Precision.HIGHEST) + b_ref[0]


def _modulation(c, ada_w, ada_b):
    nl, d, n6 = ada_w.shape
    b = c.shape[0]
    tn = 1536
    return pl.pallas_call(
        _mod_kernel,
        grid=(nl, n6 // tn),
        in_specs=[pl.BlockSpec((b, d), lambda l, j: (0, 0)),
                  pl.BlockSpec((1, d, tn), lambda l, j: (l, 0, j)),
                  pl.BlockSpec((1, 1, tn), lambda l, j: (l, 0, j))],
        out_specs=pl.BlockSpec((1, b, tn), lambda l, j: (l, 0, j)),
        out_shape=jax.ShapeDtypeStruct((nl, b, n6), F32),
        compiler_params=_cp(("arbitrary", "arbitrary")),
        name="modulation",
    )(c, ada_w, ada_b.reshape(nl, 1, n6))


W1_COLS = 4864


def _inproj_kernel(x_ref, g_ref, sh_ref, sc_ref, w_ref,
                   ohg_ref, ohf_ref, omla_ref, odiff_ref, oswa_ref, oh_ref):
    x = x_ref[...]
    ms = jnp.mean(x * x, axis=-1, keepdims=True)
    h = x * lax.rsqrt(ms + EPS) * g_ref[...]
    h = h * (1.0 + sc_ref[0]) + sh_ref[0]
    hb = h.astype(BF16)
    oh_ref[...] = hb

    def proj(lo, hi):
        return _dot(hb, w_ref[:, lo:hi])

    for k in range(3):
        ohg_ref[:, 512 * k:512 * (k + 1)] = proj(512 * k, 512 * (k + 1)).astype(BF16)
    ohf_ref[...] = proj(1536, 2048)
    omla_ref[...] = proj(2048, 2560).astype(BF16)
    for k in range(3):
        odiff_ref[:, 512 * k:512 * (k + 1)] = proj(2560 + 512 * k, 3072 + 512 * k).astype(BF16)
    oswa_ref[:, 0:512] = proj(4096, 4608).astype(BF16)
    oswa_ref[:, 512:768] = proj(4608, 4864).astype(BF16)


def _inproj(x2, gain, sh, sc, w1, seq):
    t, d = x2.shape
    tm = 512
    tpb = seq // tm
    row = lambda i: (i, 0)
    per_b = lambda i: (i // tpb, 0, 0)
    outs = [(1536, BF16), (512, F32), (512, BF16), (1536, BF16), (768, BF16), (d, BF16)]
    return pl.pallas_call(
        _inproj_kernel,
        grid=(t // tm,),
        in_specs=[pl.BlockSpec((tm, d), row),
                  pl.BlockSpec((1, d), lambda i: (0, 0)),
                  pl.BlockSpec((1, 1, d), per_b),
                  pl.BlockSpec((1, 1, d), per_b),
                  pl.BlockSpec((d, W1_COLS), lambda i: (0, 0))],
        out_specs=[pl.BlockSpec((tm, w), row) for w, _ in outs],
        out_shape=[jax.ShapeDtypeStruct((t, w), dt) for w, dt in outs],
        compiler_params=_cp(("arbitrary",)),
        name="inproj",
    )(x2, gain, sh, sc, w1)


def _segment_cumsum(x, seg):
    n = x.shape[0]
    r = lax.broadcasted_iota(jnp.int32, (n, n), 0)
    cc = lax.broadcasted_iota(jnp.int32, (n, n), 1)
    same = _same_seg(n, seg)
    tri = (same & (cc <= r)).astype(BF16)
    blk = same.astype(BF16)
    parts = _split3(x)
    c = _dot(tri, parts[0]) + _dot(tri, parts[1]) + _dot(tri, parts[2])
    tot = _dot(blk, parts[0]) + _dot(blk, parts[1]) + _dot(blk, parts[2])
    return c, tot


def _hgrn_chunk_path(i_ref, st_ref, c_s, tot_s, qs_s, kk_s, qe_s, kd_s, ke_s, dec_s, od_s):
    rows_blk = c_s.shape[0]
    tot = tot_s[...]
    rel = c_s[...] - 0.5 * tot
    half_dec = jnp.exp(0.5 * tot)
    kd = kk_s[...] * jnp.exp(-rel)
    qe_s[...] = (qs_s[...] * jnp.exp(rel)).astype(BF16)
    kd_s[...] = kd.astype(BF16)
    ke_s[...] = (kd * half_dec).astype(BF16)
    dec_s[...] = half_dec

    row = lax.broadcasted_iota(jnp.int32, (2 * rows_blk, rows_blk), 0) & (rows_blk - 1)
    col = lax.broadcasted_iota(jnp.int32, (2 * rows_blk, rows_blk), 1)
    intra = (_seg_id(row, HG_CHUNK) == _seg_id(col, HG_CHUNK)) & (col <= row)
    low_q = lax.broadcasted_iota(jnp.int32, (rows_blk, LANE), 1) < HG_DK
    low_c = lax.broadcasted_iota(jnp.int32, (HG_CHUNK, LANE), 1) < HG_DK

    for j in range(HG_W // LANE):
        cols = slice(LANE * j, LANE * (j + 1))
        qe = qe_s[:, cols]
        zero = jnp.zeros_like(qe)
        q2 = jnp.concatenate([jnp.where(low_q, qe, zero), jnp.where(low_q, zero, qe)], axis=0)
        attn = jnp.where(intra, _nt(q2, kd_s[:, cols]), 0.0).astype(BF16)
        o2 = _dot(attn, i_ref[:, cols])
        for ch in range(rows_blk // HG_CHUNK):
            rows = slice(HG_CHUNK * ch, HG_CHUNK * (ch + 1))
            rows_hi = slice(rows_blk + HG_CHUNK * ch, rows_blk + HG_CHUNK * (ch + 1))
            st = st_ref[j]
            hd = dec_s[HG_CHUNK * ch:HG_CHUNK * ch + 1, cols]
            inter = _nt(jnp.concatenate([q2[rows], q2[rows_hi]], axis=0), (st * hd).astype(BF16))
            od_s[rows, cols] = jnp.where(low_c, o2[rows] + inter[:HG_CHUNK], o2[rows_hi] + inter[HG_CHUNK:])
            upd = _tn(i_ref[rows, cols], ke_s[rows, cols])
            st_ref[j] = st * (hd * hd) + upd


def _hgrn_exact_path(i_ref, st_ref, lf_s, c_s, qs_s, kk_s, qe_s, ke_s, dec_s, od_s, t_s, a_s):
    rows_blk = c_s.shape[0]
    n_sub = rows_blk // HG_SUB
    c, tot = _segment_cumsum(lf_s[...], HG_SUB)
    c_s[...] = c
    qe_s[...] = (qs_s[...] * jnp.exp(c)).astype(BF16)
    ke_s[...] = (kk_s[...] * jnp.exp(tot - c)).astype(BF16)
    dec_s[...] = jnp.exp(tot)

    same_head = _same_seg(LANE, HG_DK)
    head_mask = same_head.astype(F32)
    head_ones = same_head.astype(BF16)
    for j in range(HG_W // LANE):
        st_ref[j] = st_ref[j] * head_mask
    half = HG_SUB // 2
    trow = lax.broadcasted_iota(jnp.int32, (half, HG_W), 0)

    def body(i, carry):
        r0 = pl.multiple_of(i * HG_SUB, HG_SUB)
        rows = pl.ds(r0, HG_SUB)
        c_i = c_s[rows, :]
        qs_i = qs_s[rows, :]
        kk_i = kk_s[rows, :]
        v_i = i_ref[rows, :].astype(F32)
        c_lo, c_hi = c_i[:half], c_i[half:]
        q_lo, q_hi = qs_i[:half], qs_i[half:]
        for s in range(half):
            c_row, k_row = c_i[s:s + 1, :], kk_i[s:s + 1, :]
            e_lo = jnp.exp(jnp.where(trow >= s, c_lo - c_row, NEG))
            e_hi = jnp.exp(c_hi - c_row)
            both = jnp.concatenate([e_lo * q_lo, e_hi * q_hi], axis=0) * k_row
            t_s[s * HG_SUB:(s + 1) * HG_SUB, :] = both.astype(BF16)
        for s in range(half, HG_SUB, 2):
            pair = []
            for u in (s, s + 1):
                e_hi = jnp.exp(jnp.where(trow >= u - half, c_hi - c_i[u:u + 1, :], NEG))
                pair.append(e_hi * q_hi * kk_i[u:u + 1, :])
            base = half * HG_SUB + (s - half) * half
            t_s[base:base + HG_SUB, :] = jnp.concatenate(pair, axis=0).astype(BF16)
        for j in range(HG_W // LANE):
            cols = slice(LANE * j, LANE * (j + 1))
            a_s[:, cols] = _dot(t_s[:, cols], head_ones)
        acc_lo = jnp.zeros((half, HG_W), F32)
        acc_hi = jnp.zeros((half, HG_W), F32)
        for s in range(half):
            acc_lo = acc_lo + a_s[s * HG_SUB:s * HG_SUB + half, :] * v_i[s:s + 1, :]
            acc_hi = acc_hi + a_s[s * HG_SUB + half:(s + 1) * HG_SUB, :] * v_i[s:s + 1, :]
        for s in range(half, HG_SUB):
            base = half * HG_SUB + (s - half) * half
            acc_hi = acc_hi + a_s[base:base + half, :] * v_i[s:s + 1, :]
        acc = jnp.concatenate([acc_lo, acc_hi], axis=0)
        for j in range(HG_W // LANE):
            cols = slice(LANE * j, LANE * (j + 1))
            st = st_ref[j]
            o_int = _nt(qe_s[rows, cols], st.astype(BF16))
            upd = _tn(i_ref[rows, cols], ke_s[rows, cols])
            st_ref[j] = st * dec_s[pl.ds(r0, 1), cols] + upd * head_mask
            od_s[rows, cols] = acc[:, cols] + o_int
        return carry

    lax.fori_loop(0, n_sub, body, 0)


def _hgrn_kernel(q_ref, i_ref, g_ref, f_ref, loglb_ref, log1mlb_ref, og_ref, o_ref,
                 st_ref, lf_s, c_s, tot_s, qs_s, kk_s, qe_s, kd_s, ke_s, dec_s, od_s, t_s, a_s):
    @pl.when(pl.program_id(1) == 0)
    def _():
        st_ref[...] = jnp.zeros_like(st_ref)

    fr = f_ref[...]
    ls = jnp.minimum(fr, 0.0) - jnp.log(1.0 + jnp.exp(-jnp.abs(fr)))
    a = loglb_ref[...]
    c2 = log1mlb_ref[...] + ls
    lf = jnp.maximum(a, c2) + jnp.log(1.0 + jnp.exp(-jnp.abs(a - c2)))
    lf_s[...] = lf
    qs_s[...] = _silu(q_ref[...].astype(F32))
    kk_s[...] = 1.0 - jnp.exp(lf)
    c, tot = _segment_cumsum(lf, HG_CHUNK)
    c_s[...] = c
    tot_s[...] = tot
    safe = 0.5 * jnp.max(-tot) <= HG_SAFE_DECAY

    @pl.when(safe)
    def _():
        _hgrn_chunk_path(i_ref, st_ref, c_s, tot_s, qs_s, kk_s, qe_s, kd_s, ke_s, dec_s, od_s)

    @pl.when(jnp.logical_not(safe))
    def _():
        _hgrn_exact_path(i_ref, st_ref, lf_s, c_s, qs_s, kk_s, qe_s, ke_s, dec_s, od_s, t_s, a_s)

    o = od_s[...]
    ms = _seg_mean_sq(o, HG_DK)
    on = o * lax.rsqrt(ms + EPS) * og_ref[...]
    o_ref[...] = (on * _silu(g_ref[...].astype(F32))).astype(BF16)


def _hgrn(hg3, hf, loglb, log1mlb, ogain, batch, seq):
    t = hf.shape[0]
    rb = 256
    nb = seq // rb
    blk = lambda k: pl.BlockSpec((rb, HG_W), lambda b, n, k=k: (b * nb + n, k))
    vec = pl.BlockSpec((1, HG_W), lambda b, n: (0, 0))
    f32_blk = pltpu.VMEM((rb, HG_W), F32)
    bf16_blk = pltpu.VMEM((rb, HG_W), BF16)
    return pl.pallas_call(
        _hgrn_kernel,
        grid=(batch, nb),
        in_specs=[blk(0), blk(1), blk(2), blk(0), vec, vec, vec],
        out_specs=blk(0),
        out_shape=jax.ShapeDtypeStruct((t, HG_W), BF16),
        scratch_shapes=[pltpu.VMEM((HG_W // LANE, LANE, LANE), F32),
                        f32_blk, f32_blk, f32_blk, f32_blk, f32_blk,
                        bf16_blk, bf16_blk, bf16_blk,
                        f32_blk, f32_blk,
                        pltpu.VMEM((HG_DIAG_ROWS, HG_W), BF16),
                        pltpu.VMEM((HG_DIAG_ROWS, HG_W), F32)],
        compiler_params=_cp(("arbitrary", "arbitrary")),
        name="hgrn2",
    )(hg3, hg3, hg3, hf, loglb, log1mlb, ogain)


def _store_transposed_blocks(out_ref, v):
    blk = out_ref.shape[2]
    for u in range(out_ref.shape[0]):
        out_ref[u] = v[u * blk:(u + 1) * blk, :].T.astype(BF16)


def _prep_kernel(mla_ref, diff_ref, swa_ref, cos_ref, sin_ref,
                 qng_ref, kvg_ref, wqa_ref, wqb_ref, wka_ref, wkb_ref, wv_ref,
                 gq_ref, gqs_ref, gk_ref, gks_ref, dgq_ref, dgk_ref, sgq_ref, sgk_ref,
                 qm_ref, km_ref, vmt_ref, qd_ref, kd_ref, vdt_ref, qs_ref, ks_ref, vst_ref):
    blk = mla_ref[...].astype(F32)
    cq = blk[:, :MLA_Q_RANK]
    rest = blk[:, MLA_Q_RANK:]
    cqn = cq * lax.rsqrt(jnp.mean(cq * cq, axis=-1, keepdims=True) + EPS) * qng_ref[...]
    lane = lax.broadcasted_iota(jnp.int32, rest.shape, 1)
    is_kv = lane < MLA_KV_RANK
    ms_kv = jnp.sum(jnp.where(is_kv, rest * rest, 0.0), axis=-1, keepdims=True) * (1.0 / MLA_KV_RANK)
    restn = jnp.where(is_kv, rest * lax.rsqrt(ms_kv + EPS) * kvg_ref[...], rest)
    cqb = cqn.astype(BF16)
    rb = restn.astype(BF16)
    qa = _dot(cqb, wqa_ref[...])
    qb = _dot(cqb, wqb_ref[...])
    ka = _dot(rb, wka_ref[...])
    kb = _dot(rb, wkb_ref[...])
    _store_transposed_blocks(vmt_ref, _dot(rb, wv_ref[...]))
    cosf = cos_ref[...]
    sinf = sin_ref[...]
    cq_t = cosf * gq_ref[...]
    sq_t = sinf * gqs_ref[...]
    ck_t = cosf * gk_ref[...]
    sk_t = sinf * gks_ref[...]
    inv_n = 1.0 / (MLA_NOPE + MLA_ROPE)
    scale = (MLA_NOPE + MLA_ROPE) ** -0.5 * LOG2E
    for h in range(MLA_HEADS):
        cols = slice(LANE * h, LANE * (h + 1))
        x = qa[:, cols]
        rinv = lax.rsqrt(jnp.sum(x * x, axis=-1, keepdims=True) * inv_n + EPS)
        qm_ref[:, cols] = ((x * cq_t + qb[:, cols] * sq_t) * (rinv * scale)).astype(BF16)
        y = ka[:, cols]
        rinv = lax.rsqrt(jnp.sum(y * y, axis=-1, keepdims=True) * inv_n + EPS)
        km_ref[:, cols] = ((y * ck_t + kb[:, cols] * sk_t) * rinv).astype(BF16)

    def seg_norm(x, gain, scale):
        return x * lax.rsqrt(_seg_mean_sq(x, HEAD_DIM) + EPS) * (gain * scale)

    dq = diff_ref[:, 0:512].astype(F32)
    dk = diff_ref[:, 512:1024].astype(F32)
    qd_ref[...] = seg_norm(dq, dgq_ref[...], DIFF_QK ** -0.5 * LOG2E).astype(BF16)
    kd_ref[...] = seg_norm(dk, dgk_ref[...], 1.0).astype(BF16)
    _store_transposed_blocks(vdt_ref, diff_ref[:, 1024:1536].astype(F32))

    sq = swa_ref[:, 0:512].astype(F32)
    qs_ref[...] = seg_norm(sq, sgq_ref[...], HEAD_DIM ** -0.5 * LOG2E).astype(BF16)
    skv = swa_ref[:, 512:768].astype(F32)
    kn = seg_norm(skv[:, :LANE], sgk_ref[...], 1.0)
    low = lax.broadcasted_iota(jnp.int32, kn.shape, 1) < HEAD_DIM
    sw = pltpu.roll(kn, HEAD_DIM, 1)
    ks_ref[:, :LANE] = jnp.where(low, kn, sw).astype(BF16)
    ks_ref[:, LANE:] = jnp.where(low, sw, kn).astype(BF16)
    _store_transposed_blocks(vst_ref, skv[:, LANE:])


def _prep(mla, diff, swa, cosf, sinf, p):
    t = mla.shape[0]
    tm = 512
    row = lambda i: (i, 0)
    full = lambda a: pl.BlockSpec(a.shape, lambda i: (0,) * a.ndim)
    consts = [p["qng"], p["kvg"], p["wqa"], p["wqb"], p["wka"], p["wkb"], p["wv"],
              p["gq"], p["gqs"], p["gk"], p["gks"], p["dgq"], p["dgk"], p["sgq"], p["sgk"]]
    def rows_out(w):
        return pl.BlockSpec((tm, w), row), jax.ShapeDtypeStruct((t, w), BF16)

    def transposed_out(n, blk):
        return (pl.BlockSpec((tm // blk, n, blk), lambda i: (i, 0, 0)),
                jax.ShapeDtypeStruct((t // blk, n, blk), BF16))

    outs = [rows_out(1024), rows_out(1024), transposed_out(512, ATT_BLK),
            rows_out(512), rows_out(512), transposed_out(512, ATT_BLK),
            rows_out(512), rows_out(256), transposed_out(LANE, SWA_WINDOW)]
    return pl.pallas_call(
        _prep_kernel,
        grid=(t // tm,),
        in_specs=[pl.BlockSpec((tm, 512), row),
                  pl.BlockSpec((tm, 1536), row),
                  pl.BlockSpec((tm, 768), row),
                  pl.BlockSpec((tm, LANE), row),
                  pl.BlockSpec((tm, LANE), row)] + [full(a) for a in consts],
        out_specs=[o[0] for o in outs],
        out_shape=[o[1] for o in outs],
        compiler_params=_cp(("arbitrary",)),
        name="attn_prep",
    )(mla, diff, swa, cosf, sinf, *consts)


def _causal_t(blk):
    key = lax.broadcasted_iota(jnp.int32, (blk, blk), 0)
    qry = lax.broadcasted_iota(jnp.int32, (blk, blk), 1)
    return key <= qry


def _two_pass_attention(n_sets, score_fn, value_fn, s_scr, acc_scr, blk):
    qi = pl.program_id(1)
    causal = _causal_t(blk)

    def scores(ki, m, masked):
        out = []
        for i in range(n_sets):
            s = score_fn(i, ki)
            if masked:
                s = jnp.where(causal, s, NEG)
            s_scr[i, ki] = s
            out.append(jnp.maximum(m[i], jnp.max(s, axis=0, keepdims=True)))
        return tuple(out)

    m = tuple(jnp.full((1, blk), NEG, F32) for _ in range(n_sets))
    m = lax.fori_loop(0, qi, lambda ki, c: scores(ki, c, False), m)
    m = scores(qi, m, True)

    acc_scr[...] = jnp.zeros_like(acc_scr)

    def accumulate(ki, l):
        out = []
        for i in range(n_sets):
            p = jnp.exp2(s_scr[i, ki] - m[i])
            out.append(l[i] + jnp.sum(p, axis=0, keepdims=True))
            acc_scr[i] += _dot(value_fn(i, ki), p.astype(BF16))
        return tuple(out)

    zero = tuple(jnp.zeros((1, blk), F32) for _ in range(n_sets))
    return lax.fori_loop(0, qi + 1, accumulate, zero)


def _mla_attn_kernel(q_ref, k_ref, vt_ref, o_ref, s_scr, acc_scr):
    blk = q_ref.shape[0]

    def score_fn(h, ki):
        rows = pl.ds(pl.multiple_of(ki * blk, blk), blk)
        cols = slice(LANE * h, LANE * (h + 1))
        return _nt(k_ref[rows, cols], q_ref[:, cols])

    def value_fn(h, ki):
        return vt_ref[ki, MLA_V * h:MLA_V * (h + 1), :]

    l = _two_pass_attention(MLA_HEADS, score_fn, value_fn, s_scr, acc_scr, blk)
    for j in range(MLA_HEADS // 2):
        o_t = jnp.concatenate([acc_scr[2 * j] / l[2 * j], acc_scr[2 * j + 1] / l[2 * j + 1]], axis=0)
        o_ref[:, LANE * j:LANE * (j + 1)] = o_t.T.astype(BF16)


def _mla_attn(qm, km, vmt, batch, seq):
    t = qm.shape[0]
    nq = seq // ATT_BLK
    return pl.pallas_call(
        _mla_attn_kernel,
        grid=(batch, nq),
        in_specs=[pl.BlockSpec((ATT_BLK, 1024), lambda b, i: (b * nq + i, 0)),
                  pl.BlockSpec((seq, 1024), lambda b, i: (b, 0)),
                  pl.BlockSpec((nq, 512, ATT_BLK), lambda b, i: (b, 0, 0))],
        out_specs=pl.BlockSpec((ATT_BLK, 512), lambda b, i: (b * nq + i, 0)),
        out_shape=jax.ShapeDtypeStruct((t, 512), BF16),
        scratch_shapes=[pltpu.VMEM((MLA_HEADS, nq, ATT_BLK, ATT_BLK), F32),
                        pltpu.VMEM((MLA_HEADS, MLA_V, ATT_BLK), F32)],
        compiler_params=_cp(("arbitrary", "arbitrary")),
        name="mla_attn",
    )(qm, km, vmt)


def _diff_attn_kernel(q_ref, k_ref, vt_ref, lam_ref, og_ref, o_ref, s_scr, acc_scr, qm_scr, *, lam_init):
    blk = q_ref.shape[0]
    low = lax.broadcasted_iota(jnp.int32, (blk, LANE), 1) < DIFF_QK
    lp = lam_ref[...]
    lam = (jnp.exp(jnp.sum(lp[0:1] * lp[1:2], axis=-1, keepdims=True))
           - jnp.exp(jnp.sum(lp[2:3] * lp[3:4], axis=-1, keepdims=True)) + lam_init)

    for h in range(DIFF_HEADS):
        qt = q_ref[:, LANE * h:LANE * (h + 1)]
        zero = jnp.zeros_like(qt)
        qm_scr[2 * h] = jnp.where(low, qt, zero)
        qm_scr[2 * h + 1] = jnp.where(low, zero, qt)

    def score_fn(i, ki):
        rows = pl.ds(pl.multiple_of(ki * blk, blk), blk)
        h = i // 2
        return _nt(k_ref[rows, LANE * h:LANE * (h + 1)], qm_scr[i])

    def value_fn(i, ki):
        h = i // 2
        return vt_ref[ki, DIFF_V * h:DIFF_V * (h + 1), :]

    l = _two_pass_attention(2 * DIFF_HEADS, score_fn, value_fn, s_scr, acc_scr, blk)
    for h in range(DIFF_HEADS):
        o_t = acc_scr[2 * h] / l[2 * h] - lam * (acc_scr[2 * h + 1] / l[2 * h + 1])
        on_t = o_t * lax.rsqrt(jnp.mean(o_t * o_t, axis=0, keepdims=True) + EPS)
        o_ref[:, LANE * h:LANE * (h + 1)] = (on_t.T * (og_ref[...] * (1.0 - lam_init))).astype(BF16)


def _diff_attn(qd, kd, vdt, lam_p, og, lam_init, batch, seq):
    t = qd.shape[0]
    nq = seq // ATT_BLK
    return pl.pallas_call(
        functools.partial(_diff_attn_kernel, lam_init=lam_init),
        grid=(batch, nq),
        in_specs=[pl.BlockSpec((ATT_BLK, 512), lambda b, i: (b * nq + i, 0)),
                  pl.BlockSpec((seq, 512), lambda b, i: (b, 0)),
                  pl.BlockSpec((nq, 512, ATT_BLK), lambda b, i: (b, 0, 0)),
                  pl.BlockSpec(lam_p.shape, lambda b, i: (0, 0)),
                  pl.BlockSpec(og.shape, lambda b, i: (0, 0))],
        out_specs=pl.BlockSpec((ATT_BLK, 512), lambda b, i: (b * nq + i, 0)),
        out_shape=jax.ShapeDtypeStruct((t, 512), BF16),
        scratch_shapes=[pltpu.VMEM((2 * DIFF_HEADS, nq, ATT_BLK, ATT_BLK), F32),
                        pltpu.VMEM((2 * DIFF_HEADS, DIFF_V, ATT_BLK), F32),
                        pltpu.VMEM((2 * DIFF_HEADS, ATT_BLK, LANE), BF16)],
        compiler_params=_cp(("arbitrary", "arbitrary")),
        name="diff_attn",
    )(qd, kd, vdt, lam_p, og)


def _swa_kernel(q_ref, kp_ref, kc_ref, vtp_ref, vtc_ref, sink_ref, o_ref):
    w = SWA_WINDOW
    grp = SWA_Q_HEADS // SWA_KV_HEADS
    n = pl.program_id(1)
    key = lax.broadcasted_iota(jnp.int32, (2 * w, grp * w), 0)
    qry = lax.broadcasted_iota(jnp.int32, (2 * w, grp * w), 1) & (w - 1)
    cur_ok = (key >= w) & (key - w <= qry)
    prev_ok = (key < w) & (key > qry)
    low = lax.broadcasted_iota(jnp.int32, (w, LANE), 1) < HEAD_DIM

    for t in range(q_ref.shape[0] // w):
        rows = slice(t * w, (t + 1) * w)
        if t == 0:
            kp, vtp = kp_ref[...], vtp_ref[0]
            valid = cur_ok | (prev_ok & (n > 0))
        else:
            kp, vtp = kc_ref[(t - 1) * w:t * w, :], vtc_ref[t - 1]
            valid = cur_ok | prev_ok
        kc, vtc = kc_ref[rows, :], vtc_ref[t]
        for kv in range(SWA_KV_HEADS):
            kcols = slice(LANE * kv, LANE * (kv + 1))
            vrows = slice(HEAD_DIM * kv, HEAD_DIM * (kv + 1))
            k_win = jnp.concatenate([kp[:, kcols], kc[:, kcols]], axis=0)
            parts = []
            for u in range(2):
                qt = q_ref[rows, LANE * (2 * kv + u):LANE * (2 * kv + u + 1)]
                zero = jnp.zeros_like(qt)
                parts += [jnp.where(low, qt, zero), jnp.where(low, zero, qt)]
            s = jnp.where(valid, _nt(k_win, jnp.concatenate(parts, axis=0)), NEG)
            sink = sink_ref[:, grp * w * kv:grp * w * (kv + 1)]
            m = jnp.maximum(jnp.max(s, axis=0, keepdims=True), sink)
            p = jnp.exp2(s - m)
            den = jnp.sum(p, axis=0, keepdims=True) + jnp.exp2(sink - m)
            vt_win = jnp.concatenate([vtp[vrows, :], vtc[vrows, :]], axis=1)
            o_t = _dot(vt_win, p.astype(BF16)) / den
            for u in range(2):
                pair = jnp.concatenate([o_t[:, 2 * u * w:(2 * u + 1) * w],
                                        o_t[:, (2 * u + 1) * w:(2 * u + 2) * w]], axis=0)
                o_ref[rows, LANE * (2 * kv + u):LANE * (2 * kv + u + 1)] = pair.T.astype(BF16)


def _swa(qs, ks, vst, sink_row, batch, seq):
    t = qs.shape[0]
    w = SWA_WINDOW
    nb = seq // w
    ns = nb // SWA_QB
    cur = lambda b, n: (b * ns + n, 0)
    cur3 = lambda b, n: (b * ns + n, 0, 0)
    prev = lambda b, n: (b * nb + jnp.maximum(n * SWA_QB - 1, 0), 0)
    prev3 = lambda b, n: (b * nb + jnp.maximum(n * SWA_QB - 1, 0), 0, 0)
    return pl.pallas_call(
        _swa_kernel,
        grid=(batch, ns),
        in_specs=[pl.BlockSpec((SWA_QB * w, 512), cur),
                  pl.BlockSpec((w, 256), prev), pl.BlockSpec((SWA_QB * w, 256), cur),
                  pl.BlockSpec((1, LANE, w), prev3), pl.BlockSpec((SWA_QB, LANE, w), cur3),
                  pl.BlockSpec(sink_row.shape, lambda b, n: (0, 0))],
        out_specs=pl.BlockSpec((SWA_QB * w, 512), cur),
        out_shape=jax.ShapeDtypeStruct((t, 512), BF16),
        compiler_params=_cp(("arbitrary", "arbitrary")),
        name="swa_attn",
    )(qs, ks, ks, vst, vst, sink_row)


def _merge_kernel(h_ref, ya_ref, yb_ref, yc_ref, yd_ref, x_ref, gt1_ref, wg_ref, wb_ref, wo_ref,
                  g2_ref, sh2_ref, sc2_ref, rw_ref, rb_ref, xo_ref, h2_ref, comb_ref, dest_ref, meta_ref):
    h = h_ref[...]
    d = x_ref.shape[1]
    merged = None
    for b, y_ref in enumerate((ya_ref, yb_ref, yc_ref, yd_ref)):
        gate = jax.nn.sigmoid(_dot(h, wg_ref[:, d * b:d * (b + 1)]))
        term = gate * _dot(y_ref[...], wb_ref[b])
        merged = term if merged is None else merged + term
    xn = x_ref[...] + gt1_ref[0] * _dot(merged.astype(BF16), wo_ref[...])
    xo_ref[...] = xn
    ms = jnp.mean(xn * xn, axis=-1, keepdims=True)
    h2 = xn * lax.rsqrt(ms + EPS) * g2_ref[...]
    h2 = h2 * (1.0 + sc2_ref[0]) + sh2_ref[0]
    h2_ref[...] = h2.astype(BF16)

    hh, hm, _ = _split3(h2)
    wh, wm = _split2(rw_ref[...])
    logits = _dot(hh, wh) + _dot(hm, wh) + _dot(hh, wm)
    lt = logits.T
    scores = jax.nn.sigmoid(lt[0:N_EXPERTS, :])
    sel = scores + rb_ref[...]
    per = N_EXPERTS // N_GROUPS
    srow = [sel[e:e + 1, :] for e in range(N_EXPERTS)]
    gsum = []
    for g in range(N_GROUPS):
        a, b_, c, e_ = srow[per * g:per * (g + 1)]
        gsum.append(jnp.maximum(jnp.maximum(jnp.maximum(a + b_, a + c), jnp.maximum(a + e_, b_ + c)),
                                jnp.maximum(b_ + e_, c + e_)))
    best = jnp.maximum(jnp.maximum(gsum[0], gsum[1]), jnp.maximum(gsum[2], gsum[3]))
    taken = None
    rows = []
    picks = []
    for g in range(N_GROUPS):
        hit = gsum[g] == best
        pick = hit if taken is None else hit & jnp.logical_not(taken)
        taken = hit if taken is None else taken | hit
        picks.append(pick.astype(F32))
        for e in range(per * g, per * (g + 1)):
            rank = jnp.zeros_like(best)
            for o in range(per * g, per * (g + 1)):
                if o == e:
                    continue
                ahead = (srow[o] > srow[e]) | ((srow[o] == srow[e]) & (o < e))
                rank = rank + ahead.astype(F32)
            rows.append(jnp.where(pick & (rank < 1.5), scores[e:e + 1, :], 0.0))
    wsum = rows[0]
    for r_ in rows[1:]:
        wsum = wsum + r_
    inv = 1.0 / wsum
    rid = lax.broadcasted_iota(jnp.int32, scores.shape, 0)
    comb_e = jnp.zeros_like(scores)
    for e, r_ in enumerate(rows):
        comb_e = jnp.where(rid == e, r_ * inv, comb_e)

    tm = lt.shape[1]
    gid = lax.broadcasted_iota(jnp.int32, (8, tm), 0)
    onehot = jnp.zeros((8, tm), F32)
    for g in range(N_GROUPS):
        onehot = jnp.where(gid == g, picks[g], onehot)
    before = (lax.broadcasted_iota(jnp.int32, (tm, tm), 0) < lax.broadcasted_iota(jnp.int32, (tm, tm), 1))
    rank = _dot(onehot.astype(BF16), before.astype(BF16))
    dest = jnp.zeros((1, tm), F32)
    off = jnp.zeros((1, 1), F32)
    meta = jnp.zeros((8, LANE), F32)
    mrow = lax.broadcasted_iota(jnp.int32, (8, LANE), 0)
    for g in range(N_GROUPS):
        cnt = jnp.sum(picks[g], axis=-1, keepdims=True)
        dest = dest + picks[g] * (off + rank[g:g + 1, :])
        meta = jnp.where(mrow == g, cnt, meta)
        meta = jnp.where(mrow == N_GROUPS + g, off, meta)
        off = off + jnp.ceil(cnt * (1.0 / MOE_ALIGN)) * MOE_ALIGN
    dest_ref[0] = dest.astype(jnp.int32)
    meta_ref[0] = meta.astype(jnp.int32)
    pad = jnp.zeros((LANE - N_EXPERTS - 8, tm), F32)
    comb_t = jnp.concatenate([comb_e, jnp.where(gid == 0, dest, 0.0), pad], axis=0)
    comb_ref[...] = comb_t.T


def _merge(h, ys, x2, gt1, wg, wb, wo, g2, sh2, sc2, rw, rb, seq):
    t, d = x2.shape
    tm = MOE_TILE
    tpb = seq // tm
    row = lambda i: (i, 0)
    per_b = lambda i: (i // tpb, 0, 0)
    c2 = lambda i: (0, 0)
    return pl.pallas_call(
        _merge_kernel,
        grid=(t // tm,),
        in_specs=[pl.BlockSpec((tm, d), row)] + [pl.BlockSpec((tm, 512), row)] * 4
                 + [pl.BlockSpec((tm, d), row), pl.BlockSpec((1, 1, d), per_b),
                    pl.BlockSpec(wg.shape, c2), pl.BlockSpec(wb.shape, lambda i: (0, 0, 0)),
                    pl.BlockSpec(wo.shape, c2), pl.BlockSpec((1, d), c2),
                    pl.BlockSpec((1, 1, d), per_b), pl.BlockSpec((1, 1, d), per_b),
                    pl.BlockSpec(rw.shape, c2), pl.BlockSpec(rb.shape, c2)],
        out_specs=[pl.BlockSpec((tm, d), row), pl.BlockSpec((tm, d), row), pl.BlockSpec((tm, LANE), row),
                   pl.BlockSpec((1, 1, tm), lambda i: (i, 0, 0)), pl.BlockSpec((1, 8, LANE), lambda i: (i, 0, 0))],
        out_shape=[jax.ShapeDtypeStruct((t, d), F32), jax.ShapeDtypeStruct((t, d), BF16),
                   jax.ShapeDtypeStruct((t, LANE), F32),
                   jax.ShapeDtypeStruct((t // tm, 1, tm), jnp.int32),
                   jax.ShapeDtypeStruct((t // tm, 8, LANE), jnp.int32)],
        compiler_params=_cp(("arbitrary",)),
        name="merge_router",
    )(h, *ys, x2, gt1, wg, wb, wo, g2, sh2, sc2, rw, rb)


def _moe_kernel(meta_ref, h2_ref, comb_ref, dest_ref, x_ref, gt2_ref, wg_ref, wu_ref, wd_ref, o_ref,
                sorted_s, csort_s, out_s):
    i = pl.program_id(0)
    tm = h2_ref.shape[0]
    n_rows = sorted_s.shape[0]
    per = N_EXPERTS // N_GROUPS
    comb = comb_ref[...]

    place = (lax.broadcasted_iota(jnp.int32, (n_rows, tm), 0) == dest_ref[0]).astype(BF16)
    sorted_s[...] = _dot(place, h2_ref[...]).astype(BF16)
    c_hi, c_lo = _split2(comb)
    csort_s[...] = _dot(place, c_hi) + _dot(place, c_lo)
    out_s[...] = jnp.zeros_like(out_s)
    lane = lax.broadcasted_iota(jnp.int32, (MOE_CHUNK, LANE), 1)

    for g in range(N_GROUPS):
        cnt = meta_ref[i, g]
        off = meta_ref[i, N_GROUPS + g]

        def chunk(c, carry):
            rows = pl.ds(pl.multiple_of(off + c * MOE_CHUNK, MOE_ALIGN), MOE_CHUNK)
            xk = sorted_s[rows, :]
            cw = csort_s[rows, :]
            acc = None
            for e in range(per * g, per * (g + 1)):
                hid = _silu(_dot(xk, wg_ref[e])) * _dot(xk, wu_ref[e])
                ce = jnp.sum(jnp.where(lane == e, cw, 0.0), axis=-1, keepdims=True)
                term = _dot((hid * ce).astype(BF16), wd_ref[e])
                acc = term if acc is None else acc + term
            out_s[rows, :] += acc
            return carry

        lax.fori_loop(0, (cnt + MOE_CHUNK - 1) // MOE_CHUNK, chunk, 0)

    dest_col = comb[:, MOE_DEST_LANE:MOE_DEST_LANE + 1].astype(jnp.int32)
    back = (lax.broadcasted_iota(jnp.int32, (tm, n_rows), 1) == dest_col).astype(BF16)
    o_ref[...] = x_ref[...] + gt2_ref[0] * _dot(back, out_s[...].astype(BF16))


def _moe(h2, comb, dest, meta, x2, gt2, wg, wu, wd, seq):
    t, d = x2.shape
    tm = MOE_TILE
    tpb = seq // tm
    n_rows = MOE_SORT_ROWS
    assert tm + N_GROUPS * (MOE_ALIGN - 1) + MOE_CHUNK - 1 <= n_rows
    row = lambda i, m: (i, 0)
    whole = lambda a: pl.BlockSpec(a.shape, lambda i, m: (0, 0, 0), pipeline_mode=pl.Buffered(1))
    grid_spec = pltpu.PrefetchScalarGridSpec(
        num_scalar_prefetch=1,
        grid=(t // tm,),
        in_specs=[pl.BlockSpec((tm, d), row), pl.BlockSpec((tm, LANE), row),
                  pl.BlockSpec((1, 1, tm), lambda i, m: (i, 0, 0)),
                  pl.BlockSpec((tm, d), row),
                  pl.BlockSpec((1, 1, d), lambda i, m: (i // tpb, 0, 0)),
                  whole(wg), whole(wu), whole(wd)],
        out_specs=pl.BlockSpec((tm, d), row),
        scratch_shapes=[pltpu.VMEM((n_rows, d), BF16), pltpu.VMEM((n_rows, LANE), F32),
                        pltpu.VMEM((n_rows, d), F32)])
    return pl.pallas_call(
        _moe_kernel,
        grid_spec=grid_spec,
        out_shape=jax.ShapeDtypeStruct((t, d), F32),
        compiler_params=_cp(("arbitrary",)),
        name="moe",
    )(meta, h2, comb, dest, x2, gt2, wg, wu, wd)


def _split_cols(w):
    out, o = [], 0
    for n in IN_SPLITS:
        out.append(w[:, o:o + n])
        o += n
    return out


def _layer_params(l, w_in, hg_onorm, mla_q_norm, mla_kv_norm, mla_w_uq, mla_w_ukv, mla_qk_norm,
                  diff_qk_norm, swa_qk_norm, swa_sinks, lb_all):
    (hq, hf, hi, hgate, mcq, mckv, mkr, dq, dk, dv, sq, sk, sv, gates) = _split_cols(w_in[l])
    d = w_in.shape[1]
    w1 = jnp.concatenate([hq, hi, hgate, hf, mcq, mckv, mkr, jnp.zeros((d, 96), F32),
                          dq, dk, dv, sq, sk, sv], axis=1).astype(BF16)
    p = {"w1": w1, "wg": gates.astype(BF16)}

    lb = lb_all[l]
    p["loglb"] = jnp.log(lb)[None, :]
    p["log1mlb"] = jnp.log1p(-lb)[None, :]
    p["ogain"] = jnp.tile(hg_onorm[l], HG_HEADS)[None, :]

    hd = MLA_NOPE + MLA_ROPE
    half = MLA_ROPE // 2
    wq = mla_w_uq[l].reshape(MLA_Q_RANK, MLA_HEADS, hd)
    z = lambda r, n: jnp.zeros((r, MLA_HEADS, n), F32)
    nope, rope = wq[:, :, :MLA_NOPE], wq[:, :, MLA_NOPE:]
    p["wqa"] = jnp.concatenate([nope, rope, z(MLA_Q_RANK, 32)], -1).reshape(MLA_Q_RANK, -1).astype(BF16)
    p["wqb"] = jnp.concatenate([z(MLA_Q_RANK, MLA_NOPE), rope[:, :, half:], rope[:, :, :half],
                                z(MLA_Q_RANK, 32)], -1).reshape(MLA_Q_RANK, -1).astype(BF16)
    wkv = mla_w_ukv[l].reshape(MLA_KV_RANK, MLA_HEADS, MLA_NOPE + MLA_V)
    knope, vproj = wkv[:, :, :MLA_NOPE], wkv[:, :, MLA_NOPE:]
    eye = jnp.eye(MLA_ROPE, dtype=F32)
    swap = jnp.concatenate([eye[:, half:], eye[:, :half]], axis=1)
    place = lambda m: jnp.broadcast_to(
        jnp.concatenate([jnp.zeros((MLA_ROPE, MLA_NOPE), F32), m, jnp.zeros((MLA_ROPE, 32), F32)], -1)[:, None, :],
        (MLA_ROPE, MLA_HEADS, LANE))
    pad_rows = 256 - MLA_KV_RANK - MLA_ROPE
    p["wka"] = jnp.concatenate([jnp.concatenate([knope, z(MLA_KV_RANK, 64)], -1), place(eye),
                                z(pad_rows, LANE)], 0).reshape(256, -1).astype(BF16)
    p["wkb"] = jnp.concatenate([z(MLA_KV_RANK, LANE), place(swap), z(pad_rows, LANE)], 0
                               ).reshape(256, -1).astype(BF16)
    p["wv"] = jnp.concatenate([vproj.reshape(MLA_KV_RANK, -1),
                               jnp.zeros((256 - MLA_KV_RANK, MLA_HEADS * MLA_V), F32)], 0).astype(BF16)
    p["qng"] = mla_q_norm[l][None, :]
    p["kvg"] = jnp.concatenate([mla_kv_norm[l], jnp.ones((256 - MLA_KV_RANK,), F32)])[None, :]

    def rope_gains(g):
        base = jnp.concatenate([g, jnp.zeros((LANE - hd,), F32)])
        part = jnp.concatenate([jnp.zeros((MLA_NOPE,), F32), g[MLA_NOPE + half:], g[MLA_NOPE:MLA_NOPE + half],
                                jnp.zeros((LANE - hd,), F32)])
        return base[None, :], part[None, :]

    p["gq"], p["gqs"] = rope_gains(mla_qk_norm[l, 0])
    p["gk"], p["gks"] = rope_gains(mla_qk_norm[l, 1])
    p["dgq"] = jnp.tile(diff_qk_norm[l, 0], 8)[None, :]
    p["dgk"] = jnp.tile(diff_qk_norm[l, 1], 8)[None, :]
    p["sgq"] = jnp.tile(swa_qk_norm[l, 0], 8)[None, :]
    p["sgk"] = jnp.tile(swa_qk_norm[l, 1], 2)[None, :]
    p["sinks"] = jnp.repeat(swa_sinks[l].astype(F32) * LOG2E, SWA_WINDOW)[None, :]
    return p


def _rope_tables(positions):
    inv_freq = ROPE_BASE ** (-jnp.arange(0, MLA_ROPE, 2, dtype=F32) / MLA_ROPE)
    ang = positions.astype(F32).reshape(-1)[:, None] * inv_freq
    cos, sin = jnp.cos(ang), jnp.sin(ang)
    t = ang.shape[0]
    cosf = jnp.concatenate([jnp.ones((t, MLA_NOPE), F32), cos, cos, jnp.zeros((t, 32), F32)], axis=1)
    sinf = jnp.concatenate([jnp.zeros((t, MLA_NOPE), F32), -sin, sin, jnp.zeros((t, 32), F32)], axis=1)
    return cosf, sinf


def kernel(x, c, positions, ada_w, ada_b, norm_mix, norm_ffn, w_in, hg_lb_logits, hg_onorm, mla_q_norm, mla_kv_norm, mla_w_uq, mla_w_ukv, mla_qk_norm, diff_qk_norm, diff_lam, diff_onorm, swa_qk_norm, swa_sinks, w_branch, w_out, router_w, router_bias, moe_w_gate, moe_w_up, moe_w_down):
    batch, seq, d = x.shape
    x2 = x.reshape(batch * seq, d)
    cosf, sinf = _rope_tables(positions)
    lb_all = jnp.cumsum(jax.nn.softmax(hg_lb_logits.astype(F32), axis=0), axis=0)
    lb_all = lb_all - lb_all[0]
    mod = _modulation(c, ada_w, ada_b)
    rw = jnp.concatenate([router_w, jnp.zeros((d, LANE - N_EXPERTS), F32)], axis=1)
    rb = router_bias.astype(F32)[:, None]

    for l in range(DEPTH):
        sh1, sc1, gt1, sh2, sc2, gt2 = [mod[l, :, d * k:d * (k + 1)][:, None, :] for k in range(6)]
        p = _layer_params(l, w_in, hg_onorm, mla_q_norm, mla_kv_norm, mla_w_uq, mla_w_ukv, mla_qk_norm,
                          diff_qk_norm, swa_qk_norm, swa_sinks, lb_all)
        hg3, hf, mla, diff, swa, h = _inproj(x2, norm_mix[l][None, :], sh1, sc1, p["w1"], seq)
        y_a = _hgrn(hg3, hf, p["loglb"], p["log1mlb"], p["ogain"], batch, seq)
        qm, km, vmt, qd, kd, vdt, qs, ks, vst = _prep(mla, diff, swa, cosf, sinf, p)
        y_b = _mla_attn(qm, km, vmt, batch, seq)
        lam_init = 0.8 - 0.6 * math.exp(-0.3 * l)
        y_c = _diff_attn(qd, kd, vdt, diff_lam[l], diff_onorm[l][None, :], lam_init, batch, seq)
        y_d = _swa(qs, ks, vst, p["sinks"], batch, seq)
        x2, h2, comb, dest, meta = _merge(h, (y_a, y_b, y_c, y_d), x2, gt1, p["wg"], w_branch[l].astype(BF16),
                                          w_out[l].astype(BF16), norm_ffn[l][None, :], sh2, sc2, rw, rb, seq)
        x2 = _moe(h2, comb, dest, meta[:, :, 0], x2, gt2, moe_w_gate[l].astype(BF16),
                  moe_w_up[l].astype(BF16), moe_w_down[l].astype(BF16), seq)
    return x2.reshape(batch, seq, d)
```

```python
import functools
import math

import jax
import jax.numpy as jnp
from jax import lax
from jax.experimental import pallas as pl
from jax.experimental.pallas import tpu as pltpu

F32 = jnp.float32
BF16 = jnp.bfloat16

D_MODEL = 1024
DEPTH = 2
EPS = 1e-6
N_BRANCH = 4
HG_HEADS = 8
HG_DK = 64
HG_W = HG_HEADS * HG_DK
HG_SUB = 16
HG_CHUNK = 64
HG_SAFE_DECAY = 80.0
HG_DIAG_ROWS =(HG_SUB // 2) * HG_SUB + (HG_SUB // 2) ** 2
MLA_HEADS = 8
MLA_Q_RANK = 256
MLA_KV_RANK = 128
MLA_NOPE = 64
MLA_ROPE = 32
MLA_V = 64
ROPE_BASE = 10000.0
DIFF_HEADS = 4
DIFF_QK = 64
DIFF_V = 128
SWA_Q_HEADS = 8
SWA_KV_HEADS = 2
SWA_WINDOW = 128
HEAD_DIM = 64
N_EXPERTS = 16
N_GROUPS = 4
D_FF_EXPERT = 256
IN_SPLITS = (512, 512, 512, 512, 256, 128, 32, 512, 512, 512, 512, 128, 128, 4096)

MOE_TILE = 512
MOE_ALIGN = 16
MOE_CHUNK = 160
MOE_SORT_ROWS = 768
MOE_DEST_LANE = N_EXPERTS
LANE = 128
ATT_BLK = 256
SWA_QB = 4
LOG2E = 1.4426950408889634
NEG = -1e30
VMEM_LIMIT = 56 * 1024 * 1024


def _cp(sem, vmem=VMEM_LIMIT):
    return pltpu.CompilerParams(dimension_semantics=sem, vmem_limit_bytes=vmem)


def _nt(a, b):
    return lax.dot_general(a, b, (((1,), (1,)), ((), ())), preferred_element_type=F32)


def _tn(a, b):
    return lax.dot_general(a, b, (((0,), (0,)), ((), ())), preferred_element_type=F32)


def _dot(a, b):
    return jnp.dot(a, b, preferred_element_type=F32)


def _split2(x):
    hi = x.astype(BF16)
    lo = (x - hi.astype(F32)).astype(BF16)
    return hi, lo


def _split3(x):
    hi = x.astype(BF16)
    r = x - hi.astype(F32)
    mid = r.astype(BF16)
    lo = (r - mid.astype(F32)).astype(BF16)
    return hi, mid, lo


def _seg_id(idx, seg):
    shift = seg.bit_length() - 1
    assert 1 << shift == seg
    return lax.shift_right_logical(idx, shift)


def _same_seg(n, seg):
    r = lax.broadcasted_iota(jnp.int32, (n, n), 0)
    c = lax.broadcasted_iota(jnp.int32, (n, n), 1)
    return _seg_id(r, seg) == _seg_id(c, seg)


def _seg_ones(n, seg):
    return _same_seg(n, seg).astype(BF16)


def _seg_mean_sq(x, seg):
    n = x.shape[-1]
    hi, lo = _split2(x * x)
    ones = _seg_ones(n, seg)
    return (_dot(hi, ones) + _dot(lo, ones)) * (1.0 / seg)


def _silu(x):
    return x * jax.nn.sigmoid(x)


def _mod_kernel(c_ref, w_ref, b_ref, o_ref):
    c = c_ref[...]
    o_ref[0] = jnp.dot(_silu(c), w_ref[0], preferred_element_type=F32,
                       precision=lax.Precision.HIGHEST) + b_ref[0]


def _modulation(c, ada_w, ada_b):
    nl, d, n6 = ada_w.shape
    b = c.shape[0]
    tn = 1536
    return pl.pallas_call(
        _mod_kernel,
        grid=(nl, n6 // tn),
        in_specs=[pl.BlockSpec((b, d), lambda l, j: (0, 0)),
                  pl.BlockSpec((1, d, tn), lambda l, j: (l, 0, j)),
                  pl.BlockSpec((1, 1, tn), lambda l, j: (l, 0, j))],
        out_specs=pl.BlockSpec((1, b, tn), lambda l, j: (l, 0, j)),
        out_shape=jax.ShapeDtypeStruct((nl, b, n6), F32),
        compiler_params=_cp(("arbitrary", "arbitrary")),
        name="modulation",
    )(c, ada_w, ada_b.reshape(nl, 1, n6))


W1_COLS = 4864


def _inproj_kernel(x_ref, g_ref, sh_ref, sc_ref, w_ref,
                   ohg_ref, ohf_ref, omla_ref, odiff_ref, oswa_ref, oh_ref):
    x = x_ref[...]
    ms = jnp.mean(x * x, axis=-1, keepdims=True)
    h = x * lax.rsqrt(ms + EPS) * g_ref[...]
    h = h * (1.0 + sc_ref[0]) + sh_ref[0]
    hb = h.astype(BF16)
    oh_ref[...] = hb

    def proj(lo, hi):
        return _dot(hb, w_ref[:, lo:hi])

    for k in range(3):
        ohg_ref[:, 512 * k:512 * (k + 1)] = proj(512 * k, 512 * (k + 1)).astype(BF16)
    ohf_ref[...] = proj(1536, 2048)
    omla_ref[...] = proj(2048, 2560).astype(BF16)
    for k in range(3):
        odiff_ref[:, 512 * k:512 * (k + 1)] = proj(2560 + 512 * k, 3072 + 512 * k).astype(BF16)
    oswa_ref[:, 0:512] = proj(4096, 4608).astype(BF16)
    oswa_ref[:, 512:768] = proj(4608, 4864).astype(BF16)


def _inproj(x2, gain, sh, sc, w1, seq):
    t, d = x2.shape
    tm = 512
    tpb = seq // tm
    row = lambda i: (i, 0)
    per_b = lambda i: (i // tpb, 0, 0)
    outs = [(1536, BF16), (512, F32), (512, BF16), (1536, BF16), (768, BF16), (d, BF16)]
    return pl.pallas_call(
        _inproj_kernel,
        grid=(t // tm,),
        in_specs=[pl.BlockSpec((tm, d), row),
                  pl.BlockSpec((1, d), lambda i: (0, 0)),
                  pl.BlockSpec((1, 1, d), per_b),
                  pl.BlockSpec((1, 1, d), per_b),
                  pl.BlockSpec((d, W1_COLS), lambda i: (0, 0))],
        out_specs=[pl.BlockSpec((tm, w), row) for w, _ in outs],
        out_shape=[jax.ShapeDtypeStruct((t, w), dt) for w, dt in outs],
        compiler_params=_cp(("arbitrary",)),
        name="inproj",
    )(x2, gain, sh, sc, w1)


def _segment_cumsum(x, seg):
    n = x.shape[0]
    r = lax.broadcasted_iota(jnp.int32, (n, n), 0)
    cc = lax.broadcasted_iota(jnp.int32, (n, n), 1)
    same = _same_seg(n, seg)
    tri = (same & (cc <= r)).astype(BF16)
    blk = same.astype(BF16)
    parts = _split3(x)
    c = _dot(tri, parts[0]) + _dot(tri, parts[1]) + _dot(tri, parts[2])
    tot = _dot(blk, parts[0]) + _dot(blk, parts[1]) + _dot(blk, parts[2])
    return c, tot


def _hgrn_chunk_path(i_ref, st_ref, c_s, tot_s, qs_s, kk_s, qe_s, kd_s, ke_s, dec_s, od_s):
    rows_blk = c_s.shape[0]
    tot = tot_s[...]
    rel = c_s[...] - 0.5 * tot
    half_dec = jnp.exp(0.5 * tot)
    kd = kk_s[...] * jnp.exp(-rel)
    qe_s[...] = (qs_s[...] * jnp.exp(rel)).astype(BF16)
    kd_s[...] = kd.astype(BF16)
    ke_s[...] = (kd * half_dec).astype(BF16)
    dec_s[...] = half_dec

    row = lax.broadcasted_iota(jnp.int32, (2 * rows_blk, rows_blk), 0) & (rows_blk - 1)
    col = lax.broadcasted_iota(jnp.int32, (2 * rows_blk, rows_blk), 1)
    intra = (_seg_id(row, HG_CHUNK) == _seg_id(col, HG_CHUNK)) & (col <= row)
    low_q = lax.broadcasted_iota(jnp.int32, (rows_blk, LANE), 1) < HG_DK
    low_c = lax.broadcasted_iota(jnp.int32, (HG_CHUNK, LANE), 1) < HG_DK

    for j in range(HG_W // LANE):
        cols = slice(LANE * j, LANE * (j + 1))
        qe = qe_s[:, cols]
        zero = jnp.zeros_like(qe)
        q2 = jnp.concatenate([jnp.where(low_q, qe, zero), jnp.where(low_q, zero, qe)], axis=0)
        attn = jnp.where(intra, _nt(q2, kd_s[:, cols]), 0.0).astype(BF16)
        o2 = _dot(attn, i_ref[:, cols])
        n_chunks = rows_blk // HG_CHUNK
        chunk_rows = [slice(HG_CHUNK * ch, HG_CHUNK * (ch + 1)) for ch in range(n_chunks)]
        upd = [_tn(i_ref[rows, cols], ke_s[rows, cols]) for rows in chunk_rows]
        st = st_ref[j]
        for ch, rows in enumerate(chunk_rows):
            rows_hi = slice(rows_blk + HG_CHUNK * ch, rows_blk + HG_CHUNK * (ch + 1))
            hd = dec_s[HG_CHUNK * ch:HG_CHUNK * ch + 1, cols]
            inter = _nt(jnp.concatenate([q2[rows], q2[rows_hi]], axis=0), (st * hd).astype(BF16))
            od_s[rows, cols] = jnp.where(low_c, o2[rows] + inter[:HG_CHUNK], o2[rows_hi] + inter[HG_CHUNK:])
            st = st * (hd * hd) + upd[ch]
        st_ref[j] = st


def _hgrn_exact_path(i_ref, st_ref, lf_s, c_s, qs_s, kk_s, qe_s, ke_s, dec_s, od_s, t_s, a_s):
    rows_blk = c_s.shape[0]
    n_sub = rows_blk // HG_SUB
    c, tot = _segment_cumsum(lf_s[...], HG_SUB)
    c_s[...] = c
    qe_s[...] = (qs_s[...] * jnp.exp(c)).astype(BF16)
    ke_s[...] = (kk_s[...] * jnp.exp(tot - c)).astype(BF16)
    dec_s[...] = jnp.exp(tot)

    same_head = _same_seg(LANE, HG_DK)
    head_mask = same_head.astype(F32)
    head_ones = same_head.astype(BF16)
    for j in range(HG_W // LANE):
        st_ref[j] = st_ref[j] * head_mask
    half = HG_SUB // 2
    trow = lax.broadcasted_iota(jnp.int32, (half, HG_W), 0)

    def body(i, carry):
        r0 = pl.multiple_of(i * HG_SUB, HG_SUB)
        rows = pl.ds(r0, HG_SUB)
        c_i = c_s[rows, :]
        qs_i = qs_s[rows, :]
        kk_i = kk_s[rows, :]
        v_i = i_ref[rows, :].astype(F32)
        c_lo, c_hi = c_i[:half], c_i[half:]
        q_lo, q_hi = qs_i[:half], qs_i[half:]
        for s in range(half):
            c_row, k_row = c_i[s:s + 1, :], kk_i[s:s + 1, :]
            e_lo = jnp.exp(jnp.where(trow >= s, c_lo - c_row, NEG))
            e_hi = jnp.exp(c_hi - c_row)
            both = jnp.concatenate([e_lo * q_lo, e_hi * q_hi], axis=0) * k_row
            t_s[s * HG_SUB:(s + 1) * HG_SUB, :] = both.astype(BF16)
        for s in range(half, HG_SUB, 2):
            pair = []
            for u in (s, s + 1):
                e_hi = jnp.exp(jnp.where(trow >= u - half, c_hi - c_i[u:u + 1, :], NEG))
                pair.append(e_hi * q_hi * kk_i[u:u + 1, :])
            base = half * HG_SUB + (s - half) * half
            t_s[base:base + HG_SUB, :] = jnp.concatenate(pair, axis=0).astype(BF16)
        for j in range(HG_W // LANE):
            cols = slice(LANE * j, LANE * (j + 1))
            a_s[:, cols] = _dot(t_s[:, cols], head_ones)
        acc_lo = jnp.zeros((half, HG_W), F32)
        acc_hi = jnp.zeros((half, HG_W), F32)
        for s in range(half):
            acc_lo = acc_lo + a_s[s * HG_SUB:s * HG_SUB + half, :] * v_i[s:s + 1, :]
            acc_hi = acc_hi + a_s[s * HG_SUB + half:(s + 1) * HG_SUB, :] * v_i[s:s + 1, :]
        for s in range(half, HG_SUB):
            base = half * HG_SUB + (s - half) * half
            acc_hi = acc_hi + a_s[base:base + half, :] * v_i[s:s + 1, :]
        acc = jnp.concatenate([acc_lo, acc_hi], axis=0)
        for j in range(HG_W // LANE):
            cols = slice(LANE * j, LANE * (j + 1))
            st = st_ref[j]
            o_int = _nt(qe_s[rows, cols], st.astype(BF16))
            upd = _tn(i_ref[rows, cols], ke_s[rows, cols])
            st_ref[j] = st * dec_s[pl.ds(r0, 1), cols] + upd * head_mask
            od_s[rows, cols] = acc[:, cols] + o_int
        return carry

    lax.fori_loop(0, n_sub, body, 0)


def _hgrn_kernel(q_ref, i_ref, g_ref, f_ref, loglb_ref, log1mlb_ref, og_ref, o_ref,
                 st_ref, lf_s, c_s, tot_s, qs_s, kk_s, qe_s, kd_s, ke_s, dec_s, od_s, t_s, a_s):
    @pl.when(pl.program_id(1) == 0)
    def _():
        st_ref[...] = jnp.zeros_like(st_ref)

    fr = f_ref[...]
    ls = jnp.minimum(fr, 0.0) - jnp.log(1.0 + jnp.exp(-jnp.abs(fr)))
    a = loglb_ref[...]
    c2 = log1mlb_ref[...] + ls
    lf = jnp.maximum(a, c2) + jnp.log(1.0 + jnp.exp(-jnp.abs(a - c2)))
    lf_s[...] = lf
    qs_s[...] = _silu(q_ref[...].astype(F32))
    kk_s[...] = 1.0 - jnp.exp(lf)
    c, tot = _segment_cumsum(lf, HG_CHUNK)
    c_s[...] = c
    tot_s[...] = tot
    safe = 0.5 * jnp.max(-tot) <= HG_SAFE_DECAY

    @pl.when(safe)
    def _():
        _hgrn_chunk_path(i_ref, st_ref, c_s, tot_s, qs_s, kk_s, qe_s, kd_s, ke_s, dec_s, od_s)

    @pl.when(jnp.logical_not(safe))
    def _():
        _hgrn_exact_path(i_ref, st_ref, lf_s, c_s, qs_s, kk_s, qe_s, ke_s, dec_s, od_s, t_s, a_s)

    o = od_s[...]
    ms = _seg_mean_sq(o, HG_DK)
    on = o * lax.rsqrt(ms + EPS) * og_ref[...]
    o_ref[...] = (on * _silu(g_ref[...].astype(F32))).astype(BF16)


def _hgrn(hg3, hf, loglb, log1mlb, ogain, batch, seq):
    t = hf.shape[0]
    rb = 256
    nb = seq // rb
    blk = lambda k: pl.BlockSpec((rb, HG_W), lambda b, n, k=k: (b * nb + n, k))
    vec = pl.BlockSpec((1, HG_W), lambda b, n: (0, 0))
    f32_blk = pltpu.VMEM((rb, HG_W), F32)
    bf16_blk = pltpu.VMEM((rb, HG_W), BF16)
    return pl.pallas_call(
        _hgrn_kernel,
        grid=(batch, nb),
        in_specs=[blk(0), blk(1), blk(2), blk(0), vec, vec, vec],
        out_specs=blk(0),
        out_shape=jax.ShapeDtypeStruct((t, HG_W), BF16),
        scratch_shapes=[pltpu.VMEM((HG_W // LANE, LANE, LANE), F32),
                        f32_blk, f32_blk, f32_blk, f32_blk, f32_blk,
                        bf16_blk, bf16_blk, bf16_blk,
                        f32_blk, f32_blk,
                        pltpu.VMEM((HG_DIAG_ROWS, HG_W), BF16),
                        pltpu.VMEM((HG_DIAG_ROWS, HG_W), F32)],
        compiler_params=_cp(("arbitrary", "arbitrary")),
        name="hgrn2",
    )(hg3, hg3, hg3, hf, loglb, log1mlb, ogain)


def _store_transposed_blocks(out_ref, v):
    blk = out_ref.shape[2]
    for u in range(out_ref.shape[0]):
        out_ref[u] = v[u * blk:(u + 1) * blk, :].T.astype(BF16)


def _prep_kernel(mla_ref, diff_ref, swa_ref, cos_ref, sin_ref,
                 qng_ref, kvg_ref, wqa_ref, wqb_ref, wka_ref, wkb_ref, wv_ref,
                 gq_ref, gqs_ref, gk_ref, gks_ref, dgq_ref, dgk_ref, sgq_ref, sgk_ref,
                 qm_ref, km_ref, vmt_ref, qd_ref, kd_ref, vdt_ref, qs_ref, ks_ref, vst_ref):
    blk = mla_ref[...].astype(F32)
    cq = blk[:, :MLA_Q_RANK]
    rest = blk[:, MLA_Q_RANK:]
    cqn = cq * lax.rsqrt(jnp.mean(cq * cq, axis=-1, keepdims=True) + EPS) * qng_ref[...]
    lane = lax.broadcasted_iota(jnp.int32, rest.shape, 1)
    is_kv = lane < MLA_KV_RANK
    ms_kv = jnp.sum(jnp.where(is_kv, rest * rest, 0.0), axis=-1, keepdims=True) * (1.0 / MLA_KV_RANK)
    restn = jnp.where(is_kv, rest * lax.rsqrt(ms_kv + EPS) * kvg_ref[...], rest)
    cqb = cqn.astype(BF16)
    rb = restn.astype(BF16)
    qa = _dot(cqb, wqa_ref[...])
    qb = _dot(cqb, wqb_ref[...])
    ka = _dot(rb, wka_ref[...])
    kb = _dot(rb, wkb_ref[...])
    _store_transposed_blocks(vmt_ref, _dot(rb, wv_ref[...]))
    cosf = cos_ref[...]
    sinf = sin_ref[...]
    cq_t = cosf * gq_ref[...]
    sq_t = sinf * gqs_ref[...]
    ck_t = cosf * gk_ref[...]
    sk_t = sinf * gks_ref[...]
    inv_n = 1.0 / (MLA_NOPE + MLA_ROPE)
    scale = (MLA_NOPE + MLA_ROPE) ** -0.5 * LOG2E
    for h in range(MLA_HEADS):
        cols = slice(LANE * h, LANE * (h + 1))
        x = qa[:, cols]
        rinv = lax.rsqrt(jnp.sum(x * x, axis=-1, keepdims=True) * inv_n + EPS)
        qm_ref[:, cols] = ((x * cq_t + qb[:, cols] * sq_t) * (rinv * scale)).astype(BF16)
        y = ka[:, cols]
        rinv = lax.rsqrt(jnp.sum(y * y, axis=-1, keepdims=True) * inv_n + EPS)
        km_ref[:, cols] = ((y * ck_t + kb[:, cols] * sk_t) * rinv).astype(BF16)

    def seg_norm(x, gain, scale):
        return x * lax.rsqrt(_seg_mean_sq(x, HEAD_DIM) + EPS) * (gain * scale)

    dq = diff_ref[:, 0:512].astype(F32)
    dk = diff_ref[:, 512:1024].astype(F32)
    qd_ref[...] = seg_norm(dq, dgq_ref[...], DIFF_QK ** -0.5 * LOG2E).astype(BF16)
    kd_ref[...] = seg_norm(dk, dgk_ref[...], 1.0).astype(BF16)
    _store_transposed_blocks(vdt_ref, diff_ref[:, 1024:1536].astype(F32))

    sq = swa_ref[:, 0:512].astype(F32)
    qs_ref[...] = seg_norm(sq, sgq_ref[...], HEAD_DIM ** -0.5 * LOG2E).astype(BF16)
    skv = swa_ref[:, 512:768].astype(F32)
    kn = seg_norm(skv[:, :LANE], sgk_ref[...], 1.0)
    low = lax.broadcasted_iota(jnp.int32, kn.shape, 1) < HEAD_DIM
    sw = pltpu.roll(kn, HEAD_DIM, 1)
    ks_ref[:, :LANE] = jnp.where(low, kn, sw).astype(BF16)
    ks_ref[:, LANE:] = jnp.where(low, sw, kn).astype(BF16)
    _store_transposed_blocks(vst_ref, skv[:, LANE:])


def _prep(mla, diff, swa, cosf, sinf, p):
    t = mla.shape[0]
    tm = 512
    row = lambda i: (i, 0)
    full = lambda a: pl.BlockSpec(a.shape, lambda i: (0,) * a.ndim)
    consts = [p["qng"], p["kvg"], p["wqa"], p["wqb"], p["wka"], p["wkb"], p["wv"],
              p["gq"], p["gqs"], p["gk"], p["gks"], p["dgq"], p["dgk"], p["sgq"], p["sgk"]]
    def rows_out(w):
        return pl.BlockSpec((tm, w), row), jax.ShapeDtypeStruct((t, w), BF16)

    def transposed_out(n, blk):
        return (pl.BlockSpec((tm // blk, n, blk), lambda i: (i, 0, 0)),
                jax.ShapeDtypeStruct((t // blk, n, blk), BF16))

    outs = [rows_out(1024), rows_out(1024), transposed_out(512, ATT_BLK),
            rows_out(512), rows_out(512), transposed_out(512, ATT_BLK),
            rows_out(512), rows_out(256), transposed_out(LANE, SWA_WINDOW)]
    return pl.pallas_call(
        _prep_kernel,
        grid=(t // tm,),
        in_specs=[pl.BlockSpec((tm, 512), row),
                  pl.BlockSpec((tm, 1536), row),
                  pl.BlockSpec((tm, 768), row),
                  pl.BlockSpec((tm, LANE), row),
                  pl.BlockSpec((tm, LANE), row)] + [full(a) for a in consts],
        out_specs=[o[0] for o in outs],
        out_shape=[o[1] for o in outs],
        compiler_params=_cp(("arbitrary",)),
        name="attn_prep",
    )(mla, diff, swa, cosf, sinf, *consts)


def _causal_t(blk):
    key = lax.broadcasted_iota(jnp.int32, (blk, blk), 0)
    qry = lax.broadcasted_iota(jnp.int32, (blk, blk), 1)
    return key <= qry


def _two_pass_attention(n_sets, score_fn, value_fn, s_scr, acc_scr, blk):
    qi = pl.program_id(1)
    causal = _causal_t(blk)

    def scores(ki, m, masked):
        out = []
        for i in range(n_sets):
            s = score_fn(i, ki)
            if masked:
                s = jnp.where(causal, s, NEG)
            s_scr[i, ki] = s
            out.append(jnp.maximum(m[i], jnp.max(s, axis=0, keepdims=True)))
        return tuple(out)

    m = tuple(jnp.full((1, blk), NEG, F32) for _ in range(n_sets))
    m = lax.fori_loop(0, qi // 2, lambda kp, c: scores(2 * kp + 1, scores(2 * kp, c, False), False), m)
    m = lax.cond(qi % 2 == 1, lambda c: scores(qi - 1, c, False), lambda c: c, m)
    m = scores(qi, m, True)

    acc_scr[...] = jnp.zeros_like(acc_scr)

    def accumulate(kis, l):
        out = []
        for i in range(n_sets):
            li, pv = l[i], None
            for ki in kis:
                p = jnp.exp2(s_scr[i, ki] - m[i])
                li = li + jnp.sum(p, axis=0, keepdims=True)
                term = _dot(value_fn(i, ki), p.astype(BF16))
                pv = term if pv is None else pv + term
            out.append(li)
            acc_scr[i] += pv
        return tuple(out)

    n_blk = qi + 1
    l = tuple(jnp.zeros((1, blk), F32) for _ in range(n_sets))
    l = lax.fori_loop(0, n_blk // 2, lambda kp, c: accumulate((2 * kp, 2 * kp + 1), c), l)
    return lax.cond(n_blk % 2 == 1, lambda c: accumulate((n_blk - 1,), c), lambda c: c, l)


def _mla_attn_kernel(q_ref, k_ref, vt_ref, o_ref, s_scr, acc_scr):
    blk = q_ref.shape[0]

    def score_fn(h, ki):
        rows = pl.ds(pl.multiple_of(ki * blk, blk), blk)
        cols = slice(LANE * h, LANE * (h + 1))
        return _nt(k_ref[rows, cols], q_ref[:, cols])

    def value_fn(h, ki):
        return vt_ref[ki, MLA_V * h:MLA_V * (h + 1), :]

    l = _two_pass_attention(MLA_HEADS, score_fn, value_fn, s_scr, acc_scr, blk)
    for j in range(MLA_HEADS // 2):
        o_t = jnp.concatenate([acc_scr[2 * j] / l[2 * j], acc_scr[2 * j + 1] / l[2 * j + 1]], axis=0)
        o_ref[:, LANE * j:LANE * (j + 1)] = o_t.T.astype(BF16)


def _mla_attn(qm, km, vmt, batch, seq):
    t = qm.shape[0]
    nq = seq // ATT_BLK
    return pl.pallas_call(
        _mla_attn_kernel,
        grid=(batch, nq),
        in_specs=[pl.BlockSpec((ATT_BLK, 1024), lambda b, i: (b * nq + i, 0)),
                  pl.BlockSpec((seq, 1024), lambda b, i: (b, 0)),
                  pl.BlockSpec((nq, 512, ATT_BLK), lambda b, i: (b, 0, 0))],
        out_specs=pl.BlockSpec((ATT_BLK, 512), lambda b, i: (b * nq + i, 0)),
        out_shape=jax.ShapeDtypeStruct((t, 512), BF16),
        scratch_shapes=[pltpu.VMEM((MLA_HEADS, nq, ATT_BLK, ATT_BLK), F32),
                        pltpu.VMEM((MLA_HEADS, MLA_V, ATT_BLK), F32)],
        compiler_params=_cp(("arbitrary", "arbitrary")),
        name="mla_attn",
    )(qm, km, vmt)


def _diff_attn_kernel(q_ref, k_ref, vt_ref, lam_ref, og_ref, o_ref, s_scr, acc_scr, qm_scr, *, lam_init):
    blk = q_ref.shape[0]
    low = lax.broadcasted_iota(jnp.int32, (blk, LANE), 1) < DIFF_QK
    lp = lam_ref[...]
    lam = (jnp.exp(jnp.sum(lp[0:1] * lp[1:2], axis=-1, keepdims=True))
           - jnp.exp(jnp.sum(lp[2:3] * lp[3:4], axis=-1, keepdims=True)) + lam_init)

    for h in range(DIFF_HEADS):
        qt = q_ref[:, LANE * h:LANE * (h + 1)]
        zero = jnp.zeros_like(qt)
        qm_scr[2 * h] = jnp.where(low, qt, zero)
        qm_scr[2 * h + 1] = jnp.where(low, zero, qt)

    def score_fn(i, ki):
        rows = pl.ds(pl.multiple_of(ki * blk, blk), blk)
        h = i // 2
        return _nt(k_ref[rows, LANE * h:LANE * (h + 1)], qm_scr[i])

    def value_fn(i, ki):
        h = i // 2
        return vt_ref[ki, DIFF_V * h:DIFF_V * (h + 1), :]

    l = _two_pass_attention(2 * DIFF_HEADS, score_fn, value_fn, s_scr, acc_scr, blk)
    for h in range(DIFF_HEADS):
        o_t = acc_scr[2 * h] / l[2 * h] - lam * (acc_scr[2 * h + 1] / l[2 * h + 1])
        on_t = o_t * lax.rsqrt(jnp.mean(o_t * o_t, axis=0, keepdims=True) + EPS)
        o_ref[:, LANE * h:LANE * (h + 1)] = (on_t.T * (og_ref[...] * (1.0 - lam_init))).astype(BF16)


def _diff_attn(qd, kd, vdt, lam_p, og, lam_init, batch, seq):
    t = qd.shape[0]
    nq = seq // ATT_BLK
    return pl.pallas_call(
        functools.partial(_diff_attn_kernel, lam_init=lam_init),
        grid=(batch, nq),
        in_specs=[pl.BlockSpec((ATT_BLK, 512), lambda b, i: (b * nq + i, 0)),
                  pl.BlockSpec((seq, 512), lambda b, i: (b, 0)),
                  pl.BlockSpec((nq, 512, ATT_BLK), lambda b, i: (b, 0, 0)),
                  pl.BlockSpec(lam_p.shape, lambda b, i: (0, 0)),
                  pl.BlockSpec(og.shape, lambda b, i: (0, 0))],
        out_specs=pl.BlockSpec((ATT_BLK, 512), lambda b, i: (b * nq + i, 0)),
        out_shape=jax.ShapeDtypeStruct((t, 512), BF16),
        scratch_shapes=[pltpu.VMEM((2 * DIFF_HEADS, nq, ATT_BLK, ATT_BLK), F32),
                        pltpu.VMEM((2 * DIFF_HEADS, DIFF_V, ATT_BLK), F32),
                        pltpu.VMEM((2 * DIFF_HEADS, ATT_BLK, LANE), BF16)],
        compiler_params=_cp(("arbitrary", "arbitrary")),
        name="diff_attn",
    )(qd, kd, vdt, lam_p, og)


def _swa_kernel(q_ref, kp_ref, kc_ref, vtp_ref, vtc_ref, sink_ref, o_ref):
    w = SWA_WINDOW
    grp = SWA_Q_HEADS // SWA_KV_HEADS
    n = pl.program_id(1)
    key = lax.broadcasted_iota(jnp.int32, (2 * w, grp * w), 0)
    qry = lax.broadcasted_iota(jnp.int32, (2 * w, grp * w), 1) & (w - 1)
    cur_ok = (key >= w) & (key - w <= qry)
    prev_ok = (key < w) & (key > qry)
    low = lax.broadcasted_iota(jnp.int32, (w, LANE), 1) < HEAD_DIM

    for t in range(q_ref.shape[0] // w):
        rows = slice(t * w, (t + 1) * w)
        if t == 0:
            kp, vtp = kp_ref[...], vtp_ref[0]
            valid = cur_ok | (prev_ok & (n > 0))
        else:
            kp, vtp = kc_ref[(t - 1) * w:t * w, :], vtc_ref[t - 1]
            valid = cur_ok | prev_ok
        kc, vtc = kc_ref[rows, :], vtc_ref[t]
        for kv in range(SWA_KV_HEADS):
            kcols = slice(LANE * kv, LANE * (kv + 1))
            vrows = slice(HEAD_DIM * kv, HEAD_DIM * (kv + 1))
            k_win = jnp.concatenate([kp[:, kcols], kc[:, kcols]], axis=0)
            parts = []
            for u in range(2):
                qt = q_ref[rows, LANE * (2 * kv + u):LANE * (2 * kv + u + 1)]
                zero = jnp.zeros_like(qt)
                parts += [jnp.where(low, qt, zero), jnp.where(low, zero, qt)]
            s = jnp.where(valid, _nt(k_win, jnp.concatenate(parts, axis=0)), NEG)
            sink = sink_ref[:, grp * w * kv:grp * w * (kv + 1)]
            m = jnp.maximum(jnp.max(s, axis=0, keepdims=True), sink)
            p = jnp.exp2(s - m)
            den = jnp.sum(p, axis=0, keepdims=True) + jnp.exp2(sink - m)
            vt_win = jnp.concatenate([vtp[vrows, :], vtc[vrows, :]], axis=1)
            o_t = _dot(vt_win, p.astype(BF16)) / den
            for u in range(2):
                pair = jnp.concatenate([o_t[:, 2 * u * w:(2 * u + 1) * w],
                                        o_t[:, (2 * u + 1) * w:(2 * u + 2) * w]], axis=0)
                o_ref[rows, LANE * (2 * kv + u):LANE * (2 * kv + u + 1)] = pair.T.astype(BF16)


def _swa(qs, ks, vst, sink_row, batch, seq):
    t = qs.shape[0]
    w = SWA_WINDOW
    nb = seq // w
    ns = nb // SWA_QB
    cur = lambda b, n: (b * ns + n, 0)
    cur3 = lambda b, n: (b * ns + n, 0, 0)
    prev = lambda b, n: (b * nb + jnp.maximum(n * SWA_QB - 1, 0), 0)
    prev3 = lambda b, n: (b * nb + jnp.maximum(n * SWA_QB - 1, 0), 0, 0)
    return pl.pallas_call(
        _swa_kernel,
        grid=(batch, ns),
        in_specs=[pl.BlockSpec((SWA_QB * w, 512), cur),
                  pl.BlockSpec((w, 256), prev), pl.BlockSpec((SWA_QB * w, 256), cur),
                  pl.BlockSpec((1, LANE, w), prev3), pl.BlockSpec((SWA_QB, LANE, w), cur3),
                  pl.BlockSpec(sink_row.shape, lambda b, n: (0, 0))],
        out_specs=pl.BlockSpec((SWA_QB * w, 512), cur),
        out_shape=jax.ShapeDtypeStruct((t, 512), BF16),
        compiler_params=_cp(("arbitrary", "arbitrary")),
        name="swa_attn",
    )(qs, ks, ks, vst, vst, sink_row)


def _merge_kernel(h_ref, ya_ref, yb_ref, yc_ref, yd_ref, x_ref, gt1_ref, wg_ref, wb_ref, wo_ref,
                  g2_ref, sh2_ref, sc2_ref, rw_ref, rb_ref, xo_ref, h2_ref, comb_ref, dest_ref, meta_ref):
    h = h_ref[...]
    d = x_ref.shape[1]
    merged = None
    for b, y_ref in enumerate((ya_ref, yb_ref, yc_ref, yd_ref)):
        gate = jax.nn.sigmoid(_dot(h, wg_ref[:, d * b:d * (b + 1)]))
        term = gate * _dot(y_ref[...], wb_ref[b])
        merged = term if merged is None else merged + term
    xn = x_ref[...] + gt1_ref[0] * _dot(merged.astype(BF16), wo_ref[...])
    xo_ref[...] = xn
    ms = jnp.mean(xn * xn, axis=-1, keepdims=True)
    h2 = xn * lax.rsqrt(ms + EPS) * g2_ref[...]
    h2 = h2 * (1.0 + sc2_ref[0]) + sh2_ref[0]
    h2_ref[...] = h2.astype(BF16)

    hh, hm, _ = _split3(h2)
    wh, wm = _split2(rw_ref[...])
    logits = _dot(hh, wh) + _dot(hm, wh) + _dot(hh, wm)
    lt = logits.T
    scores = jax.nn.sigmoid(lt[0:N_EXPERTS, :])
    sel = scores + rb_ref[...]
    per = N_EXPERTS // N_GROUPS
    srow = [sel[e:e + 1, :] for e in range(N_EXPERTS)]
    gsum = []
    for g in range(N_GROUPS):
        a, b_, c, e_ = srow[per * g:per * (g + 1)]
        gsum.append(jnp.maximum(jnp.maximum(jnp.maximum(a + b_, a + c), jnp.maximum(a + e_, b_ + c)),
                                jnp.maximum(b_ + e_, c + e_)))
    best = jnp.maximum(jnp.maximum(gsum[0], gsum[1]), jnp.maximum(gsum[2], gsum[3]))
    taken = None
    rows = []
    picks = []
    for g in range(N_GROUPS):
        hit = gsum[g] == best
        pick = hit if taken is None else hit & jnp.logical_not(taken)
        taken = hit if taken is None else taken | hit
        picks.append(pick.astype(F32))
        for e in range(per * g, per * (g + 1)):
            rank = jnp.zeros_like(best)
            for o in range(per * g, per * (g + 1)):
                if o == e:
                    continue
                ahead = (srow[o] > srow[e]) | ((srow[o] == srow[e]) & (o < e))
                rank = rank + ahead.astype(F32)
            rows.append(jnp.where(pick & (rank < 1.5), scores[e:e + 1, :], 0.0))
    wsum = rows[0]
    for r_ in rows[1:]:
        wsum = wsum + r_
    inv = 1.0 / wsum
    rid = lax.broadcasted_iota(jnp.int32, scores.shape, 0)
    comb_e = jnp.zeros_like(scores)
    for e, r_ in enumerate(rows):
        comb_e = jnp.where(rid == e, r_ * inv, comb_e)

    tm = lt.shape[1]
    gid = lax.broadcasted_iota(jnp.int32, (8, tm), 0)
    onehot = jnp.zeros((8, tm), F32)
    for g in range(N_GROUPS):
        onehot = jnp.where(gid == g, picks[g], onehot)
    before = (lax.broadcasted_iota(jnp.int32, (tm, tm), 0) < lax.broadcasted_iota(jnp.int32, (tm, tm), 1))
    rank = _dot(onehot.astype(BF16), before.astype(BF16))
    dest = jnp.zeros((1, tm), F32)
    off = jnp.zeros((1, 1), F32)
    meta = jnp.zeros((8, LANE), F32)
    mrow = lax.broadcasted_iota(jnp.int32, (8, LANE), 0)
    for g in range(N_GROUPS):
        cnt = jnp.sum(picks[g], axis=-1, keepdims=True)
        dest = dest + picks[g] * (off + rank[g:g + 1, :])
        meta = jnp.where(mrow == g, cnt, meta)
        meta = jnp.where(mrow == N_GROUPS + g, off, meta)
        off = off + jnp.ceil(cnt * (1.0 / MOE_ALIGN)) * MOE_ALIGN
    dest_ref[0] = dest.astype(jnp.int32)
    meta_ref[0] = meta.astype(jnp.int32)
    pad = jnp.zeros((LANE - N_EXPERTS - 8, tm), F32)
    comb_t = jnp.concatenate([comb_e, jnp.where(gid == 0, dest, 0.0), pad], axis=0)
    comb_ref[...] = comb_t.T


def _merge(h, ys, x2, gt1, wg, wb, wo, g2, sh2, sc2, rw, rb, seq):
    t, d = x2.shape
    tm = MOE_TILE
    tpb = seq // tm
    row = lambda i: (i, 0)
    per_b = lambda i: (i // tpb, 0, 0)
    c2 = lambda i: (0, 0)
    return pl.pallas_call(
        _merge_kernel,
        grid=(t // tm,),
        in_specs=[pl.BlockSpec((tm, d), row)] + [pl.BlockSpec((tm, 512), row)] * 4
                 + [pl.BlockSpec((tm, d), row), pl.BlockSpec((1, 1, d), per_b),
                    pl.BlockSpec(wg.shape, c2), pl.BlockSpec(wb.shape, lambda i: (0, 0, 0)),
                    pl.BlockSpec(wo.shape, c2), pl.BlockSpec((1, d), c2),
                    pl.BlockSpec((1, 1, d), per_b), pl.BlockSpec((1, 1, d), per_b),
                    pl.BlockSpec(rw.shape, c2), pl.BlockSpec(rb.shape, c2)],
        out_specs=[pl.BlockSpec((tm, d), row), pl.BlockSpec((tm, d), row), pl.BlockSpec((tm, LANE), row),
                   pl.BlockSpec((1, 1, tm), lambda i: (i, 0, 0)), pl.BlockSpec((1, 8, LANE), lambda i: (i, 0, 0))],
        out_shape=[jax.ShapeDtypeStruct((t, d), F32), jax.ShapeDtypeStruct((t, d), BF16),
                   jax.ShapeDtypeStruct((t, LANE), F32),
                   jax.ShapeDtypeStruct((t // tm, 1, tm), jnp.int32),
                   jax.ShapeDtypeStruct((t // tm, 8, LANE), jnp.int32)],
        compiler_params=_cp(("arbitrary",)),
        name="merge_router",
    )(h, *ys, x2, gt1, wg, wb, wo, g2, sh2, sc2, rw, rb)


def _moe_kernel(meta_ref, h2_ref, comb_ref, dest_ref, x_ref, gt2_ref, wg_ref, wu_ref, wd_ref, o_ref,
                sorted_s, csort_s, out_s):
    i = pl.program_id(0)
    tm = h2_ref.shape[0]
    n_rows = sorted_s.shape[0]
    per = N_EXPERTS // N_GROUPS
    comb = comb_ref[...]

    place = (lax.broadcasted_iota(jnp.int32, (n_rows, tm), 0) == dest_ref[0]).astype(BF16)
    sorted_s[...] = _dot(place, h2_ref[...]).astype(BF16)
    c_hi, c_lo = _split2(comb)
    csort_s[...] = _dot(place, c_hi) + _dot(place, c_lo)
    out_s[...] = jnp.zeros_like(out_s)
    lane = lax.broadcasted_iota(jnp.int32, (MOE_CHUNK, LANE), 1)

    for g in range(N_GROUPS):
        cnt = meta_ref[i, g]
        off = meta_ref[i, N_GROUPS + g]

        def chunk(c, carry):
            rows = pl.ds(pl.multiple_of(off + c * MOE_CHUNK, MOE_ALIGN), MOE_CHUNK)
            xk = sorted_s[rows, :]
            cw = csort_s[rows, :]
            acc = None
            for e in range(per * g, per * (g + 1)):
                hid = _silu(_dot(xk, wg_ref[e])) * _dot(xk, wu_ref[e])
                ce = jnp.sum(jnp.where(lane == e, cw, 0.0), axis=-1, keepdims=True)
                term = _dot((hid * ce).astype(BF16), wd_ref[e])
                acc = term if acc is None else acc + term
            out_s[rows, :] += acc
            return carry

        lax.fori_loop(0, (cnt + MOE_CHUNK - 1) // MOE_CHUNK, chunk, 0)

    dest_col = comb[:, MOE_DEST_LANE:MOE_DEST_LANE + 1].astype(jnp.int32)
    back = (lax.broadcasted_iota(jnp.int32, (tm, n_rows), 1) == dest_col).astype(BF16)
    o_ref[...] = x_ref[...] + gt2_ref[0] * _dot(back, out_s[...].astype(BF16))


def _moe(h2, comb, dest, meta, x2, gt2, wg, wu, wd, seq):
    t, d = x2.shape
    tm = MOE_TILE
    tpb = seq // tm
    n_rows = MOE_SORT_ROWS
    assert tm + N_GROUPS * (MOE_ALIGN - 1) + MOE_CHUNK - 1 <= n_rows
    row = lambda i, m: (i, 0)
    whole = lambda a: pl.BlockSpec(a.shape, lambda i, m: (0, 0, 0), pipeline_mode=pl.Buffered(1))
    grid_spec = pltpu.PrefetchScalarGridSpec(
        num_scalar_prefetch=1,
        grid=(t // tm,),
        in_specs=[pl.BlockSpec((tm, d), row), pl.BlockSpec((tm, LANE), row),
                  pl.BlockSpec((1, 1, tm), lambda i, m: (i, 0, 0)),
                  pl.BlockSpec((tm, d), row),
                  pl.BlockSpec((1, 1, d), lambda i, m: (i // tpb, 0, 0)),
                  whole(wg), whole(wu), whole(wd)],
        out_specs=pl.BlockSpec((tm, d), row),
        scratch_shapes=[pltpu.VMEM((n_rows, d), BF16), pltpu.VMEM((n_rows, LANE), F32),
                        pltpu.VMEM((n_rows, d), F32)])
    return pl.pallas_call(
        _moe_kernel,
        grid_spec=grid_spec,
        out_shape=jax.ShapeDtypeStruct((t, d), F32),
        compiler_params=_cp(("arbitrary",)),
        name="moe",
    )(meta, h2, comb, dest, x2, gt2, wg, wu, wd)


def _split_cols(w):
    out, o = [], 0
    for n in IN_SPLITS:
        out.append(w[:, o:o + n])
        o += n
    return out


def _layer_params(l, w_in, hg_onorm, mla_q_norm, mla_kv_norm, mla_w_uq, mla_w_ukv, mla_qk_norm,
                  diff_qk_norm, swa_qk_norm, swa_sinks, lb_all):
    (hq, hf, hi, hgate, mcq, mckv, mkr, dq, dk, dv, sq, sk, sv, gates) = _split_cols(w_in[l])
    d = w_in.shape[1]
    w1 = jnp.concatenate([hq, hi, hgate, hf, mcq, mckv, mkr, jnp.zeros((d, 96), F32),
                          dq, dk, dv, sq, sk, sv], axis=1).astype(BF16)
    p = {"w1": w1, "wg": gates.astype(BF16)}

    lb = lb_all[l]
    p["loglb"] = jnp.log(lb)[None, :]
    p["log1mlb"] = jnp.log1p(-lb)[None, :]
    p["ogain"] = jnp.tile(hg_onorm[l], HG_HEADS)[None, :]

    hd = MLA_NOPE + MLA_ROPE
    half = MLA_ROPE // 2
    wq = mla_w_uq[l].reshape(MLA_Q_RANK, MLA_HEADS, hd)
    z = lambda r, n: jnp.zeros((r, MLA_HEADS, n), F32)
    nope, rope = wq[:, :, :MLA_NOPE], wq[:, :, MLA_NOPE:]
    p["wqa"] = jnp.concatenate([nope, rope, z(MLA_Q_RANK, 32)], -1).reshape(MLA_Q_RANK, -1).astype(BF16)
    p["wqb"] = jnp.concatenate([z(MLA_Q_RANK, MLA_NOPE), rope[:, :, half:], rope[:, :, :half],
                                z(MLA_Q_RANK, 32)], -1).reshape(MLA_Q_RANK, -1).astype(BF16)
    wkv = mla_w_ukv[l].reshape(MLA_KV_RANK, MLA_HEADS, MLA_NOPE + MLA_V)
    knope, vproj = wkv[:, :, :MLA_NOPE], wkv[:, :, MLA_NOPE:]
    eye = jnp.eye(MLA_ROPE, dtype=F32)
    swap = jnp.concatenate([eye[:, half:], eye[:, :half]], axis=1)
    place = lambda m: jnp.broadcast_to(
        jnp.concatenate([jnp.zeros((MLA_ROPE, MLA_NOPE), F32), m, jnp.zeros((MLA_ROPE, 32), F32)], -1)[:, None, :],
        (MLA_ROPE, MLA_HEADS, LANE))
    pad_rows = 256 - MLA_KV_RANK - MLA_ROPE
    p["wka"] = jnp.concatenate([jnp.concatenate([knope, z(MLA_KV_RANK, 64)], -1), place(eye),
                                z(pad_rows, LANE)], 0).reshape(256, -1).astype(BF16)
    p["wkb"] = jnp.concatenate([z(MLA_KV_RANK, LANE), place(swap), z(pad_rows, LANE)], 0
                               ).reshape(256, -1).astype(BF16)
    p["wv"] = jnp.concatenate([vproj.reshape(MLA_KV_RANK, -1),
                               jnp.zeros((256 - MLA_KV_RANK, MLA_HEADS * MLA_V), F32)], 0).astype(BF16)
    p["qng"] = mla_q_norm[l][None, :]
    p["kvg"] = jnp.concatenate([mla_kv_norm[l], jnp.ones((256 - MLA_KV_RANK,), F32)])[None, :]

    def rope_gains(g):
        base = jnp.concatenate([g, jnp.zeros((LANE - hd,), F32)])
        part = jnp.concatenate([jnp.zeros((MLA_NOPE,), F32), g[MLA_NOPE + half:], g[MLA_NOPE:MLA_NOPE + half],
                                jnp.zeros((LANE - hd,), F32)])
        return base[None, :], part[None, :]

    p["gq"], p["gqs"] = rope_gains(mla_qk_norm[l, 0])
    p["gk"], p["gks"] = rope_gains(mla_qk_norm[l, 1])
    p["dgq"] = jnp.tile(diff_qk_norm[l, 0], 8)[None, :]
    p["dgk"] = jnp.tile(diff_qk_norm[l, 1], 8)[None, :]
    p["sgq"] = jnp.tile(swa_qk_norm[l, 0], 8)[None, :]
    p["sgk"] = jnp.tile(swa_qk_norm[l, 1], 2)[None, :]
    p["sinks"] = jnp.repeat(swa_sinks[l].astype(F32) * LOG2E, SWA_WINDOW)[None, :]
    return p


def _rope_tables(positions):
    inv_freq = ROPE_BASE ** (-jnp.arange(0, MLA_ROPE, 2, dtype=F32) / MLA_ROPE)
    ang = positions.astype(F32).reshape(-1)[:, None] * inv_freq
    cos, sin = jnp.cos(ang), jnp.sin(ang)
    t = ang.shape[0]
    cosf = jnp.concatenate([jnp.ones((t, MLA_NOPE), F32), cos, cos, jnp.zeros((t, 32), F32)], axis=1)
    sinf = jnp.concatenate([jnp.zeros((t, MLA_NOPE), F32), -sin, sin, jnp.zeros((t, 32), F32)], axis=1)
    return cosf, sinf


def kernel(x, c, positions, ada_w, ada_b, norm_mix, norm_ffn, w_in, hg_lb_logits, hg_onorm, mla_q_norm, mla_kv_norm, mla_w_uq, mla_w_ukv, mla_qk_norm, diff_qk_norm, diff_lam, diff_onorm, swa_qk_norm, swa_sinks, w_branch, w_out, router_w, router_bias, moe_w_gate, moe_w_up, moe_w_down):
    batch, seq, d = x.shape
    x2 = x.reshape(batch * seq, d)
    cosf, sinf = _rope_tables(positions)
    lb_all = jnp.cumsum(jax.nn.softmax(hg_lb_logits.astype(F32), axis=0), axis=0)
    lb_all = lb_all - lb_all[0]
    mod = _modulation(c, ada_w, ada_b)
    rw = jnp.concatenate([router_w, jnp.zeros((d, LANE - N_EXPERTS), F32)], axis=1)
    rb = router_bias.astype(F32)[:, None]

    for l in range(DEPTH):
        sh1, sc1, gt1, sh2, sc2, gt2 = [mod[l, :, d * k:d * (k + 1)][:, None, :] for k in range(6)]
        p = _layer_params(l, w_in, hg_onorm, mla_q_norm, mla_kv_norm, mla_w_uq, mla_w_ukv, mla_qk_norm,
                          diff_qk_norm, swa_qk_norm, swa_sinks, lb_all)
        hg3, hf, mla, diff, swa, h = _inproj(x2, norm_mix[l][None, :], sh1, sc1, p["w1"], seq)
        y_a = _hgrn(hg3, hf, p["loglb"], p["log1mlb"], p["ogain"], batch, seq)
        qm, km, vmt, qd, kd, vdt, qs, ks, vst = _prep(mla, diff, swa, cosf, sinf, p)
        y_b = _mla_attn(qm, km, vmt, batch, seq)
        lam_init = 0.8 - 0.6 * math.exp(-0.3 * l)
        y_c = _diff_attn(qd, kd, vdt, diff_lam[l], diff_onorm[l][None, :], lam_init, batch, seq)
        y_d = _swa(qs, ks, vst, p["sinks"], batch, seq)
        x2, h2, comb, dest, meta = _merge(h, (y_a, y_b, y_c, y_d), x2, gt1, p["wg"], w_branch[l].astype(BF16),
                                          w_out[l].astype(BF16), norm_ffn[l][None, :], sh2, sc2, rw, rb, seq)
        x2 = _moe(h2, comb, dest, meta[:, :, 0], x2, gt2, moe_w_gate[l].astype(BF16),
                  moe_w_up[l].astype(BF16), moe_w_down[l].astype(BF16), seq)
    return x2.reshape(batch, seq, d)
```

```python
import functools
import math

import jax
import jax.numpy as jnp
from jax import lax
from jax.experimental import pallas as pl
from jax.experimental.pallas import tpu as pltpu

F32 = jnp.float32
BF16 = jnp.bfloat16

D_MODEL = 1024
DEPTH = 2
EPS = 1e-6
N_BRANCH = 4
HG_HEADS = 8
HG_DK = 64
HG_W = HG_HEADS * HG_DK
HG_SUB = 16
HG_CHUNK = 64
HG_SAFE_DECAY = 80.0
HG_DIAG_ROWS =(HG_SUB // 2) * HG_SUB + (HG_SUB // 2) ** 2
MLA_HEADS = 8
MLA_Q_RANK = 256
MLA_KV_RANK = 128
MLA_NOPE = 64
MLA_ROPE = 32
MLA_V = 64
ROPE_BASE = 10000.0
DIFF_HEADS = 4
DIFF_QK = 64
DIFF_V = 128
SWA_Q_HEADS = 8
SWA_KV_HEADS = 2
SWA_WINDOW = 128
HEAD_DIM = 64
N_EXPERTS = 16
N_GROUPS = 4
D_FF_EXPERT = 256
IN_SPLITS = (512, 512, 512, 512, 256, 128, 32, 512, 512, 512, 512, 128, 128, 4096)

MOE_TILE = 512
MERGE_SUBTILES = 1
MOE_ALIGN = 16
MOE_CHUNKS = (64, 128, 192, 256)
MOE_SORT_ROWS = 640
MOE_DEST_LANE = N_EXPERTS
LANE = 128
ATT_BLK = 256
SWA_QB = 4
LOG2E = 1.4426950408889634
NEG = -1e30
VMEM_LIMIT = 56 * 1024 * 1024


def _cp(sem, vmem=VMEM_LIMIT):
    return pltpu.CompilerParams(dimension_semantics=sem, vmem_limit_bytes=vmem)


def _nt(a, b):
    return lax.dot_general(a, b, (((1,), (1,)), ((), ())), preferred_element_type=F32)


def _tn(a, b):
    return lax.dot_general(a, b, (((0,), (0,)), ((), ())), preferred_element_type=F32)


def _dot(a, b):
    return jnp.dot(a, b, preferred_element_type=F32)


def _split2(x):
    hi = x.astype(BF16)
    lo = (x - hi.astype(F32)).astype(BF16)
    return hi, lo


def _split3(x):
    hi = x.astype(BF16)
    r = x - hi.astype(F32)
    mid = r.astype(BF16)
    lo = (r - mid.astype(F32)).astype(BF16)
    return hi, mid, lo


def _seg_id(idx, seg):
    shift = seg.bit_length() - 1
    assert 1 << shift == seg
    return lax.shift_right_logical(idx, shift)


def _same_seg(n, seg):
    r = lax.broadcasted_iota(jnp.int32, (n, n), 0)
    c = lax.broadcasted_iota(jnp.int32, (n, n), 1)
    return _seg_id(r, seg) == _seg_id(c, seg)


def _seg_ones(n, seg):
    return _same_seg(n, seg).astype(BF16)


def _seg_mean_sq(x, seg):
    n = x.shape[-1]
    hi, lo = _split2(x * x)
    ones = _seg_ones(n, seg)
    return (_dot(hi, ones) + _dot(lo, ones)) * (1.0 / seg)


def _silu(x):
    return x * jax.nn.sigmoid(x)


def _mod_kernel(c_ref, w_ref, b_ref, o_ref):
    c = c_ref[...]
    o_ref[0] = jnp.dot(_silu(c), w_ref[0], preferred_element_type=F32,
                       precision=lax.Precision.HIGHEST) + b_ref[0]


def _modulation(c, ada_w, ada_b):
    nl, d, n6 = ada_w.shape
    b = c.shape[0]
    tn = 1536
    return pl.pallas_call(
        _mod_kernel,
        grid=(nl, n6 // tn),
        in_specs=[pl.BlockSpec((b, d), lambda l, j: (0, 0)),
                  pl.BlockSpec((1, d, tn), lambda l, j: (l, 0, j)),
                  pl.BlockSpec((1, 1, tn), lambda l, j: (l, 0, j))],
        out_specs=pl.BlockSpec((1, b, tn), lambda l, j: (l, 0, j)),
        out_shape=jax.ShapeDtypeStruct((nl, b, n6), F32),
        compiler_params=_cp(("arbitrary", "arbitrary")),
        name="modulation",
    )(c, ada_w, ada_b.reshape(nl, 1, n6))


def _inproj_kernel(x_ref, g_ref, sh_ref, sc_ref, whg_ref, wmla_ref, wdiff_ref, wswa_ref,
                   ohg_ref, ohf_ref, omla_ref, odiff_ref, oswa_ref, oh_ref):
    x = x_ref[...]
    ms = jnp.mean(x * x, axis=-1, keepdims=True)
    h = x * lax.rsqrt(ms + EPS) * g_ref[...]
    h = h * (1.0 + sc_ref[0]) + sh_ref[0]
    hb = h.astype(BF16)
    oh_ref[...] = hb

    ohg_ref[:, 0:512] = _dot(hb, whg_ref[:, 0:512]).astype(BF16)
    ohf_ref[...] = _dot(hb, whg_ref[:, 512:1024])
    ohg_ref[:, 512:1024] = _dot(hb, whg_ref[:, 1024:1536]).astype(BF16)
    ohg_ref[:, 1024:1536] = _dot(hb, whg_ref[:, 1536:2048]).astype(BF16)
    omla_ref[...] = _dot(hb, wmla_ref[...]).astype(BF16)
    for k in range(3):
        odiff_ref[:, 512 * k:512 * (k + 1)] = _dot(hb, wdiff_ref[:, 512 * k:512 * (k + 1)]).astype(BF16)
    oswa_ref[:, 0:512] = _dot(hb, wswa_ref[:, 0:512]).astype(BF16)
    oswa_ref[:, 512:768] = _dot(hb, wswa_ref[:, 512:768]).astype(BF16)


def _inproj(x2, gain, sh, sc, weights, seq):
    t, d = x2.shape
    tm = 512
    tpb = seq // tm
    row = lambda i: (i, 0)
    per_b = lambda i: (i // tpb, 0, 0)
    outs = [(1536, BF16), (512, F32), (512, BF16), (1536, BF16), (768, BF16), (d, BF16)]
    return pl.pallas_call(
        _inproj_kernel,
        grid=(t // tm,),
        in_specs=[pl.BlockSpec((tm, d), row),
                  pl.BlockSpec((1, d), lambda i: (0, 0)),
                  pl.BlockSpec((1, 1, d), per_b),
                  pl.BlockSpec((1, 1, d), per_b)]
                 + [pl.BlockSpec(w.shape, lambda i: (0, 0)) for w in weights],
        out_specs=[pl.BlockSpec((tm, w), row) for w, _ in outs],
        out_shape=[jax.ShapeDtypeStruct((t, w), dt) for w, dt in outs],
        compiler_params=_cp(("arbitrary",)),
        name="inproj",
    )(x2, gain, sh, sc, *weights)


def _segment_cumsum(x, seg):
    n = x.shape[0]
    r = lax.broadcasted_iota(jnp.int32, (n, n), 0)
    cc = lax.broadcasted_iota(jnp.int32, (n, n), 1)
    same = _same_seg(n, seg)
    tri = (same & (cc <= r)).astype(BF16)
    blk = same.astype(BF16)
    parts = _split3(x)
    c = _dot(tri, parts[0]) + _dot(tri, parts[1]) + _dot(tri, parts[2])
    tot = _dot(blk, parts[0]) + _dot(blk, parts[1]) + _dot(blk, parts[2])
    return c, tot


def _hgrn_chunk_path(i_ref, st_ref, c_s, tot_s, qs_s, kk_s, qe_s, kd_s, ke_s, dec_s, od_s):
    rows_blk = c_s.shape[0]
    tot = tot_s[...]
    rel = c_s[...] - 0.5 * tot
    half_dec = jnp.exp(0.5 * tot)
    kd = kk_s[...] * jnp.exp(-rel)
    qe_s[...] = (qs_s[...] * jnp.exp(rel)).astype(BF16)
    kd_s[...] = kd.astype(BF16)
    ke_s[...] = (kd * half_dec).astype(BF16)
    dec_s[...] = half_dec

    row = lax.broadcasted_iota(jnp.int32, (2 * rows_blk, rows_blk), 0) & (rows_blk - 1)
    col = lax.broadcasted_iota(jnp.int32, (2 * rows_blk, rows_blk), 1)
    intra = (_seg_id(row, HG_CHUNK) == _seg_id(col, HG_CHUNK)) & (col <= row)
    low_q = lax.broadcasted_iota(jnp.int32, (rows_blk, LANE), 1) < HG_DK
    low_c = lax.broadcasted_iota(jnp.int32, (HG_CHUNK, LANE), 1) < HG_DK

    for j in range(HG_W // LANE):
        cols = slice(LANE * j, LANE * (j + 1))
        qe = qe_s[:, cols]
        zero = jnp.zeros_like(qe)
        q2 = jnp.concatenate([jnp.where(low_q, qe, zero), jnp.where(low_q, zero, qe)], axis=0)
        attn = jnp.where(intra, _nt(q2, kd_s[:, cols]), 0.0).astype(BF16)
        o2 = _dot(attn, i_ref[:, cols])
        n_chunks = rows_blk // HG_CHUNK
        chunk_rows = [slice(HG_CHUNK * ch, HG_CHUNK * (ch + 1)) for ch in range(n_chunks)]
        upd = [_tn(i_ref[rows, cols], ke_s[rows, cols]) for rows in chunk_rows]
        st = st_ref[j]
        for ch, rows in enumerate(chunk_rows):
            rows_hi = slice(rows_blk + HG_CHUNK * ch, rows_blk + HG_CHUNK * (ch + 1))
            hd = dec_s[HG_CHUNK * ch:HG_CHUNK * ch + 1, cols]
            inter = _nt(jnp.concatenate([q2[rows], q2[rows_hi]], axis=0), (st * hd).astype(BF16))
            od_s[rows, cols] = jnp.where(low_c, o2[rows] + inter[:HG_CHUNK], o2[rows_hi] + inter[HG_CHUNK:])
            st = st * (hd * hd) + upd[ch]
        st_ref[j] = st


def _hgrn_exact_path(i_ref, st_ref, lf_s, c_s, qs_s, kk_s, qe_s, ke_s, dec_s, od_s, t_s, a_s):
    rows_blk = c_s.shape[0]
    n_sub = rows_blk // HG_SUB
    c, tot = _segment_cumsum(lf_s[...], HG_SUB)
    c_s[...] = c
    qe_s[...] = (qs_s[...] * jnp.exp(c)).astype(BF16)
    ke_s[...] = (kk_s[...] * jnp.exp(tot - c)).astype(BF16)
    dec_s[...] = jnp.exp(tot)

    same_head = _same_seg(LANE, HG_DK)
    head_mask = same_head.astype(F32)
    head_ones = same_head.astype(BF16)
    for j in range(HG_W // LANE):
        st_ref[j] = st_ref[j] * head_mask
    half = HG_SUB // 2
    trow = lax.broadcasted_iota(jnp.int32, (half, HG_W), 0)

    def body(i, carry):
        r0 = pl.multiple_of(i * HG_SUB, HG_SUB)
        rows = pl.ds(r0, HG_SUB)
        c_i = c_s[rows, :]
        qs_i = qs_s[rows, :]
        kk_i = kk_s[rows, :]
        v_i = i_ref[rows, :].astype(F32)
        c_lo, c_hi = c_i[:half], c_i[half:]
        q_lo, q_hi = qs_i[:half], qs_i[half:]
        for s in range(half):
            c_row, k_row = c_i[s:s + 1, :], kk_i[s:s + 1, :]
            e_lo = jnp.exp(jnp.where(trow >= s, c_lo - c_row, NEG))
            e_hi = jnp.exp(c_hi - c_row)
            both = jnp.concatenate([e_lo * q_lo, e_hi * q_hi], axis=0) * k_row
            t_s[s * HG_SUB:(s + 1) * HG_SUB, :] = both.astype(BF16)
        for s in range(half, HG_SUB, 2):
            pair = []
            for u in (s, s + 1):
                e_hi = jnp.exp(jnp.where(trow >= u - half, c_hi - c_i[u:u + 1, :], NEG))
                pair.append(e_hi * q_hi * kk_i[u:u + 1, :])
            base = half * HG_SUB + (s - half) * half
            t_s[base:base + HG_SUB, :] = jnp.concatenate(pair, axis=0).astype(BF16)
        for j in range(HG_W // LANE):
            cols = slice(LANE * j, LANE * (j + 1))
            a_s[:, cols] = _dot(t_s[:, cols], head_ones)
        acc_lo = jnp.zeros((half, HG_W), F32)
        acc_hi = jnp.zeros((half, HG_W), F32)
        for s in range(half):
            acc_lo = acc_lo + a_s[s * HG_SUB:s * HG_SUB + half, :] * v_i[s:s + 1, :]
            acc_hi = acc_hi + a_s[s * HG_SUB + half:(s + 1) * HG_SUB, :] * v_i[s:s + 1, :]
        for s in range(half, HG_SUB):
            base = half * HG_SUB + (s - half) * half
            acc_hi = acc_hi + a_s[base:base + half, :] * v_i[s:s + 1, :]
        acc = jnp.concatenate([acc_lo, acc_hi], axis=0)
        for j in range(HG_W // LANE):
            cols = slice(LANE * j, LANE * (j + 1))
            st = st_ref[j]
            o_int = _nt(qe_s[rows, cols], st.astype(BF16))
            upd = _tn(i_ref[rows, cols], ke_s[rows, cols])
            st_ref[j] = st * dec_s[pl.ds(r0, 1), cols] + upd * head_mask
            od_s[rows, cols] = acc[:, cols] + o_int
        return carry

    lax.fori_loop(0, n_sub, body, 0)


def _hgrn_kernel(q_ref, i_ref, g_ref, f_ref, loglb_ref, log1mlb_ref, og_ref, o_ref,
                 st_ref, lf_s, c_s, tot_s, qs_s, kk_s, qe_s, kd_s, ke_s, dec_s, od_s, t_s, a_s):
    @pl.when(pl.program_id(1) == 0)
    def _():
        st_ref[...] = jnp.zeros_like(st_ref)

    fr = f_ref[...]
    ls = jnp.minimum(fr, 0.0) - jnp.log(1.0 + jnp.exp(-jnp.abs(fr)))
    a = loglb_ref[...]
    c2 = log1mlb_ref[...] + ls
    lf = jnp.maximum(a, c2) + jnp.log(1.0 + jnp.exp(-jnp.abs(a - c2)))
    lf_s[...] = lf
    qs_s[...] = _silu(q_ref[...].astype(F32))
    kk_s[...] = 1.0 - jnp.exp(lf)
    c, tot = _segment_cumsum(lf, HG_CHUNK)
    c_s[...] = c
    tot_s[...] = tot
    safe = 0.5 * jnp.max(-tot) <= HG_SAFE_DECAY

    @pl.when(safe)
    def _():
        _hgrn_chunk_path(i_ref, st_ref, c_s, tot_s, qs_s, kk_s, qe_s, kd_s, ke_s, dec_s, od_s)

    @pl.when(jnp.logical_not(safe))
    def _():
        _hgrn_exact_path(i_ref, st_ref, lf_s, c_s, qs_s, kk_s, qe_s, ke_s, dec_s, od_s, t_s, a_s)

    o = od_s[...]
    ms = _seg_mean_sq(o, HG_DK)
    on = o * lax.rsqrt(ms + EPS) * og_ref[...]
    o_ref[...] = (on * _silu(g_ref[...].astype(F32))).astype(BF16)


def _hgrn(hg3, hf, loglb, log1mlb, ogain, batch, seq):
    t = hf.shape[0]
    rb = 256
    nb = seq // rb
    blk = lambda k: pl.BlockSpec((rb, HG_W), lambda b, n, k=k: (b * nb + n, k))
    vec = pl.BlockSpec((1, HG_W), lambda b, n: (0, 0))
    f32_blk = pltpu.VMEM((rb, HG_W), F32)
    bf16_blk = pltpu.VMEM((rb, HG_W), BF16)
    return pl.pallas_call(
        _hgrn_kernel,
        grid=(batch, nb),
        in_specs=[blk(0), blk(1), blk(2), blk(0), vec, vec, vec],
        out_specs=blk(0),
        out_shape=jax.ShapeDtypeStruct((t, HG_W), BF16),
        scratch_shapes=[pltpu.VMEM((HG_W // LANE, LANE, LANE), F32),
                        f32_blk, f32_blk, f32_blk, f32_blk, f32_blk,
                        bf16_blk, bf16_blk, bf16_blk,
                        f32_blk, f32_blk,
                        pltpu.VMEM((HG_DIAG_ROWS, HG_W), BF16),
                        pltpu.VMEM((HG_DIAG_ROWS, HG_W), F32)],
        compiler_params=_cp(("arbitrary", "arbitrary")),
        name="hgrn2",
    )(hg3, hg3, hg3, hf, loglb, log1mlb, ogain)


def _store_transposed_blocks(out_ref, v):
    blk = out_ref.shape[2]
    for u in range(out_ref.shape[0]):
        out_ref[u] = v[u * blk:(u + 1) * blk, :].T.astype(BF16)


def _prep_kernel(mla_ref, diff_ref, swa_ref, cos_ref, sin_ref,
                 qng_ref, kvg_ref, wqa_ref, wqb_ref, wka_ref, wkb_ref, wv_ref,
                 gq_ref, gqs_ref, gk_ref, gks_ref, dgq_ref, dgk_ref, sgq_ref, sgk_ref,
                 qm_ref, km_ref, vmt_ref, qd_ref, kd_ref, vdt_ref, qs_ref, ks_ref, vst_ref):
    blk = mla_ref[...].astype(F32)
    cq = blk[:, :MLA_Q_RANK]
    rest = blk[:, MLA_Q_RANK:]
    cqn = cq * lax.rsqrt(jnp.mean(cq * cq, axis=-1, keepdims=True) + EPS) * qng_ref[...]
    lane = lax.broadcasted_iota(jnp.int32, rest.shape, 1)
    is_kv = lane < MLA_KV_RANK
    ms_kv = jnp.sum(jnp.where(is_kv, rest * rest, 0.0), axis=-1, keepdims=True) * (1.0 / MLA_KV_RANK)
    restn = jnp.where(is_kv, rest * lax.rsqrt(ms_kv + EPS) * kvg_ref[...], rest)
    cqb = cqn.astype(BF16)
    rb = restn.astype(BF16)
    qa = _dot(cqb, wqa_ref[...])
    qb = _dot(cqb, wqb_ref[...])
    ka = _dot(rb, wka_ref[...])
    kb = _dot(rb, wkb_ref[...])
    _store_transposed_blocks(vmt_ref, _dot(rb, wv_ref[...]))
    cosf = cos_ref[...]
    sinf = sin_ref[...]
    cq_t = cosf * gq_ref[...]
    sq_t = sinf * gqs_ref[...]
    ck_t = cosf * gk_ref[...]
    sk_t = sinf * gks_ref[...]
    inv_n = 1.0 / (MLA_NOPE + MLA_ROPE)
    scale = (MLA_NOPE + MLA_ROPE) ** -0.5 * LOG2E
    for h in range(MLA_HEADS):
        cols = slice(LANE * h, LANE * (h + 1))
        x = qa[:, cols]
        rinv = lax.rsqrt(jnp.sum(x * x, axis=-1, keepdims=True) * inv_n + EPS)
        qm_ref[:, cols] = ((x * cq_t + qb[:, cols] * sq_t) * (rinv * scale)).astype(BF16)
        y = ka[:, cols]
        rinv = lax.rsqrt(jnp.sum(y * y, axis=-1, keepdims=True) * inv_n + EPS)
        km_ref[:, cols] = ((y * ck_t + kb[:, cols] * sk_t) * rinv).astype(BF16)

    def seg_norm(x, gain, scale):
        return x * lax.rsqrt(_seg_mean_sq(x, HEAD_DIM) + EPS) * (gain * scale)

    dq = diff_ref[:, 0:512].astype(F32)
    dk = diff_ref[:, 512:1024].astype(F32)
    qd_ref[...] = seg_norm(dq, dgq_ref[...], DIFF_QK ** -0.5 * LOG2E).astype(BF16)
    kd_ref[...] = seg_norm(dk, dgk_ref[...], 1.0).astype(BF16)
    _store_transposed_blocks(vdt_ref, diff_ref[:, 1024:1536].astype(F32))

    sq = swa_ref[:, 0:512].astype(F32)
    qs_ref[...] = seg_norm(sq, sgq_ref[...], HEAD_DIM ** -0.5 * LOG2E).astype(BF16)
    skv = swa_ref[:, 512:768].astype(F32)
    kn = seg_norm(skv[:, :LANE], sgk_ref[...], 1.0)
    low = lax.broadcasted_iota(jnp.int32, kn.shape, 1) < HEAD_DIM
    sw = pltpu.roll(kn, HEAD_DIM, 1)
    ks_ref[:, :LANE] = jnp.where(low, kn, sw).astype(BF16)
    ks_ref[:, LANE:] = jnp.where(low, sw, kn).astype(BF16)
    _store_transposed_blocks(vst_ref, skv[:, LANE:])


def _prep(mla, diff, swa, cosf, sinf, p):
    t = mla.shape[0]
    tm = 512
    row = lambda i: (i, 0)
    full = lambda a: pl.BlockSpec(a.shape, lambda i: (0,) * a.ndim)
    consts = [p["qng"], p["kvg"], p["wqa"], p["wqb"], p["wka"], p["wkb"], p["wv"],
              p["gq"], p["gqs"], p["gk"], p["gks"], p["dgq"], p["dgk"], p["sgq"], p["sgk"]]
    def rows_out(w):
        return pl.BlockSpec((tm, w), row), jax.ShapeDtypeStruct((t, w), BF16)

    def transposed_out(n, blk):
        return (pl.BlockSpec((tm // blk, n, blk), lambda i: (i, 0, 0)),
                jax.ShapeDtypeStruct((t // blk, n, blk), BF16))

    outs = [rows_out(1024), rows_out(1024), transposed_out(512, ATT_BLK),
            rows_out(512), rows_out(512), transposed_out(512, ATT_BLK),
            rows_out(512), rows_out(256), transposed_out(LANE, SWA_WINDOW)]
    return pl.pallas_call(
        _prep_kernel,
        grid=(t // tm,),
        in_specs=[pl.BlockSpec((tm, 512), row),
                  pl.BlockSpec((tm, 1536), row),
                  pl.BlockSpec((tm, 768), row),
                  pl.BlockSpec((tm, LANE), row),
                  pl.BlockSpec((tm, LANE), row)] + [full(a) for a in consts],
        out_specs=[o[0] for o in outs],
        out_shape=[o[1] for o in outs],
        compiler_params=_cp(("arbitrary",)),
        name="attn_prep",
    )(mla, diff, swa, cosf, sinf, *consts)


def _causal_t(blk):
    key = lax.broadcasted_iota(jnp.int32, (blk, blk), 0)
    qry = lax.broadcasted_iota(jnp.int32, (blk, blk), 1)
    return key <= qry


def _two_pass_attention(n_sets, score_fn, value_fn, s_scr, acc_scr, blk):
    qi = pl.program_id(1)
    causal = _causal_t(blk)

    def scores(ki, m, masked):
        out = []
        for i in range(n_sets):
            s = score_fn(i, ki)
            if masked:
                s = jnp.where(causal, s, NEG)
            s_scr[i, ki] = s
            out.append(jnp.maximum(m[i], jnp.max(s, axis=0, keepdims=True)))
        return tuple(out)

    m = tuple(jnp.full((1, blk), NEG, F32) for _ in range(n_sets))
    m = lax.fori_loop(0, qi // 2, lambda kp, c: scores(2 * kp + 1, scores(2 * kp, c, False), False), m)
    m = lax.cond(qi % 2 == 1, lambda c: scores(qi - 1, c, False), lambda c: c, m)
    m = scores(qi, m, True)

    acc_scr[...] = jnp.zeros_like(acc_scr)

    def accumulate(kis, l):
        out = []
        for i in range(n_sets):
            li, pv = l[i], None
            for ki in kis:
                p = jnp.exp2(s_scr[i, ki] - m[i])
                li = li + jnp.sum(p, axis=0, keepdims=True)
                term = _dot(value_fn(i, ki), p.astype(BF16))
                pv = term if pv is None else pv + term
            out.append(li)
            acc_scr[i] += pv
        return tuple(out)

    n_blk = qi + 1
    l = tuple(jnp.zeros((1, blk), F32) for _ in range(n_sets))
    l = lax.fori_loop(0, n_blk // 2, lambda kp, c: accumulate((2 * kp, 2 * kp + 1), c), l)
    return lax.cond(n_blk % 2 == 1, lambda c: accumulate((n_blk - 1,), c), lambda c: c, l)


def _mla_attn_kernel(q_ref, k_ref, vt_ref, o_ref, s_scr, acc_scr):
    blk = q_ref.shape[0]

    def score_fn(h, ki):
        rows = pl.ds(pl.multiple_of(ki * blk, blk), blk)
        cols = slice(LANE * h, LANE * (h + 1))
        return _nt(k_ref[rows, cols], q_ref[:, cols])

    def value_fn(h, ki):
        return vt_ref[ki, MLA_V * h:MLA_V * (h + 1), :]

    l = _two_pass_attention(MLA_HEADS, score_fn, value_fn, s_scr, acc_scr, blk)
    for j in range(MLA_HEADS // 2):
        o_t = jnp.concatenate([acc_scr[2 * j] / l[2 * j], acc_scr[2 * j + 1] / l[2 * j + 1]], axis=0)
        o_ref[:, LANE * j:LANE * (j + 1)] = o_t.T.astype(BF16)


def _mla_attn(qm, km, vmt, batch, seq):
    t = qm.shape[0]
    nq = seq // ATT_BLK
    return pl.pallas_call(
        _mla_attn_kernel,
        grid=(batch, nq),
        in_specs=[pl.BlockSpec((ATT_BLK, 1024), lambda b, i: (b * nq + i, 0)),
                  pl.BlockSpec((seq, 1024), lambda b, i: (b, 0)),
                  pl.BlockSpec((nq, 512, ATT_BLK), lambda b, i: (b, 0, 0))],
        out_specs=pl.BlockSpec((ATT_BLK, 512), lambda b, i: (b * nq + i, 0)),
        out_shape=jax.ShapeDtypeStruct((t, 512), BF16),
        scratch_shapes=[pltpu.VMEM((MLA_HEADS, nq, ATT_BLK, ATT_BLK), F32),
                        pltpu.VMEM((MLA_HEADS, MLA_V, ATT_BLK), F32)],
        compiler_params=_cp(("arbitrary", "arbitrary")),
        name="mla_attn",
    )(qm, km, vmt)


def _diff_attn_kernel(q_ref, k_ref, vt_ref, lam_ref, og_ref, o_ref, s_scr, acc_scr, qm_scr, *, lam_init):
    blk = q_ref.shape[0]
    low = lax.broadcasted_iota(jnp.int32, (blk, LANE), 1) < DIFF_QK
    lp = lam_ref[...]
    lam = (jnp.exp(jnp.sum(lp[0:1] * lp[1:2], axis=-1, keepdims=True))
           - jnp.exp(jnp.sum(lp[2:3] * lp[3:4], axis=-1, keepdims=True)) + lam_init)

    for h in range(DIFF_HEADS):
        qt = q_ref[:, LANE * h:LANE * (h + 1)]
        zero = jnp.zeros_like(qt)
        qm_scr[2 * h] = jnp.where(low, qt, zero)
        qm_scr[2 * h + 1] = jnp.where(low, zero, qt)

    def score_fn(i, ki):
        rows = pl.ds(pl.multiple_of(ki * blk, blk), blk)
        h = i // 2
        return _nt(k_ref[rows, LANE * h:LANE * (h + 1)], qm_scr[i])

    def value_fn(i, ki):
        h = i // 2
        return vt_ref[ki, DIFF_V * h:DIFF_V * (h + 1), :]

    l = _two_pass_attention(2 * DIFF_HEADS, score_fn, value_fn, s_scr, acc_scr, blk)
    for h in range(DIFF_HEADS):
        o_t = acc_scr[2 * h] / l[2 * h] - lam * (acc_scr[2 * h + 1] / l[2 * h + 1])
        on_t = o_t * lax.rsqrt(jnp.mean(o_t * o_t, axis=0, keepdims=True) + EPS)
        o_ref[:, LANE * h:LANE * (h + 1)] = (on_t.T * (og_ref[...] * (1.0 - lam_init))).astype(BF16)


def _diff_attn(qd, kd, vdt, lam_p, og, lam_init, batch, seq):
    t = qd.shape[0]
    nq = seq // ATT_BLK
    return pl.pallas_call(
        functools.partial(_diff_attn_kernel, lam_init=lam_init),
        grid=(batch, nq),
        in_specs=[pl.BlockSpec((ATT_BLK, 512), lambda b, i: (b * nq + i, 0)),
                  pl.BlockSpec((seq, 512), lambda b, i: (b, 0)),
                  pl.BlockSpec((nq, 512, ATT_BLK), lambda b, i: (b, 0, 0)),
                  pl.BlockSpec(lam_p.shape, lambda b, i: (0, 0)),
                  pl.BlockSpec(og.shape, lambda b, i: (0, 0))],
        out_specs=pl.BlockSpec((ATT_BLK, 512), lambda b, i: (b * nq + i, 0)),
        out_shape=jax.ShapeDtypeStruct((t, 512), BF16),
        scratch_shapes=[pltpu.VMEM((2 * DIFF_HEADS, nq, ATT_BLK, ATT_BLK), F32),
                        pltpu.VMEM((2 * DIFF_HEADS, DIFF_V, ATT_BLK), F32),
                        pltpu.VMEM((2 * DIFF_HEADS, ATT_BLK, LANE), BF16)],
        compiler_params=_cp(("arbitrary", "arbitrary")),
        name="diff_attn",
    )(qd, kd, vdt, lam_p, og)


def _swa_kernel(q_ref, kp_ref, kc_ref, vtp_ref, vtc_ref, sink_ref, o_ref):
    w = SWA_WINDOW
    grp = SWA_Q_HEADS // SWA_KV_HEADS
    n = pl.program_id(1)
    key = lax.broadcasted_iota(jnp.int32, (2 * w, grp * w), 0)
    qry = lax.broadcasted_iota(jnp.int32, (2 * w, grp * w), 1) & (w - 1)
    cur_ok = (key >= w) & (key - w <= qry)
    prev_ok = (key < w) & (key > qry)
    low = lax.broadcasted_iota(jnp.int32, (w, LANE), 1) < HEAD_DIM

    for t in range(q_ref.shape[0] // w):
        rows = slice(t * w, (t + 1) * w)
        if t == 0:
            kp, vtp = kp_ref[...], vtp_ref[0]
            valid = cur_ok | (prev_ok & (n > 0))
        else:
            kp, vtp = kc_ref[(t - 1) * w:t * w, :], vtc_ref[t - 1]
            valid = cur_ok | prev_ok
        kc, vtc = kc_ref[rows, :], vtc_ref[t]
        for kv in range(SWA_KV_HEADS):
            kcols = slice(LANE * kv, LANE * (kv + 1))
            vrows = slice(HEAD_DIM * kv, HEAD_DIM * (kv + 1))
            k_win = jnp.concatenate([kp[:, kcols], kc[:, kcols]], axis=0)
            parts = []
            for u in range(2):
                qt = q_ref[rows, LANE * (2 * kv + u):LANE * (2 * kv + u + 1)]
                zero = jnp.zeros_like(qt)
                parts += [jnp.where(low, qt, zero), jnp.where(low, zero, qt)]
            s = jnp.where(valid, _nt(k_win, jnp.concatenate(parts, axis=0)), NEG)
            sink = sink_ref[:, grp * w * kv:grp * w * (kv + 1)]
            m = jnp.maximum(jnp.max(s, axis=0, keepdims=True), sink)
            p = jnp.exp2(s - m)
            den = jnp.sum(p, axis=0, keepdims=True) + jnp.exp2(sink - m)
            vt_win = jnp.concatenate([vtp[vrows, :], vtc[vrows, :]], axis=1)
            o_t = _dot(vt_win, p.astype(BF16)) / den
            for u in range(2):
                pair = jnp.concatenate([o_t[:, 2 * u * w:(2 * u + 1) * w],
                                        o_t[:, (2 * u + 1) * w:(2 * u + 2) * w]], axis=0)
                o_ref[rows, LANE * (2 * kv + u):LANE * (2 * kv + u + 1)] = pair.T.astype(BF16)


def _swa(qs, ks, vst, sink_row, batch, seq):
    t = qs.shape[0]
    w = SWA_WINDOW
    nb = seq // w
    ns = nb // SWA_QB
    cur = lambda b, n: (b * ns + n, 0)
    cur3 = lambda b, n: (b * ns + n, 0, 0)
    prev = lambda b, n: (b * nb + jnp.maximum(n * SWA_QB - 1, 0), 0)
    prev3 = lambda b, n: (b * nb + jnp.maximum(n * SWA_QB - 1, 0), 0, 0)
    return pl.pallas_call(
        _swa_kernel,
        grid=(batch, ns),
        in_specs=[pl.BlockSpec((SWA_QB * w, 512), cur),
                  pl.BlockSpec((w, 256), prev), pl.BlockSpec((SWA_QB * w, 256), cur),
                  pl.BlockSpec((1, LANE, w), prev3), pl.BlockSpec((SWA_QB, LANE, w), cur3),
                  pl.BlockSpec(sink_row.shape, lambda b, n: (0, 0))],
        out_specs=pl.BlockSpec((SWA_QB * w, 512), cur),
        out_shape=jax.ShapeDtypeStruct((t, 512), BF16),
        compiler_params=_cp(("arbitrary", "arbitrary")),
        name="swa_attn",
    )(qs, ks, ks, vst, vst, sink_row)


def _merge_kernel(*refs):
    for k in range(refs[0].shape[0] // MOE_TILE):
        _merge_rows(k, *refs)


def _merge_rows(k, h_ref, ya_ref, yb_ref, yc_ref, yd_ref, x_ref, gt1_ref, wg_ref, wb_ref, wo_ref,
                g2_ref, sh2_ref, sc2_ref, rw_ref, rb_ref, xo_ref, h2_ref, comb_ref, dest_ref, meta_ref):
    rows = slice(MOE_TILE * k, MOE_TILE * (k + 1))
    h = h_ref[rows, :]
    d = x_ref.shape[1]
    merged = None
    for b, y_ref in enumerate((ya_ref, yb_ref, yc_ref, yd_ref)):
        gate = jax.nn.sigmoid(_dot(h, wg_ref[:, d * b:d * (b + 1)]))
        term = gate * _dot(y_ref[rows, :], wb_ref[b])
        merged = term if merged is None else merged + term
    xn = x_ref[rows, :] + gt1_ref[0] * _dot(merged.astype(BF16), wo_ref[...])
    xo_ref[rows, :] = xn
    ms = jnp.mean(xn * xn, axis=-1, keepdims=True)
    h2 = xn * lax.rsqrt(ms + EPS) * g2_ref[...]
    h2 = h2 * (1.0 + sc2_ref[0]) + sh2_ref[0]
    h2_ref[rows, :] = h2.astype(BF16)

    hh, hm, _ = _split3(h2)
    wh, wm = _split2(rw_ref[...])
    logits = _dot(hh, wh) + _dot(hm, wh) + _dot(hh, wm)
    lt = logits.T
    scores = jax.nn.sigmoid(lt[0:N_EXPERTS, :])
    sel = scores + rb_ref[...]
    per = N_EXPERTS // N_GROUPS
    srow = [sel[e:e + 1, :] for e in range(N_EXPERTS)]
    gsum = []
    for g in range(N_GROUPS):
        a, b_, c, e_ = srow[per * g:per * (g + 1)]
        gsum.append(jnp.maximum(jnp.maximum(jnp.maximum(a + b_, a + c), jnp.maximum(a + e_, b_ + c)),
                                jnp.maximum(b_ + e_, c + e_)))
    best = jnp.maximum(jnp.maximum(gsum[0], gsum[1]), jnp.maximum(gsum[2], gsum[3]))
    taken = None
    weights = []
    picks = []
    for g in range(N_GROUPS):
        hit = gsum[g] == best
        pick = hit if taken is None else hit & jnp.logical_not(taken)
        taken = hit if taken is None else taken | hit
        picks.append(pick.astype(F32))
        for e in range(per * g, per * (g + 1)):
            rank = jnp.zeros_like(best)
            for o in range(per * g, per * (g + 1)):
                if o == e:
                    continue
                ahead = (srow[o] > srow[e]) | ((srow[o] == srow[e]) & (o < e))
                rank = rank + ahead.astype(F32)
            weights.append(jnp.where(pick & (rank < 1.5), scores[e:e + 1, :], 0.0))
    wsum = weights[0]
    for r_ in weights[1:]:
        wsum = wsum + r_
    inv = 1.0 / wsum
    rid = lax.broadcasted_iota(jnp.int32, scores.shape, 0)
    comb_e = jnp.zeros_like(scores)
    for e, r_ in enumerate(weights):
        comb_e = jnp.where(rid == e, r_ * inv, comb_e)

    tm = lt.shape[1]
    gid = lax.broadcasted_iota(jnp.int32, (8, tm), 0)
    onehot = jnp.zeros((8, tm), F32)
    for g in range(N_GROUPS):
        onehot = jnp.where(gid == g, picks[g], onehot)
    before = (lax.broadcasted_iota(jnp.int32, (tm, tm), 0) < lax.broadcasted_iota(jnp.int32, (tm, tm), 1))
    rank = _dot(onehot.astype(BF16), before.astype(BF16))
    dest = jnp.zeros((1, tm), F32)
    off = jnp.zeros((1, 1), F32)
    meta = jnp.zeros((8, LANE), F32)
    mrow = lax.broadcasted_iota(jnp.int32, (8, LANE), 0)
    for g in range(N_GROUPS):
        cnt = jnp.sum(picks[g], axis=-1, keepdims=True)
        dest = dest + picks[g] * (off + rank[g:g + 1, :])
        meta = jnp.where(mrow == g, cnt, meta)
        meta = jnp.where(mrow == N_GROUPS + g, off, meta)
        off = off + jnp.ceil(cnt * (1.0 / MOE_ALIGN)) * MOE_ALIGN
    dest_ref[k] = dest.astype(jnp.int32)
    meta_ref[k] = meta.astype(jnp.int32)
    pad = jnp.zeros((LANE - N_EXPERTS - 8, tm), F32)
    comb_t = jnp.concatenate([comb_e, jnp.where(gid == 0, dest, 0.0), pad], axis=0)
    comb_ref[rows, :] = comb_t.T


def _merge(h, ys, x2, gt1, wg, wb, wo, g2, sh2, sc2, rw, rb, seq):
    t, d = x2.shape
    sub = MERGE_SUBTILES
    tm = sub * MOE_TILE
    tpb = seq // tm
    row = lambda i: (i, 0)
    per_b = lambda i: (i // tpb, 0, 0)
    c2 = lambda i: (0, 0)
    once = pl.Buffered(1)
    return pl.pallas_call(
        _merge_kernel,
        grid=(t // tm,),
        in_specs=[pl.BlockSpec((tm, d), row)] + [pl.BlockSpec((tm, 512), row)] * 4
                 + [pl.BlockSpec((tm, d), row), pl.BlockSpec((1, 1, d), per_b),
                    pl.BlockSpec(wg.shape, c2, pipeline_mode=once),
                    pl.BlockSpec(wb.shape, lambda i: (0, 0, 0), pipeline_mode=once),
                    pl.BlockSpec(wo.shape, c2, pipeline_mode=once), pl.BlockSpec((1, d), c2),
                    pl.BlockSpec((1, 1, d), per_b), pl.BlockSpec((1, 1, d), per_b),
                    pl.BlockSpec(rw.shape, c2), pl.BlockSpec(rb.shape, c2)],
        out_specs=[pl.BlockSpec((tm, d), row), pl.BlockSpec((tm, d), row), pl.BlockSpec((tm, LANE), row),
                   pl.BlockSpec((sub, 1, MOE_TILE), lambda i: (i, 0, 0)),
                   pl.BlockSpec((sub, 8, LANE), lambda i: (i, 0, 0))],
        out_shape=[jax.ShapeDtypeStruct((t, d), F32), jax.ShapeDtypeStruct((t, d), BF16),
                   jax.ShapeDtypeStruct((t, LANE), F32),
                   jax.ShapeDtypeStruct((t // MOE_TILE, 1, MOE_TILE), jnp.int32),
                   jax.ShapeDtypeStruct((t // MOE_TILE, 8, LANE), jnp.int32)],
        compiler_params=_cp(("arbitrary",)),
        name="merge_router",
    )(h, *ys, x2, gt1, wg, wb, wo, g2, sh2, sc2, rw, rb)


def _moe_kernel(meta_ref, h2_ref, comb_ref, dest_ref, x_ref, gt2_ref, wg_ref, wu_ref, wd_ref, o_ref,
                sorted_s, csort_s, out_s):
    i = pl.program_id(0)
    tm = h2_ref.shape[0]
    n_rows = sorted_s.shape[0]
    per = N_EXPERTS // N_GROUPS
    comb = comb_ref[...]

    place = (lax.broadcasted_iota(jnp.int32, (n_rows, tm), 0) == dest_ref[0]).astype(BF16)
    sorted_s[...] = _dot(place, h2_ref[...]).astype(BF16)
    c_hi, c_lo = _split2(comb)
    csort_s[...] = _dot(place, c_hi) + _dot(place, c_lo)
    out_s[...] = jnp.zeros_like(out_s)
    def run_experts(g, start, size):
        rows = pl.ds(pl.multiple_of(start, MOE_ALIGN), size)
        xk = sorted_s[rows, :]
        cw = csort_s[rows, :]
        lane = lax.broadcasted_iota(jnp.int32, (size, LANE), 1)
        acc = None
        for j in range(per):
            e = per * g + j
            hid = _silu(_dot(xk, wg_ref[0, e])) * _dot(xk, wu_ref[0, e])
            ce = jnp.sum(jnp.where(lane == e, cw, 0.0), axis=-1, keepdims=True)
            term = _dot((hid * ce).astype(BF16), wd_ref[0, e])
            acc = term if acc is None else acc + term
        out_s[rows, :] += acc

    def group(g, carry):
        cnt = meta_ref[i, g]
        off = meta_ref[i, N_GROUPS + g]
        big = MOE_CHUNKS[-1]
        n_big = cnt // big

        def big_chunk(c, inner):
            run_experts(g, off + c * big, big)
            return inner

        lax.fori_loop(0, n_big, big_chunk, 0)
        rest = cnt - n_big * big
        lower = 0
        for size in MOE_CHUNKS:
            @pl.when((rest > lower) & (rest <= size))
            def _(size=size):
                run_experts(g, off + n_big * big, size)
            lower = size
        return carry

    lax.fori_loop(0, N_GROUPS, group, 0)

    dest_col = comb[:, MOE_DEST_LANE:MOE_DEST_LANE + 1].astype(jnp.int32)
    back = (lax.broadcasted_iota(jnp.int32, (tm, n_rows), 1) == dest_col).astype(BF16)
    o_ref[...] = x_ref[...] + gt2_ref[0] * _dot(back, out_s[...].astype(BF16))


def _moe(h2, comb, dest, meta, x2, gt2, wg, wu, wd, layer, seq):
    t, d = x2.shape
    tm = MOE_TILE
    tpb = seq // tm
    n_rows = MOE_SORT_ROWS
    overhang = max(b - a for a, b in zip((0,) + MOE_CHUNKS, MOE_CHUNKS)) - 1
    assert tm + N_GROUPS * (MOE_ALIGN - 1) + overhang <= n_rows
    row = lambda i, m: (i, 0)
    whole = lambda a: pl.BlockSpec((1,) + a.shape[1:], lambda i, m: (layer, 0, 0, 0),
                                   pipeline_mode=pl.Buffered(1))
    grid_spec = pltpu.PrefetchScalarGridSpec(
        num_scalar_prefetch=1,
        grid=(t // tm,),
        in_specs=[pl.BlockSpec((tm, d), row), pl.BlockSpec((tm, LANE), row),
                  pl.BlockSpec((1, 1, tm), lambda i, m: (i, 0, 0)),
                  pl.BlockSpec((tm, d), row),
                  pl.BlockSpec((1, 1, d), lambda i, m: (i // tpb, 0, 0)),
                  whole(wg), whole(wu), whole(wd)],
        out_specs=pl.BlockSpec((tm, d), row),
        scratch_shapes=[pltpu.VMEM((n_rows, d), BF16), pltpu.VMEM((n_rows, LANE), F32),
                        pltpu.VMEM((n_rows, d), F32)])
    return pl.pallas_call(
        _moe_kernel,
        grid_spec=grid_spec,
        out_shape=jax.ShapeDtypeStruct((t, d), F32),
        compiler_params=_cp(("arbitrary",)),
        name="moe",
    )(meta, h2, comb, dest, x2, gt2, wg, wu, wd)


def _layer_params(l, w_in, hg_onorm, mla_q_norm, mla_kv_norm, mla_w_uq, mla_w_ukv, mla_qk_norm,
                  diff_qk_norm, swa_qk_norm, swa_sinks, lb_all):
    ends = [sum(IN_SPLITS[:i]) for i in range(len(IN_SPLITS) + 1)]
    cols = lambda a, b: w_in[l, :, ends[a]:ends[b]].astype(BF16)
    mla_pad = 512 - (ends[7] - ends[4])
    p = {"w_hg": cols(0, 4),
         "w_mla": jnp.pad(cols(4, 7), ((0, 0), (0, mla_pad))),
         "w_diff": cols(7, 10), "w_swa": cols(10, 13), "wg": cols(13, 14)}

    lb = lb_all[l]
    p["loglb"] = jnp.log(lb)[None, :]
    p["log1mlb"] = jnp.log1p(-lb)[None, :]
    p["ogain"] = jnp.tile(hg_onorm[l], HG_HEADS)[None, :]

    hd = MLA_NOPE + MLA_ROPE
    half = MLA_ROPE // 2
    wq = mla_w_uq[l].reshape(MLA_Q_RANK, MLA_HEADS, hd)
    z = lambda r, n: jnp.zeros((r, MLA_HEADS, n), F32)
    nope, rope = wq[:, :, :MLA_NOPE], wq[:, :, MLA_NOPE:]
    p["wqa"] = jnp.concatenate([nope, rope, z(MLA_Q_RANK, 32)], -1).reshape(MLA_Q_RANK, -1).astype(BF16)
    p["wqb"] = jnp.concatenate([z(MLA_Q_RANK, MLA_NOPE), rope[:, :, half:], rope[:, :, :half],
                                z(MLA_Q_RANK, 32)], -1).reshape(MLA_Q_RANK, -1).astype(BF16)
    wkv = mla_w_ukv[l].reshape(MLA_KV_RANK, MLA_HEADS, MLA_NOPE + MLA_V)
    knope, vproj = wkv[:, :, :MLA_NOPE], wkv[:, :, MLA_NOPE:]
    eye = jnp.eye(MLA_ROPE, dtype=F32)
    swap = jnp.concatenate([eye[:, half:], eye[:, :half]], axis=1)
    place = lambda m: jnp.broadcast_to(
        jnp.concatenate([jnp.zeros((MLA_ROPE, MLA_NOPE), F32), m, jnp.zeros((MLA_ROPE, 32), F32)], -1)[:, None, :],
        (MLA_ROPE, MLA_HEADS, LANE))
    pad_rows = 256 - MLA_KV_RANK - MLA_ROPE
    p["wka"] = jnp.concatenate([jnp.concatenate([knope, z(MLA_KV_RANK, 64)], -1), place(eye),
                                z(pad_rows, LANE)], 0).reshape(256, -1).astype(BF16)
    p["wkb"] = jnp.concatenate([z(MLA_KV_RANK, LANE), place(swap), z(pad_rows, LANE)], 0
                               ).reshape(256, -1).astype(BF16)
    p["wv"] = jnp.concatenate([vproj.reshape(MLA_KV_RANK, -1),
                               jnp.zeros((256 - MLA_KV_RANK, MLA_HEADS * MLA_V), F32)], 0).astype(BF16)
    p["qng"] = mla_q_norm[l][None, :]
    p["kvg"] = jnp.concatenate([mla_kv_norm[l], jnp.ones((256 - MLA_KV_RANK,), F32)])[None, :]

    def rope_gains(g):
        base = jnp.concatenate([g, jnp.zeros((LANE - hd,), F32)])
        part = jnp.concatenate([jnp.zeros((MLA_NOPE,), F32), g[MLA_NOPE + half:], g[MLA_NOPE:MLA_NOPE + half],
                                jnp.zeros((LANE - hd,), F32)])
        return base[None, :], part[None, :]

    p["gq"], p["gqs"] = rope_gains(mla_qk_norm[l, 0])
    p["gk"], p["gks"] = rope_gains(mla_qk_norm[l, 1])
    p["dgq"] = jnp.tile(diff_qk_norm[l, 0], 8)[None, :]
    p["dgk"] = jnp.tile(diff_qk_norm[l, 1], 8)[None, :]
    p["sgq"] = jnp.tile(swa_qk_norm[l, 0], 8)[None, :]
    p["sgk"] = jnp.tile(swa_qk_norm[l, 1], 2)[None, :]
    p["sinks"] = jnp.repeat(swa_sinks[l].astype(F32) * LOG2E, SWA_WINDOW)[None, :]
    return p


def _rope_tables(positions):
    inv_freq = ROPE_BASE ** (-jnp.arange(0, MLA_ROPE, 2, dtype=F32) / MLA_ROPE)
    ang = positions.astype(F32).reshape(-1)[:, None] * inv_freq
    cos, sin = jnp.cos(ang), jnp.sin(ang)
    t = ang.shape[0]
    cosf = jnp.concatenate([jnp.ones((t, MLA_NOPE), F32), cos, cos, jnp.zeros((t, 32), F32)], axis=1)
    sinf = jnp.concatenate([jnp.zeros((t, MLA_NOPE), F32), -sin, sin, jnp.zeros((t, 32), F32)], axis=1)
    return cosf, sinf


def kernel(x, c, positions, ada_w, ada_b, norm_mix, norm_ffn, w_in, hg_lb_logits, hg_onorm, mla_q_norm, mla_kv_norm, mla_w_uq, mla_w_ukv, mla_qk_norm, diff_qk_norm, diff_lam, diff_onorm, swa_qk_norm, swa_sinks, w_branch, w_out, router_w, router_bias, moe_w_gate, moe_w_up, moe_w_down):
    batch, seq, d = x.shape
    x2 = x.reshape(batch * seq, d)
    cosf, sinf = _rope_tables(positions)
    lb_all = jnp.cumsum(jax.nn.softmax(hg_lb_logits.astype(F32), axis=0), axis=0)
    lb_all = lb_all - lb_all[0]
    mod = _modulation(c, ada_w, ada_b)
    rw = jnp.concatenate([router_w, jnp.zeros((d, LANE - N_EXPERTS), F32)], axis=1)
    rb = router_bias.astype(F32)[:, None]
    moe_w = (moe_w_gate.astype(BF16), moe_w_up.astype(BF16), moe_w_down.astype(BF16))

    for l in range(DEPTH):
        sh1, sc1, gt1, sh2, sc2, gt2 = [mod[l, :, d * k:d * (k + 1)][:, None, :] for k in range(6)]
        p = _layer_params(l, w_in, hg_onorm, mla_q_norm, mla_kv_norm, mla_w_uq, mla_w_ukv, mla_qk_norm,
                          diff_qk_norm, swa_qk_norm, swa_sinks, lb_all)
        hg3, hf, mla, diff, swa, h = _inproj(x2, norm_mix[l][None, :], sh1, sc1,
                                             (p["w_hg"], p["w_mla"], p["w_diff"], p["w_swa"]), seq)
        y_a = _hgrn(hg3, hf, p["loglb"], p["log1mlb"], p["ogain"], batch, seq)
        qm, km, vmt, qd, kd, vdt, qs, ks, vst = _prep(mla, diff, swa, cosf, sinf, p)
        y_b = _mla_attn(qm, km, vmt, batch, seq)
        lam_init = 0.8 - 0.6 * math.exp(-0.3 * l)
        y_c = _diff_attn(qd, kd, vdt, diff_lam[l], diff_onorm[l][None, :], lam_init, batch, seq)
        y_d = _swa(qs, ks, vst, p["sinks"], batch, seq)
        x2, h2, comb, dest, meta = _merge(h, (y_a, y_b, y_c, y_d), x2, gt1, p["wg"], w_branch[l].astype(BF16),
                                          w_out[l].astype(BF16), norm_ffn[l][None, :], sh2, sc2, rw, rb, seq)
        x2 = _moe(h2, comb, dest, meta[:, :, 0], x2, gt2, *moe_w, l, seq)
    return x2.reshape(batch, seq, d)
```

```python
import functools
import math

import jax
import jax.numpy as jnp
from jax import lax
from jax.experimental import pallas as pl
from jax.experimental.pallas import tpu as pltpu

F32 = jnp.float32
BF16 = jnp.bfloat16

D_MODEL = 1024
DEPTH = 2
EPS = 1e-6
N_BRANCH = 4
HG_HEADS = 8
HG_DK = 64
HG_W = HG_HEADS * HG_DK
HG_SUB = 16
HG_CHUNK = 64
HG_SAFE_DECAY = 80.0
HG_DIAG_ROWS =(HG_SUB // 2) * HG_SUB + (HG_SUB // 2) ** 2
MLA_HEADS = 8
MLA_Q_RANK = 256
MLA_KV_RANK = 128
MLA_NOPE = 64
MLA_ROPE = 32
MLA_V = 64
ROPE_BASE = 10000.0
DIFF_HEADS = 4
DIFF_QK = 64
DIFF_V = 128
SWA_Q_HEADS = 8
SWA_KV_HEADS = 2
SWA_WINDOW = 128
HEAD_DIM = 64
N_EXPERTS = 16
N_GROUPS = 4
D_FF_EXPERT = 256
IN_SPLITS = (512, 512, 512, 512, 256, 128, 32, 512, 512, 512, 512, 128, 128, 4096)

MOE_TILE = 512
MERGE_SUBTILES = 2
MOE_ALIGN = 16
MOE_CHUNKS = (64, 128, 192, 256)
MOE_SORT_ROWS = 640
MOE_DEST_LANE = N_EXPERTS
LANE = 128
ATT_BLK = 256
SWA_QB = 4
LOG2E = 1.4426950408889634
NEG = -1e30
VMEM_LIMIT = 56 * 1024 * 1024


def _cp(sem, vmem=VMEM_LIMIT):
    return pltpu.CompilerParams(dimension_semantics=sem, vmem_limit_bytes=vmem)


def _nt(a, b):
    return lax.dot_general(a, b, (((1,), (1,)), ((), ())), preferred_element_type=F32)


def _tn(a, b):
    return lax.dot_general(a, b, (((0,), (0,)), ((), ())), preferred_element_type=F32)


def _dot(a, b):
    return jnp.dot(a, b, preferred_element_type=F32)


def _split2(x):
    hi = x.astype(BF16)
    lo = (x - hi.astype(F32)).astype(BF16)
    return hi, lo


def _split3(x):
    hi = x.astype(BF16)
    r = x - hi.astype(F32)
    mid = r.astype(BF16)
    lo = (r - mid.astype(F32)).astype(BF16)
    return hi, mid, lo


def _seg_id(idx, seg):
    shift = seg.bit_length() - 1
    assert 1 << shift == seg
    return lax.shift_right_logical(idx, shift)


def _same_seg(n, seg):
    r = lax.broadcasted_iota(jnp.int32, (n, n), 0)
    c = lax.broadcasted_iota(jnp.int32, (n, n), 1)
    return _seg_id(r, seg) == _seg_id(c, seg)


def _seg_ones(n, seg):
    return _same_seg(n, seg).astype(BF16)


def _seg_mean_sq(x, seg):
    n = x.shape[-1]
    hi, lo = _split2(x * x)
    ones = _seg_ones(n, seg)
    return (_dot(hi, ones) + _dot(lo, ones)) * (1.0 / seg)


def _silu(x):
    return x * jax.nn.sigmoid(x)


def _mod_kernel(c_ref, w_ref, b_ref, o_ref):
    c = c_ref[...]
    o_ref[0] = jnp.dot(_silu(c), w_ref[0], preferred_element_type=F32,
                       precision=lax.Precision.HIGHEST) + b_ref[0]


def _modulation(c, ada_w, ada_b):
    nl, d, n6 = ada_w.shape
    b = c.shape[0]
    tn = 1536
    return pl.pallas_call(
        _mod_kernel,
        grid=(nl, n6 // tn),
        in_specs=[pl.BlockSpec((b, d), lambda l, j: (0, 0)),
                  pl.BlockSpec((1, d, tn), lambda l, j: (l, 0, j)),
                  pl.BlockSpec((1, 1, tn), lambda l, j: (l, 0, j))],
        out_specs=pl.BlockSpec((1, b, tn), lambda l, j: (l, 0, j)),
        out_shape=jax.ShapeDtypeStruct((nl, b, n6), F32),
        compiler_params=_cp(("arbitrary", "arbitrary")),
        name="modulation",
    )(c, ada_w, ada_b.reshape(nl, 1, n6))


def _inproj_kernel(x_ref, g_ref, sh_ref, sc_ref, whg_ref, wmla_ref, wdiff_ref, wswa_ref,
                   ohg_ref, ohf_ref, omla_ref, odiff_ref, oswa_ref, oh_ref):
    x = x_ref[...]
    ms = jnp.mean(x * x, axis=-1, keepdims=True)
    h = x * lax.rsqrt(ms + EPS) * g_ref[...]
    h = h * (1.0 + sc_ref[0]) + sh_ref[0]
    hb = h.astype(BF16)
    oh_ref[...] = hb

    ohg_ref[:, 0:512] = _dot(hb, whg_ref[:, 0:512]).astype(BF16)
    ohf_ref[...] = _dot(hb, whg_ref[:, 512:1024])
    ohg_ref[:, 512:1024] = _dot(hb, whg_ref[:, 1024:1536]).astype(BF16)
    ohg_ref[:, 1024:1536] = _dot(hb, whg_ref[:, 1536:2048]).astype(BF16)
    omla_ref[...] = _dot(hb, wmla_ref[...]).astype(BF16)
    for k in range(3):
        odiff_ref[:, 512 * k:512 * (k + 1)] = _dot(hb, wdiff_ref[:, 512 * k:512 * (k + 1)]).astype(BF16)
    oswa_ref[:, 0:512] = _dot(hb, wswa_ref[:, 0:512]).astype(BF16)
    oswa_ref[:, 512:768] = _dot(hb, wswa_ref[:, 512:768]).astype(BF16)


def _inproj(x2, gain, sh, sc, weights, seq):
    t, d = x2.shape
    tm = 512
    tpb = seq // tm
    row = lambda i: (i, 0)
    per_b = lambda i: (i // tpb, 0, 0)
    outs = [(1536, BF16), (512, F32), (512, BF16), (1536, BF16), (768, BF16), (d, BF16)]
    return pl.pallas_call(
        _inproj_kernel,
        grid=(t // tm,),
        in_specs=[pl.BlockSpec((tm, d), row),
                  pl.BlockSpec((1, d), lambda i: (0, 0)),
                  pl.BlockSpec((1, 1, d), per_b),
                  pl.BlockSpec((1, 1, d), per_b)]
                 + [pl.BlockSpec(w.shape, lambda i: (0, 0)) for w in weights],
        out_specs=[pl.BlockSpec((tm, w), row) for w, _ in outs],
        out_shape=[jax.ShapeDtypeStruct((t, w), dt) for w, dt in outs],
        compiler_params=_cp(("arbitrary",)),
        name="inproj",
    )(x2, gain, sh, sc, *weights)


def _segment_cumsum(x, seg):
    n = x.shape[0]
    r = lax.broadcasted_iota(jnp.int32, (n, n), 0)
    cc = lax.broadcasted_iota(jnp.int32, (n, n), 1)
    same = _same_seg(n, seg)
    tri = (same & (cc <= r)).astype(BF16)
    blk = same.astype(BF16)
    parts = _split3(x)
    c = _dot(tri, parts[0]) + _dot(tri, parts[1]) + _dot(tri, parts[2])
    tot = _dot(blk, parts[0]) + _dot(blk, parts[1]) + _dot(blk, parts[2])
    return c, tot


def _hgrn_chunk_path(i_ref, st_ref, c_s, tot_s, qs_s, kk_s, qe_s, kd_s, ke_s, dec_s, od_s):
    rows_blk = c_s.shape[0]
    tot = tot_s[...]
    rel = c_s[...] - 0.5 * tot
    half_dec = jnp.exp(0.5 * tot)
    kd = kk_s[...] * jnp.exp(-rel)
    qe_s[...] = (qs_s[...] * jnp.exp(rel)).astype(BF16)
    kd_s[...] = kd.astype(BF16)
    ke_s[...] = (kd * half_dec).astype(BF16)
    dec_s[...] = half_dec

    row = lax.broadcasted_iota(jnp.int32, (2 * rows_blk, rows_blk), 0) & (rows_blk - 1)
    col = lax.broadcasted_iota(jnp.int32, (2 * rows_blk, rows_blk), 1)
    intra = (_seg_id(row, HG_CHUNK) == _seg_id(col, HG_CHUNK)) & (col <= row)
    low_q = lax.broadcasted_iota(jnp.int32, (rows_blk, LANE), 1) < HG_DK
    low_c = lax.broadcasted_iota(jnp.int32, (HG_CHUNK, LANE), 1) < HG_DK

    for j in range(HG_W // LANE):
        cols = slice(LANE * j, LANE * (j + 1))
        qe = qe_s[:, cols]
        zero = jnp.zeros_like(qe)
        q2 = jnp.concatenate([jnp.where(low_q, qe, zero), jnp.where(low_q, zero, qe)], axis=0)
        attn = jnp.where(intra, _nt(q2, kd_s[:, cols]), 0.0).astype(BF16)
        o2 = _dot(attn, i_ref[:, cols])
        n_chunks = rows_blk // HG_CHUNK
        chunk_rows = [slice(HG_CHUNK * ch, HG_CHUNK * (ch + 1)) for ch in range(n_chunks)]
        upd = [_tn(i_ref[rows, cols], ke_s[rows, cols]) for rows in chunk_rows]
        st = st_ref[j]
        for ch, rows in enumerate(chunk_rows):
            rows_hi = slice(rows_blk + HG_CHUNK * ch, rows_blk + HG_CHUNK * (ch + 1))
            hd = dec_s[HG_CHUNK * ch:HG_CHUNK * ch + 1, cols]
            inter = _nt(jnp.concatenate([q2[rows], q2[rows_hi]], axis=0), (st * hd).astype(BF16))
            od_s[rows, cols] = jnp.where(low_c, o2[rows] + inter[:HG_CHUNK], o2[rows_hi] + inter[HG_CHUNK:])
            st = st * (hd * hd) + upd[ch]
        st_ref[j] = st


def _hgrn_exact_path(i_ref, st_ref, lf_s, c_s, qs_s, kk_s, qe_s, ke_s, dec_s, od_s, t_s, a_s):
    rows_blk = c_s.shape[0]
    n_sub = rows_blk // HG_SUB
    c, tot = _segment_cumsum(lf_s[...], HG_SUB)
    c_s[...] = c
    qe_s[...] = (qs_s[...] * jnp.exp(c)).astype(BF16)
    ke_s[...] = (kk_s[...] * jnp.exp(tot - c)).astype(BF16)
    dec_s[...] = jnp.exp(tot)

    same_head = _same_seg(LANE, HG_DK)
    head_mask = same_head.astype(F32)
    head_ones = same_head.astype(BF16)
    for j in range(HG_W // LANE):
        st_ref[j] = st_ref[j] * head_mask
    half = HG_SUB // 2
    trow = lax.broadcasted_iota(jnp.int32, (half, HG_W), 0)

    def body(i, carry):
        r0 = pl.multiple_of(i * HG_SUB, HG_SUB)
        rows = pl.ds(r0, HG_SUB)
        c_i = c_s[rows, :]
        qs_i = qs_s[rows, :]
        kk_i = kk_s[rows, :]
        v_i = i_ref[rows, :].astype(F32)
        c_lo, c_hi = c_i[:half], c_i[half:]
        q_lo, q_hi = qs_i[:half], qs_i[half:]
        for s in range(half):
            c_row, k_row = c_i[s:s + 1, :], kk_i[s:s + 1, :]
            e_lo = jnp.exp(jnp.where(trow >= s, c_lo - c_row, NEG))
            e_hi = jnp.exp(c_hi - c_row)
            both = jnp.concatenate([e_lo * q_lo, e_hi * q_hi], axis=0) * k_row
            t_s[s * HG_SUB:(s + 1) * HG_SUB, :] = both.astype(BF16)
        for s in range(half, HG_SUB, 2):
            pair = []
            for u in (s, s + 1):
                e_hi = jnp.exp(jnp.where(trow >= u - half, c_hi - c_i[u:u + 1, :], NEG))
                pair.append(e_hi * q_hi * kk_i[u:u + 1, :])
            base = half * HG_SUB + (s - half) * half
            t_s[base:base + HG_SUB, :] = jnp.concatenate(pair, axis=0).astype(BF16)
        for j in range(HG_W // LANE):
            cols = slice(LANE * j, LANE * (j + 1))
            a_s[:, cols] = _dot(t_s[:, cols], head_ones)
        acc_lo = jnp.zeros((half, HG_W), F32)
        acc_hi = jnp.zeros((half, HG_W), F32)
        for s in range(half):
            acc_lo = acc_lo + a_s[s * HG_SUB:s * HG_SUB + half, :] * v_i[s:s + 1, :]
            acc_hi = acc_hi + a_s[s * HG_SUB + half:(s + 1) * HG_SUB, :] * v_i[s:s + 1, :]
        for s in range(half, HG_SUB):
            base = half * HG_SUB + (s - half) * half
            acc_hi = acc_hi + a_s[base:base + half, :] * v_i[s:s + 1, :]
        acc = jnp.concatenate([acc_lo, acc_hi], axis=0)
        for j in range(HG_W // LANE):
            cols = slice(LANE * j, LANE * (j + 1))
            st = st_ref[j]
            o_int = _nt(qe_s[rows, cols], st.astype(BF16))
            upd = _tn(i_ref[rows, cols], ke_s[rows, cols])
            st_ref[j] = st * dec_s[pl.ds(r0, 1), cols] + upd * head_mask
            od_s[rows, cols] = acc[:, cols] + o_int
        return carry

    lax.fori_loop(0, n_sub, body, 0)


def _hgrn_kernel(q_ref, i_ref, g_ref, f_ref, loglb_ref, log1mlb_ref, og_ref, o_ref,
                 st_ref, lf_s, c_s, tot_s, qs_s, kk_s, qe_s, kd_s, ke_s, dec_s, od_s, t_s, a_s):
    @pl.when(pl.program_id(1) == 0)
    def _():
        st_ref[...] = jnp.zeros_like(st_ref)

    fr = f_ref[...]
    ls = jnp.minimum(fr, 0.0) - jnp.log(1.0 + jnp.exp(-jnp.abs(fr)))
    a = loglb_ref[...]
    c2 = log1mlb_ref[...] + ls
    lf = jnp.maximum(a, c2) + jnp.log(1.0 + jnp.exp(-jnp.abs(a - c2)))
    lf_s[...] = lf
    qs_s[...] = _silu(q_ref[...].astype(F32))
    kk_s[...] = 1.0 - jnp.exp(lf)
    c, tot = _segment_cumsum(lf, HG_CHUNK)
    c_s[...] = c
    tot_s[...] = tot
    safe = 0.5 * jnp.max(-tot) <= HG_SAFE_DECAY

    @pl.when(safe)
    def _():
        _hgrn_chunk_path(i_ref, st_ref, c_s, tot_s, qs_s, kk_s, qe_s, kd_s, ke_s, dec_s, od_s)

    @pl.when(jnp.logical_not(safe))
    def _():
        _hgrn_exact_path(i_ref, st_ref, lf_s, c_s, qs_s, kk_s, qe_s, ke_s, dec_s, od_s, t_s, a_s)

    o = od_s[...]
    ms = _seg_mean_sq(o, HG_DK)
    on = o * lax.rsqrt(ms + EPS) * og_ref[...]
    o_ref[...] = (on * _silu(g_ref[...].astype(F32))).astype(BF16)


def _hgrn(hg3, hf, loglb, log1mlb, ogain, batch, seq):
    t = hf.shape[0]
    rb = 256
    nb = seq // rb
    blk = lambda k: pl.BlockSpec((rb, HG_W), lambda b, n, k=k: (b * nb + n, k))
    vec = pl.BlockSpec((1, HG_W), lambda b, n: (0, 0))
    f32_blk = pltpu.VMEM((rb, HG_W), F32)
    bf16_blk = pltpu.VMEM((rb, HG_W), BF16)
    return pl.pallas_call(
        _hgrn_kernel,
        grid=(batch, nb),
        in_specs=[blk(0), blk(1), blk(2), blk(0), vec, vec, vec],
        out_specs=blk(0),
        out_shape=jax.ShapeDtypeStruct((t, HG_W), BF16),
        scratch_shapes=[pltpu.VMEM((HG_W // LANE, LANE, LANE), F32),
                        f32_blk, f32_blk, f32_blk, f32_blk, f32_blk,
                        bf16_blk, bf16_blk, bf16_blk,
                        f32_blk, f32_blk,
                        pltpu.VMEM((HG_DIAG_ROWS, HG_W), BF16),
                        pltpu.VMEM((HG_DIAG_ROWS, HG_W), F32)],
        compiler_params=_cp(("arbitrary", "arbitrary")),
        name="hgrn2",
    )(hg3, hg3, hg3, hf, loglb, log1mlb, ogain)


def _store_transposed_blocks(out_ref, v):
    blk = out_ref.shape[2]
    for u in range(out_ref.shape[0]):
        out_ref[u] = v[u * blk:(u + 1) * blk, :].T.astype(BF16)


def _prep_kernel(mla_ref, diff_ref, swa_ref, cos_ref, sin_ref,
                 qng_ref, kvg_ref, wqa_ref, wqb_ref, wka_ref, wkb_ref, wv_ref,
                 gq_ref, gqs_ref, gk_ref, gks_ref, dgq_ref, dgk_ref, sgq_ref, sgk_ref,
                 qm_ref, km_ref, vmt_ref, qd_ref, kd_ref, vdt_ref, qs_ref, ks_ref, vst_ref):
    blk = mla_ref[...].astype(F32)
    cq = blk[:, :MLA_Q_RANK]
    rest = blk[:, MLA_Q_RANK:]
    cqn = cq * lax.rsqrt(jnp.mean(cq * cq, axis=-1, keepdims=True) + EPS) * qng_ref[...]
    lane = lax.broadcasted_iota(jnp.int32, rest.shape, 1)
    is_kv = lane < MLA_KV_RANK
    ms_kv = jnp.sum(jnp.where(is_kv, rest * rest, 0.0), axis=-1, keepdims=True) * (1.0 / MLA_KV_RANK)
    restn = jnp.where(is_kv, rest * lax.rsqrt(ms_kv + EPS) * kvg_ref[...], rest)
    cqb = cqn.astype(BF16)
    rb = restn.astype(BF16)
    qa = _dot(cqb, wqa_ref[...])
    qb = _dot(cqb, wqb_ref[...])
    ka = _dot(rb, wka_ref[...])
    kb = _dot(rb, wkb_ref[...])
    _store_transposed_blocks(vmt_ref, _dot(rb, wv_ref[...]))
    cosf = cos_ref[...]
    sinf = sin_ref[...]
    cq_t = cosf * gq_ref[...]
    sq_t = sinf * gqs_ref[...]
    ck_t = cosf * gk_ref[...]
    sk_t = sinf * gks_ref[...]
    inv_n = 1.0 / (MLA_NOPE + MLA_ROPE)
    scale = (MLA_NOPE + MLA_ROPE) ** -0.5 * LOG2E
    for h in range(MLA_HEADS):
        cols = slice(LANE * h, LANE * (h + 1))
        x = qa[:, cols]
        rinv = lax.rsqrt(jnp.sum(x * x, axis=-1, keepdims=True) * inv_n + EPS)
        qm_ref[:, cols] = ((x * cq_t + qb[:, cols] * sq_t) * (rinv * scale)).astype(BF16)
        y = ka[:, cols]
        rinv = lax.rsqrt(jnp.sum(y * y, axis=-1, keepdims=True) * inv_n + EPS)
        km_ref[:, cols] = ((y * ck_t + kb[:, cols] * sk_t) * rinv).astype(BF16)

    def seg_norm(x, gain, scale):
        return x * lax.rsqrt(_seg_mean_sq(x, HEAD_DIM) + EPS) * (gain * scale)

    dq = diff_ref[:, 0:512].astype(F32)
    dk = diff_ref[:, 512:1024].astype(F32)
    qd_ref[...] = seg_norm(dq, dgq_ref[...], DIFF_QK ** -0.5 * LOG2E).astype(BF16)
    kd_ref[...] = seg_norm(dk, dgk_ref[...], 1.0).astype(BF16)
    _store_transposed_blocks(vdt_ref, diff_ref[:, 1024:1536].astype(F32))

    sq = swa_ref[:, 0:512].astype(F32)
    qs_ref[...] = seg_norm(sq, sgq_ref[...], HEAD_DIM ** -0.5 * LOG2E).astype(BF16)
    skv = swa_ref[:, 512:768].astype(F32)
    kn = seg_norm(skv[:, :LANE], sgk_ref[...], 1.0)
    low = lax.broadcasted_iota(jnp.int32, kn.shape, 1) < HEAD_DIM
    sw = pltpu.roll(kn, HEAD_DIM, 1)
    ks_ref[:, :LANE] = jnp.where(low, kn, sw).astype(BF16)
    ks_ref[:, LANE:] = jnp.where(low, sw, kn).astype(BF16)
    _store_transposed_blocks(vst_ref, skv[:, LANE:])


def _prep(mla, diff, swa, cosf, sinf, p):
    t = mla.shape[0]
    tm = 512
    row = lambda i: (i, 0)
    full = lambda a: pl.BlockSpec(a.shape, lambda i: (0,) * a.ndim)
    consts = [p["qng"], p["kvg"], p["wqa"], p["wqb"], p["wka"], p["wkb"], p["wv"],
              p["gq"], p["gqs"], p["gk"], p["gks"], p["dgq"], p["dgk"], p["sgq"], p["sgk"]]
    def rows_out(w):
        return pl.BlockSpec((tm, w), row), jax.ShapeDtypeStruct((t, w), BF16)

    def transposed_out(n, blk):
        return (pl.BlockSpec((tm // blk, n, blk), lambda i: (i, 0, 0)),
                jax.ShapeDtypeStruct((t // blk, n, blk), BF16))

    outs = [rows_out(1024), rows_out(1024), transposed_out(512, ATT_BLK),
            rows_out(512), rows_out(512), transposed_out(512, ATT_BLK),
            rows_out(512), rows_out(256), transposed_out(LANE, SWA_WINDOW)]
    return pl.pallas_call(
        _prep_kernel,
        grid=(t // tm,),
        in_specs=[pl.BlockSpec((tm, 512), row),
                  pl.BlockSpec((tm, 1536), row),
                  pl.BlockSpec((tm, 768), row),
                  pl.BlockSpec((tm, LANE), row),
                  pl.BlockSpec((tm, LANE), row)] + [full(a) for a in consts],
        out_specs=[o[0] for o in outs],
        out_shape=[o[1] for o in outs],
        compiler_params=_cp(("arbitrary",)),
        name="attn_prep",
    )(mla, diff, swa, cosf, sinf, *consts)


def _causal_t(blk):
    key = lax.broadcasted_iota(jnp.int32, (blk, blk), 0)
    qry = lax.broadcasted_iota(jnp.int32, (blk, blk), 1)
    return key <= qry


def _two_pass_attention(n_sets, score_fn, value_fn, s_scr, acc_scr, blk):
    qi = pl.program_id(1)
    causal = _causal_t(blk)

    def scores(ki, m, masked):
        out = []
        for i in range(n_sets):
            s = score_fn(i, ki)
            if masked:
                s = jnp.where(causal, s, NEG)
            s_scr[i, ki] = s
            out.append(jnp.maximum(m[i], jnp.max(s, axis=0, keepdims=True)))
        return tuple(out)

    def blocked(n, step, carry):
        def many(k0, count, c):
            return step(tuple(k0 + u for u in range(count)), c)
        carry = lax.fori_loop(0, n // 4, lambda kp, c: many(4 * kp, 4, c), carry)
        done = (n // 4) * 4
        carry = lax.cond(n - done >= 2, lambda c: many(done, 2, c), lambda c: c, carry)
        done = (n // 2) * 2
        return lax.cond(n - done == 1, lambda c: many(done, 1, c), lambda c: c, carry)

    def scores_step(kis, m):
        for ki in kis:
            m = scores(ki, m, False)
        return m

    m = tuple(jnp.full((1, blk), NEG, F32) for _ in range(n_sets))
    m = blocked(qi, scores_step, m)
    m = scores(qi, m, True)

    acc_scr[...] = jnp.zeros_like(acc_scr)

    def accumulate(kis, l):
        out = []
        for i in range(n_sets):
            li, pv = l[i], None
            for ki in kis:
                p = jnp.exp2(s_scr[i, ki] - m[i])
                li = li + jnp.sum(p, axis=0, keepdims=True)
                term = _dot(value_fn(i, ki), p.astype(BF16))
                pv = term if pv is None else pv + term
            out.append(li)
            acc_scr[i] += pv
        return tuple(out)

    l = tuple(jnp.zeros((1, blk), F32) for _ in range(n_sets))
    return blocked(qi + 1, accumulate, l)


def _mla_attn_kernel(q_ref, k_ref, vt_ref, o_ref, s_scr, acc_scr):
    blk = q_ref.shape[0]

    def score_fn(h, ki):
        rows = pl.ds(pl.multiple_of(ki * blk, blk), blk)
        cols = slice(LANE * h, LANE * (h + 1))
        return _nt(k_ref[rows, cols], q_ref[:, cols])

    def value_fn(h, ki):
        return vt_ref[ki, MLA_V * h:MLA_V * (h + 1), :]

    l = _two_pass_attention(MLA_HEADS, score_fn, value_fn, s_scr, acc_scr, blk)
    for j in range(MLA_HEADS // 2):
        o_t = jnp.concatenate([acc_scr[2 * j] / l[2 * j], acc_scr[2 * j + 1] / l[2 * j + 1]], axis=0)
        o_ref[:, LANE * j:LANE * (j + 1)] = o_t.T.astype(BF16)


def _mla_attn(qm, km, vmt, batch, seq):
    t = qm.shape[0]
    nq = seq // ATT_BLK
    return pl.pallas_call(
        _mla_attn_kernel,
        grid=(batch, nq),
        in_specs=[pl.BlockSpec((ATT_BLK, 1024), lambda b, i: (b * nq + i, 0)),
                  pl.BlockSpec((seq, 1024), lambda b, i: (b, 0)),
                  pl.BlockSpec((nq, 512, ATT_BLK), lambda b, i: (b, 0, 0))],
        out_specs=pl.BlockSpec((ATT_BLK, 512), lambda b, i: (b * nq + i, 0)),
        out_shape=jax.ShapeDtypeStruct((t, 512), BF16),
        scratch_shapes=[pltpu.VMEM((MLA_HEADS, nq, ATT_BLK, ATT_BLK), F32),
                        pltpu.VMEM((MLA_HEADS, MLA_V, ATT_BLK), F32)],
        compiler_params=_cp(("arbitrary", "arbitrary")),
        name="mla_attn",
    )(qm, km, vmt)


def _diff_attn_kernel(q_ref, k_ref, vt_ref, lam_ref, og_ref, o_ref, s_scr, acc_scr, qm_scr, *, lam_init):
    blk = q_ref.shape[0]
    low = lax.broadcasted_iota(jnp.int32, (blk, LANE), 1) < DIFF_QK
    lp = lam_ref[...]
    lam = (jnp.exp(jnp.sum(lp[0:1] * lp[1:2], axis=-1, keepdims=True))
           - jnp.exp(jnp.sum(lp[2:3] * lp[3:4], axis=-1, keepdims=True)) + lam_init)

    for h in range(DIFF_HEADS):
        qt = q_ref[:, LANE * h:LANE * (h + 1)]
        zero = jnp.zeros_like(qt)
        qm_scr[2 * h] = jnp.where(low, qt, zero)
        qm_scr[2 * h + 1] = jnp.where(low, zero, qt)

    def score_fn(i, ki):
        rows = pl.ds(pl.multiple_of(ki * blk, blk), blk)
        h = i // 2
        return _nt(k_ref[rows, LANE * h:LANE * (h + 1)], qm_scr[i])

    def value_fn(i, ki):
        h = i // 2
        return vt_ref[ki, DIFF_V * h:DIFF_V * (h + 1), :]

    l = _two_pass_attention(2 * DIFF_HEADS, score_fn, value_fn, s_scr, acc_scr, blk)
    for h in range(DIFF_HEADS):
        o_t = acc_scr[2 * h] / l[2 * h] - lam * (acc_scr[2 * h + 1] / l[2 * h + 1])
        on_t = o_t * lax.rsqrt(jnp.mean(o_t * o_t, axis=0, keepdims=True) + EPS)
        o_ref[:, LANE * h:LANE * (h + 1)] = (on_t.T * (og_ref[...] * (1.0 - lam_init))).astype(BF16)


def _diff_attn(qd, kd, vdt, lam_p, og, lam_init, batch, seq):
    t = qd.shape[0]
    nq = seq // ATT_BLK
    return pl.pallas_call(
        functools.partial(_diff_attn_kernel, lam_init=lam_init),
        grid=(batch, nq),
        in_specs=[pl.BlockSpec((ATT_BLK, 512), lambda b, i: (b * nq + i, 0)),
                  pl.BlockSpec((seq, 512), lambda b, i: (b, 0)),
                  pl.BlockSpec((nq, 512, ATT_BLK), lambda b, i: (b, 0, 0)),
                  pl.BlockSpec(lam_p.shape, lambda b, i: (0, 0)),
                  pl.BlockSpec(og.shape, lambda b, i: (0, 0))],
        out_specs=pl.BlockSpec((ATT_BLK, 512), lambda b, i: (b * nq + i, 0)),
        out_shape=jax.ShapeDtypeStruct((t, 512), BF16),
        scratch_shapes=[pltpu.VMEM((2 * DIFF_HEADS, nq, ATT_BLK, ATT_BLK), F32),
                        pltpu.VMEM((2 * DIFF_HEADS, DIFF_V, ATT_BLK), F32),
                        pltpu.VMEM((2 * DIFF_HEADS, ATT_BLK, LANE), BF16)],
        compiler_params=_cp(("arbitrary", "arbitrary")),
        name="diff_attn",
    )(qd, kd, vdt, lam_p, og)


def _swa_kernel(q_ref, kp_ref, kc_ref, vtp_ref, vtc_ref, sink_ref, o_ref):
    w = SWA_WINDOW
    grp = SWA_Q_HEADS // SWA_KV_HEADS
    n = pl.program_id(1)
    key = lax.broadcasted_iota(jnp.int32, (2 * w, grp * w), 0)
    qry = lax.broadcasted_iota(jnp.int32, (2 * w, grp * w), 1) & (w - 1)
    cur_ok = (key >= w) & (key - w <= qry)
    prev_ok = (key < w) & (key > qry)
    low = lax.broadcasted_iota(jnp.int32, (w, LANE), 1) < HEAD_DIM

    for t in range(q_ref.shape[0] // w):
        rows = slice(t * w, (t + 1) * w)
        if t == 0:
            kp, vtp = kp_ref[...], vtp_ref[0]
            valid = cur_ok | (prev_ok & (n > 0))
        else:
            kp, vtp = kc_ref[(t - 1) * w:t * w, :], vtc_ref[t - 1]
            valid = cur_ok | prev_ok
        kc, vtc = kc_ref[rows, :], vtc_ref[t]
        for kv in range(SWA_KV_HEADS):
            kcols = slice(LANE * kv, LANE * (kv + 1))
            vrows = slice(HEAD_DIM * kv, HEAD_DIM * (kv + 1))
            k_win = jnp.concatenate([kp[:, kcols], kc[:, kcols]], axis=0)
            parts = []
            for u in range(2):
                qt = q_ref[rows, LANE * (2 * kv + u):LANE * (2 * kv + u + 1)]
                zero = jnp.zeros_like(qt)
                parts += [jnp.where(low, qt, zero), jnp.where(low, zero, qt)]
            s = jnp.where(valid, _nt(k_win, jnp.concatenate(parts, axis=0)), NEG)
            sink = sink_ref[:, grp * w * kv:grp * w * (kv + 1)]
            m = jnp.maximum(jnp.max(s, axis=0, keepdims=True), sink)
            p = jnp.exp2(s - m)
            den = jnp.sum(p, axis=0, keepdims=True) + jnp.exp2(sink - m)
            vt_win = jnp.concatenate([vtp[vrows, :], vtc[vrows, :]], axis=1)
            o_t = _dot(vt_win, p.astype(BF16)) / den
            for u in range(2):
                pair = jnp.concatenate([o_t[:, 2 * u * w:(2 * u + 1) * w],
                                        o_t[:, (2 * u + 1) * w:(2 * u + 2) * w]], axis=0)
                o_ref[rows, LANE * (2 * kv + u):LANE * (2 * kv + u + 1)] = pair.T.astype(BF16)


def _swa(qs, ks, vst, sink_row, batch, seq):
    t = qs.shape[0]
    w = SWA_WINDOW
    nb = seq // w
    ns = nb // SWA_QB
    cur = lambda b, n: (b * ns + n, 0)
    cur3 = lambda b, n: (b * ns + n, 0, 0)
    prev = lambda b, n: (b * nb + jnp.maximum(n * SWA_QB - 1, 0), 0)
    prev3 = lambda b, n: (b * nb + jnp.maximum(n * SWA_QB - 1, 0), 0, 0)
    return pl.pallas_call(
        _swa_kernel,
        grid=(batch, ns),
        in_specs=[pl.BlockSpec((SWA_QB * w, 512), cur),
                  pl.BlockSpec((w, 256), prev), pl.BlockSpec((SWA_QB * w, 256), cur),
                  pl.BlockSpec((1, LANE, w), prev3), pl.BlockSpec((SWA_QB, LANE, w), cur3),
                  pl.BlockSpec(sink_row.shape, lambda b, n: (0, 0))],
        out_specs=pl.BlockSpec((SWA_QB * w, 512), cur),
        out_shape=jax.ShapeDtypeStruct((t, 512), BF16),
        compiler_params=_cp(("arbitrary", "arbitrary")),
        name="swa_attn",
    )(qs, ks, ks, vst, vst, sink_row)


def _merge_kernel(*refs):
    n_sub = refs[0].shape[0] // MOE_TILE
    h2s = [_merge_mix(k, *refs) for k in range(n_sub)]
    for k in range(n_sub):
        _merge_route(k, h2s[k], *refs)


def _merge_mix(k, h_ref, ya_ref, yb_ref, yc_ref, yd_ref, x_ref, gt1_ref, wg_ref, wb_ref, wo_ref,
               g2_ref, sh2_ref, sc2_ref, rw_ref, rb_ref, xo_ref, h2_ref, comb_ref, dest_ref, meta_ref):
    rows = slice(MOE_TILE * k, MOE_TILE * (k + 1))
    h = h_ref[rows, :]
    d = x_ref.shape[1]
    merged = None
    for b, y_ref in enumerate((ya_ref, yb_ref, yc_ref, yd_ref)):
        gate = jax.nn.sigmoid(_dot(h, wg_ref[:, d * b:d * (b + 1)]))
        term = gate * _dot(y_ref[rows, :], wb_ref[b])
        merged = term if merged is None else merged + term
    xn = x_ref[rows, :] + gt1_ref[0] * _dot(merged.astype(BF16), wo_ref[...])
    xo_ref[rows, :] = xn
    ms = jnp.mean(xn * xn, axis=-1, keepdims=True)
    h2 = xn * lax.rsqrt(ms + EPS) * g2_ref[...]
    h2 = h2 * (1.0 + sc2_ref[0]) + sh2_ref[0]
    h2_ref[rows, :] = h2.astype(BF16)
    return h2


def _merge_route(k, h2, h_ref, ya_ref, yb_ref, yc_ref, yd_ref, x_ref, gt1_ref, wg_ref, wb_ref, wo_ref,
                 g2_ref, sh2_ref, sc2_ref, rw_ref, rb_ref, xo_ref, h2_ref, comb_ref, dest_ref, meta_ref):
    rows = slice(MOE_TILE * k, MOE_TILE * (k + 1))
    hh, hm = _split2(h2)
    wh = rw_ref[...].astype(BF16)
    logits = _dot(hh, wh) + _dot(hm, wh)
    lt = logits.T
    scores = jax.nn.sigmoid(lt[0:N_EXPERTS, :])
    sel = scores + rb_ref[...]
    per = N_EXPERTS // N_GROUPS
    srow = [sel[e:e + 1, :] for e in range(N_EXPERTS)]
    gsum = []
    for g in range(N_GROUPS):
        a, b_, c, e_ = srow[per * g:per * (g + 1)]
        gsum.append(jnp.maximum(jnp.maximum(jnp.maximum(a + b_, a + c), jnp.maximum(a + e_, b_ + c)),
                                jnp.maximum(b_ + e_, c + e_)))
    best = jnp.maximum(jnp.maximum(gsum[0], gsum[1]), jnp.maximum(gsum[2], gsum[3]))
    taken = None
    weights = []
    picks = []
    for g in range(N_GROUPS):
        hit = gsum[g] == best
        pick = hit if taken is None else hit & jnp.logical_not(taken)
        taken = hit if taken is None else taken | hit
        picks.append(pick.astype(F32))
        for e in range(per * g, per * (g + 1)):
            rank = jnp.zeros_like(best)
            for o in range(per * g, per * (g + 1)):
                if o == e:
                    continue
                ahead = (srow[o] > srow[e]) | ((srow[o] == srow[e]) & (o < e))
                rank = rank + ahead.astype(F32)
            weights.append(jnp.where(pick & (rank < 1.5), scores[e:e + 1, :], 0.0))
    wsum = weights[0]
    for r_ in weights[1:]:
        wsum = wsum + r_
    inv = 1.0 / wsum
    rid = lax.broadcasted_iota(jnp.int32, scores.shape, 0)
    comb_e = jnp.zeros_like(scores)
    for e, r_ in enumerate(weights):
        comb_e = jnp.where(rid == e, r_ * inv, comb_e)

    tm = lt.shape[1]
    gid = lax.broadcasted_iota(jnp.int32, (8, tm), 0)
    onehot = jnp.zeros((8, tm), F32)
    for g in range(N_GROUPS):
        onehot = jnp.where(gid == g, picks[g], onehot)
    before = (lax.broadcasted_iota(jnp.int32, (tm, tm), 0) < lax.broadcasted_iota(jnp.int32, (tm, tm), 1))
    rank = _dot(onehot.astype(BF16), before.astype(BF16))
    dest = jnp.zeros((1, tm), F32)
    off = jnp.zeros((1, 1), F32)
    meta = jnp.zeros((8, LANE), F32)
    mrow = lax.broadcasted_iota(jnp.int32, (8, LANE), 0)
    for g in range(N_GROUPS):
        cnt = jnp.sum(picks[g], axis=-1, keepdims=True)
        dest = dest + picks[g] * (off + rank[g:g + 1, :])
        meta = jnp.where(mrow == g, cnt, meta)
        meta = jnp.where(mrow == N_GROUPS + g, off, meta)
        off = off + jnp.ceil(cnt * (1.0 / MOE_ALIGN)) * MOE_ALIGN
    dest_ref[k] = dest.astype(jnp.int32)
    meta_ref[k] = meta.astype(jnp.int32)
    pad = jnp.zeros((LANE - N_EXPERTS - 8, tm), F32)
    comb_t = jnp.concatenate([comb_e, jnp.where(gid == 0, dest, 0.0), pad], axis=0)
    comb_ref[rows, :] = comb_t.T


def _merge(h, ys, x2, gt1, wg, wb, wo, g2, sh2, sc2, rw, rb, seq):
    t, d = x2.shape
    sub = MERGE_SUBTILES
    tm = sub * MOE_TILE
    tpb = seq // tm
    row = lambda i: (i, 0)
    per_b = lambda i: (i // tpb, 0, 0)
    c2 = lambda i: (0, 0)
    once = pl.Buffered(1)
    return pl.pallas_call(
        _merge_kernel,
        grid=(t // tm,),
        in_specs=[pl.BlockSpec((tm, d), row)] + [pl.BlockSpec((tm, 512), row)] * 4
                 + [pl.BlockSpec((tm, d), row), pl.BlockSpec((1, 1, d), per_b),
                    pl.BlockSpec(wg.shape, c2, pipeline_mode=once),
                    pl.BlockSpec(wb.shape, lambda i: (0, 0, 0), pipeline_mode=once),
                    pl.BlockSpec(wo.shape, c2, pipeline_mode=once), pl.BlockSpec((1, d), c2),
                    pl.BlockSpec((1, 1, d), per_b), pl.BlockSpec((1, 1, d), per_b),
                    pl.BlockSpec(rw.shape, c2), pl.BlockSpec(rb.shape, c2)],
        out_specs=[pl.BlockSpec((tm, d), row), pl.BlockSpec((tm, d), row), pl.BlockSpec((tm, LANE), row),
                   pl.BlockSpec((sub, 1, MOE_TILE), lambda i: (i, 0, 0)),
                   pl.BlockSpec((sub, 8, LANE), lambda i: (i, 0, 0))],
        out_shape=[jax.ShapeDtypeStruct((t, d), F32), jax.ShapeDtypeStruct((t, d), BF16),
                   jax.ShapeDtypeStruct((t, LANE), F32),
                   jax.ShapeDtypeStruct((t // MOE_TILE, 1, MOE_TILE), jnp.int32),
                   jax.ShapeDtypeStruct((t // MOE_TILE, 8, LANE), jnp.int32)],
        compiler_params=_cp(("arbitrary",)),
        name="merge_router",
    )(h, *ys, x2, gt1, wg, wb, wo, g2, sh2, sc2, rw, rb)


def _moe_kernel(meta_ref, h2_ref, comb_ref, dest_ref, x_ref, gt2_ref, wg_ref, wu_ref, wd_ref, o_ref,
                sorted_s, csort_s, out_s):
    i = pl.program_id(0)
    tm = h2_ref.shape[0]
    n_rows = sorted_s.shape[0]
    per = N_EXPERTS // N_GROUPS
    comb = comb_ref[...]

    place = (lax.broadcasted_iota(jnp.int32, (n_rows, tm), 0) == dest_ref[0]).astype(BF16)
    sorted_s[...] = _dot(place, h2_ref[...]).astype(BF16)
    c_hi, c_lo = _split2(comb)
    csort_s[...] = _dot(place, c_hi) + _dot(place, c_lo)
    out_s[...] = jnp.zeros_like(out_s)
    def run_experts(g, start, size):
        rows = pl.ds(pl.multiple_of(start, MOE_ALIGN), size)
        xk = sorted_s[rows, :]
        cw = csort_s[rows, :]
        lane = lax.broadcasted_iota(jnp.int32, (size, LANE), 1)
        acc = None
        for j in range(per):
            e = per * g + j
            hid = _silu(_dot(xk, wg_ref[0, e])) * _dot(xk, wu_ref[0, e])
            ce = jnp.sum(jnp.where(lane == e, cw, 0.0), axis=-1, keepdims=True)
            term = _dot((hid * ce).astype(BF16), wd_ref[0, e])
            acc = term if acc is None else acc + term
        out_s[rows, :] += acc

    def group(g, carry):
        cnt = meta_ref[i, g]
        off = meta_ref[i, N_GROUPS + g]
        big = MOE_CHUNKS[-1]
        n_big = cnt // big

        def big_chunk(c, inner):
            run_experts(g, off + c * big, big)
            return inner

        lax.fori_loop(0, n_big, big_chunk, 0)
        rest = cnt - n_big * big
        lower = 0
        for size in MOE_CHUNKS:
            @pl.when((rest > lower) & (rest <= size))
            def _(size=size):
                run_experts(g, off + n_big * big, size)
            lower = size
        return carry

    lax.fori_loop(0, N_GROUPS, group, 0)

    dest_col = comb[:, MOE_DEST_LANE:MOE_DEST_LANE + 1].astype(jnp.int32)
    back = (lax.broadcasted_iota(jnp.int32, (tm, n_rows), 1) == dest_col).astype(BF16)
    o_ref[...] = x_ref[...] + gt2_ref[0] * _dot(back, out_s[...].astype(BF16))


def _moe(h2, comb, dest, meta, x2, gt2, wg, wu, wd, layer, seq):
    t, d = x2.shape
    tm = MOE_TILE
    tpb = seq // tm
    n_rows = MOE_SORT_ROWS
    overhang = max(b - a for a, b in zip((0,) + MOE_CHUNKS, MOE_CHUNKS)) - 1
    assert tm + N_GROUPS * (MOE_ALIGN - 1) + overhang <= n_rows
    row = lambda i, m: (i, 0)
    whole = lambda a: pl.BlockSpec((1,) + a.shape[1:], lambda i, m: (layer, 0, 0, 0),
                                   pipeline_mode=pl.Buffered(1))
    grid_spec = pltpu.PrefetchScalarGridSpec(
        num_scalar_prefetch=1,
        grid=(t // tm,),
        in_specs=[pl.BlockSpec((tm, d), row), pl.BlockSpec((tm, LANE), row),
                  pl.BlockSpec((1, 1, tm), lambda i, m: (i, 0, 0)),
                  pl.BlockSpec((tm, d), row),
                  pl.BlockSpec((1, 1, d), lambda i, m: (i // tpb, 0, 0)),
                  whole(wg), whole(wu), whole(wd)],
        out_specs=pl.BlockSpec((tm, d), row),
        scratch_shapes=[pltpu.VMEM((n_rows, d), BF16), pltpu.VMEM((n_rows, LANE), F32),
                        pltpu.VMEM((n_rows, d), F32)])
    return pl.pallas_call(
        _moe_kernel,
        grid_spec=grid_spec,
        out_shape=jax.ShapeDtypeStruct((t, d), F32),
        compiler_params=_cp(("arbitrary",)),
        name="moe",
    )(meta, h2, comb, dest, x2, gt2, wg, wu, wd)


def _layer_params(l, w_in, hg_onorm, mla_q_norm, mla_kv_norm, mla_w_uq, mla_w_ukv, mla_qk_norm,
                  diff_qk_norm, swa_qk_norm, swa_sinks, lb_all):
    ends = [sum(IN_SPLITS[:i]) for i in range(len(IN_SPLITS) + 1)]
    cols = lambda a, b: w_in[l, :, ends[a]:ends[b]].astype(BF16)
    mla_pad = 512 - (ends[7] - ends[4])
    p = {"w_hg": cols(0, 4),
         "w_mla": jnp.pad(cols(4, 7), ((0, 0), (0, mla_pad))),
         "w_diff": cols(7, 10), "w_swa": cols(10, 13), "wg": cols(13, 14)}

    lb = lb_all[l]
    p["loglb"] = jnp.log(lb)[None, :]
    p["log1mlb"] = jnp.log1p(-lb)[None, :]
    p["ogain"] = jnp.tile(hg_onorm[l], HG_HEADS)[None, :]

    hd = MLA_NOPE + MLA_ROPE
    half = MLA_ROPE // 2
    wq = mla_w_uq[l].reshape(MLA_Q_RANK, MLA_HEADS, hd)
    z = lambda r, n: jnp.zeros((r, MLA_HEADS, n), F32)
    nope, rope = wq[:, :, :MLA_NOPE], wq[:, :, MLA_NOPE:]
    p["wqa"] = jnp.concatenate([nope, rope, z(MLA_Q_RANK, 32)], -1).reshape(MLA_Q_RANK, -1).astype(BF16)
    p["wqb"] = jnp.concatenate([z(MLA_Q_RANK, MLA_NOPE), rope[:, :, half:], rope[:, :, :half],
                                z(MLA_Q_RANK, 32)], -1).reshape(MLA_Q_RANK, -1).astype(BF16)
    wkv = mla_w_ukv[l].reshape(MLA_KV_RANK, MLA_HEADS, MLA_NOPE + MLA_V)
    knope, vproj = wkv[:, :, :MLA_NOPE], wkv[:, :, MLA_NOPE:]
    eye = jnp.eye(MLA_ROPE, dtype=F32)
    swap = jnp.concatenate([eye[:, half:], eye[:, :half]], axis=1)
    place = lambda m: jnp.broadcast_to(
        jnp.concatenate([jnp.zeros((MLA_ROPE, MLA_NOPE), F32), m, jnp.zeros((MLA_ROPE, 32), F32)], -1)[:, None, :],
        (MLA_ROPE, MLA_HEADS, LANE))
    pad_rows = 256 - MLA_KV_RANK - MLA_ROPE
    p["wka"] = jnp.concatenate([jnp.concatenate([knope, z(MLA_KV_RANK, 64)], -1), place(eye),
                                z(pad_rows, LANE)], 0).reshape(256, -1).astype(BF16)
    p["wkb"] = jnp.concatenate([z(MLA_KV_RANK, LANE), place(swap), z(pad_rows, LANE)], 0
                               ).reshape(256, -1).astype(BF16)
    p["wv"] = jnp.concatenate([vproj.reshape(MLA_KV_RANK, -1),
                               jnp.zeros((256 - MLA_KV_RANK, MLA_HEADS * MLA_V), F32)], 0).astype(BF16)
    p["qng"] = mla_q_norm[l][None, :]
    p["kvg"] = jnp.concatenate([mla_kv_norm[l], jnp.ones((256 - MLA_KV_RANK,), F32)])[None, :]

    def rope_gains(g):
        base = jnp.concatenate([g, jnp.zeros((LANE - hd,), F32)])
        part = jnp.concatenate([jnp.zeros((MLA_NOPE,), F32), g[MLA_NOPE + half:], g[MLA_NOPE:MLA_NOPE + half],
                                jnp.zeros((LANE - hd,), F32)])
        return base[None, :], part[None, :]

    p["gq"], p["gqs"] = rope_gains(mla_qk_norm[l, 0])
    p["gk"], p["gks"] = rope_gains(mla_qk_norm[l, 1])
    p["dgq"] = jnp.tile(diff_qk_norm[l, 0], 8)[None, :]
    p["dgk"] = jnp.tile(diff_qk_norm[l, 1], 8)[None, :]
    p["sgq"] = jnp.tile(swa_qk_norm[l, 0], 8)[None, :]
    p["sgk"] = jnp.tile(swa_qk_norm[l, 1], 2)[None, :]
    p["sinks"] = jnp.repeat(swa_sinks[l].astype(F32) * LOG2E, SWA_WINDOW)[None, :]
    return p


def _rope_tables(positions):
    inv_freq = ROPE_BASE ** (-jnp.arange(0, MLA_ROPE, 2, dtype=F32) / MLA_ROPE)
    ang = positions.astype(F32).reshape(-1)[:, None] * inv_freq
    cos, sin = jnp.cos(ang), jnp.sin(ang)
    t = ang.shape[0]
    cosf = jnp.concatenate([jnp.ones((t, MLA_NOPE), F32), cos, cos, jnp.zeros((t, 32), F32)], axis=1)
    sinf = jnp.concatenate([jnp.zeros((t, MLA_NOPE), F32), -sin, sin, jnp.zeros((t, 32), F32)], axis=1)
    return cosf, sinf


def kernel(x, c, positions, ada_w, ada_b, norm_mix, norm_ffn, w_in, hg_lb_logits, hg_onorm, mla_q_norm, mla_kv_norm, mla_w_uq, mla_w_ukv, mla_qk_norm, diff_qk_norm, diff_lam, diff_onorm, swa_qk_norm, swa_sinks, w_branch, w_out, router_w, router_bias, moe_w_gate, moe_w_up, moe_w_down):
    batch, seq, d = x.shape
    x2 = x.reshape(batch * seq, d)
    cosf, sinf = _rope_tables(positions)
    lb_all = jnp.cumsum(jax.nn.softmax(hg_lb_logits.astype(F32), axis=0), axis=0)
    lb_all = lb_all - lb_all[0]
    mod = _modulation(c, ada_w, ada_b)
    rw = jnp.concatenate([router_w, jnp.zeros((d, LANE - N_EXPERTS), F32)], axis=1)
    rb = router_bias.astype(F32)[:, None]
    moe_w = (moe_w_gate.astype(BF16), moe_w_up.astype(BF16), moe_w_down.astype(BF16))

    for l in range(DEPTH):
        sh1, sc1, gt1, sh2, sc2, gt2 = [mod[l, :, d * k:d * (k + 1)][:, None, :] for k in range(6)]
        p = _layer_params(l, w_in, hg_onorm, mla_q_norm, mla_kv_norm, mla_w_uq, mla_w_ukv, mla_qk_norm,
                          diff_qk_norm, swa_qk_norm, swa_sinks, lb_all)
        hg3, hf, mla, diff, swa, h = _inproj(x2, norm_mix[l][None, :], sh1, sc1,
                                             (p["w_hg"], p["w_mla"], p["w_diff"], p["w_swa"]), seq)
        y_a = _hgrn(hg3, hf, p["loglb"], p["log1mlb"], p["ogain"], batch, seq)
        qm, km, vmt, qd, kd, vdt, qs, ks, vst = _prep(mla, diff, swa, cosf, sinf, p)
        y_b = _mla_attn(qm, km, vmt, batch, seq)
        lam_init = 0.8 - 0.6 * math.exp(-0.3 * l)
        y_c = _diff_attn(qd, kd, vdt, diff_lam[l], diff_onorm[l][None, :], lam_init, batch, seq)
        y_d = _swa(qs, ks, vst, p["sinks"], batch, seq)
        x2, h2, comb, dest, meta = _merge(h, (y_a, y_b, y_c, y_d), x2, gt1, p["wg"], w_branch[l].astype(BF16),
                                          w_out[l].astype(BF16), norm_ffn[l][None, :], sh2, sc2, rw, rb, seq)
        x2 = _moe(h2, comb, dest, meta[:, :, 0], x2, gt2, *moe_w, l, seq)
    return x2.reshape(batch, seq, d)
```

```python
import functools
import math

import jax
import jax.numpy as jnp
from jax import lax
from jax.experimental import pallas as pl
from jax.experimental.pallas import tpu as pltpu

F32 = jnp.float32
BF16 = jnp.bfloat16

D_MODEL = 1024
DEPTH = 2
EPS = 1e-6
N_BRANCH = 4
HG_HEADS = 8
HG_DK = 64
HG_W = HG_HEADS * HG_DK
HG_SUB = 16
HG_CHUNK = 64
HG_SAFE_DECAY = 80.0
HG_DIAG_ROWS =(HG_SUB // 2) * HG_SUB + (HG_SUB // 2) ** 2
MLA_HEADS = 8
MLA_Q_RANK = 256
MLA_KV_RANK = 128
MLA_NOPE = 64
MLA_ROPE = 32
MLA_V = 64
ROPE_BASE = 10000.0
DIFF_HEADS = 4
DIFF_QK = 64
DIFF_V = 128
SWA_Q_HEADS = 8
SWA_KV_HEADS = 2
SWA_WINDOW = 128
HEAD_DIM = 64
N_EXPERTS = 16
N_GROUPS = 4
D_FF_EXPERT = 256
IN_SPLITS = (512, 512, 512, 512, 256, 128, 32, 512, 512, 512, 512, 128, 128, 4096)

MOE_TILE = 512
MERGE_SUBTILES = 2
MOE_ALIGN = 16
MOE_CHUNKS = (64, 128, 192, 256)
MOE_SORT_ROWS = 640
MOE_DEST_LANE = N_EXPERTS
LANE = 128
ATT_BLK = 256
SWA_QB = 8
LOG2E = 1.4426950408889634
NEG = -1e30
VMEM_LIMIT = 56 * 1024 * 1024


def _cp(sem, vmem=VMEM_LIMIT):
    return pltpu.CompilerParams(dimension_semantics=sem, vmem_limit_bytes=vmem)


def _nt(a, b):
    return lax.dot_general(a, b, (((1,), (1,)), ((), ())), preferred_element_type=F32)


def _tn(a, b):
    return lax.dot_general(a, b, (((0,), (0,)), ((), ())), preferred_element_type=F32)


def _dot(a, b):
    return jnp.dot(a, b, preferred_element_type=F32)


def _split2(x):
    hi = x.astype(BF16)
    lo = (x - hi.astype(F32)).astype(BF16)
    return hi, lo


def _seg_id(idx, seg):
    shift = seg.bit_length() - 1
    assert 1 << shift == seg
    return lax.shift_right_logical(idx, shift)


def _same_seg(n, seg):
    r = lax.broadcasted_iota(jnp.int32, (n, n), 0)
    c = lax.broadcasted_iota(jnp.int32, (n, n), 1)
    return _seg_id(r, seg) == _seg_id(c, seg)


def _seg_ones(n, seg):
    return _same_seg(n, seg).astype(BF16)


def _seg_mean_sq(x, seg):
    n = x.shape[-1]
    hi, lo = _split2(x * x)
    ones = _seg_ones(n, seg)
    return (_dot(hi, ones) + _dot(lo, ones)) * (1.0 / seg)


def _silu(x):
    return x * jax.nn.sigmoid(x)


def _mod_kernel(c_ref, w_ref, b_ref, o_ref):
    c = c_ref[...]
    o_ref[0] = jnp.dot(_silu(c), w_ref[0], preferred_element_type=F32,
                       precision=lax.Precision.HIGHEST) + b_ref[0]


def _modulation(c, ada_w, ada_b):
    nl, d, n6 = ada_w.shape
    b = c.shape[0]
    tn = 1536
    return pl.pallas_call(
        _mod_kernel,
        grid=(nl, n6 // tn),
        in_specs=[pl.BlockSpec((b, d), lambda l, j: (0, 0)),
                  pl.BlockSpec((1, d, tn), lambda l, j: (l, 0, j)),
                  pl.BlockSpec((1, 1, tn), lambda l, j: (l, 0, j))],
        out_specs=pl.BlockSpec((1, b, tn), lambda l, j: (l, 0, j)),
        out_shape=jax.ShapeDtypeStruct((nl, b, n6), F32),
        compiler_params=_cp(("arbitrary", "arbitrary")),
        name="modulation",
    )(c, ada_w, ada_b.reshape(nl, 1, n6))


def _inproj_kernel(x_ref, g_ref, sh_ref, sc_ref, whg_ref, wmla_ref, wdiff_ref, wswa_ref,
                   ohg_ref, ohf_ref, omla_ref, odiff_ref, oswa_ref, oh_ref):
    x = x_ref[...]
    ms = jnp.mean(x * x, axis=-1, keepdims=True)
    h = x * lax.rsqrt(ms + EPS) * g_ref[...]
    h = h * (1.0 + sc_ref[0]) + sh_ref[0]
    hb = h.astype(BF16)
    oh_ref[...] = hb

    ohg_ref[:, 0:512] = _dot(hb, whg_ref[:, 0:512]).astype(BF16)
    ohf_ref[...] = _dot(hb, whg_ref[:, 512:1024])
    ohg_ref[:, 512:1024] = _dot(hb, whg_ref[:, 1024:1536]).astype(BF16)
    ohg_ref[:, 1024:1536] = _dot(hb, whg_ref[:, 1536:2048]).astype(BF16)
    omla_ref[...] = _dot(hb, wmla_ref[...]).astype(BF16)
    for k in range(3):
        odiff_ref[:, 512 * k:512 * (k + 1)] = _dot(hb, wdiff_ref[:, 512 * k:512 * (k + 1)]).astype(BF16)
    oswa_ref[:, 0:512] = _dot(hb, wswa_ref[:, 0:512]).astype(BF16)
    oswa_ref[:, 512:768] = _dot(hb, wswa_ref[:, 512:768]).astype(BF16)


def _inproj(x2, gain, sh, sc, weights, seq):
    t, d = x2.shape
    tm = 512
    tpb = seq // tm
    row = lambda i: (i, 0)
    per_b = lambda i: (i // tpb, 0, 0)
    outs = [(1536, BF16), (512, F32), (512, BF16), (1536, BF16), (768, BF16), (d, BF16)]
    return pl.pallas_call(
        _inproj_kernel,
        grid=(t // tm,),
        in_specs=[pl.BlockSpec((tm, d), row),
                  pl.BlockSpec((1, d), lambda i: (0, 0)),
                  pl.BlockSpec((1, 1, d), per_b),
                  pl.BlockSpec((1, 1, d), per_b)]
                 + [pl.BlockSpec(w.shape, lambda i: (0, 0)) for w in weights],
        out_specs=[pl.BlockSpec((tm, w), row) for w, _ in outs],
        out_shape=[jax.ShapeDtypeStruct((t, w), dt) for w, dt in outs],
        compiler_params=_cp(("arbitrary",)),
        name="inproj",
    )(x2, gain, sh, sc, *weights)


def _segment_cumsum(x, seg):
    n = x.shape[0]
    r = lax.broadcasted_iota(jnp.int32, (n, n), 0)
    cc = lax.broadcasted_iota(jnp.int32, (n, n), 1)
    same = _same_seg(n, seg)
    tri = (same & (cc <= r)).astype(BF16)
    blk = same.astype(BF16)
    hi, lo = _split2(x)
    return _dot(tri, hi) + _dot(tri, lo), _dot(blk, hi) + _dot(blk, lo)


def _hgrn_chunk_path(i_ref, st_ref, c_s, tot_s, qs_s, kk_s, qe_s, kd_s, ke_s, dec_s, od_s):
    rows_blk = c_s.shape[0]
    tot = tot_s[...]
    rel = c_s[...] - 0.5 * tot
    half_dec = jnp.exp(0.5 * tot)
    kd = kk_s[...] * jnp.exp(-rel)
    qe_s[...] = (qs_s[...] * jnp.exp(rel)).astype(BF16)
    kd_s[...] = kd.astype(BF16)
    ke_s[...] = (kd * half_dec).astype(BF16)
    dec_s[...] = half_dec

    row = lax.broadcasted_iota(jnp.int32, (2 * rows_blk, rows_blk), 0) & (rows_blk - 1)
    col = lax.broadcasted_iota(jnp.int32, (2 * rows_blk, rows_blk), 1)
    intra = (_seg_id(row, HG_CHUNK) == _seg_id(col, HG_CHUNK)) & (col <= row)
    low_q = lax.broadcasted_iota(jnp.int32, (rows_blk, LANE), 1) < HG_DK
    low_c = lax.broadcasted_iota(jnp.int32, (HG_CHUNK, LANE), 1) < HG_DK

    for j in range(HG_W // LANE):
        cols = slice(LANE * j, LANE * (j + 1))
        qe = qe_s[:, cols]
        zero = jnp.zeros_like(qe)
        q2 = jnp.concatenate([jnp.where(low_q, qe, zero), jnp.where(low_q, zero, qe)], axis=0)
        attn = jnp.where(intra, _nt(q2, kd_s[:, cols]), 0.0).astype(BF16)
        o2 = _dot(attn, i_ref[:, cols])
        n_chunks = rows_blk // HG_CHUNK
        chunk_rows = [slice(HG_CHUNK * ch, HG_CHUNK * (ch + 1)) for ch in range(n_chunks)]
        upd = [_tn(i_ref[rows, cols], ke_s[rows, cols]) for rows in chunk_rows]
        st = st_ref[j]
        for ch, rows in enumerate(chunk_rows):
            rows_hi = slice(rows_blk + HG_CHUNK * ch, rows_blk + HG_CHUNK * (ch + 1))
            hd = dec_s[HG_CHUNK * ch:HG_CHUNK * ch + 1, cols]
            inter = _nt(jnp.concatenate([q2[rows], q2[rows_hi]], axis=0), (st * hd).astype(BF16))
            od_s[rows, cols] = jnp.where(low_c, o2[rows] + inter[:HG_CHUNK], o2[rows_hi] + inter[HG_CHUNK:])
            st = st * (hd * hd) + upd[ch]
        st_ref[j] = st


def _hgrn_exact_path(i_ref, st_ref, lf_s, c_s, qs_s, kk_s, qe_s, ke_s, dec_s, od_s, t_s, a_s):
    rows_blk = c_s.shape[0]
    n_sub = rows_blk // HG_SUB
    c, tot = _segment_cumsum(lf_s[...], HG_SUB)
    c_s[...] = c
    qe_s[...] = (qs_s[...] * jnp.exp(c)).astype(BF16)
    ke_s[...] = (kk_s[...] * jnp.exp(tot - c)).astype(BF16)
    dec_s[...] = jnp.exp(tot)

    same_head = _same_seg(LANE, HG_DK)
    head_mask = same_head.astype(F32)
    head_ones = same_head.astype(BF16)
    for j in range(HG_W // LANE):
        st_ref[j] = st_ref[j] * head_mask
    half = HG_SUB // 2
    trow = lax.broadcasted_iota(jnp.int32, (half, HG_W), 0)

    def body(i, carry):
        r0 = pl.multiple_of(i * HG_SUB, HG_SUB)
        rows = pl.ds(r0, HG_SUB)
        c_i = c_s[rows, :]
        qs_i = qs_s[rows, :]
        kk_i = kk_s[rows, :]
        v_i = i_ref[rows, :].astype(F32)
        c_lo, c_hi = c_i[:half], c_i[half:]
        q_lo, q_hi = qs_i[:half], qs_i[half:]
        for s in range(half):
            c_row, k_row = c_i[s:s + 1, :], kk_i[s:s + 1, :]
            e_lo = jnp.exp(jnp.where(trow >= s, c_lo - c_row, NEG))
            e_hi = jnp.exp(c_hi - c_row)
            both = jnp.concatenate([e_lo * q_lo, e_hi * q_hi], axis=0) * k_row
            t_s[s * HG_SUB:(s + 1) * HG_SUB, :] = both.astype(BF16)
        for s in range(half, HG_SUB, 2):
            pair = []
            for u in (s, s + 1):
                e_hi = jnp.exp(jnp.where(trow >= u - half, c_hi - c_i[u:u + 1, :], NEG))
                pair.append(e_hi * q_hi * kk_i[u:u + 1, :])
            base = half * HG_SUB + (s - half) * half
            t_s[base:base + HG_SUB, :] = jnp.concatenate(pair, axis=0).astype(BF16)
        for j in range(HG_W // LANE):
            cols = slice(LANE * j, LANE * (j + 1))
            a_s[:, cols] = _dot(t_s[:, cols], head_ones)
        acc_lo = jnp.zeros((half, HG_W), F32)
        acc_hi = jnp.zeros((half, HG_W), F32)
        for s in range(half):
            acc_lo = acc_lo + a_s[s * HG_SUB:s * HG_SUB + half, :] * v_i[s:s + 1, :]
            acc_hi = acc_hi + a_s[s * HG_SUB + half:(s + 1) * HG_SUB, :] * v_i[s:s + 1, :]
        for s in range(half, HG_SUB):
            base = half * HG_SUB + (s - half) * half
            acc_hi = acc_hi + a_s[base:base + half, :] * v_i[s:s + 1, :]
        acc = jnp.concatenate([acc_lo, acc_hi], axis=0)
        for j in range(HG_W // LANE):
            cols = slice(LANE * j, LANE * (j + 1))
            st = st_ref[j]
            o_int = _nt(qe_s[rows, cols], st.astype(BF16))
            upd = _tn(i_ref[rows, cols], ke_s[rows, cols])
            st_ref[j] = st * dec_s[pl.ds(r0, 1), cols] + upd * head_mask
            od_s[rows, cols] = acc[:, cols] + o_int
        return carry

    lax.fori_loop(0, n_sub, body, 0)


def _hgrn_kernel(q_ref, i_ref, g_ref, f_ref, loglb_ref, log1mlb_ref, og_ref, o_ref,
                 st_ref, lf_s, c_s, tot_s, qs_s, kk_s, qe_s, kd_s, ke_s, dec_s, od_s, t_s, a_s):
    @pl.when(pl.program_id(1) == 0)
    def _():
        st_ref[...] = jnp.zeros_like(st_ref)

    fr = f_ref[...]
    ls = jnp.minimum(fr, 0.0) - jnp.log(1.0 + jnp.exp(-jnp.abs(fr)))
    a = loglb_ref[...]
    c2 = log1mlb_ref[...] + ls
    lf = jnp.maximum(a, c2) + jnp.log(1.0 + jnp.exp(-jnp.abs(a - c2)))
    lf_s[...] = lf
    qs_s[...] = _silu(q_ref[...].astype(F32))
    kk_s[...] = 1.0 - jnp.exp(lf)
    c, tot = _segment_cumsum(lf, HG_CHUNK)
    c_s[...] = c
    tot_s[...] = tot
    safe = 0.5 * jnp.max(-tot) <= HG_SAFE_DECAY

    @pl.when(safe)
    def _():
        _hgrn_chunk_path(i_ref, st_ref, c_s, tot_s, qs_s, kk_s, qe_s, kd_s, ke_s, dec_s, od_s)

    @pl.when(jnp.logical_not(safe))
    def _():
        _hgrn_exact_path(i_ref, st_ref, lf_s, c_s, qs_s, kk_s, qe_s, ke_s, dec_s, od_s, t_s, a_s)

    o = od_s[...]
    ms = _seg_mean_sq(o, HG_DK)
    on = o * lax.rsqrt(ms + EPS) * og_ref[...]
    o_ref[...] = (on * _silu(g_ref[...].astype(F32))).astype(BF16)


def _hgrn(hg3, hf, loglb, log1mlb, ogain, batch, seq):
    t = hf.shape[0]
    rb = 256
    nb = seq // rb
    blk = lambda k: pl.BlockSpec((rb, HG_W), lambda b, n, k=k: (b * nb + n, k))
    vec = pl.BlockSpec((1, HG_W), lambda b, n: (0, 0))
    f32_blk = pltpu.VMEM((rb, HG_W), F32)
    bf16_blk = pltpu.VMEM((rb, HG_W), BF16)
    return pl.pallas_call(
        _hgrn_kernel,
        grid=(batch, nb),
        in_specs=[blk(0), blk(1), blk(2), blk(0), vec, vec, vec],
        out_specs=blk(0),
        out_shape=jax.ShapeDtypeStruct((t, HG_W), BF16),
        scratch_shapes=[pltpu.VMEM((HG_W // LANE, LANE, LANE), F32),
                        f32_blk, f32_blk, f32_blk, f32_blk, f32_blk,
                        bf16_blk, bf16_blk, bf16_blk,
                        f32_blk, f32_blk,
                        pltpu.VMEM((HG_DIAG_ROWS, HG_W), BF16),
                        pltpu.VMEM((HG_DIAG_ROWS, HG_W), F32)],
        compiler_params=_cp(("arbitrary", "arbitrary")),
        name="hgrn2",
    )(hg3, hg3, hg3, hf, loglb, log1mlb, ogain)


def _store_transposed_blocks(out_ref, v):
    blk = out_ref.shape[2]
    for u in range(out_ref.shape[0]):
        out_ref[u] = v[u * blk:(u + 1) * blk, :].T.astype(BF16)


def _prep_kernel(mla_ref, diff_ref, swa_ref, cos_ref, sin_ref,
                 qng_ref, kvg_ref, wqa_ref, wqb_ref, wka_ref, wkb_ref, wv_ref,
                 gq_ref, gqs_ref, gk_ref, gks_ref, dgq_ref, dgk_ref, sgq_ref, sgk_ref,
                 qm_ref, km_ref, vmt_ref, qd_ref, kd_ref, vdt_ref, qs_ref, ks_ref, vst_ref):
    blk = mla_ref[...].astype(F32)
    cq = blk[:, :MLA_Q_RANK]
    rest = blk[:, MLA_Q_RANK:]
    cqn = cq * lax.rsqrt(jnp.mean(cq * cq, axis=-1, keepdims=True) + EPS) * qng_ref[...]
    lane = lax.broadcasted_iota(jnp.int32, rest.shape, 1)
    is_kv = lane < MLA_KV_RANK
    ms_kv = jnp.sum(jnp.where(is_kv, rest * rest, 0.0), axis=-1, keepdims=True) * (1.0 / MLA_KV_RANK)
    restn = jnp.where(is_kv, rest * lax.rsqrt(ms_kv + EPS) * kvg_ref[...], rest)
    cqb = cqn.astype(BF16)
    rb = restn.astype(BF16)
    qa = _dot(cqb, wqa_ref[...])
    qb = _dot(cqb, wqb_ref[...])
    ka = _dot(rb, wka_ref[...])
    kb = _dot(rb, wkb_ref[...])
    _store_transposed_blocks(vmt_ref, _dot(rb, wv_ref[...]))
    cosf = cos_ref[...]
    sinf = sin_ref[...]
    cq_t = cosf * gq_ref[...]
    sq_t = sinf * gqs_ref[...]
    ck_t = cosf * gk_ref[...]
    sk_t = sinf * gks_ref[...]
    inv_n = 1.0 / (MLA_NOPE + MLA_ROPE)
    scale = (MLA_NOPE + MLA_ROPE) ** -0.5 * LOG2E
    for h in range(MLA_HEADS):
        cols = slice(LANE * h, LANE * (h + 1))
        x = qa[:, cols]
        rinv = lax.rsqrt(jnp.sum(x * x, axis=-1, keepdims=True) * inv_n + EPS)
        qm_ref[:, cols] = ((x * cq_t + qb[:, cols] * sq_t) * (rinv * scale)).astype(BF16)
        y = ka[:, cols]
        rinv = lax.rsqrt(jnp.sum(y * y, axis=-1, keepdims=True) * inv_n + EPS)
        km_ref[:, cols] = ((y * ck_t + kb[:, cols] * sk_t) * rinv).astype(BF16)

    def seg_norm(x, gain, scale):
        return x * lax.rsqrt(_seg_mean_sq(x, HEAD_DIM) + EPS) * (gain * scale)

    dq = diff_ref[:, 0:512].astype(F32)
    dk = diff_ref[:, 512:1024].astype(F32)
    qd_ref[...] = seg_norm(dq, dgq_ref[...], DIFF_QK ** -0.5 * LOG2E).astype(BF16)
    kd_ref[...] = seg_norm(dk, dgk_ref[...], 1.0).astype(BF16)
    _store_transposed_blocks(vdt_ref, diff_ref[:, 1024:1536].astype(F32))

    sq = swa_ref[:, 0:512].astype(F32)
    qs_ref[...] = seg_norm(sq, sgq_ref[...], HEAD_DIM ** -0.5 * LOG2E).astype(BF16)
    skv = swa_ref[:, 512:768].astype(F32)
    kn = seg_norm(skv[:, :LANE], sgk_ref[...], 1.0)
    low = lax.broadcasted_iota(jnp.int32, kn.shape, 1) < HEAD_DIM
    sw = pltpu.roll(kn, HEAD_DIM, 1)
    ks_ref[:, :LANE] = jnp.where(low, kn, sw).astype(BF16)
    ks_ref[:, LANE:] = jnp.where(low, sw, kn).astype(BF16)
    _store_transposed_blocks(vst_ref, skv[:, LANE:])


def _prep(mla, diff, swa, cosf, sinf, p):
    t = mla.shape[0]
    tm = 512
    row = lambda i: (i, 0)
    full = lambda a: pl.BlockSpec(a.shape, lambda i: (0,) * a.ndim)
    consts = [p["qng"], p["kvg"], p["wqa"], p["wqb"], p["wka"], p["wkb"], p["wv"],
              p["gq"], p["gqs"], p["gk"], p["gks"], p["dgq"], p["dgk"], p["sgq"], p["sgk"]]
    def rows_out(w):
        return pl.BlockSpec((tm, w), row), jax.ShapeDtypeStruct((t, w), BF16)

    def transposed_out(n, blk):
        return (pl.BlockSpec((tm // blk, n, blk), lambda i: (i, 0, 0)),
                jax.ShapeDtypeStruct((t // blk, n, blk), BF16))

    outs = [rows_out(1024), rows_out(1024), transposed_out(512, ATT_BLK),
            rows_out(512), rows_out(512), transposed_out(512, ATT_BLK),
            rows_out(512), rows_out(256), transposed_out(LANE, SWA_WINDOW)]
    return pl.pallas_call(
        _prep_kernel,
        grid=(t // tm,),
        in_specs=[pl.BlockSpec((tm, 512), row),
                  pl.BlockSpec((tm, 1536), row),
                  pl.BlockSpec((tm, 768), row),
                  pl.BlockSpec((tm, LANE), row),
                  pl.BlockSpec((tm, LANE), row)] + [full(a) for a in consts],
        out_specs=[o[0] for o in outs],
        out_shape=[o[1] for o in outs],
        compiler_params=_cp(("arbitrary",)),
        name="attn_prep",
    )(mla, diff, swa, cosf, sinf, *consts)


def _causal_t(blk):
    key = lax.broadcasted_iota(jnp.int32, (blk, blk), 0)
    qry = lax.broadcasted_iota(jnp.int32, (blk, blk), 1)
    return key <= qry


def _two_pass_attention(n_sets, score_fn, value_fn, s_scr, acc_scr, blk):
    qi = pl.program_id(1)
    causal = _causal_t(blk)

    def scores(ki, m, masked):
        out = []
        for i in range(n_sets):
            s = score_fn(i, ki)
            if masked:
                s = jnp.where(causal, s, NEG)
            s_scr[i, ki] = s
            out.append(jnp.maximum(m[i], jnp.max(s, axis=0, keepdims=True)))
        return tuple(out)

    def blocked(n, step, carry):
        def many(k0, count, c):
            return step(tuple(k0 + u for u in range(count)), c)
        carry = lax.fori_loop(0, n // 4, lambda kp, c: many(4 * kp, 4, c), carry)
        done = (n // 4) * 4
        carry = lax.cond(n - done >= 2, lambda c: many(done, 2, c), lambda c: c, carry)
        done = (n // 2) * 2
        return lax.cond(n - done == 1, lambda c: many(done, 1, c), lambda c: c, carry)

    def scores_step(kis, m):
        for ki in kis:
            m = scores(ki, m, False)
        return m

    m = tuple(jnp.full((1, blk), NEG, F32) for _ in range(n_sets))
    m = blocked(qi, scores_step, m)
    m = scores(qi, m, True)

    acc_scr[...] = jnp.zeros_like(acc_scr)

    def accumulate(kis, l):
        out = []
        for i in range(n_sets):
            li, pv = l[i], None
            for ki in kis:
                p = jnp.exp2(s_scr[i, ki] - m[i])
                li = li + jnp.sum(p, axis=0, keepdims=True)
                term = _dot(value_fn(i, ki), p.astype(BF16))
                pv = term if pv is None else pv + term
            out.append(li)
            acc_scr[i] += pv
        return tuple(out)

    l = tuple(jnp.zeros((1, blk), F32) for _ in range(n_sets))
    return blocked(qi + 1, accumulate, l)


def _mla_attn_kernel(q_ref, k_ref, vt_ref, o_ref, s_scr, acc_scr):
    blk = q_ref.shape[0]

    def score_fn(h, ki):
        rows = pl.ds(pl.multiple_of(ki * blk, blk), blk)
        cols = slice(LANE * h, LANE * (h + 1))
        return _nt(k_ref[rows, cols], q_ref[:, cols])

    def value_fn(h, ki):
        return vt_ref[ki, MLA_V * h:MLA_V * (h + 1), :]

    l = _two_pass_attention(MLA_HEADS, score_fn, value_fn, s_scr, acc_scr, blk)
    for j in range(MLA_HEADS // 2):
        o_t = jnp.concatenate([acc_scr[2 * j] / l[2 * j], acc_scr[2 * j + 1] / l[2 * j + 1]], axis=0)
        o_ref[:, LANE * j:LANE * (j + 1)] = o_t.T.astype(BF16)


def _mla_attn(qm, km, vmt, batch, seq):
    t = qm.shape[0]
    nq = seq // ATT_BLK
    return pl.pallas_call(
        _mla_attn_kernel,
        grid=(batch, nq),
        in_specs=[pl.BlockSpec((ATT_BLK, 1024), lambda b, i: (b * nq + i, 0)),
                  pl.BlockSpec((seq, 1024), lambda b, i: (b, 0)),
                  pl.BlockSpec((nq, 512, ATT_BLK), lambda b, i: (b, 0, 0))],
        out_specs=pl.BlockSpec((ATT_BLK, 512), lambda b, i: (b * nq + i, 0)),
        out_shape=jax.ShapeDtypeStruct((t, 512), BF16),
        scratch_shapes=[pltpu.VMEM((MLA_HEADS, nq, ATT_BLK, ATT_BLK), F32),
                        pltpu.VMEM((MLA_HEADS, MLA_V, ATT_BLK), F32)],
        compiler_params=_cp(("arbitrary", "arbitrary")),
        name="mla_attn",
    )(qm, km, vmt)


def _diff_attn_kernel(q_ref, k_ref, vt_ref, lam_ref, og_ref, o_ref, s_scr, acc_scr, qm_scr, *, lam_init):
    blk = q_ref.shape[0]
    low = lax.broadcasted_iota(jnp.int32, (blk, LANE), 1) < DIFF_QK
    lp = lam_ref[...]
    lam = (jnp.exp(jnp.sum(lp[0:1] * lp[1:2], axis=-1, keepdims=True))
           - jnp.exp(jnp.sum(lp[2:3] * lp[3:4], axis=-1, keepdims=True)) + lam_init)

    for h in range(DIFF_HEADS):
        qt = q_ref[:, LANE * h:LANE * (h + 1)]
        zero = jnp.zeros_like(qt)
        qm_scr[2 * h] = jnp.where(low, qt, zero)
        qm_scr[2 * h + 1] = jnp.where(low, zero, qt)

    def score_fn(i, ki):
        rows = pl.ds(pl.multiple_of(ki * blk, blk), blk)
        h = i // 2
        return _nt(k_ref[rows, LANE * h:LANE * (h + 1)], qm_scr[i])

    def value_fn(i, ki):
        h = i // 2
        return vt_ref[ki, DIFF_V * h:DIFF_V * (h + 1), :]

    l = _two_pass_attention(2 * DIFF_HEADS, score_fn, value_fn, s_scr, acc_scr, blk)
    for h in range(DIFF_HEADS):
        o_t = acc_scr[2 * h] / l[2 * h] - lam * (acc_scr[2 * h + 1] / l[2 * h + 1])
        on_t = o_t * lax.rsqrt(jnp.mean(o_t * o_t, axis=0, keepdims=True) + EPS)
        o_ref[:, LANE * h:LANE * (h + 1)] = (on_t.T * (og_ref[...] * (1.0 - lam_init))).astype(BF16)


def _diff_attn(qd, kd, vdt, lam_p, og, lam_init, batch, seq):
    t = qd.shape[0]
    nq = seq // ATT_BLK
    return pl.pallas_call(
        functools.partial(_diff_attn_kernel, lam_init=lam_init),
        grid=(batch, nq),
        in_specs=[pl.BlockSpec((ATT_BLK, 512), lambda b, i: (b * nq + i, 0)),
                  pl.BlockSpec((seq, 512), lambda b, i: (b, 0)),
                  pl.BlockSpec((nq, 512, ATT_BLK), lambda b, i: (b, 0, 0)),
                  pl.BlockSpec(lam_p.shape, lambda b, i: (0, 0)),
                  pl.BlockSpec(og.shape, lambda b, i: (0, 0))],
        out_specs=pl.BlockSpec((ATT_BLK, 512), lambda b, i: (b * nq + i, 0)),
        out_shape=jax.ShapeDtypeStruct((t, 512), BF16),
        scratch_shapes=[pltpu.VMEM((2 * DIFF_HEADS, nq, ATT_BLK, ATT_BLK), F32),
                        pltpu.VMEM((2 * DIFF_HEADS, DIFF_V, ATT_BLK), F32),
                        pltpu.VMEM((2 * DIFF_HEADS, ATT_BLK, LANE), BF16)],
        compiler_params=_cp(("arbitrary", "arbitrary")),
        name="diff_attn",
    )(qd, kd, vdt, lam_p, og)


def _swa_kernel(q_ref, kp_ref, kc_ref, vtp_ref, vtc_ref, sink_ref, o_ref):
    w = SWA_WINDOW
    grp = SWA_Q_HEADS // SWA_KV_HEADS
    n = pl.program_id(1)
    key = lax.broadcasted_iota(jnp.int32, (2 * w, grp * w), 0)
    qry = lax.broadcasted_iota(jnp.int32, (2 * w, grp * w), 1) & (w - 1)
    cur_ok = (key >= w) & (key - w <= qry)
    prev_ok = (key < w) & (key > qry)
    low = lax.broadcasted_iota(jnp.int32, (w, LANE), 1) < HEAD_DIM

    for t in range(q_ref.shape[0] // w):
        rows = slice(t * w, (t + 1) * w)
        if t == 0:
            kp, vtp = kp_ref[...], vtp_ref[0]
            valid = cur_ok | (prev_ok & (n > 0))
        else:
            kp, vtp = kc_ref[(t - 1) * w:t * w, :], vtc_ref[t - 1]
            valid = cur_ok | prev_ok
        kc, vtc = kc_ref[rows, :], vtc_ref[t]
        for kv in range(SWA_KV_HEADS):
            kcols = slice(LANE * kv, LANE * (kv + 1))
            vrows = slice(HEAD_DIM * kv, HEAD_DIM * (kv + 1))
            k_win = jnp.concatenate([kp[:, kcols], kc[:, kcols]], axis=0)
            parts = []
            for u in range(2):
                qt = q_ref[rows, LANE * (2 * kv + u):LANE * (2 * kv + u + 1)]
                zero = jnp.zeros_like(qt)
                parts += [jnp.where(low, qt, zero), jnp.where(low, zero, qt)]
            s = jnp.where(valid, _nt(k_win, jnp.concatenate(parts, axis=0)), NEG)
            sink = sink_ref[:, grp * w * kv:grp * w * (kv + 1)]
            m = jnp.maximum(jnp.max(s, axis=0, keepdims=True), sink)
            p = jnp.exp2(s - m)
            den = jnp.sum(p, axis=0, keepdims=True) + jnp.exp2(sink - m)
            vt_win = jnp.concatenate([vtp[vrows, :], vtc[vrows, :]], axis=1)
            o_t = _dot(vt_win, p.astype(BF16)) / den
            for u in range(2):
                pair = jnp.concatenate([o_t[:, 2 * u * w:(2 * u + 1) * w],
                                        o_t[:, (2 * u + 1) * w:(2 * u + 2) * w]], axis=0)
                o_ref[rows, LANE * (2 * kv + u):LANE * (2 * kv + u + 1)] = pair.T.astype(BF16)


def _swa(qs, ks, vst, sink_row, batch, seq):
    t = qs.shape[0]
    w = SWA_WINDOW
    nb = seq // w
    ns = nb // SWA_QB
    cur = lambda b, n: (b * ns + n, 0)
    cur3 = lambda b, n: (b * ns + n, 0, 0)
    prev = lambda b, n: (b * nb + jnp.maximum(n * SWA_QB - 1, 0), 0)
    prev3 = lambda b, n: (b * nb + jnp.maximum(n * SWA_QB - 1, 0), 0, 0)
    return pl.pallas_call(
        _swa_kernel,
        grid=(batch, ns),
        in_specs=[pl.BlockSpec((SWA_QB * w, 512), cur),
                  pl.BlockSpec((w, 256), prev), pl.BlockSpec((SWA_QB * w, 256), cur),
                  pl.BlockSpec((1, LANE, w), prev3), pl.BlockSpec((SWA_QB, LANE, w), cur3),
                  pl.BlockSpec(sink_row.shape, lambda b, n: (0, 0))],
        out_specs=pl.BlockSpec((SWA_QB * w, 512), cur),
        out_shape=jax.ShapeDtypeStruct((t, 512), BF16),
        compiler_params=_cp(("arbitrary", "arbitrary")),
        name="swa_attn",
    )(qs, ks, ks, vst, vst, sink_row)


def _merge_kernel(*refs):
    n_sub = refs[0].shape[0] // MOE_TILE
    h2s = [_merge_mix(k, *refs) for k in range(n_sub)]
    for k in range(n_sub):
        _merge_route(k, h2s[k], *refs)


def _merge_mix(k, h_ref, ya_ref, yb_ref, yc_ref, yd_ref, x_ref, gt1_ref, wg_ref, wb_ref, wo_ref,
               g2_ref, sh2_ref, sc2_ref, rw_ref, rb_ref, xo_ref, h2_ref, comb_ref, dest_ref, meta_ref):
    rows = slice(MOE_TILE * k, MOE_TILE * (k + 1))
    h = h_ref[rows, :]
    d = x_ref.shape[1]
    merged = None
    for b, y_ref in enumerate((ya_ref, yb_ref, yc_ref, yd_ref)):
        gate = jax.nn.sigmoid(_dot(h, wg_ref[:, d * b:d * (b + 1)]))
        term = gate * _dot(y_ref[rows, :], wb_ref[b])
        merged = term if merged is None else merged + term
    xn = x_ref[rows, :] + gt1_ref[0] * _dot(merged.astype(BF16), wo_ref[...])
    xo_ref[rows, :] = xn
    ms = jnp.mean(xn * xn, axis=-1, keepdims=True)
    h2 = xn * lax.rsqrt(ms + EPS) * g2_ref[...]
    h2 = h2 * (1.0 + sc2_ref[0]) + sh2_ref[0]
    h2_ref[rows, :] = h2.astype(BF16)
    return h2


def _merge_route(k, h2, h_ref, ya_ref, yb_ref, yc_ref, yd_ref, x_ref, gt1_ref, wg_ref, wb_ref, wo_ref,
                 g2_ref, sh2_ref, sc2_ref, rw_ref, rb_ref, xo_ref, h2_ref, comb_ref, dest_ref, meta_ref):
    rows = slice(MOE_TILE * k, MOE_TILE * (k + 1))
    hh, hm = _split2(h2)
    wh = rw_ref[...].astype(BF16)
    logits = _dot(hh, wh) + _dot(hm, wh)
    lt = logits.T
    scores = jax.nn.sigmoid(lt[0:N_EXPERTS, :])
    sel = scores + rb_ref[...]
    per = N_EXPERTS // N_GROUPS
    srow = [sel[e:e + 1, :] for e in range(N_EXPERTS)]
    gsum = []
    for g in range(N_GROUPS):
        a, b_, c, e_ = srow[per * g:per * (g + 1)]
        gsum.append(jnp.maximum(jnp.maximum(jnp.maximum(a + b_, a + c), jnp.maximum(a + e_, b_ + c)),
                                jnp.maximum(b_ + e_, c + e_)))
    best = jnp.maximum(jnp.maximum(gsum[0], gsum[1]), jnp.maximum(gsum[2], gsum[3]))
    taken = None
    weights = []
    picks = []
    for g in range(N_GROUPS):
        hit = gsum[g] == best
        pick = hit if taken is None else hit & jnp.logical_not(taken)
        taken = hit if taken is None else taken | hit
        picks.append(pick.astype(F32))
        for e in range(per * g, per * (g + 1)):
            rank = jnp.zeros_like(best)
            for o in range(per * g, per * (g + 1)):
                if o == e:
                    continue
                ahead = (srow[o] > srow[e]) | ((srow[o] == srow[e]) & (o < e))
                rank = rank + ahead.astype(F32)
            weights.append(jnp.where(pick & (rank < 1.5), scores[e:e + 1, :], 0.0))
    wsum = weights[0]
    for r_ in weights[1:]:
        wsum = wsum + r_
    inv = 1.0 / wsum
    rid = lax.broadcasted_iota(jnp.int32, scores.shape, 0)
    comb_e = jnp.zeros_like(scores)
    for e, r_ in enumerate(weights):
        comb_e = jnp.where(rid == e, r_ * inv, comb_e)

    tm = lt.shape[1]
    gid = lax.broadcasted_iota(jnp.int32, (8, tm), 0)
    onehot = jnp.zeros((8, tm), F32)
    for g in range(N_GROUPS):
        onehot = jnp.where(gid == g, picks[g], onehot)
    before = (lax.broadcasted_iota(jnp.int32, (tm, tm), 0) < lax.broadcasted_iota(jnp.int32, (tm, tm), 1))
    rank = _dot(onehot.astype(BF16), before.astype(BF16))
    dest = jnp.zeros((1, tm), F32)
    off = jnp.zeros((1, 1), F32)
    meta = jnp.zeros((8, LANE), F32)
    mrow = lax.broadcasted_iota(jnp.int32, (8, LANE), 0)
    for g in range(N_GROUPS):
        cnt = jnp.sum(picks[g], axis=-1, keepdims=True)
        dest = dest + picks[g] * (off + rank[g:g + 1, :])
        meta = jnp.where(mrow == g, cnt, meta)
        meta = jnp.where(mrow == N_GROUPS + g, off, meta)
        off = off + jnp.ceil(cnt * (1.0 / MOE_ALIGN)) * MOE_ALIGN
    dest_ref[k] = dest.astype(jnp.int32)
    meta_ref[k] = meta.astype(jnp.int32)
    pad = jnp.zeros((LANE - N_EXPERTS - 8, tm), F32)
    comb_t = jnp.concatenate([comb_e, jnp.where(gid == 0, dest, 0.0), pad], axis=0)
    comb_ref[rows, :] = comb_t.T


def _merge(h, ys, x2, gt1, wg, wb, wo, g2, sh2, sc2, rw, rb, seq):
    t, d = x2.shape
    sub = MERGE_SUBTILES
    tm = sub * MOE_TILE
    tpb = seq // tm
    row = lambda i: (i, 0)
    per_b = lambda i: (i // tpb, 0, 0)
    c2 = lambda i: (0, 0)
    once = pl.Buffered(1)
    return pl.pallas_call(
        _merge_kernel,
        grid=(t // tm,),
        in_specs=[pl.BlockSpec((tm, d), row)] + [pl.BlockSpec((tm, 512), row)] * 4
                 + [pl.BlockSpec((tm, d), row), pl.BlockSpec((1, 1, d), per_b),
                    pl.BlockSpec(wg.shape, c2, pipeline_mode=once),
                    pl.BlockSpec(wb.shape, lambda i: (0, 0, 0), pipeline_mode=once),
                    pl.BlockSpec(wo.shape, c2, pipeline_mode=once), pl.BlockSpec((1, d), c2),
                    pl.BlockSpec((1, 1, d), per_b), pl.BlockSpec((1, 1, d), per_b),
                    pl.BlockSpec(rw.shape, c2), pl.BlockSpec(rb.shape, c2)],
        out_specs=[pl.BlockSpec((tm, d), row), pl.BlockSpec((tm, d), row), pl.BlockSpec((tm, LANE), row),
                   pl.BlockSpec((sub, 1, MOE_TILE), lambda i: (i, 0, 0)),
                   pl.BlockSpec((sub, 8, LANE), lambda i: (i, 0, 0))],
        out_shape=[jax.ShapeDtypeStruct((t, d), F32), jax.ShapeDtypeStruct((t, d), BF16),
                   jax.ShapeDtypeStruct((t, LANE), F32),
                   jax.ShapeDtypeStruct((t // MOE_TILE, 1, MOE_TILE), jnp.int32),
                   jax.ShapeDtypeStruct((t // MOE_TILE, 8, LANE), jnp.int32)],
        compiler_params=_cp(("arbitrary",)),
        name="merge_router",
    )(h, *ys, x2, gt1, wg, wb, wo, g2, sh2, sc2, rw, rb)


def _moe_kernel(meta_ref, h2_ref, comb_ref, dest_ref, x_ref, gt2_ref, wg_ref, wu_ref, wd_ref, o_ref,
                sorted_s, csort_s, out_s):
    i = pl.program_id(0)
    tm = h2_ref.shape[0]
    n_rows = sorted_s.shape[0]
    per = N_EXPERTS // N_GROUPS
    comb = comb_ref[...]

    place = (lax.broadcasted_iota(jnp.int32, (n_rows, tm), 0) == dest_ref[0]).astype(BF16)
    sorted_s[...] = _dot(place, h2_ref[...]).astype(BF16)
    c_hi, c_lo = _split2(comb)
    csort_s[...] = _dot(place, c_hi) + _dot(place, c_lo)
    out_s[...] = jnp.zeros_like(out_s)
    def run_experts(g, start, size):
        rows = pl.ds(pl.multiple_of(start, MOE_ALIGN), size)
        xk = sorted_s[rows, :]
        cw = csort_s[rows, :]
        lane = lax.broadcasted_iota(jnp.int32, (size, LANE), 1)
        acc = None
        for j in range(per):
            e = per * g + j
            hid = _silu(_dot(xk, wg_ref[0, e])) * _dot(xk, wu_ref[0, e])
            ce = jnp.sum(jnp.where(lane == e, cw, 0.0), axis=-1, keepdims=True)
            term = _dot((hid * ce).astype(BF16), wd_ref[0, e])
            acc = term if acc is None else acc + term
        out_s[rows, :] += acc

    def group(g, carry):
        cnt = meta_ref[i, g]
        off = meta_ref[i, N_GROUPS + g]
        big = MOE_CHUNKS[-1]
        n_big = cnt // big

        def big_chunk(c, inner):
            run_experts(g, off + c * big, big)
            return inner

        lax.fori_loop(0, n_big, big_chunk, 0)
        rest = cnt - n_big * big
        lower = 0
        for size in MOE_CHUNKS:
            @pl.when((rest > lower) & (rest <= size))
            def _(size=size):
                run_experts(g, off + n_big * big, size)
            lower = size
        return carry

    lax.fori_loop(0, N_GROUPS, group, 0)

    dest_col = comb[:, MOE_DEST_LANE:MOE_DEST_LANE + 1].astype(jnp.int32)
    back = (lax.broadcasted_iota(jnp.int32, (tm, n_rows), 1) == dest_col).astype(BF16)
    o_ref[...] = x_ref[...] + gt2_ref[0] * _dot(back, out_s[...].astype(BF16))


def _moe(h2, comb, dest, meta, x2, gt2, wg, wu, wd, layer, seq):
    t, d = x2.shape
    tm = MOE_TILE
    tpb = seq // tm
    n_rows = MOE_SORT_ROWS
    overhang = max(b - a for a, b in zip((0,) + MOE_CHUNKS, MOE_CHUNKS)) - 1
    assert tm + N_GROUPS * (MOE_ALIGN - 1) + overhang <= n_rows
    row = lambda i, m: (i, 0)
    whole = lambda a: pl.BlockSpec((1,) + a.shape[1:], lambda i, m: (layer, 0, 0, 0),
                                   pipeline_mode=pl.Buffered(1))
    grid_spec = pltpu.PrefetchScalarGridSpec(
        num_scalar_prefetch=1,
        grid=(t // tm,),
        in_specs=[pl.BlockSpec((tm, d), row), pl.BlockSpec((tm, LANE), row),
                  pl.BlockSpec((1, 1, tm), lambda i, m: (i, 0, 0)),
                  pl.BlockSpec((tm, d), row),
                  pl.BlockSpec((1, 1, d), lambda i, m: (i // tpb, 0, 0)),
                  whole(wg), whole(wu), whole(wd)],
        out_specs=pl.BlockSpec((tm, d), row),
        scratch_shapes=[pltpu.VMEM((n_rows, d), BF16), pltpu.VMEM((n_rows, LANE), F32),
                        pltpu.VMEM((n_rows, d), F32)])
    return pl.pallas_call(
        _moe_kernel,
        grid_spec=grid_spec,
        out_shape=jax.ShapeDtypeStruct((t, d), F32),
        compiler_params=_cp(("arbitrary",)),
        name="moe",
    )(meta, h2, comb, dest, x2, gt2, wg, wu, wd)


def _layer_params(l, w_in, hg_onorm, mla_q_norm, mla_kv_norm, mla_w_uq, mla_w_ukv, mla_qk_norm,
                  diff_qk_norm, swa_qk_norm, swa_sinks, lb_all):
    ends = [sum(IN_SPLITS[:i]) for i in range(len(IN_SPLITS) + 1)]
    cols = lambda a, b: w_in[l, :, ends[a]:ends[b]].astype(BF16)
    mla_pad = 512 - (ends[7] - ends[4])
    p = {"w_hg": cols(0, 4),
         "w_mla": jnp.pad(cols(4, 7), ((0, 0), (0, mla_pad))),
         "w_diff": cols(7, 10), "w_swa": cols(10, 13), "wg": cols(13, 14)}

    lb = lb_all[l]
    p["loglb"] = jnp.log(lb)[None, :]
    p["log1mlb"] = jnp.log1p(-lb)[None, :]
    p["ogain"] = jnp.tile(hg_onorm[l], HG_HEADS)[None, :]

    hd = MLA_NOPE + MLA_ROPE
    half = MLA_ROPE // 2
    wq = mla_w_uq[l].reshape(MLA_Q_RANK, MLA_HEADS, hd)
    z = lambda r, n: jnp.zeros((r, MLA_HEADS, n), F32)
    nope, rope = wq[:, :, :MLA_NOPE], wq[:, :, MLA_NOPE:]
    p["wqa"] = jnp.concatenate([nope, rope, z(MLA_Q_RANK, 32)], -1).reshape(MLA_Q_RANK, -1).astype(BF16)
    p["wqb"] = jnp.concatenate([z(MLA_Q_RANK, MLA_NOPE), rope[:, :, half:], rope[:, :, :half],
                                z(MLA_Q_RANK, 32)], -1).reshape(MLA_Q_RANK, -1).astype(BF16)
    wkv = mla_w_ukv[l].reshape(MLA_KV_RANK, MLA_HEADS, MLA_NOPE + MLA_V)
    knope, vproj = wkv[:, :, :MLA_NOPE], wkv[:, :, MLA_NOPE:]
    eye = jnp.eye(MLA_ROPE, dtype=F32)
    swap = jnp.concatenate([eye[:, half:], eye[:, :half]], axis=1)
    place = lambda m: jnp.broadcast_to(
        jnp.concatenate([jnp.zeros((MLA_ROPE, MLA_NOPE), F32), m, jnp.zeros((MLA_ROPE, 32), F32)], -1)[:, None, :],
        (MLA_ROPE, MLA_HEADS, LANE))
    pad_rows = 256 - MLA_KV_RANK - MLA_ROPE
    p["wka"] = jnp.concatenate([jnp.concatenate([knope, z(MLA_KV_RANK, 64)], -1), place(eye),
                                z(pad_rows, LANE)], 0).reshape(256, -1).astype(BF16)
    p["wkb"] = jnp.concatenate([z(MLA_KV_RANK, LANE), place(swap), z(pad_rows, LANE)], 0
                               ).reshape(256, -1).astype(BF16)
    p["wv"] = jnp.concatenate([vproj.reshape(MLA_KV_RANK, -1),
                               jnp.zeros((256 - MLA_KV_RANK, MLA_HEADS * MLA_V), F32)], 0).astype(BF16)
    p["qng"] = mla_q_norm[l][None, :]
    p["kvg"] = jnp.concatenate([mla_kv_norm[l], jnp.ones((256 - MLA_KV_RANK,), F32)])[None, :]

    def rope_gains(g):
        base = jnp.concatenate([g, jnp.zeros((LANE - hd,), F32)])
        part = jnp.concatenate([jnp.zeros((MLA_NOPE,), F32), g[MLA_NOPE + half:], g[MLA_NOPE:MLA_NOPE + half],
                                jnp.zeros((LANE - hd,), F32)])
        return base[None, :], part[None, :]

    p["gq"], p["gqs"] = rope_gains(mla_qk_norm[l, 0])
    p["gk"], p["gks"] = rope_gains(mla_qk_norm[l, 1])
    p["dgq"] = jnp.tile(diff_qk_norm[l, 0], 8)[None, :]
    p["dgk"] = jnp.tile(diff_qk_norm[l, 1], 8)[None, :]
    p["sgq"] = jnp.tile(swa_qk_norm[l, 0], 8)[None, :]
    p["sgk"] = jnp.tile(swa_qk_norm[l, 1], 2)[None, :]
    p["sinks"] = jnp.repeat(swa_sinks[l].astype(F32) * LOG2E, SWA_WINDOW)[None, :]
    return p


def _rope_tables(positions):
    inv_freq = ROPE_BASE ** (-jnp.arange(0, MLA_ROPE, 2, dtype=F32) / MLA_ROPE)
    zeros = lambda n: jnp.zeros((n,), F32)
    half = MLA_ROPE // 2
    pad = LANE - MLA_NOPE - MLA_ROPE
    freq = jnp.concatenate([zeros(MLA_NOPE), inv_freq, inv_freq, zeros(pad)])
    keep = jnp.concatenate([jnp.ones((MLA_NOPE + MLA_ROPE,), F32), zeros(pad)])
    sign = jnp.concatenate([zeros(MLA_NOPE), -jnp.ones((half,), F32), jnp.ones((half,), F32), zeros(pad)])
    ang = positions.astype(F32).reshape(-1)[:, None] * freq
    return jnp.cos(ang) * keep, jnp.sin(ang) * sign


def kernel(x, c, positions, ada_w, ada_b, norm_mix, norm_ffn, w_in, hg_lb_logits, hg_onorm, mla_q_norm, mla_kv_norm, mla_w_uq, mla_w_ukv, mla_qk_norm, diff_qk_norm, diff_lam, diff_onorm, swa_qk_norm, swa_sinks, w_branch, w_out, router_w, router_bias, moe_w_gate, moe_w_up, moe_w_down):
    batch, seq, d = x.shape
    x2 = x.reshape(batch * seq, d)
    cosf, sinf = _rope_tables(positions)
    lb_all = jnp.cumsum(jax.nn.softmax(hg_lb_logits.astype(F32), axis=0), axis=0)
    lb_all = lb_all - lb_all[0]
    mod = _modulation(c, ada_w, ada_b)
    rw = jnp.concatenate([router_w, jnp.zeros((d, LANE - N_EXPERTS), F32)], axis=1)
    rb = router_bias.astype(F32)[:, None]
    moe_w = (moe_w_gate.astype(BF16), moe_w_up.astype(BF16), moe_w_down.astype(BF16))

    for l in range(DEPTH):
        sh1, sc1, gt1, sh2, sc2, gt2 = [mod[l, :, d * k:d * (k + 1)][:, None, :] for k in range(6)]
        p = _layer_params(l, w_in, hg_onorm, mla_q_norm, mla_kv_norm, mla_w_uq, mla_w_ukv, mla_qk_norm,
                          diff_qk_norm, swa_qk_norm, swa_sinks, lb_all)
        hg3, hf, mla, diff, swa, h = _inproj(x2, norm_mix[l][None, :], sh1, sc1,
                                             (p["w_hg"], p["w_mla"], p["w_diff"], p["w_swa"]), seq)
        y_a = _hgrn(hg3, hf, p["loglb"], p["log1mlb"], p["ogain"], batch, seq)
        qm, km, vmt, qd, kd, vdt, qs, ks, vst = _prep(mla, diff, swa, cosf, sinf, p)
        y_b = _mla_attn(qm, km, vmt, batch, seq)
        lam_init = 0.8 - 0.6 * math.exp(-0.3 * l)
        y_c = _diff_attn(qd, kd, vdt, diff_lam[l], diff_onorm[l][None, :], lam_init, batch, seq)
        y_d = _swa(qs, ks, vst, p["sinks"], batch, seq)
        x2, h2, comb, dest, meta = _merge(h, (y_a, y_b, y_c, y_d), x2, gt1, p["wg"], w_branch[l].astype(BF16),
                                          w_out[l].astype(BF16), norm_ffn[l][None, :], sh2, sc2, rw, rb, seq)
        x2 = _moe(h2, comb, dest, meta[:, :, 0], x2, gt2, *moe_w, l, seq)
    return x2.reshape(batch, seq, d)
```

```python
import functools
import math

import jax
import jax.numpy as jnp
from jax import lax
from jax.experimental import pallas as pl
from jax.experimental.pallas import tpu as pltpu

F32 = jnp.float32
BF16 = jnp.bfloat16

D_MODEL = 1024
DEPTH = 2
EPS = 1e-6
N_BRANCH = 4
HG_HEADS = 8
HG_DK = 64
HG_W = HG_HEADS * HG_DK
HG_SUB = 16
HG_CHUNK = 64
HG_SAFE_DECAY = 80.0
HG_DIAG_ROWS =(HG_SUB // 2) * HG_SUB + (HG_SUB // 2) ** 2
MLA_HEADS = 8
MLA_Q_RANK = 256
MLA_KV_RANK = 128
MLA_NOPE = 64
MLA_ROPE = 32
MLA_V = 64
ROPE_BASE = 10000.0
DIFF_HEADS = 4
DIFF_QK = 64
DIFF_V = 128
SWA_Q_HEADS = 8
SWA_KV_HEADS = 2
SWA_WINDOW = 128
HEAD_DIM = 64
N_EXPERTS = 16
N_GROUPS = 4
D_FF_EXPERT = 256
IN_SPLITS = (512, 512, 512, 512, 256, 128, 32, 512, 512, 512, 512, 128, 128, 4096)

MOE_TILE = 512
MERGE_SUBTILES = 2
MOE_ALIGN = 16
MOE_CHUNKS = (64, 128, 192, 256)
MOE_SORT_ROWS = 640
MOE_DEST_LANE = N_EXPERTS
LANE = 128
ATT_BLK = 256
SWA_QB = 8
LOG2E = 1.4426950408889634
NEG = -1e30
VMEM_LIMIT = 56 * 1024 * 1024


def _cp(sem, vmem=VMEM_LIMIT):
    return pltpu.CompilerParams(dimension_semantics=sem, vmem_limit_bytes=vmem)


def _nt(a, b):
    return lax.dot_general(a, b, (((1,), (1,)), ((), ())), preferred_element_type=F32)


def _tn(a, b):
    return lax.dot_general(a, b, (((0,), (0,)), ((), ())), preferred_element_type=F32)


def _dot(a, b):
    return jnp.dot(a, b, preferred_element_type=F32)


def _split2(x):
    hi = x.astype(BF16)
    lo = (x - hi.astype(F32)).astype(BF16)
    return hi, lo


def _seg_id(idx, seg):
    shift = seg.bit_length() - 1
    assert 1 << shift == seg
    return lax.shift_right_logical(idx, shift)


def _same_seg(n, seg):
    r = lax.broadcasted_iota(jnp.int32, (n, n), 0)
    c = lax.broadcasted_iota(jnp.int32, (n, n), 1)
    return _seg_id(r, seg) == _seg_id(c, seg)


def _seg_ones(n, seg):
    return _same_seg(n, seg).astype(BF16)


def _seg_mean_sq(x, seg):
    n = x.shape[-1]
    hi, lo = _split2(x * x)
    ones = _seg_ones(n, seg)
    return (_dot(hi, ones) + _dot(lo, ones)) * (1.0 / seg)


def _silu(x):
    return x * jax.nn.sigmoid(x)


def _mod_kernel(c_ref, w_ref, b_ref, o_ref):
    c = c_ref[...]
    o_ref[0] = jnp.dot(_silu(c), w_ref[0], preferred_element_type=F32,
                       precision=lax.Precision.HIGHEST) + b_ref[0]


def _modulation(c, ada_w, ada_b):
    nl, d, n6 = ada_w.shape
    b = c.shape[0]
    tn = 1536
    return pl.pallas_call(
        _mod_kernel,
        grid=(nl, n6 // tn),
        in_specs=[pl.BlockSpec((b, d), lambda l, j: (0, 0)),
                  pl.BlockSpec((1, d, tn), lambda l, j: (l, 0, j)),
                  pl.BlockSpec((1, 1, tn), lambda l, j: (l, 0, j))],
        out_specs=pl.BlockSpec((1, b, tn), lambda l, j: (l, 0, j)),
        out_shape=jax.ShapeDtypeStruct((nl, b, n6), F32),
        compiler_params=_cp(("arbitrary", "arbitrary")),
        name="modulation",
    )(c, ada_w, ada_b.reshape(nl, 1, n6))


def _inproj_kernel(x_ref, g_ref, sh_ref, sc_ref, whg_ref, wmla_ref, wdiff_ref, wswa_ref,
                   ohg_ref, ohf_ref, omla_ref, odiff_ref, oswa_ref, oh_ref):
    x = x_ref[...]
    ms = jnp.mean(x * x, axis=-1, keepdims=True)
    h = x * lax.rsqrt(ms + EPS) * g_ref[...]
    h = h * (1.0 + sc_ref[0]) + sh_ref[0]
    hb = h.astype(BF16)
    oh_ref[...] = hb

    ohg_ref[:, 0:512] = _dot(hb, whg_ref[0, :, 0:512]).astype(BF16)
    ohf_ref[...] = _dot(hb, whg_ref[0, :, 512:1024])
    ohg_ref[:, 512:1024] = _dot(hb, whg_ref[0, :, 1024:1536]).astype(BF16)
    ohg_ref[:, 1024:1536] = _dot(hb, whg_ref[0, :, 1536:2048]).astype(BF16)
    omla_ref[...] = _dot(hb, wmla_ref[0]).astype(BF16)
    for k in range(3):
        odiff_ref[:, 512 * k:512 * (k + 1)] = _dot(hb, wdiff_ref[0, :, 512 * k:512 * (k + 1)]).astype(BF16)
    oswa_ref[:, 0:512] = _dot(hb, wswa_ref[0, :, 0:512]).astype(BF16)
    oswa_ref[:, 512:768] = _dot(hb, wswa_ref[0, :, 512:768]).astype(BF16)


def _in_weight_kernel(w_ref, hg_ref, mla_ref, diff_ref, swa_ref):
    ends = [sum(IN_SPLITS[:i]) for i in range(len(IN_SPLITS) + 1)]

    def piece(a, b, width):
        lo, hi = ends[a], ends[b]
        start = lo // LANE * LANE
        span = -(-(hi - start) // LANE) * LANE
        win = w_ref[0, :, start:start + span]
        if lo != start:
            win = pltpu.roll(win, span - (lo - start), 1)
        out = win[:, :width]
        if hi - lo < width:
            lane = lax.broadcasted_iota(jnp.int32, out.shape, 1)
            out = jnp.where(lane < hi - lo, out, 0.0)
        return out.astype(BF16)

    hg_ref[0] = piece(0, 4, hg_ref.shape[2])
    mla_ref[0] = piece(4, 7, mla_ref.shape[2])
    diff_ref[0] = piece(7, 10, diff_ref.shape[2])
    swa_ref[0] = piece(10, 13, swa_ref.shape[2])


def _in_weights(w_in):
    nl, d, _ = w_in.shape
    rb = 256
    mixer_cols = -(-sum(IN_SPLITS[:13]) // LANE) * LANE
    widths = (2048, 512, 1536, 768)
    return pl.pallas_call(
        _in_weight_kernel,
        grid=(nl, d // rb),
        in_specs=[pl.BlockSpec((1, rb, mixer_cols), lambda l, r: (l, r, 0))],
        out_specs=[pl.BlockSpec((1, rb, w), lambda l, r: (l, r, 0)) for w in widths],
        out_shape=[jax.ShapeDtypeStruct((nl, d, w), BF16) for w in widths],
        compiler_params=_cp(("arbitrary", "arbitrary")),
        name="in_weights",
    )(w_in)


def _inproj(x2, gain, sh, sc, weights, layer, seq):
    t, d = x2.shape
    tm = 512
    tpb = seq // tm
    row = lambda i: (i, 0)
    per_b = lambda i: (i // tpb, 0, 0)
    outs = [(1536, BF16), (512, F32), (512, BF16), (1536, BF16), (768, BF16), (d, BF16)]
    return pl.pallas_call(
        _inproj_kernel,
        grid=(t // tm,),
        in_specs=[pl.BlockSpec((tm, d), row),
                  pl.BlockSpec((1, d), lambda i: (0, 0)),
                  pl.BlockSpec((1, 1, d), per_b),
                  pl.BlockSpec((1, 1, d), per_b)]
                 + [pl.BlockSpec((1,) + w.shape[1:], lambda i: (layer, 0, 0)) for w in weights],
        out_specs=[pl.BlockSpec((tm, w), row) for w, _ in outs],
        out_shape=[jax.ShapeDtypeStruct((t, w), dt) for w, dt in outs],
        compiler_params=_cp(("arbitrary",)),
        name="inproj",
    )(x2, gain, sh, sc, *weights)


def _segment_cumsum(x, seg):
    n = x.shape[0]
    r = lax.broadcasted_iota(jnp.int32, (n, n), 0)
    cc = lax.broadcasted_iota(jnp.int32, (n, n), 1)
    same = _same_seg(n, seg)
    tri = (same & (cc <= r)).astype(BF16)
    blk = same.astype(BF16)
    hi, lo = _split2(x)
    return _dot(tri, hi) + _dot(tri, lo), _dot(blk, hi) + _dot(blk, lo)


def _hgrn_chunk_path(i_ref, st_ref, c_s, tot_s, qs_s, kk_s, qe_s, kd_s, ke_s, dec_s, od_s):
    rows_blk = c_s.shape[0]
    tot = tot_s[...]
    rel = c_s[...] - 0.5 * tot
    half_dec = jnp.exp(0.5 * tot)
    kd = kk_s[...] * jnp.exp(-rel)
    qe_s[...] = (qs_s[...] * jnp.exp(rel)).astype(BF16)
    kd_s[...] = kd.astype(BF16)
    ke_s[...] = (kd * half_dec).astype(BF16)
    dec_s[...] = half_dec

    row = lax.broadcasted_iota(jnp.int32, (2 * rows_blk, rows_blk), 0) & (rows_blk - 1)
    col = lax.broadcasted_iota(jnp.int32, (2 * rows_blk, rows_blk), 1)
    intra = (_seg_id(row, HG_CHUNK) == _seg_id(col, HG_CHUNK)) & (col <= row)
    low_q = lax.broadcasted_iota(jnp.int32, (rows_blk, LANE), 1) < HG_DK
    low_c = lax.broadcasted_iota(jnp.int32, (HG_CHUNK, LANE), 1) < HG_DK

    for j in range(HG_W // LANE):
        cols = slice(LANE * j, LANE * (j + 1))
        qe = qe_s[:, cols]
        zero = jnp.zeros_like(qe)
        q2 = jnp.concatenate([jnp.where(low_q, qe, zero), jnp.where(low_q, zero, qe)], axis=0)
        attn = jnp.where(intra, _nt(q2, kd_s[:, cols]), 0.0).astype(BF16)
        o2 = _dot(attn, i_ref[:, cols])
        n_chunks = rows_blk // HG_CHUNK
        chunk_rows = [slice(HG_CHUNK * ch, HG_CHUNK * (ch + 1)) for ch in range(n_chunks)]
        upd = [_tn(i_ref[rows, cols], ke_s[rows, cols]) for rows in chunk_rows]
        st = st_ref[j]
        for ch, rows in enumerate(chunk_rows):
            rows_hi = slice(rows_blk + HG_CHUNK * ch, rows_blk + HG_CHUNK * (ch + 1))
            hd = dec_s[HG_CHUNK * ch:HG_CHUNK * ch + 1, cols]
            inter = _nt(jnp.concatenate([q2[rows], q2[rows_hi]], axis=0), (st * hd).astype(BF16))
            od_s[rows, cols] = jnp.where(low_c, o2[rows] + inter[:HG_CHUNK], o2[rows_hi] + inter[HG_CHUNK:])
            st = st * (hd * hd) + upd[ch]
        st_ref[j] = st


def _hgrn_exact_path(i_ref, st_ref, lf_s, c_s, qs_s, kk_s, qe_s, ke_s, dec_s, od_s, t_s, a_s):
    rows_blk = c_s.shape[0]
    n_sub = rows_blk // HG_SUB
    c, tot = _segment_cumsum(lf_s[...], HG_SUB)
    c_s[...] = c
    qe_s[...] = (qs_s[...] * jnp.exp(c)).astype(BF16)
    ke_s[...] = (kk_s[...] * jnp.exp(tot - c)).astype(BF16)
    dec_s[...] = jnp.exp(tot)

    same_head = _same_seg(LANE, HG_DK)
    head_mask = same_head.astype(F32)
    head_ones = same_head.astype(BF16)
    for j in range(HG_W // LANE):
        st_ref[j] = st_ref[j] * head_mask
    half = HG_SUB // 2
    trow = lax.broadcasted_iota(jnp.int32, (half, HG_W), 0)

    def body(i, carry):
        r0 = pl.multiple_of(i * HG_SUB, HG_SUB)
        rows = pl.ds(r0, HG_SUB)
        c_i = c_s[rows, :]
        qs_i = qs_s[rows, :]
        kk_i = kk_s[rows, :]
        v_i = i_ref[rows, :].astype(F32)
        c_lo, c_hi = c_i[:half], c_i[half:]
        q_lo, q_hi = qs_i[:half], qs_i[half:]
        for s in range(half):
            c_row, k_row = c_i[s:s + 1, :], kk_i[s:s + 1, :]
            e_lo = jnp.exp(jnp.where(trow >= s, c_lo - c_row, NEG))
            e_hi = jnp.exp(c_hi - c_row)
            both = jnp.concatenate([e_lo * q_lo, e_hi * q_hi], axis=0) * k_row
            t_s[s * HG_SUB:(s + 1) * HG_SUB, :] = both.astype(BF16)
        for s in range(half, HG_SUB, 2):
            pair = []
            for u in (s, s + 1):
                e_hi = jnp.exp(jnp.where(trow >= u - half, c_hi - c_i[u:u + 1, :], NEG))
                pair.append(e_hi * q_hi * kk_i[u:u + 1, :])
            base = half * HG_SUB + (s - half) * half
            t_s[base:base + HG_SUB, :] = jnp.concatenate(pair, axis=0).astype(BF16)
        for j in range(HG_W // LANE):
            cols = slice(LANE * j, LANE * (j + 1))
            a_s[:, cols] = _dot(t_s[:, cols], head_ones)
        acc_lo = jnp.zeros((half, HG_W), F32)
        acc_hi = jnp.zeros((half, HG_W), F32)
        for s in range(half):
            acc_lo = acc_lo + a_s[s * HG_SUB:s * HG_SUB + half, :] * v_i[s:s + 1, :]
            acc_hi = acc_hi + a_s[s * HG_SUB + half:(s + 1) * HG_SUB, :] * v_i[s:s + 1, :]
        for s in range(half, HG_SUB):
            base = half * HG_SUB + (s - half) * half
            acc_hi = acc_hi + a_s[base:base + half, :] * v_i[s:s + 1, :]
        acc = jnp.concatenate([acc_lo, acc_hi], axis=0)
        for j in range(HG_W // LANE):
            cols = slice(LANE * j, LANE * (j + 1))
            st = st_ref[j]
            o_int = _nt(qe_s[rows, cols], st.astype(BF16))
            upd = _tn(i_ref[rows, cols], ke_s[rows, cols])
            st_ref[j] = st * dec_s[pl.ds(r0, 1), cols] + upd * head_mask
            od_s[rows, cols] = acc[:, cols] + o_int
        return carry

    lax.fori_loop(0, n_sub, body, 0)


def _hgrn_kernel(q_ref, i_ref, g_ref, f_ref, loglb_ref, log1mlb_ref, og_ref, o_ref,
                 st_ref, lf_s, c_s, tot_s, qs_s, kk_s, qe_s, kd_s, ke_s, dec_s, od_s, t_s, a_s):
    @pl.when(pl.program_id(1) == 0)
    def _():
        st_ref[...] = jnp.zeros_like(st_ref)

    fr = f_ref[...]
    ls = jnp.minimum(fr, 0.0) - jnp.log(1.0 + jnp.exp(-jnp.abs(fr)))
    a = loglb_ref[...]
    c2 = log1mlb_ref[...] + ls
    lf = jnp.maximum(a, c2) + jnp.log(1.0 + jnp.exp(-jnp.abs(a - c2)))
    lf_s[...] = lf
    qs_s[...] = _silu(q_ref[...].astype(F32))
    kk_s[...] = 1.0 - jnp.exp(lf)
    c, tot = _segment_cumsum(lf, HG_CHUNK)
    c_s[...] = c
    tot_s[...] = tot
    safe = 0.5 * jnp.max(-tot) <= HG_SAFE_DECAY

    @pl.when(safe)
    def _():
        _hgrn_chunk_path(i_ref, st_ref, c_s, tot_s, qs_s, kk_s, qe_s, kd_s, ke_s, dec_s, od_s)

    @pl.when(jnp.logical_not(safe))
    def _():
        _hgrn_exact_path(i_ref, st_ref, lf_s, c_s, qs_s, kk_s, qe_s, ke_s, dec_s, od_s, t_s, a_s)

    o = od_s[...]
    ms = _seg_mean_sq(o, HG_DK)
    on = o * lax.rsqrt(ms + EPS) * og_ref[...]
    o_ref[...] = (on * _silu(g_ref[...].astype(F32))).astype(BF16)


def _hgrn(hg3, hf, loglb, log1mlb, ogain, batch, seq):
    t = hf.shape[0]
    rb = 256
    nb = seq // rb
    blk = lambda k: pl.BlockSpec((rb, HG_W), lambda b, n, k=k: (b * nb + n, k))
    vec = pl.BlockSpec((1, HG_W), lambda b, n: (0, 0))
    f32_blk = pltpu.VMEM((rb, HG_W), F32)
    bf16_blk = pltpu.VMEM((rb, HG_W), BF16)
    return pl.pallas_call(
        _hgrn_kernel,
        grid=(batch, nb),
        in_specs=[blk(0), blk(1), blk(2), blk(0), vec, vec, vec],
        out_specs=blk(0),
        out_shape=jax.ShapeDtypeStruct((t, HG_W), BF16),
        scratch_shapes=[pltpu.VMEM((HG_W // LANE, LANE, LANE), F32),
                        f32_blk, f32_blk, f32_blk, f32_blk, f32_blk,
                        bf16_blk, bf16_blk, bf16_blk,
                        f32_blk, f32_blk,
                        pltpu.VMEM((HG_DIAG_ROWS, HG_W), BF16),
                        pltpu.VMEM((HG_DIAG_ROWS, HG_W), F32)],
        compiler_params=_cp(("arbitrary", "arbitrary")),
        name="hgrn2",
    )(hg3, hg3, hg3, hf, loglb, log1mlb, ogain)


def _store_transposed_blocks(out_ref, v):
    blk = out_ref.shape[2]
    for u in range(out_ref.shape[0]):
        out_ref[u] = v[u * blk:(u + 1) * blk, :].T.astype(BF16)


def _prep_kernel(mla_ref, diff_ref, swa_ref, cos_ref, sin_ref,
                 qng_ref, kvg_ref, wqa_ref, wqb_ref, wka_ref, wkb_ref, wv_ref,
                 gq_ref, gqs_ref, gk_ref, gks_ref, dgq_ref, dgk_ref, sgq_ref, sgk_ref,
                 qm_ref, km_ref, vmt_ref, qd_ref, kd_ref, vdt_ref, qs_ref, ks_ref, vst_ref):
    blk = mla_ref[...].astype(F32)
    cq = blk[:, :MLA_Q_RANK]
    rest = blk[:, MLA_Q_RANK:]
    cqn = cq * lax.rsqrt(jnp.mean(cq * cq, axis=-1, keepdims=True) + EPS) * qng_ref[...]
    lane = lax.broadcasted_iota(jnp.int32, rest.shape, 1)
    is_kv = lane < MLA_KV_RANK
    ms_kv = jnp.sum(jnp.where(is_kv, rest * rest, 0.0), axis=-1, keepdims=True) * (1.0 / MLA_KV_RANK)
    restn = jnp.where(is_kv, rest * lax.rsqrt(ms_kv + EPS) * kvg_ref[...], rest)
    cqb = cqn.astype(BF16)
    rb = restn.astype(BF16)
    qa = _dot(cqb, wqa_ref[...])
    qb = _dot(cqb, wqb_ref[...])
    ka = _dot(rb, wka_ref[...])
    kb = _dot(rb, wkb_ref[...])
    _store_transposed_blocks(vmt_ref, _dot(rb, wv_ref[...]))
    cosf = cos_ref[...]
    sinf = sin_ref[...]
    cq_t = cosf * gq_ref[...]
    sq_t = sinf * gqs_ref[...]
    ck_t = cosf * gk_ref[...]
    sk_t = sinf * gks_ref[...]
    inv_n = 1.0 / (MLA_NOPE + MLA_ROPE)
    scale = (MLA_NOPE + MLA_ROPE) ** -0.5 * LOG2E
    for h in range(MLA_HEADS):
        cols = slice(LANE * h, LANE * (h + 1))
        x = qa[:, cols]
        rinv = lax.rsqrt(jnp.sum(x * x, axis=-1, keepdims=True) * inv_n + EPS)
        qm_ref[:, cols] = ((x * cq_t + qb[:, cols] * sq_t) * (rinv * scale)).astype(BF16)
        y = ka[:, cols]
        rinv = lax.rsqrt(jnp.sum(y * y, axis=-1, keepdims=True) * inv_n + EPS)
        km_ref[:, cols] = ((y * ck_t + kb[:, cols] * sk_t) * rinv).astype(BF16)

    def seg_norm(x, gain, scale):
        return x * lax.rsqrt(_seg_mean_sq(x, HEAD_DIM) + EPS) * (gain * scale)

    dq = diff_ref[:, 0:512].astype(F32)
    dk = diff_ref[:, 512:1024].astype(F32)
    qd_ref[...] = seg_norm(dq, dgq_ref[...], DIFF_QK ** -0.5 * LOG2E).astype(BF16)
    kd_ref[...] = seg_norm(dk, dgk_ref[...], 1.0).astype(BF16)
    _store_transposed_blocks(vdt_ref, diff_ref[:, 1024:1536].astype(F32))

    sq = swa_ref[:, 0:512].astype(F32)
    qs_ref[...] = seg_norm(sq, sgq_ref[...], HEAD_DIM ** -0.5 * LOG2E).astype(BF16)
    skv = swa_ref[:, 512:768].astype(F32)
    kn = seg_norm(skv[:, :LANE], sgk_ref[...], 1.0)
    low = lax.broadcasted_iota(jnp.int32, kn.shape, 1) < HEAD_DIM
    sw = pltpu.roll(kn, HEAD_DIM, 1)
    ks_ref[:, :LANE] = jnp.where(low, kn, sw).astype(BF16)
    ks_ref[:, LANE:] = jnp.where(low, sw, kn).astype(BF16)
    _store_transposed_blocks(vst_ref, skv[:, LANE:])


def _prep(mla, diff, swa, cosf, sinf, p):
    t = mla.shape[0]
    tm = 512
    row = lambda i: (i, 0)
    full = lambda a: pl.BlockSpec(a.shape, lambda i: (0,) * a.ndim)
    consts = [p["qng"], p["kvg"], p["wqa"], p["wqb"], p["wka"], p["wkb"], p["wv"],
              p["gq"], p["gqs"], p["gk"], p["gks"], p["dgq"], p["dgk"], p["sgq"], p["sgk"]]
    def rows_out(w):
        return pl.BlockSpec((tm, w), row), jax.ShapeDtypeStruct((t, w), BF16)

    def transposed_out(n, blk):
        return (pl.BlockSpec((tm // blk, n, blk), lambda i: (i, 0, 0)),
                jax.ShapeDtypeStruct((t // blk, n, blk), BF16))

    outs = [rows_out(1024), rows_out(1024), transposed_out(512, ATT_BLK),
            rows_out(512), rows_out(512), transposed_out(512, ATT_BLK),
            rows_out(512), rows_out(256), transposed_out(LANE, SWA_WINDOW)]
    return pl.pallas_call(
        _prep_kernel,
        grid=(t // tm,),
        in_specs=[pl.BlockSpec((tm, 512), row),
                  pl.BlockSpec((tm, 1536), row),
                  pl.BlockSpec((tm, 768), row),
                  pl.BlockSpec((tm, LANE), row),
                  pl.BlockSpec((tm, LANE), row)] + [full(a) for a in consts],
        out_specs=[o[0] for o in outs],
        out_shape=[o[1] for o in outs],
        compiler_params=_cp(("arbitrary",)),
        name="attn_prep",
    )(mla, diff, swa, cosf, sinf, *consts)


def _causal_t(blk):
    key = lax.broadcasted_iota(jnp.int32, (blk, blk), 0)
    qry = lax.broadcasted_iota(jnp.int32, (blk, blk), 1)
    return key <= qry


def _two_pass_attention(n_sets, score_fn, value_fn, s_scr, acc_scr, blk):
    qi = pl.program_id(1)
    causal = _causal_t(blk)

    def scores(ki, m, masked):
        out = []
        for i in range(n_sets):
            s = score_fn(i, ki)
            if masked:
                s = jnp.where(causal, s, NEG)
            s_scr[i, ki] = s
            out.append(jnp.maximum(m[i], jnp.max(s, axis=0, keepdims=True)))
        return tuple(out)

    def blocked(n, step, carry):
        def many(k0, count, c):
            return step(tuple(k0 + u for u in range(count)), c)
        carry = lax.fori_loop(0, n // 4, lambda kp, c: many(4 * kp, 4, c), carry)
        done = (n // 4) * 4
        carry = lax.cond(n - done >= 2, lambda c: many(done, 2, c), lambda c: c, carry)
        done = (n // 2) * 2
        return lax.cond(n - done == 1, lambda c: many(done, 1, c), lambda c: c, carry)

    def scores_step(kis, m):
        for ki in kis:
            m = scores(ki, m, False)
        return m

    m = tuple(jnp.full((1, blk), NEG, F32) for _ in range(n_sets))
    m = blocked(qi, scores_step, m)
    m = scores(qi, m, True)

    acc_scr[...] = jnp.zeros_like(acc_scr)

    def accumulate(kis, l):
        out = []
        for i in range(n_sets):
            li, pv = l[i], None
            for ki in kis:
                p = jnp.exp2(s_scr[i, ki] - m[i])
                li = li + jnp.sum(p, axis=0, keepdims=True)
                term = _dot(value_fn(i, ki), p.astype(BF16))
                pv = term if pv is None else pv + term
            out.append(li)
            acc_scr[i] += pv
        return tuple(out)

    l = tuple(jnp.zeros((1, blk), F32) for _ in range(n_sets))
    return blocked(qi + 1, accumulate, l)


def _mla_attn_kernel(q_ref, k_ref, vt_ref, o_ref, s_scr, acc_scr):
    blk = q_ref.shape[0]

    def score_fn(h, ki):
        rows = pl.ds(pl.multiple_of(ki * blk, blk), blk)
        cols = slice(LANE * h, LANE * (h + 1))
        return _nt(k_ref[rows, cols], q_ref[:, cols])

    def value_fn(h, ki):
        return vt_ref[ki, MLA_V * h:MLA_V * (h + 1), :]

    l = _two_pass_attention(MLA_HEADS, score_fn, value_fn, s_scr, acc_scr, blk)
    for j in range(MLA_HEADS // 2):
        o_t = jnp.concatenate([acc_scr[2 * j] / l[2 * j], acc_scr[2 * j + 1] / l[2 * j + 1]], axis=0)
        o_ref[:, LANE * j:LANE * (j + 1)] = o_t.T.astype(BF16)


def _mla_attn(qm, km, vmt, batch, seq):
    t = qm.shape[0]
    nq = seq // ATT_BLK
    return pl.pallas_call(
        _mla_attn_kernel,
        grid=(batch, nq),
        in_specs=[pl.BlockSpec((ATT_BLK, 1024), lambda b, i: (b * nq + i, 0)),
                  pl.BlockSpec((seq, 1024), lambda b, i: (b, 0)),
                  pl.BlockSpec((nq, 512, ATT_BLK), lambda b, i: (b, 0, 0))],
        out_specs=pl.BlockSpec((ATT_BLK, 512), lambda b, i: (b * nq + i, 0)),
        out_shape=jax.ShapeDtypeStruct((t, 512), BF16),
        scratch_shapes=[pltpu.VMEM((MLA_HEADS, nq, ATT_BLK, ATT_BLK), F32),
                        pltpu.VMEM((MLA_HEADS, MLA_V, ATT_BLK), F32)],
        compiler_params=_cp(("arbitrary", "arbitrary")),
        name="mla_attn",
    )(qm, km, vmt)


def _diff_attn_kernel(q_ref, k_ref, vt_ref, lam_ref, og_ref, o_ref, s_scr, acc_scr, qm_scr, *, lam_init):
    blk = q_ref.shape[0]
    low = lax.broadcasted_iota(jnp.int32, (blk, LANE), 1) < DIFF_QK
    lp = lam_ref[...]
    lam = (jnp.exp(jnp.sum(lp[0:1] * lp[1:2], axis=-1, keepdims=True))
           - jnp.exp(jnp.sum(lp[2:3] * lp[3:4], axis=-1, keepdims=True)) + lam_init)

    for h in range(DIFF_HEADS):
        qt = q_ref[:, LANE * h:LANE * (h + 1)]
        zero = jnp.zeros_like(qt)
        qm_scr[2 * h] = jnp.where(low, qt, zero)
        qm_scr[2 * h + 1] = jnp.where(low, zero, qt)

    def score_fn(i, ki):
        rows = pl.ds(pl.multiple_of(ki * blk, blk), blk)
        h = i // 2
        return _nt(k_ref[rows, LANE * h:LANE * (h + 1)], qm_scr[i])

    def value_fn(i, ki):
        h = i // 2
        return vt_ref[ki, DIFF_V * h:DIFF_V * (h + 1), :]

    l = _two_pass_attention(2 * DIFF_HEADS, score_fn, value_fn, s_scr, acc_scr, blk)
    for h in range(DIFF_HEADS):
        o_t = acc_scr[2 * h] / l[2 * h] - lam * (acc_scr[2 * h + 1] / l[2 * h + 1])
        on_t = o_t * lax.rsqrt(jnp.mean(o_t * o_t, axis=0, keepdims=True) + EPS)
        o_ref[:, LANE * h:LANE * (h + 1)] = (on_t.T * (og_ref[...] * (1.0 - lam_init))).astype(BF16)


def _diff_attn(qd, kd, vdt, lam_p, og, lam_init, batch, seq):
    t = qd.shape[0]
    nq = seq // ATT_BLK
    return pl.pallas_call(
        functools.partial(_diff_attn_kernel, lam_init=lam_init),
        grid=(batch, nq),
        in_specs=[pl.BlockSpec((ATT_BLK, 512), lambda b, i: (b * nq + i, 0)),
                  pl.BlockSpec((seq, 512), lambda b, i: (b, 0)),
                  pl.BlockSpec((nq, 512, ATT_BLK), lambda b, i: (b, 0, 0)),
                  pl.BlockSpec(lam_p.shape, lambda b, i: (0, 0)),
                  pl.BlockSpec(og.shape, lambda b, i: (0, 0))],
        out_specs=pl.BlockSpec((ATT_BLK, 512), lambda b, i: (b * nq + i, 0)),
        out_shape=jax.ShapeDtypeStruct((t, 512), BF16),
        scratch_shapes=[pltpu.VMEM((2 * DIFF_HEADS, nq, ATT_BLK, ATT_BLK), F32),
                        pltpu.VMEM((2 * DIFF_HEADS, DIFF_V, ATT_BLK), F32),
                        pltpu.VMEM((2 * DIFF_HEADS, ATT_BLK, LANE), BF16)],
        compiler_params=_cp(("arbitrary", "arbitrary")),
        name="diff_attn",
    )(qd, kd, vdt, lam_p, og)


def _swa_kernel(q_ref, kp_ref, kc_ref, vtp_ref, vtc_ref, sink_ref, o_ref):
    w = SWA_WINDOW
    grp = SWA_Q_HEADS // SWA_KV_HEADS
    n = pl.program_id(1)
    key = lax.broadcasted_iota(jnp.int32, (2 * w, grp * w), 0)
    qry = lax.broadcasted_iota(jnp.int32, (2 * w, grp * w), 1) & (w - 1)
    cur_ok = (key >= w) & (key - w <= qry)
    prev_ok = (key < w) & (key > qry)
    low = lax.broadcasted_iota(jnp.int32, (w, LANE), 1) < HEAD_DIM

    for t in range(q_ref.shape[0] // w):
        rows = slice(t * w, (t + 1) * w)
        if t == 0:
            kp, vtp = kp_ref[...], vtp_ref[0]
            valid = cur_ok | (prev_ok & (n > 0))
        else:
            kp, vtp = kc_ref[(t - 1) * w:t * w, :], vtc_ref[t - 1]
            valid = cur_ok | prev_ok
        kc, vtc = kc_ref[rows, :], vtc_ref[t]
        for kv in range(SWA_KV_HEADS):
            kcols = slice(LANE * kv, LANE * (kv + 1))
            vrows = slice(HEAD_DIM * kv, HEAD_DIM * (kv + 1))
            k_win = jnp.concatenate([kp[:, kcols], kc[:, kcols]], axis=0)
            parts = []
            for u in range(2):
                qt = q_ref[rows, LANE * (2 * kv + u):LANE * (2 * kv + u + 1)]
                zero = jnp.zeros_like(qt)
                parts += [jnp.where(low, qt, zero), jnp.where(low, zero, qt)]
            s = jnp.where(valid, _nt(k_win, jnp.concatenate(parts, axis=0)), NEG)
            sink = sink_ref[:, grp * w * kv:grp * w * (kv + 1)]
            m = jnp.maximum(jnp.max(s, axis=0, keepdims=True), sink)
            p = jnp.exp2(s - m)
            den = jnp.sum(p, axis=0, keepdims=True) + jnp.exp2(sink - m)
            vt_win = jnp.concatenate([vtp[vrows, :], vtc[vrows, :]], axis=1)
            o_t = _dot(vt_win, p.astype(BF16)) / den
            for u in range(2):
                pair = jnp.concatenate([o_t[:, 2 * u * w:(2 * u + 1) * w],
                                        o_t[:, (2 * u + 1) * w:(2 * u + 2) * w]], axis=0)
                o_ref[rows, LANE * (2 * kv + u):LANE * (2 * kv + u + 1)] = pair.T.astype(BF16)


def _swa(qs, ks, vst, sink_row, batch, seq):
    t = qs.shape[0]
    w = SWA_WINDOW
    nb = seq // w
    ns = nb // SWA_QB
    cur = lambda b, n: (b * ns + n, 0)
    cur3 = lambda b, n: (b * ns + n, 0, 0)
    prev = lambda b, n: (b * nb + jnp.maximum(n * SWA_QB - 1, 0), 0)
    prev3 = lambda b, n: (b * nb + jnp.maximum(n * SWA_QB - 1, 0), 0, 0)
    return pl.pallas_call(
        _swa_kernel,
        grid=(batch, ns),
        in_specs=[pl.BlockSpec((SWA_QB * w, 512), cur),
                  pl.BlockSpec((w, 256), prev), pl.BlockSpec((SWA_QB * w, 256), cur),
                  pl.BlockSpec((1, LANE, w), prev3), pl.BlockSpec((SWA_QB, LANE, w), cur3),
                  pl.BlockSpec(sink_row.shape, lambda b, n: (0, 0))],
        out_specs=pl.BlockSpec((SWA_QB * w, 512), cur),
        out_shape=jax.ShapeDtypeStruct((t, 512), BF16),
        compiler_params=_cp(("arbitrary", "arbitrary")),
        name="swa_attn",
    )(qs, ks, ks, vst, vst, sink_row)


def _merge_kernel(*refs):
    n_sub = refs[0].shape[0] // MOE_TILE
    h2s = [_merge_mix(k, *refs) for k in range(n_sub)]
    for k in range(n_sub):
        _merge_route(k, h2s[k], *refs)


def _merge_mix(k, h_ref, ya_ref, yb_ref, yc_ref, yd_ref, x_ref, gt1_ref, wg_ref, wb_ref, wo_ref,
               g2_ref, sh2_ref, sc2_ref, rw_ref, rb_ref, xo_ref, h2_ref, comb_ref, dest_ref, meta_ref):
    rows = slice(MOE_TILE * k, MOE_TILE * (k + 1))
    h = h_ref[rows, :]
    d = x_ref.shape[1]
    merged = None
    for b, y_ref in enumerate((ya_ref, yb_ref, yc_ref, yd_ref)):
        gate = jax.nn.sigmoid(_dot(h, wg_ref[0, :, d * b:d * (b + 1)]))
        term = gate * _dot(y_ref[rows, :], wb_ref[0, b])
        merged = term if merged is None else merged + term
    xn = x_ref[rows, :] + gt1_ref[0] * _dot(merged.astype(BF16), wo_ref[0])
    xo_ref[rows, :] = xn
    ms = jnp.mean(xn * xn, axis=-1, keepdims=True)
    h2 = xn * lax.rsqrt(ms + EPS) * g2_ref[...]
    h2 = h2 * (1.0 + sc2_ref[0]) + sh2_ref[0]
    h2_ref[rows, :] = h2.astype(BF16)
    return h2


def _merge_route(k, h2, h_ref, ya_ref, yb_ref, yc_ref, yd_ref, x_ref, gt1_ref, wg_ref, wb_ref, wo_ref,
                 g2_ref, sh2_ref, sc2_ref, rw_ref, rb_ref, xo_ref, h2_ref, comb_ref, dest_ref, meta_ref):
    rows = slice(MOE_TILE * k, MOE_TILE * (k + 1))
    hh, hm = _split2(h2)
    wh = rw_ref[...].astype(BF16)
    logits = _dot(hh, wh) + _dot(hm, wh)
    lt = logits.T
    scores = jax.nn.sigmoid(lt[0:N_EXPERTS, :])
    sel = scores + rb_ref[...]
    per = N_EXPERTS // N_GROUPS
    srow = [sel[e:e + 1, :] for e in range(N_EXPERTS)]
    gsum = []
    for g in range(N_GROUPS):
        a, b_, c, e_ = srow[per * g:per * (g + 1)]
        gsum.append(jnp.maximum(jnp.maximum(jnp.maximum(a + b_, a + c), jnp.maximum(a + e_, b_ + c)),
                                jnp.maximum(b_ + e_, c + e_)))
    best = jnp.maximum(jnp.maximum(gsum[0], gsum[1]), jnp.maximum(gsum[2], gsum[3]))
    taken = None
    weights = []
    picks = []
    for g in range(N_GROUPS):
        hit = gsum[g] == best
        pick = hit if taken is None else hit & jnp.logical_not(taken)
        taken = hit if taken is None else taken | hit
        picks.append(pick.astype(F32))
        for e in range(per * g, per * (g + 1)):
            rank = jnp.zeros_like(best)
            for o in range(per * g, per * (g + 1)):
                if o == e:
                    continue
                ahead = (srow[o] > srow[e]) | ((srow[o] == srow[e]) & (o < e))
                rank = rank + ahead.astype(F32)
            weights.append(jnp.where(pick & (rank < 1.5), scores[e:e + 1, :], 0.0))
    wsum = weights[0]
    for r_ in weights[1:]:
        wsum = wsum + r_
    inv = 1.0 / wsum
    rid = lax.broadcasted_iota(jnp.int32, scores.shape, 0)
    comb_e = jnp.zeros_like(scores)
    for e, r_ in enumerate(weights):
        comb_e = jnp.where(rid == e, r_ * inv, comb_e)

    tm = lt.shape[1]
    gid = lax.broadcasted_iota(jnp.int32, (8, tm), 0)
    onehot = jnp.zeros((8, tm), F32)
    for g in range(N_GROUPS):
        onehot = jnp.where(gid == g, picks[g], onehot)
    before = (lax.broadcasted_iota(jnp.int32, (tm, tm), 0) < lax.broadcasted_iota(jnp.int32, (tm, tm), 1))
    rank = _dot(onehot.astype(BF16), before.astype(BF16))
    dest = jnp.zeros((1, tm), F32)
    off = jnp.zeros((1, 1), F32)
    meta = jnp.zeros((8, LANE), F32)
    mrow = lax.broadcasted_iota(jnp.int32, (8, LANE), 0)
    for g in range(N_GROUPS):
        cnt = jnp.sum(picks[g], axis=-1, keepdims=True)
        dest = dest + picks[g] * (off + rank[g:g + 1, :])
        meta = jnp.where(mrow == g, cnt, meta)
        meta = jnp.where(mrow == N_GROUPS + g, off, meta)
        off = off + jnp.ceil(cnt * (1.0 / MOE_ALIGN)) * MOE_ALIGN
    dest_ref[k] = dest.astype(jnp.int32)
    meta_ref[k] = meta.astype(jnp.int32)
    pad = jnp.zeros((LANE - N_EXPERTS - 8, tm), F32)
    comb_t = jnp.concatenate([comb_e, jnp.where(gid == 0, dest, 0.0), pad], axis=0)
    comb_ref[rows, :] = comb_t.T


def _merge(h, ys, x2, gt1, wg, wb, wo, g2, sh2, sc2, rw, rb, layer, seq):
    t, d = x2.shape
    sub = MERGE_SUBTILES
    tm = sub * MOE_TILE
    tpb = seq // tm
    row = lambda i: (i, 0)
    per_b = lambda i: (i // tpb, 0, 0)
    c2 = lambda i: (0, 0)
    once = pl.Buffered(1)
    return pl.pallas_call(
        _merge_kernel,
        grid=(t // tm,),
        in_specs=[pl.BlockSpec((tm, d), row)] + [pl.BlockSpec((tm, 512), row)] * 4
                 + [pl.BlockSpec((tm, d), row), pl.BlockSpec((1, 1, d), per_b),
                    pl.BlockSpec((1,) + wg.shape[1:], lambda i: (layer, 0, 0), pipeline_mode=once),
                    pl.BlockSpec((1,) + wb.shape[1:], lambda i: (layer, 0, 0, 0), pipeline_mode=once),
                    pl.BlockSpec((1,) + wo.shape[1:], lambda i: (layer, 0, 0), pipeline_mode=once),
                    pl.BlockSpec((1, d), c2),
                    pl.BlockSpec((1, 1, d), per_b), pl.BlockSpec((1, 1, d), per_b),
                    pl.BlockSpec(rw.shape, c2), pl.BlockSpec(rb.shape, c2)],
        out_specs=[pl.BlockSpec((tm, d), row), pl.BlockSpec((tm, d), row), pl.BlockSpec((tm, LANE), row),
                   pl.BlockSpec((sub, 1, MOE_TILE), lambda i: (i, 0, 0)),
                   pl.BlockSpec((sub, 8, LANE), lambda i: (i, 0, 0))],
        out_shape=[jax.ShapeDtypeStruct((t, d), F32), jax.ShapeDtypeStruct((t, d), BF16),
                   jax.ShapeDtypeStruct((t, LANE), F32),
                   jax.ShapeDtypeStruct((t // MOE_TILE, 1, MOE_TILE), jnp.int32),
                   jax.ShapeDtypeStruct((t // MOE_TILE, 8, LANE), jnp.int32)],
        compiler_params=_cp(("arbitrary",)),
        name="merge_router",
    )(h, *ys, x2, gt1, wg, wb, wo, g2, sh2, sc2, rw, rb)


def _moe_kernel(meta_ref, h2_ref, comb_ref, dest_ref, x_ref, gt2_ref, wg_ref, wu_ref, wd_ref, o_ref,
                sorted_s, csort_s, out_s):
    i = pl.program_id(0)
    tm = h2_ref.shape[0]
    n_rows = sorted_s.shape[0]
    per = N_EXPERTS // N_GROUPS
    comb = comb_ref[...]

    place = (lax.broadcasted_iota(jnp.int32, (n_rows, tm), 0) == dest_ref[0]).astype(BF16)
    sorted_s[...] = _dot(place, h2_ref[...]).astype(BF16)
    c_hi, c_lo = _split2(comb)
    csort_s[...] = _dot(place, c_hi) + _dot(place, c_lo)
    out_s[...] = jnp.zeros_like(out_s)
    def run_experts(g, start, size):
        rows = pl.ds(pl.multiple_of(start, MOE_ALIGN), size)
        xk = sorted_s[rows, :]
        cw = csort_s[rows, :]
        lane = lax.broadcasted_iota(jnp.int32, (size, LANE), 1)
        acc = None
        for j in range(per):
            e = per * g + j
            hid = _silu(_dot(xk, wg_ref[0, e])) * _dot(xk, wu_ref[0, e])
            ce = jnp.sum(jnp.where(lane == e, cw, 0.0), axis=-1, keepdims=True)
            term = _dot((hid * ce).astype(BF16), wd_ref[0, e])
            acc = term if acc is None else acc + term
        out_s[rows, :] += acc

    def group(g, carry):
        cnt = meta_ref[i, g]
        off = meta_ref[i, N_GROUPS + g]
        big = MOE_CHUNKS[-1]
        n_big = cnt // big

        def big_chunk(c, inner):
            run_experts(g, off + c * big, big)
            return inner

        lax.fori_loop(0, n_big, big_chunk, 0)
        rest = cnt - n_big * big
        lower = 0
        for size in MOE_CHUNKS:
            @pl.when((rest > lower) & (rest <= size))
            def _(size=size):
                run_experts(g, off + n_big * big, size)
            lower = size
        return carry

    lax.fori_loop(0, N_GROUPS, group, 0)

    dest_col = comb[:, MOE_DEST_LANE:MOE_DEST_LANE + 1].astype(jnp.int32)
    back = (lax.broadcasted_iota(jnp.int32, (tm, n_rows), 1) == dest_col).astype(BF16)
    o_ref[...] = x_ref[...] + gt2_ref[0] * _dot(back, out_s[...].astype(BF16))


def _moe(h2, comb, dest, meta, x2, gt2, wg, wu, wd, layer, seq):
    t, d = x2.shape
    tm = MOE_TILE
    tpb = seq // tm
    n_rows = MOE_SORT_ROWS
    overhang = max(b - a for a, b in zip((0,) + MOE_CHUNKS, MOE_CHUNKS)) - 1
    assert tm + N_GROUPS * (MOE_ALIGN - 1) + overhang <= n_rows
    row = lambda i, m: (i, 0)
    whole = lambda a: pl.BlockSpec((1,) + a.shape[1:], lambda i, m: (layer, 0, 0, 0),
                                   pipeline_mode=pl.Buffered(1))
    grid_spec = pltpu.PrefetchScalarGridSpec(
        num_scalar_prefetch=1,
        grid=(t // tm,),
        in_specs=[pl.BlockSpec((tm, d), row), pl.BlockSpec((tm, LANE), row),
                  pl.BlockSpec((1, 1, tm), lambda i, m: (i, 0, 0)),
                  pl.BlockSpec((tm, d), row),
                  pl.BlockSpec((1, 1, d), lambda i, m: (i // tpb, 0, 0)),
                  whole(wg), whole(wu), whole(wd)],
        out_specs=pl.BlockSpec((tm, d), row),
        scratch_shapes=[pltpu.VMEM((n_rows, d), BF16), pltpu.VMEM((n_rows, LANE), F32),
                        pltpu.VMEM((n_rows, d), F32)])
    return pl.pallas_call(
        _moe_kernel,
        grid_spec=grid_spec,
        out_shape=jax.ShapeDtypeStruct((t, d), F32),
        compiler_params=_cp(("arbitrary",)),
        name="moe",
    )(meta, h2, comb, dest, x2, gt2, wg, wu, wd)


def _layer_params(l, hg_onorm, mla_q_norm, mla_kv_norm, mla_w_uq, mla_w_ukv, mla_qk_norm,
                  diff_qk_norm, swa_qk_norm, swa_sinks, lb_all):
    p = {}

    lb = lb_all[l]
    p["loglb"] = jnp.log(lb)[None, :]
    p["log1mlb"] = jnp.log1p(-lb)[None, :]
    p["ogain"] = jnp.tile(hg_onorm[l], HG_HEADS)[None, :]

    hd = MLA_NOPE + MLA_ROPE
    half = MLA_ROPE // 2
    wq = mla_w_uq[l].reshape(MLA_Q_RANK, MLA_HEADS, hd)
    z = lambda r, n: jnp.zeros((r, MLA_HEADS, n), F32)
    nope, rope = wq[:, :, :MLA_NOPE], wq[:, :, MLA_NOPE:]
    p["wqa"] = jnp.concatenate([nope, rope, z(MLA_Q_RANK, 32)], -1).reshape(MLA_Q_RANK, -1).astype(BF16)
    p["wqb"] = jnp.concatenate([z(MLA_Q_RANK, MLA_NOPE), rope[:, :, half:], rope[:, :, :half],
                                z(MLA_Q_RANK, 32)], -1).reshape(MLA_Q_RANK, -1).astype(BF16)
    wkv = mla_w_ukv[l].reshape(MLA_KV_RANK, MLA_HEADS, MLA_NOPE + MLA_V)
    knope, vproj = wkv[:, :, :MLA_NOPE], wkv[:, :, MLA_NOPE:]
    eye = jnp.eye(MLA_ROPE, dtype=F32)
    swap = jnp.concatenate([eye[:, half:], eye[:, :half]], axis=1)
    place = lambda m: jnp.broadcast_to(
        jnp.concatenate([jnp.zeros((MLA_ROPE, MLA_NOPE), F32), m, jnp.zeros((MLA_ROPE, 32), F32)], -1)[:, None, :],
        (MLA_ROPE, MLA_HEADS, LANE))
    pad_rows = 256 - MLA_KV_RANK - MLA_ROPE
    p["wka"] = jnp.concatenate([jnp.concatenate([knope, z(MLA_KV_RANK, 64)], -1), place(eye),
                                z(pad_rows, LANE)], 0).reshape(256, -1).astype(BF16)
    p["wkb"] = jnp.concatenate([z(MLA_KV_RANK, LANE), place(swap), z(pad_rows, LANE)], 0
                               ).reshape(256, -1).astype(BF16)
    p["wv"] = jnp.concatenate([vproj.reshape(MLA_KV_RANK, -1),
                               jnp.zeros((256 - MLA_KV_RANK, MLA_HEADS * MLA_V), F32)], 0).astype(BF16)
    p["qng"] = mla_q_norm[l][None, :]
    p["kvg"] = jnp.concatenate([mla_kv_norm[l], jnp.ones((256 - MLA_KV_RANK,), F32)])[None, :]

    def rope_gains(g):
        base = jnp.concatenate([g, jnp.zeros((LANE - hd,), F32)])
        part = jnp.concatenate([jnp.zeros((MLA_NOPE,), F32), g[MLA_NOPE + half:], g[MLA_NOPE:MLA_NOPE + half],
                                jnp.zeros((LANE - hd,), F32)])
        return base[None, :], part[None, :]

    p["gq"], p["gqs"] = rope_gains(mla_qk_norm[l, 0])
    p["gk"], p["gks"] = rope_gains(mla_qk_norm[l, 1])
    p["dgq"] = jnp.tile(diff_qk_norm[l, 0], 8)[None, :]
    p["dgk"] = jnp.tile(diff_qk_norm[l, 1], 8)[None, :]
    p["sgq"] = jnp.tile(swa_qk_norm[l, 0], 8)[None, :]
    p["sgk"] = jnp.tile(swa_qk_norm[l, 1], 2)[None, :]
    p["sinks"] = jnp.repeat(swa_sinks[l].astype(F32) * LOG2E, SWA_WINDOW)[None, :]
    return p


def _rope_tables(positions):
    inv_freq = ROPE_BASE ** (-jnp.arange(0, MLA_ROPE, 2, dtype=F32) / MLA_ROPE)
    zeros = lambda n: jnp.zeros((n,), F32)
    half = MLA_ROPE // 2
    pad = LANE - MLA_NOPE - MLA_ROPE
    freq = jnp.concatenate([zeros(MLA_NOPE), inv_freq, inv_freq, zeros(pad)])
    keep = jnp.concatenate([jnp.ones((MLA_NOPE + MLA_ROPE,), F32), zeros(pad)])
    sign = jnp.concatenate([zeros(MLA_NOPE), -jnp.ones((half,), F32), jnp.ones((half,), F32), zeros(pad)])
    ang = positions.astype(F32).reshape(-1)[:, None] * freq
    return jnp.cos(ang) * keep, jnp.sin(ang) * sign


def kernel(x, c, positions, ada_w, ada_b, norm_mix, norm_ffn, w_in, hg_lb_logits, hg_onorm, mla_q_norm, mla_kv_norm, mla_w_uq, mla_w_ukv, mla_qk_norm, diff_qk_norm, diff_lam, diff_onorm, swa_qk_norm, swa_sinks, w_branch, w_out, router_w, router_bias, moe_w_gate, moe_w_up, moe_w_down):
    batch, seq, d = x.shape
    x2 = x.reshape(batch * seq, d)
    cosf, sinf = _rope_tables(positions)
    lb_all = jnp.cumsum(jax.nn.softmax(hg_lb_logits.astype(F32), axis=0), axis=0)
    lb_all = lb_all - lb_all[0]
    mod = _modulation(c, ada_w, ada_b)
    rw = jnp.concatenate([router_w, jnp.zeros((d, LANE - N_EXPERTS), F32)], axis=1)
    rb = router_bias.astype(F32)[:, None]
    moe_w = (moe_w_gate.astype(BF16), moe_w_up.astype(BF16), moe_w_down.astype(BF16))
    in_w = _in_weights(w_in)
    gate_w = w_in[:, :, sum(IN_SPLITS[:13]):].astype(BF16)
    branch_w, out_w = w_branch.astype(BF16), w_out.astype(BF16)

    for l in range(DEPTH):
        sh1, sc1, gt1, sh2, sc2, gt2 = [mod[l, :, d * k:d * (k + 1)][:, None, :] for k in range(6)]
        p = _layer_params(l, hg_onorm, mla_q_norm, mla_kv_norm, mla_w_uq, mla_w_ukv, mla_qk_norm,
                          diff_qk_norm, swa_qk_norm, swa_sinks, lb_all)
        hg3, hf, mla, diff, swa, h = _inproj(x2, norm_mix[l][None, :], sh1, sc1, in_w, l, seq)
        y_a = _hgrn(hg3, hf, p["loglb"], p["log1mlb"], p["ogain"], batch, seq)
        qm, km, vmt, qd, kd, vdt, qs, ks, vst = _prep(mla, diff, swa, cosf, sinf, p)
        y_b = _mla_attn(qm, km, vmt, batch, seq)
        lam_init = 0.8 - 0.6 * math.exp(-0.3 * l)
        y_c = _diff_attn(qd, kd, vdt, diff_lam[l], diff_onorm[l][None, :], lam_init, batch, seq)
        y_d = _swa(qs, ks, vst, p["sinks"], batch, seq)
        x2, h2, comb, dest, meta = _merge(h, (y_a, y_b, y_c, y_d), x2, gt1, gate_w, branch_w, out_w,
                                          norm_ffn[l][None, :], sh2, sc2, rw, rb, l, seq)
        x2 = _moe(h2, comb, dest, meta[:, :, 0], x2, gt2, *moe_w, l, seq)
    return x2.reshape(batch, seq, d)
```

```python
import functools
import math

import jax
import jax.numpy as jnp
from jax import lax
from jax.experimental import pallas as pl
from jax.experimental.pallas import tpu as pltpu

F32 = jnp.float32
BF16 = jnp.bfloat16

D_MODEL = 1024
DEPTH = 2
EPS = 1e-6
N_BRANCH = 4
HG_HEADS = 8
HG_DK = 64
HG_W = HG_HEADS * HG_DK
HG_SUB = 16
HG_CHUNK = 64
HG_SAFE_DECAY = 80.0
HG_DIAG_ROWS =(HG_SUB // 2) * HG_SUB + (HG_SUB // 2) ** 2
MLA_HEADS = 8
MLA_Q_RANK = 256
MLA_KV_RANK = 128
MLA_NOPE = 64
MLA_ROPE = 32
MLA_V = 64
ROPE_BASE = 10000.0
DIFF_HEADS = 4
DIFF_QK = 64
DIFF_V = 128
SWA_Q_HEADS = 8
SWA_KV_HEADS = 2
SWA_WINDOW = 128
HEAD_DIM = 64
N_EXPERTS = 16
N_GROUPS = 4
D_FF_EXPERT = 256
IN_SPLITS = (512, 512, 512, 512, 256, 128, 32, 512, 512, 512, 512, 128, 128, 4096)

MOE_TILE = 512
MERGE_SUBTILES = 2
MOE_ALIGN = 16
MOE_CHUNKS = (64, 128, 192, 256)
MOE_SORT_ROWS = 640
MOE_DEST_LANE = N_EXPERTS
LANE = 128
ATT_BLK = 256
SWA_QB = 8
LOG2E = 1.4426950408889634
NEG = -1e30
VMEM_LIMIT = 56 * 1024 * 1024


def _cp(sem, vmem=VMEM_LIMIT):
    return pltpu.CompilerParams(dimension_semantics=sem, vmem_limit_bytes=vmem)


def _nt(a, b):
    return lax.dot_general(a, b, (((1,), (1,)), ((), ())), preferred_element_type=F32)


def _tn(a, b):
    return lax.dot_general(a, b, (((0,), (0,)), ((), ())), preferred_element_type=F32)


def _dot(a, b):
    return jnp.dot(a, b, preferred_element_type=F32)


def _split2(x):
    hi = x.astype(BF16)
    lo = (x - hi.astype(F32)).astype(BF16)
    return hi, lo


def _seg_id(idx, seg):
    shift = seg.bit_length() - 1
    assert 1 << shift == seg
    return lax.shift_right_logical(idx, shift)


def _same_seg(n, seg):
    r = lax.broadcasted_iota(jnp.int32, (n, n), 0)
    c = lax.broadcasted_iota(jnp.int32, (n, n), 1)
    return _seg_id(r, seg) == _seg_id(c, seg)


def _seg_ones(n, seg):
    return _same_seg(n, seg).astype(BF16)


def _seg_mean_sq(x, seg):
    n = x.shape[-1]
    return _dot((x * x).astype(BF16), _seg_ones(n, seg)) * (1.0 / seg)


def _silu(x):
    return x * jax.nn.sigmoid(x)


def _mod_kernel(c_ref, w_ref, b_ref, o_ref):
    c = c_ref[...]
    o_ref[0] = jnp.dot(_silu(c), w_ref[0], preferred_element_type=F32,
                       precision=lax.Precision.HIGHEST) + b_ref[0]


def _modulation(c, ada_w, ada_b):
    nl, d, n6 = ada_w.shape
    b = c.shape[0]
    tn = 1536
    return pl.pallas_call(
        _mod_kernel,
        grid=(nl, n6 // tn),
        in_specs=[pl.BlockSpec((b, d), lambda l, j: (0, 0)),
                  pl.BlockSpec((1, d, tn), lambda l, j: (l, 0, j)),
                  pl.BlockSpec((1, 1, tn), lambda l, j: (l, 0, j))],
        out_specs=pl.BlockSpec((1, b, tn), lambda l, j: (l, 0, j)),
        out_shape=jax.ShapeDtypeStruct((nl, b, n6), F32),
        compiler_params=_cp(("arbitrary", "arbitrary")),
        name="modulation",
    )(c, ada_w, ada_b.reshape(nl, 1, n6))


N_PREP_CONSTS = 15


def _inproj_kernel(x_ref, g_ref, sh_ref, sc_ref, loglb_ref, log1mlb_ref, whg_ref, wmla_ref, wdiff_ref, wswa_ref,
                   cos_ref, sin_ref, *rest):
    prep_consts, (ohg_ref, olf_ref, oh_ref), prep_outs = (
        rest[:N_PREP_CONSTS], rest[N_PREP_CONSTS:N_PREP_CONSTS + 3], rest[N_PREP_CONSTS + 3:])
    x = x_ref[...]
    ms = jnp.mean(x * x, axis=-1, keepdims=True)
    h = x * lax.rsqrt(ms + EPS) * g_ref[...]
    h = h * (1.0 + sc_ref[0]) + sh_ref[0]
    hb = h.astype(BF16)
    oh_ref[...] = hb

    def proj(w_ref, lo, hi):
        return _dot(hb, w_ref[:, lo:hi])

    def hg_forget():
        fr = proj(whg_ref, 512, 1024)
        ls = jnp.minimum(fr, 0.0) - jnp.log(1.0 + jnp.exp(-jnp.abs(fr)))
        a = loglb_ref[...]
        c2 = log1mlb_ref[...] + ls
        lf = jnp.maximum(a, c2) + jnp.log(1.0 + jnp.exp(-jnp.abs(a - c2)))
        olf_ref[...] = lf
        ohg_ref[:, 1536:2048] = (1.0 - jnp.exp(lf)).astype(BF16)

    def hg_query():
        ohg_ref[:, 0:512] = _silu(proj(whg_ref, 0, 512)).astype(BF16)

    def hg_input():
        ohg_ref[:, 512:1024] = proj(whg_ref, 1024, 1536).astype(BF16)

    def hg_gate():
        ohg_ref[:, 1024:1536] = proj(whg_ref, 1536, 2048).astype(BF16)

    proj.weights = (wmla_ref, wdiff_ref, wswa_ref)
    proj.interleave = (hg_forget, hg_query, hg_input, hg_gate)
    _attn_prep(proj, cos_ref, sin_ref, *prep_consts, *prep_outs)


def _inproj(x2, gain, sh, sc, cosf, sinf, p, seq):
    t, d = x2.shape
    tm = 512
    tpb = seq // tm
    row = lambda i: (i, 0)
    per_b = lambda i: (i // tpb, 0, 0)
    const = lambda a: pl.BlockSpec(a.shape, lambda i: (0,) * a.ndim)
    weights = [p["w_hg"], p["w_mla"], p["w_diff"], p["w_swa"]]
    prep_consts = [p["qng"], p["kvg"], p["wqa"], p["wqb"], p["wka"], p["wkb"], p["wv"],
                   p["gq"], p["gqs"], p["gk"], p["gks"], p["dgq"], p["dgk"], p["sgq"], p["sgk"]]
    assert len(prep_consts) == N_PREP_CONSTS

    def rows_out(w, dt=BF16):
        return pl.BlockSpec((tm, w), row), jax.ShapeDtypeStruct((t, w), dt)

    def transposed_out(n, blk):
        return (pl.BlockSpec((tm // blk, n, blk), lambda i: (i, 0, 0)),
                jax.ShapeDtypeStruct((t // blk, n, blk), BF16))

    outs = [rows_out(2048), rows_out(512, F32), rows_out(d),
            rows_out(1024), rows_out(1024), transposed_out(512, ATT_BLK),
            rows_out(512), rows_out(512), transposed_out(512, ATT_BLK),
            rows_out(512), rows_out(256), transposed_out(LANE, SWA_WINDOW)]
    return pl.pallas_call(
        _inproj_kernel,
        grid=(t // tm,),
        in_specs=[pl.BlockSpec((tm, d), row), const(gain),
                  pl.BlockSpec((1, 1, d), per_b),
                  pl.BlockSpec((1, 1, d), per_b), const(p["loglb"]), const(p["log1mlb"])]
                 + [const(w) for w in weights]
                 + [pl.BlockSpec((tm, LANE), row), pl.BlockSpec((tm, LANE), row)]
                 + [const(a) for a in prep_consts],
        out_specs=[o[0] for o in outs],
        out_shape=[o[1] for o in outs],
        compiler_params=_cp(("arbitrary",)),
        name="inproj",
    )(x2, gain, sh, sc, p["loglb"], p["log1mlb"], *weights, cosf, sinf, *prep_consts)


def _segment_cumsum(x, seg):
    n = x.shape[0]
    r = lax.broadcasted_iota(jnp.int32, (n, n), 0)
    cc = lax.broadcasted_iota(jnp.int32, (n, n), 1)
    same = _same_seg(n, seg)
    tri = (same & (cc <= r)).astype(BF16)
    blk = same.astype(BF16)
    hi, lo = _split2(x)
    return _dot(tri, hi) + _dot(tri, lo), _dot(blk, hi) + _dot(blk, lo)


def _hgrn_chunk_path(i_ref, st_ref, c_s, tot_s, qs_s, kk_s, qe_s, kd_s, ke_s, dec_s, od_s):
    rows_blk = c_s.shape[0]
    tot = tot_s[...]
    rel = c_s[...] - 0.5 * tot
    half_dec = jnp.exp(0.5 * tot)
    kd = kk_s[...] * jnp.exp(-rel)
    qe_s[...] = (qs_s[...] * jnp.exp(rel)).astype(BF16)
    kd_s[...] = kd.astype(BF16)
    ke_s[...] = (kd * half_dec).astype(BF16)
    dec_s[...] = half_dec

    row = lax.broadcasted_iota(jnp.int32, (2 * rows_blk, rows_blk), 0) & (rows_blk - 1)
    col = lax.broadcasted_iota(jnp.int32, (2 * rows_blk, rows_blk), 1)
    intra = (_seg_id(row, HG_CHUNK) == _seg_id(col, HG_CHUNK)) & (col <= row)
    low_q = lax.broadcasted_iota(jnp.int32, (rows_blk, LANE), 1) < HG_DK
    low_c = lax.broadcasted_iota(jnp.int32, (HG_CHUNK, LANE), 1) < HG_DK

    for j in range(HG_W // LANE):
        cols = slice(LANE * j, LANE * (j + 1))
        qe = qe_s[:, cols]
        zero = jnp.zeros_like(qe)
        q2 = jnp.concatenate([jnp.where(low_q, qe, zero), jnp.where(low_q, zero, qe)], axis=0)
        attn = jnp.where(intra, _nt(q2, kd_s[:, cols]), 0.0).astype(BF16)
        o2 = _dot(attn, i_ref[:, cols])
        n_chunks = rows_blk // HG_CHUNK
        chunk_rows = [slice(HG_CHUNK * ch, HG_CHUNK * (ch + 1)) for ch in range(n_chunks)]
        upd = [_tn(i_ref[rows, cols], ke_s[rows, cols]) for rows in chunk_rows]
        st = st_ref[j]
        for ch, rows in enumerate(chunk_rows):
            rows_hi = slice(rows_blk + HG_CHUNK * ch, rows_blk + HG_CHUNK * (ch + 1))
            hd = dec_s[HG_CHUNK * ch:HG_CHUNK * ch + 1, cols]
            inter = _nt(jnp.concatenate([q2[rows], q2[rows_hi]], axis=0), (st * hd).astype(BF16))
            od_s[rows, cols] = jnp.where(low_c, o2[rows] + inter[:HG_CHUNK], o2[rows_hi] + inter[HG_CHUNK:])
            st = st * (hd * hd) + upd[ch]
        st_ref[j] = st


def _hgrn_exact_path(i_ref, st_ref, lf_s, c_s, qs_s, kk_s, qe_s, ke_s, dec_s, od_s, t_s, a_s):
    rows_blk = c_s.shape[0]
    n_sub = rows_blk // HG_SUB
    c, tot = _segment_cumsum(lf_s[...], HG_SUB)
    c_s[...] = c
    qe_s[...] = (qs_s[...] * jnp.exp(c)).astype(BF16)
    ke_s[...] = (kk_s[...] * jnp.exp(tot - c)).astype(BF16)
    dec_s[...] = jnp.exp(tot)

    same_head = _same_seg(LANE, HG_DK)
    head_mask = same_head.astype(F32)
    head_ones = same_head.astype(BF16)
    for j in range(HG_W // LANE):
        st_ref[j] = st_ref[j] * head_mask
    half = HG_SUB // 2
    trow = lax.broadcasted_iota(jnp.int32, (half, HG_W), 0)

    def body(i, carry):
        r0 = pl.multiple_of(i * HG_SUB, HG_SUB)
        rows = pl.ds(r0, HG_SUB)
        c_i = c_s[rows, :]
        qs_i = qs_s[rows, :]
        kk_i = kk_s[rows, :]
        v_i = i_ref[rows, :].astype(F32)
        c_lo, c_hi = c_i[:half], c_i[half:]
        q_lo, q_hi = qs_i[:half], qs_i[half:]
        for s in range(half):
            c_row, k_row = c_i[s:s + 1, :], kk_i[s:s + 1, :]
            e_lo = jnp.exp(jnp.where(trow >= s, c_lo - c_row, NEG))
            e_hi = jnp.exp(c_hi - c_row)
            both = jnp.concatenate([e_lo * q_lo, e_hi * q_hi], axis=0) * k_row
            t_s[s * HG_SUB:(s + 1) * HG_SUB, :] = both.astype(BF16)
        for s in range(half, HG_SUB, 2):
            pair = []
            for u in (s, s + 1):
                e_hi = jnp.exp(jnp.where(trow >= u - half, c_hi - c_i[u:u + 1, :], NEG))
                pair.append(e_hi * q_hi * kk_i[u:u + 1, :])
            base = half * HG_SUB + (s - half) * half
            t_s[base:base + HG_SUB, :] = jnp.concatenate(pair, axis=0).astype(BF16)
        for j in range(HG_W // LANE):
            cols = slice(LANE * j, LANE * (j + 1))
            a_s[:, cols] = _dot(t_s[:, cols], head_ones)
        acc_lo = jnp.zeros((half, HG_W), F32)
        acc_hi = jnp.zeros((half, HG_W), F32)
        for s in range(half):
            acc_lo = acc_lo + a_s[s * HG_SUB:s * HG_SUB + half, :] * v_i[s:s + 1, :]
            acc_hi = acc_hi + a_s[s * HG_SUB + half:(s + 1) * HG_SUB, :] * v_i[s:s + 1, :]
        for s in range(half, HG_SUB):
            base = half * HG_SUB + (s - half) * half
            acc_hi = acc_hi + a_s[base:base + half, :] * v_i[s:s + 1, :]
        acc = jnp.concatenate([acc_lo, acc_hi], axis=0)
        for j in range(HG_W // LANE):
            cols = slice(LANE * j, LANE * (j + 1))
            st = st_ref[j]
            o_int = _nt(qe_s[rows, cols], st.astype(BF16))
            upd = _tn(i_ref[rows, cols], ke_s[rows, cols])
            st_ref[j] = st * dec_s[pl.ds(r0, 1), cols] + upd * head_mask
            od_s[rows, cols] = acc[:, cols] + o_int
        return carry

    lax.fori_loop(0, n_sub, body, 0)


def _hgrn_kernel(q_ref, i_ref, g_ref, k_ref, lf_ref, og_ref, o_ref,
                 st_ref, lf_s, c_s, tot_s, qs_s, kk_s, qe_s, kd_s, ke_s, dec_s, od_s, t_s, a_s):
    @pl.when(pl.program_id(1) == 0)
    def _():
        st_ref[...] = jnp.zeros_like(st_ref)

    lf = lf_ref[...]
    lf_s[...] = lf
    qs_s[...] = q_ref[...].astype(F32)
    kk_s[...] = k_ref[...].astype(F32)
    c, tot = _segment_cumsum(lf, HG_CHUNK)
    c_s[...] = c
    tot_s[...] = tot
    safe = 0.5 * jnp.max(-tot) <= HG_SAFE_DECAY

    @pl.when(safe)
    def _():
        _hgrn_chunk_path(i_ref, st_ref, c_s, tot_s, qs_s, kk_s, qe_s, kd_s, ke_s, dec_s, od_s)

    @pl.when(jnp.logical_not(safe))
    def _():
        _hgrn_exact_path(i_ref, st_ref, lf_s, c_s, qs_s, kk_s, qe_s, ke_s, dec_s, od_s, t_s, a_s)

    o = od_s[...]
    ms = _seg_mean_sq(o, HG_DK)
    on = o * lax.rsqrt(ms + EPS) * og_ref[...]
    o_ref[...] = (on * _silu(g_ref[...].astype(F32))).astype(BF16)


def _hgrn(hg4, lf, ogain, batch, seq):
    t = lf.shape[0]
    rb = 256
    nb = seq // rb
    blk = lambda k: pl.BlockSpec((rb, HG_W), lambda b, n, k=k: (b * nb + n, k))
    vec = pl.BlockSpec((1, HG_W), lambda b, n: (0, 0))
    f32_blk = pltpu.VMEM((rb, HG_W), F32)
    bf16_blk = pltpu.VMEM((rb, HG_W), BF16)
    return pl.pallas_call(
        _hgrn_kernel,
        grid=(batch, nb),
        in_specs=[blk(0), blk(1), blk(2), blk(3), blk(0), vec],
        out_specs=blk(0),
        out_shape=jax.ShapeDtypeStruct((t, HG_W), BF16),
        scratch_shapes=[pltpu.VMEM((HG_W // LANE, LANE, LANE), F32),
                        f32_blk, f32_blk, f32_blk, f32_blk, f32_blk,
                        bf16_blk, bf16_blk, bf16_blk,
                        f32_blk, f32_blk,
                        pltpu.VMEM((HG_DIAG_ROWS, HG_W), BF16),
                        pltpu.VMEM((HG_DIAG_ROWS, HG_W), F32)],
        compiler_params=_cp(("arbitrary", "arbitrary")),
        name="hgrn2",
    )(hg4, hg4, hg4, hg4, lf, ogain)


def _store_transposed_blocks(out_ref, v):
    blk = out_ref.shape[2]
    for u in range(out_ref.shape[0]):
        out_ref[u] = v[u * blk:(u + 1) * blk, :].T.astype(BF16)


def _attn_prep(proj, cos_ref, sin_ref,
               qng_ref, kvg_ref, wqa_ref, wqb_ref, wka_ref, wkb_ref, wv_ref,
               gq_ref, gqs_ref, gk_ref, gks_ref, dgq_ref, dgk_ref, sgq_ref, sgk_ref,
               qm_ref, km_ref, vmt_ref, qd_ref, kd_ref, vdt_ref, qs_ref, ks_ref, vst_ref):
    wmla_ref, wdiff_ref, wswa_ref = proj.weights
    blk = proj(wmla_ref, 0, 512)
    cq = blk[:, :MLA_Q_RANK]
    rest = blk[:, MLA_Q_RANK:]
    cqn = cq * lax.rsqrt(jnp.mean(cq * cq, axis=-1, keepdims=True) + EPS) * qng_ref[...]
    lane = lax.broadcasted_iota(jnp.int32, rest.shape, 1)
    is_kv = lane < MLA_KV_RANK
    ms_kv = jnp.sum(jnp.where(is_kv, rest * rest, 0.0), axis=-1, keepdims=True) * (1.0 / MLA_KV_RANK)
    restn = jnp.where(is_kv, rest * lax.rsqrt(ms_kv + EPS) * kvg_ref[...], rest)
    cqb = cqn.astype(BF16)
    rb = restn.astype(BF16)
    qa = _dot(cqb, wqa_ref[...])
    qb = _dot(cqb, wqb_ref[...])
    ka = _dot(rb, wka_ref[...])
    kb = _dot(rb, wkb_ref[...])
    _store_transposed_blocks(vmt_ref, _dot(rb, wv_ref[...]))
    cosf = cos_ref[...]
    sinf = sin_ref[...]
    cq_t = cosf * gq_ref[...]
    sq_t = sinf * gqs_ref[...]
    ck_t = cosf * gk_ref[...]
    sk_t = sinf * gks_ref[...]
    inv_n = 1.0 / (MLA_NOPE + MLA_ROPE)
    scale = (MLA_NOPE + MLA_ROPE) ** -0.5 * LOG2E
    def seg_norm(x, gain, scale):
        return x * lax.rsqrt(_seg_mean_sq(x, HEAD_DIM) + EPS) * (gain * scale)

    def diff_q():
        qd_ref[...] = seg_norm(proj(wdiff_ref, 0, 512), dgq_ref[...], DIFF_QK ** -0.5 * LOG2E).astype(BF16)

    def diff_k():
        kd_ref[...] = seg_norm(proj(wdiff_ref, 512, 1024), dgk_ref[...], 1.0).astype(BF16)

    def diff_v():
        _store_transposed_blocks(vdt_ref, proj(wdiff_ref, 1024, 1536))

    def swa_q():
        qs_ref[...] = seg_norm(proj(wswa_ref, 0, 512), sgq_ref[...], HEAD_DIM ** -0.5 * LOG2E).astype(BF16)

    def swa_kv():
        skv = proj(wswa_ref, 512, 768)
        kn = seg_norm(skv[:, :LANE], sgk_ref[...], 1.0)
        low = lax.broadcasted_iota(jnp.int32, kn.shape, 1) < HEAD_DIM
        sw = pltpu.roll(kn, HEAD_DIM, 1)
        ks_ref[:, :LANE] = jnp.where(low, kn, sw).astype(BF16)
        ks_ref[:, LANE:] = jnp.where(low, sw, kn).astype(BF16)
        _store_transposed_blocks(vst_ref, skv[:, LANE:])

    pending = list(proj.interleave) + [diff_q, diff_k, diff_v, swa_q, swa_kv]
    for h in range(MLA_HEADS):
        cols = slice(LANE * h, LANE * (h + 1))
        x = qa[:, cols]
        rinv = lax.rsqrt(jnp.sum(x * x, axis=-1, keepdims=True) * inv_n + EPS)
        qm_ref[:, cols] = ((x * cq_t + qb[:, cols] * sq_t) * (rinv * scale)).astype(BF16)
        y = ka[:, cols]
        rinv = lax.rsqrt(jnp.sum(y * y, axis=-1, keepdims=True) * inv_n + EPS)
        km_ref[:, cols] = ((y * ck_t + kb[:, cols] * sk_t) * rinv).astype(BF16)
        if pending:
            pending.pop(0)()
    for work in pending:
        work()


def _causal_t(blk):
    key = lax.broadcasted_iota(jnp.int32, (blk, blk), 0)
    qry = lax.broadcasted_iota(jnp.int32, (blk, blk), 1)
    return key <= qry


def _two_pass_attention(n_sets, score_fn, value_fn, s_scr, acc_scr, blk):
    qi = pl.program_id(1)
    causal = _causal_t(blk)

    def scores(ki, m, masked):
        out = []
        for i in range(n_sets):
            s = score_fn(i, ki)
            if masked:
                s = jnp.where(causal, s, NEG)
            s_scr[i, ki] = s
            out.append(jnp.maximum(m[i], jnp.max(s, axis=0, keepdims=True)))
        return tuple(out)

    def blocked(n, step, carry):
        def many(k0, count, c):
            return step(tuple(k0 + u for u in range(count)), c)
        carry = lax.fori_loop(0, n // 4, lambda kp, c: many(4 * kp, 4, c), carry)
        done = (n // 4) * 4
        carry = lax.cond(n - done >= 2, lambda c: many(done, 2, c), lambda c: c, carry)
        done = (n // 2) * 2
        return lax.cond(n - done == 1, lambda c: many(done, 1, c), lambda c: c, carry)

    def scores_step(kis, m):
        for ki in kis:
            m = scores(ki, m, False)
        return m

    m = tuple(jnp.full((1, blk), NEG, F32) for _ in range(n_sets))
    m = blocked(qi, scores_step, m)
    m = scores(qi, m, True)

    acc_scr[...] = jnp.zeros_like(acc_scr)

    def accumulate(kis, l):
        out = []
        for i in range(n_sets):
            li, pv = l[i], None
            for ki in kis:
                p = jnp.exp2(s_scr[i, ki] - m[i])
                li = li + jnp.sum(p, axis=0, keepdims=True)
                term = _dot(value_fn(i, ki), p.astype(BF16))
                pv = term if pv is None else pv + term
            out.append(li)
            acc_scr[i] += pv
        return tuple(out)

    l = tuple(jnp.zeros((1, blk), F32) for _ in range(n_sets))
    return blocked(qi + 1, accumulate, l)


def _mla_attn_kernel(q_ref, k_ref, vt_ref, o_ref, s_scr, acc_scr):
    blk = q_ref.shape[0]

    def score_fn(h, ki):
        rows = pl.ds(pl.multiple_of(ki * blk, blk), blk)
        cols = slice(LANE * h, LANE * (h + 1))
        return _nt(k_ref[rows, cols], q_ref[:, cols])

    def value_fn(h, ki):
        return vt_ref[ki, MLA_V * h:MLA_V * (h + 1), :]

    l = _two_pass_attention(MLA_HEADS, score_fn, value_fn, s_scr, acc_scr, blk)
    for j in range(MLA_HEADS // 2):
        o_t = jnp.concatenate([acc_scr[2 * j] / l[2 * j], acc_scr[2 * j + 1] / l[2 * j + 1]], axis=0)
        o_ref[:, LANE * j:LANE * (j + 1)] = o_t.T.astype(BF16)


def _mla_attn(qm, km, vmt, batch, seq):
    t = qm.shape[0]
    nq = seq // ATT_BLK
    return pl.pallas_call(
        _mla_attn_kernel,
        grid=(batch, nq),
        in_specs=[pl.BlockSpec((ATT_BLK, 1024), lambda b, i: (b * nq + i, 0)),
                  pl.BlockSpec((seq, 1024), lambda b, i: (b, 0)),
                  pl.BlockSpec((nq, 512, ATT_BLK), lambda b, i: (b, 0, 0))],
        out_specs=pl.BlockSpec((ATT_BLK, 512), lambda b, i: (b * nq + i, 0)),
        out_shape=jax.ShapeDtypeStruct((t, 512), BF16),
        scratch_shapes=[pltpu.VMEM((MLA_HEADS, nq, ATT_BLK, ATT_BLK), F32),
                        pltpu.VMEM((MLA_HEADS, MLA_V, ATT_BLK), F32)],
        compiler_params=_cp(("arbitrary", "arbitrary")),
        name="mla_attn",
    )(qm, km, vmt)


def _diff_attn_kernel(q_ref, k_ref, vt_ref, lam_ref, og_ref, o_ref, s_scr, acc_scr, qm_scr, *, lam_init):
    blk = q_ref.shape[0]
    low = lax.broadcasted_iota(jnp.int32, (blk, LANE), 1) < DIFF_QK
    lp = lam_ref[...]
    lam = (jnp.exp(jnp.sum(lp[0:1] * lp[1:2], axis=-1, keepdims=True))
           - jnp.exp(jnp.sum(lp[2:3] * lp[3:4], axis=-1, keepdims=True)) + lam_init)

    for h in range(DIFF_HEADS):
        qt = q_ref[:, LANE * h:LANE * (h + 1)]
        zero = jnp.zeros_like(qt)
        qm_scr[2 * h] = jnp.where(low, qt, zero)
        qm_scr[2 * h + 1] = jnp.where(low, zero, qt)

    def score_fn(i, ki):
        rows = pl.ds(pl.multiple_of(ki * blk, blk), blk)
        h = i // 2
        return _nt(k_ref[rows, LANE * h:LANE * (h + 1)], qm_scr[i])

    def value_fn(i, ki):
        h = i // 2
        return vt_ref[ki, DIFF_V * h:DIFF_V * (h + 1), :]

    l = _two_pass_attention(2 * DIFF_HEADS, score_fn, value_fn, s_scr, acc_scr, blk)
    for h in range(DIFF_HEADS):
        o_t = acc_scr[2 * h] / l[2 * h] - lam * (acc_scr[2 * h + 1] / l[2 * h + 1])
        on_t = o_t * lax.rsqrt(jnp.mean(o_t * o_t, axis=0, keepdims=True) + EPS)
        o_ref[:, LANE * h:LANE * (h + 1)] = (on_t.T * (og_ref[...] * (1.0 - lam_init))).astype(BF16)


def _diff_attn(qd, kd, vdt, lam_p, og, lam_init, batch, seq):
    t = qd.shape[0]
    nq = seq // ATT_BLK
    return pl.pallas_call(
        functools.partial(_diff_attn_kernel, lam_init=lam_init),
        grid=(batch, nq),
        in_specs=[pl.BlockSpec((ATT_BLK, 512), lambda b, i: (b * nq + i, 0)),
                  pl.BlockSpec((seq, 512), lambda b, i: (b, 0)),
                  pl.BlockSpec((nq, 512, ATT_BLK), lambda b, i: (b, 0, 0)),
                  pl.BlockSpec(lam_p.shape, lambda b, i: (0, 0)),
                  pl.BlockSpec(og.shape, lambda b, i: (0, 0))],
        out_specs=pl.BlockSpec((ATT_BLK, 512), lambda b, i: (b * nq + i, 0)),
        out_shape=jax.ShapeDtypeStruct((t, 512), BF16),
        scratch_shapes=[pltpu.VMEM((2 * DIFF_HEADS, nq, ATT_BLK, ATT_BLK), F32),
                        pltpu.VMEM((2 * DIFF_HEADS, DIFF_V, ATT_BLK), F32),
                        pltpu.VMEM((2 * DIFF_HEADS, ATT_BLK, LANE), BF16)],
        compiler_params=_cp(("arbitrary", "arbitrary")),
        name="diff_attn",
    )(qd, kd, vdt, lam_p, og)


def _swa_kernel(q_ref, kp_ref, kc_ref, vtp_ref, vtc_ref, sink_ref, o_ref):
    w = SWA_WINDOW
    grp = SWA_Q_HEADS // SWA_KV_HEADS
    n = pl.program_id(1)
    key = lax.broadcasted_iota(jnp.int32, (2 * w, grp * w), 0)
    qry = lax.broadcasted_iota(jnp.int32, (2 * w, grp * w), 1) & (w - 1)
    cur_ok = (key >= w) & (key - w <= qry)
    prev_ok = (key < w) & (key > qry)
    low = lax.broadcasted_iota(jnp.int32, (w, LANE), 1) < HEAD_DIM

    for t in range(q_ref.shape[0] // w):
        rows = slice(t * w, (t + 1) * w)
        if t == 0:
            kp, vtp = kp_ref[...], vtp_ref[0]
            valid = cur_ok | (prev_ok & (n > 0))
        else:
            kp, vtp = kc_ref[(t - 1) * w:t * w, :], vtc_ref[t - 1]
            valid = cur_ok | prev_ok
        kc, vtc = kc_ref[rows, :], vtc_ref[t]
        for kv in range(SWA_KV_HEADS):
            kcols = slice(LANE * kv, LANE * (kv + 1))
            vrows = slice(HEAD_DIM * kv, HEAD_DIM * (kv + 1))
            k_win = jnp.concatenate([kp[:, kcols], kc[:, kcols]], axis=0)
            parts = []
            for u in range(2):
                qt = q_ref[rows, LANE * (2 * kv + u):LANE * (2 * kv + u + 1)]
                zero = jnp.zeros_like(qt)
                parts += [jnp.where(low, qt, zero), jnp.where(low, zero, qt)]
            s = jnp.where(valid, _nt(k_win, jnp.concatenate(parts, axis=0)), NEG)
            sink = sink_ref[:, grp * w * kv:grp * w * (kv + 1)]
            m = jnp.maximum(jnp.max(s, axis=0, keepdims=True), sink)
            p = jnp.exp2(s - m)
            den = jnp.sum(p, axis=0, keepdims=True) + jnp.exp2(sink - m)
            vt_win = jnp.concatenate([vtp[vrows, :], vtc[vrows, :]], axis=1)
            o_t = _dot(vt_win, p.astype(BF16)) / den
            for u in range(2):
                pair = jnp.concatenate([o_t[:, 2 * u * w:(2 * u + 1) * w],
                                        o_t[:, (2 * u + 1) * w:(2 * u + 2) * w]], axis=0)
                o_ref[rows, LANE * (2 * kv + u):LANE * (2 * kv + u + 1)] = pair.T.astype(BF16)


def _swa(qs, ks, vst, sink_row, batch, seq):
    t = qs.shape[0]
    w = SWA_WINDOW
    nb = seq // w
    ns = nb // SWA_QB
    cur = lambda b, n: (b * ns + n, 0)
    cur3 = lambda b, n: (b * ns + n, 0, 0)
    prev = lambda b, n: (b * nb + jnp.maximum(n * SWA_QB - 1, 0), 0)
    prev3 = lambda b, n: (b * nb + jnp.maximum(n * SWA_QB - 1, 0), 0, 0)
    return pl.pallas_call(
        _swa_kernel,
        grid=(batch, ns),
        in_specs=[pl.BlockSpec((SWA_QB * w, 512), cur),
                  pl.BlockSpec((w, 256), prev), pl.BlockSpec((SWA_QB * w, 256), cur),
                  pl.BlockSpec((1, LANE, w), prev3), pl.BlockSpec((SWA_QB, LANE, w), cur3),
                  pl.BlockSpec(sink_row.shape, lambda b, n: (0, 0))],
        out_specs=pl.BlockSpec((SWA_QB * w, 512), cur),
        out_shape=jax.ShapeDtypeStruct((t, 512), BF16),
        compiler_params=_cp(("arbitrary", "arbitrary")),
        name="swa_attn",
    )(qs, ks, ks, vst, vst, sink_row)


def _merge_kernel(*refs):
    n_sub = refs[0].shape[0] // MOE_TILE
    h2s = [_merge_mix(k, *refs) for k in range(n_sub)]
    for k in range(n_sub):
        _merge_route(k, h2s[k], *refs)


def _merge_mix(k, h_ref, ya_ref, yb_ref, yc_ref, yd_ref, x_ref, gt1_ref, wg_ref, wb_ref, wo_ref,
               g2_ref, sh2_ref, sc2_ref, rw_ref, rb_ref, xo_ref, h2_ref, comb_ref, dest_ref, meta_ref):
    rows = slice(MOE_TILE * k, MOE_TILE * (k + 1))
    h = h_ref[rows, :]
    d = x_ref.shape[1]
    merged = None
    for b, y_ref in enumerate((ya_ref, yb_ref, yc_ref, yd_ref)):
        gate = jax.nn.sigmoid(_dot(h, wg_ref[:, d * b:d * (b + 1)]))
        term = gate * _dot(y_ref[rows, :], wb_ref[b])
        merged = term if merged is None else merged + term
    xn = x_ref[rows, :] + gt1_ref[0] * _dot(merged.astype(BF16), wo_ref[...])
    xo_ref[rows, :] = xn
    ms = jnp.mean(xn * xn, axis=-1, keepdims=True)
    h2 = xn * lax.rsqrt(ms + EPS) * g2_ref[...]
    h2 = h2 * (1.0 + sc2_ref[0]) + sh2_ref[0]
    h2_ref[rows, :] = h2.astype(BF16)
    return h2


def _merge_route(k, h2, h_ref, ya_ref, yb_ref, yc_ref, yd_ref, x_ref, gt1_ref, wg_ref, wb_ref, wo_ref,
                 g2_ref, sh2_ref, sc2_ref, rw_ref, rb_ref, xo_ref, h2_ref, comb_ref, dest_ref, meta_ref):
    rows = slice(MOE_TILE * k, MOE_TILE * (k + 1))
    hh, hm = _split2(h2)
    wh = rw_ref[...].astype(BF16)
    logits = _dot(hh, wh) + _dot(hm, wh)
    lt = logits.T
    scores = jax.nn.sigmoid(lt[0:N_EXPERTS, :])
    sel = scores + rb_ref[...]
    per = N_EXPERTS // N_GROUPS
    srow = [sel[e:e + 1, :] for e in range(N_EXPERTS)]
    gsum = []
    for g in range(N_GROUPS):
        a, b_, c, e_ = srow[per * g:per * (g + 1)]
        gsum.append(jnp.maximum(jnp.maximum(jnp.maximum(a + b_, a + c), jnp.maximum(a + e_, b_ + c)),
                                jnp.maximum(b_ + e_, c + e_)))
    best = jnp.maximum(jnp.maximum(gsum[0], gsum[1]), jnp.maximum(gsum[2], gsum[3]))
    taken = None
    weights = []
    picks = []
    for g in range(N_GROUPS):
        hit = gsum[g] == best
        pick = hit if taken is None else hit & jnp.logical_not(taken)
        taken = hit if taken is None else taken | hit
        picks.append(pick.astype(F32))
        for e in range(per * g, per * (g + 1)):
            rank = jnp.zeros_like(best)
            for o in range(per * g, per * (g + 1)):
                if o == e:
                    continue
                ahead = (srow[o] > srow[e]) | ((srow[o] == srow[e]) & (o < e))
                rank = rank + ahead.astype(F32)
            weights.append(jnp.where(pick & (rank < 1.5), scores[e:e + 1, :], 0.0))
    wsum = weights[0]
    for r_ in weights[1:]:
        wsum = wsum + r_
    inv = 1.0 / wsum
    rid = lax.broadcasted_iota(jnp.int32, scores.shape, 0)
    comb_e = jnp.zeros_like(scores)
    for e, r_ in enumerate(weights):
        comb_e = jnp.where(rid == e, r_ * inv, comb_e)

    tm = lt.shape[1]
    gid = lax.broadcasted_iota(jnp.int32, (8, tm), 0)
    onehot = jnp.zeros((8, tm), F32)
    for g in range(N_GROUPS):
        onehot = jnp.where(gid == g, picks[g], onehot)
    before = (lax.broadcasted_iota(jnp.int32, (tm, tm), 0) < lax.broadcasted_iota(jnp.int32, (tm, tm), 1))
    rank = _dot(onehot.astype(BF16), before.astype(BF16))
    dest = jnp.zeros((1, tm), F32)
    off = jnp.zeros((1, 1), F32)
    meta = jnp.zeros((8, LANE), F32)
    mrow = lax.broadcasted_iota(jnp.int32, (8, LANE), 0)
    for g in range(N_GROUPS):
        cnt = jnp.sum(picks[g], axis=-1, keepdims=True)
        dest = dest + picks[g] * (off + rank[g:g + 1, :])
        meta = jnp.where(mrow == g, cnt, meta)
        meta = jnp.where(mrow == N_GROUPS + g, off, meta)
        off = off + jnp.ceil(cnt * (1.0 / MOE_ALIGN)) * MOE_ALIGN
    dest_ref[k] = dest.astype(jnp.int32)
    meta_ref[k] = meta.astype(jnp.int32)
    pad = jnp.zeros((LANE - N_EXPERTS - 8, tm), F32)
    comb_t = jnp.concatenate([comb_e, jnp.where(gid == 0, dest, 0.0), pad], axis=0)
    comb_ref[rows, :] = comb_t.T


def _merge(h, ys, x2, gt1, wg, wb, wo, g2, sh2, sc2, rw, rb, seq):
    t, d = x2.shape
    sub = MERGE_SUBTILES
    tm = sub * MOE_TILE
    tpb = seq // tm
    row = lambda i: (i, 0)
    per_b = lambda i: (i // tpb, 0, 0)
    c2 = lambda i: (0, 0)
    once = pl.Buffered(1)
    return pl.pallas_call(
        _merge_kernel,
        grid=(t // tm,),
        in_specs=[pl.BlockSpec((tm, d), row)] + [pl.BlockSpec((tm, 512), row)] * 4
                 + [pl.BlockSpec((tm, d), row), pl.BlockSpec((1, 1, d), per_b),
                    pl.BlockSpec(wg.shape, c2, pipeline_mode=once),
                    pl.BlockSpec(wb.shape, lambda i: (0, 0, 0), pipeline_mode=once),
                    pl.BlockSpec(wo.shape, c2, pipeline_mode=once), pl.BlockSpec((1, d), c2),
                    pl.BlockSpec((1, 1, d), per_b), pl.BlockSpec((1, 1, d), per_b),
                    pl.BlockSpec(rw.shape, c2), pl.BlockSpec(rb.shape, c2)],
        out_specs=[pl.BlockSpec((tm, d), row), pl.BlockSpec((tm, d), row), pl.BlockSpec((tm, LANE), row),
                   pl.BlockSpec((sub, 1, MOE_TILE), lambda i: (i, 0, 0)),
                   pl.BlockSpec((sub, 8, LANE), lambda i: (i, 0, 0))],
        out_shape=[jax.ShapeDtypeStruct((t, d), F32), jax.ShapeDtypeStruct((t, d), BF16),
                   jax.ShapeDtypeStruct((t, LANE), F32),
                   jax.ShapeDtypeStruct((t // MOE_TILE, 1, MOE_TILE), jnp.int32),
                   jax.ShapeDtypeStruct((t // MOE_TILE, 8, LANE), jnp.int32)],
        compiler_params=_cp(("arbitrary",)),
        name="merge_router",
    )(h, *ys, x2, gt1, wg, wb, wo, g2, sh2, sc2, rw, rb)


def _moe_kernel(meta_ref, h2_ref, comb_ref, dest_ref, x_ref, gt2_ref, wg_ref, wu_ref, wd_ref, o_ref,
                sorted_s, csort_s, out_s):
    i = pl.program_id(0)
    tm = h2_ref.shape[0]
    n_rows = sorted_s.shape[0]
    per = N_EXPERTS // N_GROUPS
    comb = comb_ref[...]

    place = (lax.broadcasted_iota(jnp.int32, (n_rows, tm), 0) == dest_ref[0]).astype(BF16)
    sorted_s[...] = _dot(place, h2_ref[...]).astype(BF16)
    c_hi, c_lo = _split2(comb)
    csort_s[...] = _dot(place, c_hi) + _dot(place, c_lo)
    out_s[...] = jnp.zeros_like(out_s)
    def run_experts(g, start, size):
        rows = pl.ds(pl.multiple_of(start, MOE_ALIGN), size)
        xk = sorted_s[rows, :]
        cw = csort_s[rows, :]
        lane = lax.broadcasted_iota(jnp.int32, (size, LANE), 1)
        acc = None
        for j in range(per):
            e = per * g + j
            hid = _silu(_dot(xk, wg_ref[0, e])) * _dot(xk, wu_ref[0, e])
            ce = jnp.sum(jnp.where(lane == e, cw, 0.0), axis=-1, keepdims=True)
            term = _dot((hid * ce).astype(BF16), wd_ref[0, e])
            acc = term if acc is None else acc + term
        out_s[rows, :] += acc

    def group(g, carry):
        cnt = meta_ref[i, g]
        off = meta_ref[i, N_GROUPS + g]
        big = MOE_CHUNKS[-1]
        n_big = cnt // big

        def big_chunk(c, inner):
            run_experts(g, off + c * big, big)
            return inner

        lax.fori_loop(0, n_big, big_chunk, 0)
        rest = cnt - n_big * big
        lower = 0
        for size in MOE_CHUNKS:
            @pl.when((rest > lower) & (rest <= size))
            def _(size=size):
                run_experts(g, off + n_big * big, size)
            lower = size
        return carry

    lax.fori_loop(0, N_GROUPS, group, 0)

    dest_col = comb[:, MOE_DEST_LANE:MOE_DEST_LANE + 1].astype(jnp.int32)
    back = (lax.broadcasted_iota(jnp.int32, (tm, n_rows), 1) == dest_col).astype(BF16)
    o_ref[...] = x_ref[...] + gt2_ref[0] * _dot(back, out_s[...].astype(BF16))


def _moe(h2, comb, dest, meta, x2, gt2, wg, wu, wd, layer, seq):
    t, d = x2.shape
    tm = MOE_TILE
    tpb = seq // tm
    n_rows = MOE_SORT_ROWS
    overhang = max(b - a for a, b in zip((0,) + MOE_CHUNKS, MOE_CHUNKS)) - 1
    assert tm + N_GROUPS * (MOE_ALIGN - 1) + overhang <= n_rows
    row = lambda i, m: (i, 0)
    whole = lambda a: pl.BlockSpec((1,) + a.shape[1:], lambda i, m: (layer, 0, 0, 0),
                                   pipeline_mode=pl.Buffered(1))
    grid_spec = pltpu.PrefetchScalarGridSpec(
        num_scalar_prefetch=1,
        grid=(t // tm,),
        in_specs=[pl.BlockSpec((tm, d), row), pl.BlockSpec((tm, LANE), row),
                  pl.BlockSpec((1, 1, tm), lambda i, m: (i, 0, 0)),
                  pl.BlockSpec((tm, d), row),
                  pl.BlockSpec((1, 1, d), lambda i, m: (i // tpb, 0, 0)),
                  whole(wg), whole(wu), whole(wd)],
        out_specs=pl.BlockSpec((tm, d), row),
        scratch_shapes=[pltpu.VMEM((n_rows, d), BF16), pltpu.VMEM((n_rows, LANE), F32),
                        pltpu.VMEM((n_rows, d), F32)])
    return pl.pallas_call(
        _moe_kernel,
        grid_spec=grid_spec,
        out_shape=jax.ShapeDtypeStruct((t, d), F32),
        compiler_params=_cp(("arbitrary",)),
        name="moe",
    )(meta, h2, comb, dest, x2, gt2, wg, wu, wd)


def _layer_params(l, w_in, hg_onorm, mla_q_norm, mla_kv_norm, mla_w_uq, mla_w_ukv, mla_qk_norm,
                  diff_qk_norm, swa_qk_norm, swa_sinks, lb_all):
    ends = [sum(IN_SPLITS[:i]) for i in range(len(IN_SPLITS) + 1)]
    cols = lambda a, b: w_in[l, :, ends[a]:ends[b]].astype(BF16)
    mla_pad = 512 - (ends[7] - ends[4])
    p = {"w_hg": cols(0, 4),
         "w_mla": jnp.pad(cols(4, 7), ((0, 0), (0, mla_pad))),
         "w_diff": cols(7, 10), "w_swa": cols(10, 13), "wg": cols(13, 14)}

    lb = lb_all[l]
    p["loglb"] = jnp.log(lb)[None, :]
    p["log1mlb"] = jnp.log1p(-lb)[None, :]
    p["ogain"] = jnp.tile(hg_onorm[l], HG_HEADS)[None, :]

    hd = MLA_NOPE + MLA_ROPE
    half = MLA_ROPE // 2
    wq = mla_w_uq[l].reshape(MLA_Q_RANK, MLA_HEADS, hd)
    z = lambda r, n: jnp.zeros((r, MLA_HEADS, n), F32)
    nope, rope = wq[:, :, :MLA_NOPE], wq[:, :, MLA_NOPE:]
    p["wqa"] = jnp.concatenate([nope, rope, z(MLA_Q_RANK, 32)], -1).reshape(MLA_Q_RANK, -1).astype(BF16)
    p["wqb"] = jnp.concatenate([z(MLA_Q_RANK, MLA_NOPE), rope[:, :, half:], rope[:, :, :half],
                                z(MLA_Q_RANK, 32)], -1).reshape(MLA_Q_RANK, -1).astype(BF16)
    wkv = mla_w_ukv[l].reshape(MLA_KV_RANK, MLA_HEADS, MLA_NOPE + MLA_V)
    knope, vproj = wkv[:, :, :MLA_NOPE], wkv[:, :, MLA_NOPE:]
    eye = jnp.eye(MLA_ROPE, dtype=F32)
    swap = jnp.concatenate([eye[:, half:], eye[:, :half]], axis=1)
    place = lambda m: jnp.broadcast_to(
        jnp.concatenate([jnp.zeros((MLA_ROPE, MLA_NOPE), F32), m, jnp.zeros((MLA_ROPE, 32), F32)], -1)[:, None, :],
        (MLA_ROPE, MLA_HEADS, LANE))
    pad_rows = 256 - MLA_KV_RANK - MLA_ROPE
    p["wka"] = jnp.concatenate([jnp.concatenate([knope, z(MLA_KV_RANK, 64)], -1), place(eye),
                                z(pad_rows, LANE)], 0).reshape(256, -1).astype(BF16)
    p["wkb"] = jnp.concatenate([z(MLA_KV_RANK, LANE), place(swap), z(pad_rows, LANE)], 0
                               ).reshape(256, -1).astype(BF16)
    p["wv"] = jnp.concatenate([vproj.reshape(MLA_KV_RANK, -1),
                               jnp.zeros((256 - MLA_KV_RANK, MLA_HEADS * MLA_V), F32)], 0).astype(BF16)
    p["qng"] = mla_q_norm[l][None, :]
    p["kvg"] = jnp.concatenate([mla_kv_norm[l], jnp.ones((256 - MLA_KV_RANK,), F32)])[None, :]

    def rope_gains(g):
        base = jnp.concatenate([g, jnp.zeros((LANE - hd,), F32)])
        part = jnp.concatenate([jnp.zeros((MLA_NOPE,), F32), g[MLA_NOPE + half:], g[MLA_NOPE:MLA_NOPE + half],
                                jnp.zeros((LANE - hd,), F32)])
        return base[None, :], part[None, :]

    p["gq"], p["gqs"] = rope_gains(mla_qk_norm[l, 0])
    p["gk"], p["gks"] = rope_gains(mla_qk_norm[l, 1])
    p["dgq"] = jnp.tile(diff_qk_norm[l, 0], 8)[None, :]
    p["dgk"] = jnp.tile(diff_qk_norm[l, 1], 8)[None, :]
    p["sgq"] = jnp.tile(swa_qk_norm[l, 0], 8)[None, :]
    p["sgk"] = jnp.tile(swa_qk_norm[l, 1], 2)[None, :]
    p["sinks"] = jnp.repeat(swa_sinks[l].astype(F32) * LOG2E, SWA_WINDOW)[None, :]
    return p


def _rope_tables(positions):
    inv_freq = ROPE_BASE ** (-jnp.arange(0, MLA_ROPE, 2, dtype=F32) / MLA_ROPE)
    zeros = lambda n: jnp.zeros((n,), F32)
    half = MLA_ROPE // 2
    pad = LANE - MLA_NOPE - MLA_ROPE
    freq = jnp.concatenate([zeros(MLA_NOPE), inv_freq, inv_freq, zeros(pad)])
    keep = jnp.concatenate([jnp.ones((MLA_NOPE + MLA_ROPE,), F32), zeros(pad)])
    sign = jnp.concatenate([zeros(MLA_NOPE), -jnp.ones((half,), F32), jnp.ones((half,), F32), zeros(pad)])
    ang = positions.astype(F32).reshape(-1)[:, None] * freq
    return jnp.cos(ang) * keep, jnp.sin(ang) * sign


def kernel(x, c, positions, ada_w, ada_b, norm_mix, norm_ffn, w_in, hg_lb_logits, hg_onorm, mla_q_norm, mla_kv_norm, mla_w_uq, mla_w_ukv, mla_qk_norm, diff_qk_norm, diff_lam, diff_onorm, swa_qk_norm, swa_sinks, w_branch, w_out, router_w, router_bias, moe_w_gate, moe_w_up, moe_w_down):
    batch, seq, d = x.shape
    x2 = x.reshape(batch * seq, d)
    cosf, sinf = _rope_tables(positions)
    lb_all = jnp.cumsum(jax.nn.softmax(hg_lb_logits.astype(F32), axis=0), axis=0)
    lb_all = lb_all - lb_all[0]
    mod = _modulation(c, ada_w, ada_b)
    rw = jnp.concatenate([router_w, jnp.zeros((d, LANE - N_EXPERTS), F32)], axis=1)
    rb = router_bias.astype(F32)[:, None]
    moe_w = (moe_w_gate.astype(BF16), moe_w_up.astype(BF16), moe_w_down.astype(BF16))

    for l in range(DEPTH):
        sh1, sc1, gt1, sh2, sc2, gt2 = [mod[l, :, d * k:d * (k + 1)][:, None, :] for k in range(6)]
        p = _layer_params(l, w_in, hg_onorm, mla_q_norm, mla_kv_norm, mla_w_uq, mla_w_ukv, mla_qk_norm,
                          diff_qk_norm, swa_qk_norm, swa_sinks, lb_all)
        hg4, lf, h, qm, km, vmt, qd, kd, vdt, qs, ks, vst = _inproj(
            x2, norm_mix[l][None, :], sh1, sc1, cosf, sinf, p, seq)
        y_a = _hgrn(hg4, lf, p["ogain"], batch, seq)
        y_b = _mla_attn(qm, km, vmt, batch, seq)
        lam_init = 0.8 - 0.6 * math.exp(-0.3 * l)
        y_c = _diff_attn(qd, kd, vdt, diff_lam[l], diff_onorm[l][None, :], lam_init, batch, seq)
        y_d = _swa(qs, ks, vst, p["sinks"], batch, seq)
        x2, h2, comb, dest, meta = _merge(h, (y_a, y_b, y_c, y_d), x2, gt1, p["wg"], w_branch[l].astype(BF16),
                                          w_out[l].astype(BF16), norm_ffn[l][None, :], sh2, sc2, rw, rb, seq)
        x2 = _moe(h2, comb, dest, meta[:, :, 0], x2, gt2, *moe_w, l, seq)
    return x2.reshape(batch, seq, d)
```

```python
import functools
import math

import jax
import jax.numpy as jnp
from jax import lax
from jax.experimental import pallas as pl
from jax.experimental.pallas import tpu as pltpu

F32 = jnp.float32
BF16 = jnp.bfloat16

D_MODEL = 1024
DEPTH = 2
EPS = 1e-6
N_BRANCH = 4
HG_HEADS = 8
HG_DK = 64
HG_W = HG_HEADS * HG_DK
HG_SUB = 16
HG_CHUNK = 64
HG_SAFE_DECAY = 80.0
HG_DIAG_ROWS =(HG_SUB // 2) * HG_SUB + (HG_SUB // 2) ** 2
MLA_HEADS = 8
MLA_Q_RANK = 256
MLA_KV_RANK = 128
MLA_NOPE = 64
MLA_ROPE = 32
MLA_V = 64
ROPE_BASE = 10000.0
DIFF_HEADS = 4
DIFF_QK = 64
DIFF_V = 128
SWA_Q_HEADS = 8
SWA_KV_HEADS = 2
SWA_WINDOW = 128
HEAD_DIM = 64
N_EXPERTS = 16
N_GROUPS = 4
D_FF_EXPERT = 256
IN_SPLITS = (512, 512, 512, 512, 256, 128, 32, 512, 512, 512, 512, 128, 128, 4096)

MOE_TILE = 512
MERGE_SUBTILES = 2
MOE_ALIGN = 16
MOE_CHUNKS = (64, 128, 192, 256)
MOE_SORT_ROWS = 640
MOE_DEST_LANE = N_EXPERTS
LANE = 128
ATT_BLK = 256
SWA_QB = 8
LOG2E = 1.4426950408889634
NEG = -1e30
VMEM_LIMIT = 56 * 1024 * 1024


def _cp(sem, vmem=VMEM_LIMIT):
    return pltpu.CompilerParams(dimension_semantics=sem, vmem_limit_bytes=vmem)


def _nt(a, b):
    return lax.dot_general(a, b, (((1,), (1,)), ((), ())), preferred_element_type=F32)


def _tn(a, b):
    return lax.dot_general(a, b, (((0,), (0,)), ((), ())), preferred_element_type=F32)


def _dot(a, b):
    return jnp.dot(a, b, preferred_element_type=F32)


def _split2(x):
    hi = x.astype(BF16)
    lo = (x - hi.astype(F32)).astype(BF16)
    return hi, lo


def _seg_id(idx, seg):
    shift = seg.bit_length() - 1
    assert 1 << shift == seg
    return lax.shift_right_logical(idx, shift)


def _same_seg(n, seg):
    r = lax.broadcasted_iota(jnp.int32, (n, n), 0)
    c = lax.broadcasted_iota(jnp.int32, (n, n), 1)
    return _seg_id(r, seg) == _seg_id(c, seg)


def _seg_ones(n, seg):
    return _same_seg(n, seg).astype(BF16)


def _seg_mean_sq(x, seg):
    n = x.shape[-1]
    return _dot((x * x).astype(BF16), _seg_ones(n, seg)) * (1.0 / seg)


def _silu(x):
    return x * jax.nn.sigmoid(x)


def _mod_kernel(c_ref, w_ref, b_ref, o_ref):
    c = c_ref[...]
    o_ref[0] = jnp.dot(_silu(c), w_ref[0], preferred_element_type=F32,
                       precision=lax.Precision.HIGHEST) + b_ref[0]


def _modulation(c, ada_w, ada_b):
    nl, d, n6 = ada_w.shape
    b = c.shape[0]
    tn = 1536
    return pl.pallas_call(
        _mod_kernel,
        grid=(nl, n6 // tn),
        in_specs=[pl.BlockSpec((b, d), lambda l, j: (0, 0)),
                  pl.BlockSpec((1, d, tn), lambda l, j: (l, 0, j)),
                  pl.BlockSpec((1, 1, tn), lambda l, j: (l, 0, j))],
        out_specs=pl.BlockSpec((1, b, tn), lambda l, j: (l, 0, j)),
        out_shape=jax.ShapeDtypeStruct((nl, b, n6), F32),
        compiler_params=_cp(("arbitrary", "arbitrary")),
        name="modulation",
    )(c, ada_w, ada_b.reshape(nl, 1, n6))


N_PREP_CONSTS = 13


def _inproj_kernel(x_ref, g_ref, sh_ref, sc_ref, loglb_ref, log1mlb_ref, whg_ref, wmla_ref, wdiff_ref, wswa_ref,
                   cos_ref, sin_ref, *rest):
    prep_consts, (ohg_ref, olf_ref, oh_ref), prep_outs = (
        rest[:N_PREP_CONSTS], rest[N_PREP_CONSTS:N_PREP_CONSTS + 3], rest[N_PREP_CONSTS + 3:])
    x = x_ref[...]
    ms = jnp.mean(x * x, axis=-1, keepdims=True)
    h = x * lax.rsqrt(ms + EPS) * g_ref[...]
    h = h * (1.0 + sc_ref[0]) + sh_ref[0]
    hb = h.astype(BF16)
    oh_ref[...] = hb

    def proj(w_ref, lo, hi):
        return _dot(hb, w_ref[:, lo:hi])

    def hg_forget():
        fr = proj(whg_ref, 512, 1024)
        ls = jnp.minimum(fr, 0.0) - jnp.log(1.0 + jnp.exp(-jnp.abs(fr)))
        a = loglb_ref[...]
        c2 = log1mlb_ref[...] + ls
        lf = jnp.maximum(a, c2) + jnp.log(1.0 + jnp.exp(-jnp.abs(a - c2)))
        olf_ref[...] = lf
        ohg_ref[:, 1536:2048] = (1.0 - jnp.exp(lf)).astype(BF16)

    def hg_query():
        ohg_ref[:, 0:512] = _silu(proj(whg_ref, 0, 512)).astype(BF16)

    def hg_input():
        ohg_ref[:, 512:1024] = proj(whg_ref, 1024, 1536).astype(BF16)

    def hg_gate():
        ohg_ref[:, 1024:1536] = proj(whg_ref, 1536, 2048).astype(BF16)

    proj.weights = (wmla_ref, wdiff_ref, wswa_ref)
    proj.interleave = (hg_forget, hg_query, hg_input, hg_gate)
    _attn_prep(proj, cos_ref, sin_ref, *prep_consts, *prep_outs)


def _inproj(x2, gain, sh, sc, cosf, sinf, p, seq):
    t, d = x2.shape
    tm = 512
    tpb = seq // tm
    row = lambda i: (i, 0)
    per_b = lambda i: (i // tpb, 0, 0)
    const = lambda a: pl.BlockSpec(a.shape, lambda i: (0,) * a.ndim)
    weights = [p["w_hg"], p["w_mla"], p["w_diff"], p["w_swa"]]
    prep_consts = [p["qng"], p["kvg"], p["wqa"], p["wka"], p["wv"],
                   p["gq"], p["gqs"], p["gk"], p["gks"], p["dgq"], p["dgk"], p["sgq"], p["sgk"]]
    assert len(prep_consts) == N_PREP_CONSTS

    def rows_out(w, dt=BF16):
        return pl.BlockSpec((tm, w), row), jax.ShapeDtypeStruct((t, w), dt)

    def transposed_out(n, blk):
        return (pl.BlockSpec((tm // blk, n, blk), lambda i: (i, 0, 0)),
                jax.ShapeDtypeStruct((t // blk, n, blk), BF16))

    outs = [rows_out(2048), rows_out(512, F32), rows_out(d),
            rows_out(1024), rows_out(1024), transposed_out(512, ATT_BLK),
            rows_out(512), rows_out(512), transposed_out(512, ATT_BLK),
            rows_out(512), rows_out(256), transposed_out(LANE, SWA_WINDOW)]
    return pl.pallas_call(
        _inproj_kernel,
        grid=(t // tm,),
        in_specs=[pl.BlockSpec((tm, d), row), const(gain),
                  pl.BlockSpec((1, 1, d), per_b),
                  pl.BlockSpec((1, 1, d), per_b), const(p["loglb"]), const(p["log1mlb"])]
                 + [const(w) for w in weights]
                 + [pl.BlockSpec((tm, LANE), row), pl.BlockSpec((tm, LANE), row)]
                 + [const(a) for a in prep_consts],
        out_specs=[o[0] for o in outs],
        out_shape=[o[1] for o in outs],
        compiler_params=_cp(("arbitrary",)),
        name="inproj",
    )(x2, gain, sh, sc, p["loglb"], p["log1mlb"], *weights, cosf, sinf, *prep_consts)


def _segment_cumsum(x, seg):
    n = x.shape[0]
    r = lax.broadcasted_iota(jnp.int32, (n, n), 0)
    cc = lax.broadcasted_iota(jnp.int32, (n, n), 1)
    same = _same_seg(n, seg)
    tri = (same & (cc <= r)).astype(BF16)
    blk = same.astype(BF16)
    hi, lo = _split2(x)
    return _dot(tri, hi) + _dot(tri, lo), _dot(blk, hi) + _dot(blk, lo)


def _hgrn_chunk_path(i_ref, st_ref, c_s, tot_s, qs_s, kk_s, qe_s, kd_s, ke_s, dec_s, od_s):
    rows_blk = c_s.shape[0]
    tot = tot_s[...]
    rel = c_s[...] - 0.5 * tot
    half_dec = jnp.exp(0.5 * tot)
    kd = kk_s[...] * jnp.exp(-rel)
    qe_s[...] = (qs_s[...] * jnp.exp(rel)).astype(BF16)
    kd_s[...] = kd.astype(BF16)
    ke_s[...] = (kd * half_dec).astype(BF16)
    dec_s[...] = half_dec

    row = lax.broadcasted_iota(jnp.int32, (2 * rows_blk, rows_blk), 0) & (rows_blk - 1)
    col = lax.broadcasted_iota(jnp.int32, (2 * rows_blk, rows_blk), 1)
    intra = (_seg_id(row, HG_CHUNK) == _seg_id(col, HG_CHUNK)) & (col <= row)
    low_q = lax.broadcasted_iota(jnp.int32, (rows_blk, LANE), 1) < HG_DK
    low_c = lax.broadcasted_iota(jnp.int32, (HG_CHUNK, LANE), 1) < HG_DK

    for j in range(HG_W // LANE):
        cols = slice(LANE * j, LANE * (j + 1))
        qe = qe_s[:, cols]
        zero = jnp.zeros_like(qe)
        q2 = jnp.concatenate([jnp.where(low_q, qe, zero), jnp.where(low_q, zero, qe)], axis=0)
        attn = jnp.where(intra, _nt(q2, kd_s[:, cols]), 0.0).astype(BF16)
        o2 = _dot(attn, i_ref[:, cols])
        n_chunks = rows_blk // HG_CHUNK
        chunk_rows = [slice(HG_CHUNK * ch, HG_CHUNK * (ch + 1)) for ch in range(n_chunks)]
        upd = [_tn(i_ref[rows, cols], ke_s[rows, cols]) for rows in chunk_rows]
        st = st_ref[j]
        for ch, rows in enumerate(chunk_rows):
            rows_hi = slice(rows_blk + HG_CHUNK * ch, rows_blk + HG_CHUNK * (ch + 1))
            hd = dec_s[HG_CHUNK * ch:HG_CHUNK * ch + 1, cols]
            inter = _nt(jnp.concatenate([q2[rows], q2[rows_hi]], axis=0), (st * hd).astype(BF16))
            od_s[rows, cols] = jnp.where(low_c, o2[rows] + inter[:HG_CHUNK], o2[rows_hi] + inter[HG_CHUNK:])
            st = st * (hd * hd) + upd[ch]
        st_ref[j] = st


def _hgrn_exact_path(i_ref, st_ref, lf_s, c_s, qs_s, kk_s, qe_s, ke_s, dec_s, od_s, t_s, a_s):
    rows_blk = c_s.shape[0]
    n_sub = rows_blk // HG_SUB
    c, tot = _segment_cumsum(lf_s[...], HG_SUB)
    c_s[...] = c
    qe_s[...] = (qs_s[...] * jnp.exp(c)).astype(BF16)
    ke_s[...] = (kk_s[...] * jnp.exp(tot - c)).astype(BF16)
    dec_s[...] = jnp.exp(tot)

    same_head = _same_seg(LANE, HG_DK)
    head_mask = same_head.astype(F32)
    head_ones = same_head.astype(BF16)
    for j in range(HG_W // LANE):
        st_ref[j] = st_ref[j] * head_mask
    half = HG_SUB // 2
    trow = lax.broadcasted_iota(jnp.int32, (half, HG_W), 0)

    def body(i, carry):
        r0 = pl.multiple_of(i * HG_SUB, HG_SUB)
        rows = pl.ds(r0, HG_SUB)
        c_i = c_s[rows, :]
        qs_i = qs_s[rows, :]
        kk_i = kk_s[rows, :]
        v_i = i_ref[rows, :].astype(F32)
        c_lo, c_hi = c_i[:half], c_i[half:]
        q_lo, q_hi = qs_i[:half], qs_i[half:]
        for s in range(half):
            c_row, k_row = c_i[s:s + 1, :], kk_i[s:s + 1, :]
            e_lo = jnp.exp(jnp.where(trow >= s, c_lo - c_row, NEG))
            e_hi = jnp.exp(c_hi - c_row)
            both = jnp.concatenate([e_lo * q_lo, e_hi * q_hi], axis=0) * k_row
            t_s[s * HG_SUB:(s + 1) * HG_SUB, :] = both.astype(BF16)
        for s in range(half, HG_SUB, 2):
            pair = []
            for u in (s, s + 1):
                e_hi = jnp.exp(jnp.where(trow >= u - half, c_hi - c_i[u:u + 1, :], NEG))
                pair.append(e_hi * q_hi * kk_i[u:u + 1, :])
            base = half * HG_SUB + (s - half) * half
            t_s[base:base + HG_SUB, :] = jnp.concatenate(pair, axis=0).astype(BF16)
        for j in range(HG_W // LANE):
            cols = slice(LANE * j, LANE * (j + 1))
            a_s[:, cols] = _dot(t_s[:, cols], head_ones)
        acc_lo = jnp.zeros((half, HG_W), F32)
        acc_hi = jnp.zeros((half, HG_W), F32)
        for s in range(half):
            acc_lo = acc_lo + a_s[s * HG_SUB:s * HG_SUB + half, :] * v_i[s:s + 1, :]
            acc_hi = acc_hi + a_s[s * HG_SUB + half:(s + 1) * HG_SUB, :] * v_i[s:s + 1, :]
        for s in range(half, HG_SUB):
            base = half * HG_SUB + (s - half) * half
            acc_hi = acc_hi + a_s[base:base + half, :] * v_i[s:s + 1, :]
        acc = jnp.concatenate([acc_lo, acc_hi], axis=0)
        for j in range(HG_W // LANE):
            cols = slice(LANE * j, LANE * (j + 1))
            st = st_ref[j]
            o_int = _nt(qe_s[rows, cols], st.astype(BF16))
            upd = _tn(i_ref[rows, cols], ke_s[rows, cols])
            st_ref[j] = st * dec_s[pl.ds(r0, 1), cols] + upd * head_mask
            od_s[rows, cols] = acc[:, cols] + o_int
        return carry

    lax.fori_loop(0, n_sub, body, 0)


def _hgrn_kernel(q_ref, i_ref, g_ref, k_ref, lf_ref, og_ref, o_ref,
                 st_ref, lf_s, c_s, tot_s, qs_s, kk_s, qe_s, kd_s, ke_s, dec_s, od_s, t_s, a_s):
    @pl.when(pl.program_id(1) == 0)
    def _():
        st_ref[...] = jnp.zeros_like(st_ref)

    lf = lf_ref[...]
    lf_s[...] = lf
    qs_s[...] = q_ref[...].astype(F32)
    kk_s[...] = k_ref[...].astype(F32)
    c, tot = _segment_cumsum(lf, HG_CHUNK)
    c_s[...] = c
    tot_s[...] = tot
    safe = 0.5 * jnp.max(-tot) <= HG_SAFE_DECAY

    @pl.when(safe)
    def _():
        _hgrn_chunk_path(i_ref, st_ref, c_s, tot_s, qs_s, kk_s, qe_s, kd_s, ke_s, dec_s, od_s)

    @pl.when(jnp.logical_not(safe))
    def _():
        _hgrn_exact_path(i_ref, st_ref, lf_s, c_s, qs_s, kk_s, qe_s, ke_s, dec_s, od_s, t_s, a_s)

    o = od_s[...]
    ms = _seg_mean_sq(o, HG_DK)
    on = o * lax.rsqrt(ms + EPS) * og_ref[...]
    o_ref[...] = (on * _silu(g_ref[...].astype(F32))).astype(BF16)


def _hgrn(hg4, lf, ogain, batch, seq):
    t = lf.shape[0]
    rb = 256
    nb = seq // rb
    blk = lambda k: pl.BlockSpec((rb, HG_W), lambda b, n, k=k: (b * nb + n, k))
    vec = pl.BlockSpec((1, HG_W), lambda b, n: (0, 0))
    f32_blk = pltpu.VMEM((rb, HG_W), F32)
    bf16_blk = pltpu.VMEM((rb, HG_W), BF16)
    return pl.pallas_call(
        _hgrn_kernel,
        grid=(batch, nb),
        in_specs=[blk(0), blk(1), blk(2), blk(3), blk(0), vec],
        out_specs=blk(0),
        out_shape=jax.ShapeDtypeStruct((t, HG_W), BF16),
        scratch_shapes=[pltpu.VMEM((HG_W // LANE, LANE, LANE), F32),
                        f32_blk, f32_blk, f32_blk, f32_blk, f32_blk,
                        bf16_blk, bf16_blk, bf16_blk,
                        f32_blk, f32_blk,
                        pltpu.VMEM((HG_DIAG_ROWS, HG_W), BF16),
                        pltpu.VMEM((HG_DIAG_ROWS, HG_W), F32)],
        compiler_params=_cp(("arbitrary", "arbitrary")),
        name="hgrn2",
    )(hg4, hg4, hg4, hg4, lf, ogain)


def _store_transposed_blocks(out_ref, v):
    blk = out_ref.shape[2]
    for u in range(out_ref.shape[0]):
        out_ref[u] = v[u * blk:(u + 1) * blk, :].T.astype(BF16)


def _attn_prep(proj, cos_ref, sin_ref,
               qng_ref, kvg_ref, wqa_ref, wka_ref, wv_ref,
               gq_ref, gqs_ref, gk_ref, gks_ref, dgq_ref, dgk_ref, sgq_ref, sgk_ref,
               qm_ref, km_ref, vmt_ref, qd_ref, kd_ref, vdt_ref, qs_ref, ks_ref, vst_ref):
    wmla_ref, wdiff_ref, wswa_ref = proj.weights
    blk = proj(wmla_ref, 0, 512)
    cq = blk[:, :MLA_Q_RANK]
    rest = blk[:, MLA_Q_RANK:]
    cqn = cq * lax.rsqrt(jnp.mean(cq * cq, axis=-1, keepdims=True) + EPS) * qng_ref[...]
    lane = lax.broadcasted_iota(jnp.int32, rest.shape, 1)
    is_kv = lane < MLA_KV_RANK
    ms_kv = jnp.sum(jnp.where(is_kv, rest * rest, 0.0), axis=-1, keepdims=True) * (1.0 / MLA_KV_RANK)
    restn = jnp.where(is_kv, rest * lax.rsqrt(ms_kv + EPS) * kvg_ref[...], rest)
    cqb = cqn.astype(BF16)
    rb = restn.astype(BF16)
    qa = _dot(cqb, wqa_ref[...])
    ka = _dot(rb, wka_ref[...])
    _store_transposed_blocks(vmt_ref, _dot(rb, wv_ref[...]))
    first_half = lax.broadcasted_iota(jnp.int32, (blk.shape[0], LANE), 1) < MLA_NOPE + MLA_ROPE // 2

    def partner(x):
        return jnp.where(first_half, pltpu.roll(x, LANE - MLA_ROPE // 2, 1), pltpu.roll(x, MLA_ROPE // 2, 1))

    cosf = cos_ref[...]
    sinf = sin_ref[...]
    cq_t = cosf * gq_ref[...]
    sq_t = sinf * gqs_ref[...]
    ck_t = cosf * gk_ref[...]
    sk_t = sinf * gks_ref[...]
    inv_n = 1.0 / (MLA_NOPE + MLA_ROPE)
    scale = (MLA_NOPE + MLA_ROPE) ** -0.5 * LOG2E
    def seg_norm(x, gain, scale):
        return x * lax.rsqrt(_seg_mean_sq(x, HEAD_DIM) + EPS) * (gain * scale)

    def diff_q():
        qd_ref[...] = seg_norm(proj(wdiff_ref, 0, 512), dgq_ref[...], DIFF_QK ** -0.5 * LOG2E).astype(BF16)

    def diff_k():
        kd_ref[...] = seg_norm(proj(wdiff_ref, 512, 1024), dgk_ref[...], 1.0).astype(BF16)

    def diff_v():
        _store_transposed_blocks(vdt_ref, proj(wdiff_ref, 1024, 1536))

    def swa_q():
        qs_ref[...] = seg_norm(proj(wswa_ref, 0, 512), sgq_ref[...], HEAD_DIM ** -0.5 * LOG2E).astype(BF16)

    def swa_kv():
        skv = proj(wswa_ref, 512, 768)
        kn = seg_norm(skv[:, :LANE], sgk_ref[...], 1.0)
        low = lax.broadcasted_iota(jnp.int32, kn.shape, 1) < HEAD_DIM
        sw = pltpu.roll(kn, HEAD_DIM, 1)
        ks_ref[:, :LANE] = jnp.where(low, kn, sw).astype(BF16)
        ks_ref[:, LANE:] = jnp.where(low, sw, kn).astype(BF16)
        _store_transposed_blocks(vst_ref, skv[:, LANE:])

    pending = list(proj.interleave) + [diff_q, diff_k, diff_v, swa_q, swa_kv]
    for h in range(MLA_HEADS):
        cols = slice(LANE * h, LANE * (h + 1))
        x = qa[:, cols]
        rinv = lax.rsqrt(jnp.sum(x * x, axis=-1, keepdims=True) * inv_n + EPS)
        qm_ref[:, cols] = ((x * cq_t + partner(x) * sq_t) * (rinv * scale)).astype(BF16)
        y = ka[:, cols]
        rinv = lax.rsqrt(jnp.sum(y * y, axis=-1, keepdims=True) * inv_n + EPS)
        km_ref[:, cols] = ((y * ck_t + partner(y) * sk_t) * rinv).astype(BF16)
        if pending:
            pending.pop(0)()
    for work in pending:
        work()


def _causal_t(blk):
    key = lax.broadcasted_iota(jnp.int32, (blk, blk), 0)
    qry = lax.broadcasted_iota(jnp.int32, (blk, blk), 1)
    return key <= qry


def _two_pass_attention(n_sets, score_fn, value_fn, s_scr, acc_scr, blk):
    qi = pl.program_id(1)
    causal = _causal_t(blk)

    def scores(ki, m, masked):
        out = []
        for i in range(n_sets):
            s = score_fn(i, ki)
            if masked:
                s = jnp.where(causal, s, NEG)
            s_scr[i, ki] = s
            out.append(jnp.maximum(m[i], jnp.max(s, axis=0, keepdims=True)))
        return tuple(out)

    def blocked(n, step, carry):
        def many(k0, count, c):
            return step(tuple(k0 + u for u in range(count)), c)
        carry = lax.fori_loop(0, n // 4, lambda kp, c: many(4 * kp, 4, c), carry)
        done = (n // 4) * 4
        carry = lax.cond(n - done >= 2, lambda c: many(done, 2, c), lambda c: c, carry)
        done = (n // 2) * 2
        return lax.cond(n - done == 1, lambda c: many(done, 1, c), lambda c: c, carry)

    def scores_step(kis, m):
        for ki in kis:
            m = scores(ki, m, False)
        return m

    m = tuple(jnp.full((1, blk), NEG, F32) for _ in range(n_sets))
    m = blocked(qi, scores_step, m)
    m = scores(qi, m, True)

    acc_scr[...] = jnp.zeros_like(acc_scr)

    def accumulate(kis, l):
        out = []
        for i in range(n_sets):
            li, pv = l[i], None
            for ki in kis:
                p = jnp.exp2(s_scr[i, ki] - m[i])
                li = li + jnp.sum(p, axis=0, keepdims=True)
                term = _dot(value_fn(i, ki), p.astype(BF16))
                pv = term if pv is None else pv + term
            out.append(li)
            acc_scr[i] += pv
        return tuple(out)

    l = tuple(jnp.zeros((1, blk), F32) for _ in range(n_sets))
    return blocked(qi + 1, accumulate, l)


def _mla_attn_kernel(q_ref, k_ref, vt_ref, o_ref, s_scr, acc_scr):
    blk = q_ref.shape[0]

    def score_fn(h, ki):
        rows = pl.ds(pl.multiple_of(ki * blk, blk), blk)
        cols = slice(LANE * h, LANE * (h + 1))
        return _nt(k_ref[rows, cols], q_ref[:, cols])

    def value_fn(h, ki):
        return vt_ref[ki, MLA_V * h:MLA_V * (h + 1), :]

    l = _two_pass_attention(MLA_HEADS, score_fn, value_fn, s_scr, acc_scr, blk)
    for j in range(MLA_HEADS // 2):
        o_t = jnp.concatenate([acc_scr[2 * j] / l[2 * j], acc_scr[2 * j + 1] / l[2 * j + 1]], axis=0)
        o_ref[:, LANE * j:LANE * (j + 1)] = o_t.T.astype(BF16)


def _mla_attn(qm, km, vmt, batch, seq):
    t = qm.shape[0]
    nq = seq // ATT_BLK
    return pl.pallas_call(
        _mla_attn_kernel,
        grid=(batch, nq),
        in_specs=[pl.BlockSpec((ATT_BLK, 1024), lambda b, i: (b * nq + i, 0)),
                  pl.BlockSpec((seq, 1024), lambda b, i: (b, 0)),
                  pl.BlockSpec((nq, 512, ATT_BLK), lambda b, i: (b, 0, 0))],
        out_specs=pl.BlockSpec((ATT_BLK, 512), lambda b, i: (b * nq + i, 0)),
        out_shape=jax.ShapeDtypeStruct((t, 512), BF16),
        scratch_shapes=[pltpu.VMEM((MLA_HEADS, nq, ATT_BLK, ATT_BLK), F32),
                        pltpu.VMEM((MLA_HEADS, MLA_V, ATT_BLK), F32)],
        compiler_params=_cp(("arbitrary", "arbitrary")),
        name="mla_attn",
    )(qm, km, vmt)


def _diff_attn_kernel(q_ref, k_ref, vt_ref, lam_ref, og_ref, o_ref, s_scr, acc_scr, qm_scr, *, lam_init):
    blk = q_ref.shape[0]
    low = lax.broadcasted_iota(jnp.int32, (blk, LANE), 1) < DIFF_QK
    lp = lam_ref[...]
    lam = (jnp.exp(jnp.sum(lp[0:1] * lp[1:2], axis=-1, keepdims=True))
           - jnp.exp(jnp.sum(lp[2:3] * lp[3:4], axis=-1, keepdims=True)) + lam_init)

    for h in range(DIFF_HEADS):
        qt = q_ref[:, LANE * h:LANE * (h + 1)]
        zero = jnp.zeros_like(qt)
        qm_scr[2 * h] = jnp.where(low, qt, zero)
        qm_scr[2 * h + 1] = jnp.where(low, zero, qt)

    def score_fn(i, ki):
        rows = pl.ds(pl.multiple_of(ki * blk, blk), blk)
        h = i // 2
        return _nt(k_ref[rows, LANE * h:LANE * (h + 1)], qm_scr[i])

    def value_fn(i, ki):
        h = i // 2
        return vt_ref[ki, DIFF_V * h:DIFF_V * (h + 1), :]

    l = _two_pass_attention(2 * DIFF_HEADS, score_fn, value_fn, s_scr, acc_scr, blk)
    for h in range(DIFF_HEADS):
        o_t = acc_scr[2 * h] / l[2 * h] - lam * (acc_scr[2 * h + 1] / l[2 * h + 1])
        on_t = o_t * lax.rsqrt(jnp.mean(o_t * o_t, axis=0, keepdims=True) + EPS)
        o_ref[:, LANE * h:LANE * (h + 1)] = (on_t.T * (og_ref[...] * (1.0 - lam_init))).astype(BF16)


def _diff_attn(qd, kd, vdt, lam_p, og, lam_init, batch, seq):
    t = qd.shape[0]
    nq = seq // ATT_BLK
    return pl.pallas_call(
        functools.partial(_diff_attn_kernel, lam_init=lam_init),
        grid=(batch, nq),
        in_specs=[pl.BlockSpec((ATT_BLK, 512), lambda b, i: (b * nq + i, 0)),
                  pl.BlockSpec((seq, 512), lambda b, i: (b, 0)),
                  pl.BlockSpec((nq, 512, ATT_BLK), lambda b, i: (b, 0, 0)),
                  pl.BlockSpec(lam_p.shape, lambda b, i: (0, 0)),
                  pl.BlockSpec(og.shape, lambda b, i: (0, 0))],
        out_specs=pl.BlockSpec((ATT_BLK, 512), lambda b, i: (b * nq + i, 0)),
        out_shape=jax.ShapeDtypeStruct((t, 512), BF16),
        scratch_shapes=[pltpu.VMEM((2 * DIFF_HEADS, nq, ATT_BLK, ATT_BLK), F32),
                        pltpu.VMEM((2 * DIFF_HEADS, DIFF_V, ATT_BLK), F32),
                        pltpu.VMEM((2 * DIFF_HEADS, ATT_BLK, LANE), BF16)],
        compiler_params=_cp(("arbitrary", "arbitrary")),
        name="diff_attn",
    )(qd, kd, vdt, lam_p, og)


def _swa_kernel(q_ref, kp_ref, kc_ref, vtp_ref, vtc_ref, sink_ref, o_ref):
    w = SWA_WINDOW
    grp = SWA_Q_HEADS // SWA_KV_HEADS
    n = pl.program_id(1)
    key = lax.broadcasted_iota(jnp.int32, (2 * w, grp * w), 0)
    qry = lax.broadcasted_iota(jnp.int32, (2 * w, grp * w), 1) & (w - 1)
    cur_ok = (key >= w) & (key - w <= qry)
    prev_ok = (key < w) & (key > qry)
    low = lax.broadcasted_iota(jnp.int32, (w, LANE), 1) < HEAD_DIM

    for t in range(q_ref.shape[0] // w):
        rows = slice(t * w, (t + 1) * w)
        if t == 0:
            kp, vtp = kp_ref[...], vtp_ref[0]
            valid = cur_ok | (prev_ok & (n > 0))
        else:
            kp, vtp = kc_ref[(t - 1) * w:t * w, :], vtc_ref[t - 1]
            valid = cur_ok | prev_ok
        kc, vtc = kc_ref[rows, :], vtc_ref[t]
        for kv in range(SWA_KV_HEADS):
            kcols = slice(LANE * kv, LANE * (kv + 1))
            vrows = slice(HEAD_DIM * kv, HEAD_DIM * (kv + 1))
            k_win = jnp.concatenate([kp[:, kcols], kc[:, kcols]], axis=0)
            parts = []
            for u in range(2):
                qt = q_ref[rows, LANE * (2 * kv + u):LANE * (2 * kv + u + 1)]
                zero = jnp.zeros_like(qt)
                parts += [jnp.where(low, qt, zero), jnp.where(low, zero, qt)]
            s = jnp.where(valid, _nt(k_win, jnp.concatenate(parts, axis=0)), NEG)
            sink = sink_ref[:, grp * w * kv:grp * w * (kv + 1)]
            m = jnp.maximum(jnp.max(s, axis=0, keepdims=True), sink)
            p = jnp.exp2(s - m)
            den = jnp.sum(p, axis=0, keepdims=True) + jnp.exp2(sink - m)
            vt_win = jnp.concatenate([vtp[vrows, :], vtc[vrows, :]], axis=1)
            o_t = _dot(vt_win, p.astype(BF16)) / den
            for u in range(2):
                pair = jnp.concatenate([o_t[:, 2 * u * w:(2 * u + 1) * w],
                                        o_t[:, (2 * u + 1) * w:(2 * u + 2) * w]], axis=0)
                o_ref[rows, LANE * (2 * kv + u):LANE * (2 * kv + u + 1)] = pair.T.astype(BF16)


def _swa(qs, ks, vst, sink_row, batch, seq):
    t = qs.shape[0]
    w = SWA_WINDOW
    nb = seq // w
    ns = nb // SWA_QB
    cur = lambda b, n: (b * ns + n, 0)
    cur3 = lambda b, n: (b * ns + n, 0, 0)
    prev = lambda b, n: (b * nb + jnp.maximum(n * SWA_QB - 1, 0), 0)
    prev3 = lambda b, n: (b * nb + jnp.maximum(n * SWA_QB - 1, 0), 0, 0)
    return pl.pallas_call(
        _swa_kernel,
        grid=(batch, ns),
        in_specs=[pl.BlockSpec((SWA_QB * w, 512), cur),
                  pl.BlockSpec((w, 256), prev), pl.BlockSpec((SWA_QB * w, 256), cur),
                  pl.BlockSpec((1, LANE, w), prev3), pl.BlockSpec((SWA_QB, LANE, w), cur3),
                  pl.BlockSpec(sink_row.shape, lambda b, n: (0, 0))],
        out_specs=pl.BlockSpec((SWA_QB * w, 512), cur),
        out_shape=jax.ShapeDtypeStruct((t, 512), BF16),
        compiler_params=_cp(("arbitrary", "arbitrary")),
        name="swa_attn",
    )(qs, ks, ks, vst, vst, sink_row)


def _merge_kernel(*refs):
    n_sub = refs[0].shape[0] // MOE_TILE
    h2s = [_merge_mix(k, *refs) for k in range(n_sub)]
    for k in range(n_sub):
        _merge_route(k, h2s[k], *refs)


def _merge_mix(k, h_ref, ya_ref, yb_ref, yc_ref, yd_ref, x_ref, gt1_ref, wg_ref, wb_ref, wo_ref,
               g2_ref, sh2_ref, sc2_ref, rw_ref, rb_ref, xo_ref, h2_ref, comb_ref, dest_ref, meta_ref):
    rows = slice(MOE_TILE * k, MOE_TILE * (k + 1))
    h = h_ref[rows, :]
    d = x_ref.shape[1]
    merged = None
    for b, y_ref in enumerate((ya_ref, yb_ref, yc_ref, yd_ref)):
        gate = jax.nn.sigmoid(_dot(h, wg_ref[:, d * b:d * (b + 1)]))
        term = gate * _dot(y_ref[rows, :], wb_ref[b])
        merged = term if merged is None else merged + term
    xn = x_ref[rows, :] + gt1_ref[0] * _dot(merged.astype(BF16), wo_ref[...])
    xo_ref[rows, :] = xn
    ms = jnp.mean(xn * xn, axis=-1, keepdims=True)
    h2 = xn * lax.rsqrt(ms + EPS) * g2_ref[...]
    h2 = h2 * (1.0 + sc2_ref[0]) + sh2_ref[0]
    h2_ref[rows, :] = h2.astype(BF16)
    return h2


def _merge_route(k, h2, h_ref, ya_ref, yb_ref, yc_ref, yd_ref, x_ref, gt1_ref, wg_ref, wb_ref, wo_ref,
                 g2_ref, sh2_ref, sc2_ref, rw_ref, rb_ref, xo_ref, h2_ref, comb_ref, dest_ref, meta_ref):
    rows = slice(MOE_TILE * k, MOE_TILE * (k + 1))
    hh, hm = _split2(h2)
    wh = rw_ref[...].astype(BF16)
    logits = _dot(hh, wh) + _dot(hm, wh)
    lt = logits.T
    scores = jax.nn.sigmoid(lt[0:N_EXPERTS, :])
    sel = scores + rb_ref[...]
    per = N_EXPERTS // N_GROUPS
    srow = [sel[e:e + 1, :] for e in range(N_EXPERTS)]
    gsum = []
    for g in range(N_GROUPS):
        a, b_, c, e_ = srow[per * g:per * (g + 1)]
        gsum.append(jnp.maximum(jnp.maximum(jnp.maximum(a + b_, a + c), jnp.maximum(a + e_, b_ + c)),
                                jnp.maximum(b_ + e_, c + e_)))
    best = jnp.maximum(jnp.maximum(gsum[0], gsum[1]), jnp.maximum(gsum[2], gsum[3]))
    taken = None
    weights = []
    picks = []
    for g in range(N_GROUPS):
        hit = gsum[g] == best
        pick = hit if taken is None else hit & jnp.logical_not(taken)
        taken = hit if taken is None else taken | hit
        picks.append(pick.astype(F32))
        for e in range(per * g, per * (g + 1)):
            rank = jnp.zeros_like(best)
            for o in range(per * g, per * (g + 1)):
                if o == e:
                    continue
                ahead = (srow[o] > srow[e]) | ((srow[o] == srow[e]) & (o < e))
                rank = rank + ahead.astype(F32)
            weights.append(jnp.where(pick & (rank < 1.5), scores[e:e + 1, :], 0.0))
    wsum = weights[0]
    for r_ in weights[1:]:
        wsum = wsum + r_
    inv = 1.0 / wsum
    rid = lax.broadcasted_iota(jnp.int32, scores.shape, 0)
    comb_e = jnp.zeros_like(scores)
    for e, r_ in enumerate(weights):
        comb_e = jnp.where(rid == e, r_ * inv, comb_e)

    tm = lt.shape[1]
    gid = lax.broadcasted_iota(jnp.int32, (8, tm), 0)
    onehot = jnp.zeros((8, tm), F32)
    for g in range(N_GROUPS):
        onehot = jnp.where(gid == g, picks[g], onehot)
    before = (lax.broadcasted_iota(jnp.int32, (tm, tm), 0) < lax.broadcasted_iota(jnp.int32, (tm, tm), 1))
    rank = _dot(onehot.astype(BF16), before.astype(BF16))
    dest = jnp.zeros((1, tm), F32)
    off = jnp.zeros((1, 1), F32)
    meta = jnp.zeros((8, LANE), F32)
    mrow = lax.broadcasted_iota(jnp.int32, (8, LANE), 0)
    for g in range(N_GROUPS):
        cnt = jnp.sum(picks[g], axis=-1, keepdims=True)
        dest = dest + picks[g] * (off + rank[g:g + 1, :])
        meta = jnp.where(mrow == g, cnt, meta)
        meta = jnp.where(mrow == N_GROUPS + g, off, meta)
        off = off + jnp.ceil(cnt * (1.0 / MOE_ALIGN)) * MOE_ALIGN
    dest_ref[k] = dest.astype(jnp.int32)
    meta_ref[k] = meta.astype(jnp.int32)
    pad = jnp.zeros((LANE - N_EXPERTS - 8, tm), F32)
    comb_t = jnp.concatenate([comb_e, jnp.where(gid == 0, dest, 0.0), pad], axis=0)
    comb_ref[rows, :] = comb_t.T


def _merge(h, ys, x2, gt1, wg, wb, wo, g2, sh2, sc2, rw, rb, seq):
    t, d = x2.shape
    sub = MERGE_SUBTILES
    tm = sub * MOE_TILE
    tpb = seq // tm
    row = lambda i: (i, 0)
    per_b = lambda i: (i // tpb, 0, 0)
    c2 = lambda i: (0, 0)
    once = pl.Buffered(1)
    return pl.pallas_call(
        _merge_kernel,
        grid=(t // tm,),
        in_specs=[pl.BlockSpec((tm, d), row)] + [pl.BlockSpec((tm, 512), row)] * 4
                 + [pl.BlockSpec((tm, d), row), pl.BlockSpec((1, 1, d), per_b),
                    pl.BlockSpec(wg.shape, c2, pipeline_mode=once),
                    pl.BlockSpec(wb.shape, lambda i: (0, 0, 0), pipeline_mode=once),
                    pl.BlockSpec(wo.shape, c2, pipeline_mode=once), pl.BlockSpec((1, d), c2),
                    pl.BlockSpec((1, 1, d), per_b), pl.BlockSpec((1, 1, d), per_b),
                    pl.BlockSpec(rw.shape, c2), pl.BlockSpec(rb.shape, c2)],
        out_specs=[pl.BlockSpec((tm, d), row), pl.BlockSpec((tm, d), row), pl.BlockSpec((tm, LANE), row),
                   pl.BlockSpec((sub, 1, MOE_TILE), lambda i: (i, 0, 0)),
                   pl.BlockSpec((sub, 8, LANE), lambda i: (i, 0, 0))],
        out_shape=[jax.ShapeDtypeStruct((t, d), F32), jax.ShapeDtypeStruct((t, d), BF16),
                   jax.ShapeDtypeStruct((t, LANE), F32),
                   jax.ShapeDtypeStruct((t // MOE_TILE, 1, MOE_TILE), jnp.int32),
                   jax.ShapeDtypeStruct((t // MOE_TILE, 8, LANE), jnp.int32)],
        compiler_params=_cp(("arbitrary",)),
        name="merge_router",
    )(h, *ys, x2, gt1, wg, wb, wo, g2, sh2, sc2, rw, rb)


def _moe_kernel(meta_ref, h2_ref, comb_ref, dest_ref, x_ref, gt2_ref, wg_ref, wu_ref, wd_ref, o_ref,
                sorted_s, csort_s, out_s):
    i = pl.program_id(0)
    tm = h2_ref.shape[0]
    n_rows = sorted_s.shape[0]
    per = N_EXPERTS // N_GROUPS
    comb = comb_ref[...]

    place = (lax.broadcasted_iota(jnp.int32, (n_rows, tm), 0) == dest_ref[0]).astype(BF16)
    sorted_s[...] = _dot(place, h2_ref[...]).astype(BF16)
    c_hi, c_lo = _split2(comb)
    csort_s[...] = _dot(place, c_hi) + _dot(place, c_lo)
    out_s[...] = jnp.zeros_like(out_s)
    def run_experts(g, start, size):
        rows = pl.ds(pl.multiple_of(start, MOE_ALIGN), size)
        xk = sorted_s[rows, :]
        cw = csort_s[rows, :]
        lane = lax.broadcasted_iota(jnp.int32, (size, LANE), 1)
        acc = None
        for j in range(per):
            e = per * g + j
            hid = _silu(_dot(xk, wg_ref[0, e])) * _dot(xk, wu_ref[0, e])
            ce = jnp.sum(jnp.where(lane == e, cw, 0.0), axis=-1, keepdims=True)
            term = _dot((hid * ce).astype(BF16), wd_ref[0, e])
            acc = term if acc is None else acc + term
        out_s[rows, :] += acc

    def group(g, carry):
        cnt = meta_ref[i, g]
        off = meta_ref[i, N_GROUPS + g]
        big = MOE_CHUNKS[-1]
        n_big = cnt // big

        def big_chunk(c, inner):
            run_experts(g, off + c * big, big)
            return inner

        lax.fori_loop(0, n_big, big_chunk, 0)
        rest = cnt - n_big * big
        lower = 0
        for size in MOE_CHUNKS:
            @pl.when((rest > lower) & (rest <= size))
            def _(size=size):
                run_experts(g, off + n_big * big, size)
            lower = size
        return carry

    lax.fori_loop(0, N_GROUPS, group, 0)

    dest_col = comb[:, MOE_DEST_LANE:MOE_DEST_LANE + 1].astype(jnp.int32)
    back = (lax.broadcasted_iota(jnp.int32, (tm, n_rows), 1) == dest_col).astype(BF16)
    o_ref[...] = x_ref[...] + gt2_ref[0] * _dot(back, out_s[...].astype(BF16))


def _moe(h2, comb, dest, meta, x2, gt2, wg, wu, wd, layer, seq):
    t, d = x2.shape
    tm = MOE_TILE
    tpb = seq // tm
    n_rows = MOE_SORT_ROWS
    overhang = max(b - a for a, b in zip((0,) + MOE_CHUNKS, MOE_CHUNKS)) - 1
    assert tm + N_GROUPS * (MOE_ALIGN - 1) + overhang <= n_rows
    row = lambda i, m: (i, 0)
    whole = lambda a: pl.BlockSpec((1,) + a.shape[1:], lambda i, m: (layer, 0, 0, 0),
                                   pipeline_mode=pl.Buffered(1))
    grid_spec = pltpu.PrefetchScalarGridSpec(
        num_scalar_prefetch=1,
        grid=(t // tm,),
        in_specs=[pl.BlockSpec((tm, d), row), pl.BlockSpec((tm, LANE), row),
                  pl.BlockSpec((1, 1, tm), lambda i, m: (i, 0, 0)),
                  pl.BlockSpec((tm, d), row),
                  pl.BlockSpec((1, 1, d), lambda i, m: (i // tpb, 0, 0)),
                  whole(wg), whole(wu), whole(wd)],
        out_specs=pl.BlockSpec((tm, d), row),
        scratch_shapes=[pltpu.VMEM((n_rows, d), BF16), pltpu.VMEM((n_rows, LANE), F32),
                        pltpu.VMEM((n_rows, d), F32)])
    return pl.pallas_call(
        _moe_kernel,
        grid_spec=grid_spec,
        out_shape=jax.ShapeDtypeStruct((t, d), F32),
        compiler_params=_cp(("arbitrary",)),
        name="moe",
    )(meta, h2, comb, dest, x2, gt2, wg, wu, wd)


def _layer_params(l, w_in, hg_onorm, mla_q_norm, mla_kv_norm, mla_w_uq, mla_w_ukv, mla_qk_norm,
                  diff_qk_norm, swa_qk_norm, swa_sinks, lb_all):
    ends = [sum(IN_SPLITS[:i]) for i in range(len(IN_SPLITS) + 1)]
    cols = lambda a, b: w_in[l, :, ends[a]:ends[b]].astype(BF16)
    mla_pad = 512 - (ends[7] - ends[4])
    p = {"w_hg": cols(0, 4),
         "w_mla": jnp.pad(cols(4, 7), ((0, 0), (0, mla_pad))),
         "w_diff": cols(7, 10), "w_swa": cols(10, 13), "wg": cols(13, 14)}

    lb = lb_all[l]
    p["loglb"] = jnp.log(lb)[None, :]
    p["log1mlb"] = jnp.log1p(-lb)[None, :]
    p["ogain"] = jnp.tile(hg_onorm[l], HG_HEADS)[None, :]

    hd = MLA_NOPE + MLA_ROPE
    half = MLA_ROPE // 2
    wq = mla_w_uq[l].reshape(MLA_Q_RANK, MLA_HEADS, hd)
    z = lambda r, n: jnp.zeros((r, MLA_HEADS, n), F32)
    nope, rope = wq[:, :, :MLA_NOPE], wq[:, :, MLA_NOPE:]
    p["wqa"] = jnp.concatenate([nope, rope, z(MLA_Q_RANK, 32)], -1).reshape(MLA_Q_RANK, -1).astype(BF16)
    wkv = mla_w_ukv[l].reshape(MLA_KV_RANK, MLA_HEADS, MLA_NOPE + MLA_V)
    knope, vproj = wkv[:, :, :MLA_NOPE], wkv[:, :, MLA_NOPE:]
    eye = jnp.eye(MLA_ROPE, dtype=F32)
    place = lambda m: jnp.broadcast_to(
        jnp.concatenate([jnp.zeros((MLA_ROPE, MLA_NOPE), F32), m, jnp.zeros((MLA_ROPE, 32), F32)], -1)[:, None, :],
        (MLA_ROPE, MLA_HEADS, LANE))
    pad_rows = 256 - MLA_KV_RANK - MLA_ROPE
    p["wka"] = jnp.concatenate([jnp.concatenate([knope, z(MLA_KV_RANK, 64)], -1), place(eye),
                                z(pad_rows, LANE)], 0).reshape(256, -1).astype(BF16)
    p["wv"] = jnp.concatenate([vproj.reshape(MLA_KV_RANK, -1),
                               jnp.zeros((256 - MLA_KV_RANK, MLA_HEADS * MLA_V), F32)], 0).astype(BF16)
    p["qng"] = mla_q_norm[l][None, :]
    p["kvg"] = jnp.concatenate([mla_kv_norm[l], jnp.ones((256 - MLA_KV_RANK,), F32)])[None, :]

    def rope_gains(g):
        base = jnp.concatenate([g, jnp.zeros((LANE - hd,), F32)])
        part = jnp.concatenate([jnp.zeros((MLA_NOPE,), F32), g[MLA_NOPE + half:], g[MLA_NOPE:MLA_NOPE + half],
                                jnp.zeros((LANE - hd,), F32)])
        return base[None, :], part[None, :]

    p["gq"], p["gqs"] = rope_gains(mla_qk_norm[l, 0])
    p["gk"], p["gks"] = rope_gains(mla_qk_norm[l, 1])
    p["dgq"] = jnp.tile(diff_qk_norm[l, 0], 8)[None, :]
    p["dgk"] = jnp.tile(diff_qk_norm[l, 1], 8)[None, :]
    p["sgq"] = jnp.tile(swa_qk_norm[l, 0], 8)[None, :]
    p["sgk"] = jnp.tile(swa_qk_norm[l, 1], 2)[None, :]
    p["sinks"] = jnp.repeat(swa_sinks[l].astype(F32) * LOG2E, SWA_WINDOW)[None, :]
    return p


def _rope_tables(positions):
    inv_freq = ROPE_BASE ** (-jnp.arange(0, MLA_ROPE, 2, dtype=F32) / MLA_ROPE)
    zeros = lambda n: jnp.zeros((n,), F32)
    half = MLA_ROPE // 2
    pad = LANE - MLA_NOPE - MLA_ROPE
    freq = jnp.concatenate([zeros(MLA_NOPE), inv_freq, inv_freq, zeros(pad)])
    keep = jnp.concatenate([jnp.ones((MLA_NOPE + MLA_ROPE,), F32), zeros(pad)])
    sign = jnp.concatenate([zeros(MLA_NOPE), -jnp.ones((half,), F32), jnp.ones((half,), F32), zeros(pad)])
    ang = positions.astype(F32).reshape(-1)[:, None] * freq
    return jnp.cos(ang) * keep, jnp.sin(ang) * sign


def kernel(x, c, positions, ada_w, ada_b, norm_mix, norm_ffn, w_in, hg_lb_logits, hg_onorm, mla_q_norm, mla_kv_norm, mla_w_uq, mla_w_ukv, mla_qk_norm, diff_qk_norm, diff_lam, diff_onorm, swa_qk_norm, swa_sinks, w_branch, w_out, router_w, router_bias, moe_w_gate, moe_w_up, moe_w_down):
    batch, seq, d = x.shape
    x2 = x.reshape(batch * seq, d)
    cosf, sinf = _rope_tables(positions)
    lb_all = jnp.cumsum(jax.nn.softmax(hg_lb_logits.astype(F32), axis=0), axis=0)
    lb_all = lb_all - lb_all[0]
    mod = _modulation(c, ada_w, ada_b)
    rw = jnp.concatenate([router_w, jnp.zeros((d, LANE - N_EXPERTS), F32)], axis=1)
    rb = router_bias.astype(F32)[:, None]
    moe_w = (moe_w_gate.astype(BF16), moe_w_up.astype(BF16), moe_w_down.astype(BF16))

    for l in range(DEPTH):
        sh1, sc1, gt1, sh2, sc2, gt2 = [mod[l, :, d * k:d * (k + 1)][:, None, :] for k in range(6)]
        p = _layer_params(l, w_in, hg_onorm, mla_q_norm, mla_kv_norm, mla_w_uq, mla_w_ukv, mla_qk_norm,
                          diff_qk_norm, swa_qk_norm, swa_sinks, lb_all)
        hg4, lf, h, qm, km, vmt, qd, kd, vdt, qs, ks, vst = _inproj(
            x2, norm_mix[l][None, :], sh1, sc1, cosf, sinf, p, seq)
        y_a = _hgrn(hg4, lf, p["ogain"], batch, seq)
        y_b = _mla_attn(qm, km, vmt, batch, seq)
        lam_init = 0.8 - 0.6 * math.exp(-0.3 * l)
        y_c = _diff_attn(qd, kd, vdt, diff_lam[l], diff_onorm[l][None, :], lam_init, batch, seq)
        y_d = _swa(qs, ks, vst, p["sinks"], batch, seq)
        x2, h2, comb, dest, meta = _merge(h, (y_a, y_b, y_c, y_d), x2, gt1, p["wg"], w_branch[l].astype(BF16),
                                          w_out[l].astype(BF16), norm_ffn[l][None, :], sh2, sc2, rw, rb, seq)
        x2 = _moe(h2, comb, dest, meta[:, :, 0], x2, gt2, *moe_w, l, seq)
    return x2.reshape(batch, seq, d)
```

```python
import functools
import math

import jax
import jax.numpy as jnp
from jax import lax
from jax.experimental import pallas as pl
from jax.experimental.pallas import tpu as pltpu

F32 = jnp.float32
BF16 = jnp.bfloat16

D_MODEL = 1024
DEPTH = 2
EPS = 1e-6
N_BRANCH = 4
HG_HEADS = 8
HG_DK = 64
HG_W = HG_HEADS * HG_DK
HG_SUB = 16
HG_CHUNK = 64
HG_SAFE_DECAY = 80.0
HG_DIAG_ROWS =(HG_SUB // 2) * HG_SUB + (HG_SUB // 2) ** 2
MLA_HEADS = 8
MLA_Q_RANK = 256
MLA_KV_RANK = 128
MLA_NOPE = 64
MLA_ROPE = 32
MLA_V = 64
ROPE_BASE = 10000.0
DIFF_HEADS = 4
DIFF_QK = 64
DIFF_V = 128
SWA_Q_HEADS = 8
SWA_KV_HEADS = 2
SWA_WINDOW = 128
HEAD_DIM = 64
N_EXPERTS = 16
N_GROUPS = 4
D_FF_EXPERT = 256
IN_SPLITS = (512, 512, 512, 512, 256, 128, 32, 512, 512, 512, 512, 128, 128, 4096)

MOD_COLS = 1536
ROW_TILE = 512
HG_BLOCK = 256
MOE_TILE = 512
MERGE_SUBTILES = 2
MOE_ALIGN = 16
MOE_CHUNKS = (64, 128, 192, 256)
MOE_SORT_ROWS = 640
MOE_DEST_LANE = N_EXPERTS
LANE = 128
ATT_BLK = 256
SWA_QB = 8
LOG2E = 1.4426950408889634
NEG = -1e30
VMEM_LIMIT = 56 * 1024 * 1024


def _cp(sem, vmem=VMEM_LIMIT):
    return pltpu.CompilerParams(dimension_semantics=sem, vmem_limit_bytes=vmem)


def _nt(a, b):
    return lax.dot_general(a, b, (((1,), (1,)), ((), ())), preferred_element_type=F32)


def _tn(a, b):
    return lax.dot_general(a, b, (((0,), (0,)), ((), ())), preferred_element_type=F32)


def _dot(a, b):
    return jnp.dot(a, b, preferred_element_type=F32)


def _split2(x):
    hi = x.astype(BF16)
    lo = (x - hi.astype(F32)).astype(BF16)
    return hi, lo


def _seg_id(idx, seg):
    shift = seg.bit_length() - 1
    assert 1 << shift == seg
    return lax.shift_right_logical(idx, shift)


def _same_seg(n, seg):
    r = lax.broadcasted_iota(jnp.int32, (n, n), 0)
    c = lax.broadcasted_iota(jnp.int32, (n, n), 1)
    return _seg_id(r, seg) == _seg_id(c, seg)


def _seg_ones(n, seg):
    return _same_seg(n, seg).astype(BF16)


def _seg_mean_sq(x, seg):
    n = x.shape[-1]
    return _dot((x * x).astype(BF16), _seg_ones(n, seg)) * (1.0 / seg)


def _silu(x):
    return x * jax.nn.sigmoid(x)


def _mod_kernel(c_ref, w_ref, b_ref, o_ref):
    hi, lo = _split2(_silu(c_ref[...]))
    w = w_ref[0].astype(BF16)
    o_ref[0] = _dot(hi, w) + _dot(lo, w) + b_ref[0]


def _modulation(c, ada_w, ada_b):
    nl, d, n6 = ada_w.shape
    b = c.shape[0]
    tn = MOD_COLS
    return pl.pallas_call(
        _mod_kernel,
        grid=(nl, n6 // tn),
        in_specs=[pl.BlockSpec((b, d), lambda l, j: (0, 0)),
                  pl.BlockSpec((1, d, tn), lambda l, j: (l, 0, j)),
                  pl.BlockSpec((1, 1, tn), lambda l, j: (l, 0, j))],
        out_specs=pl.BlockSpec((1, b, tn), lambda l, j: (l, 0, j)),
        out_shape=jax.ShapeDtypeStruct((nl, b, n6), F32),
        compiler_params=_cp(("arbitrary", "arbitrary")),
        name="modulation",
    )(c, ada_w, ada_b.reshape(nl, 1, n6))


N_PREP_CONSTS = 13


def _inproj_kernel(x_ref, g_ref, sh_ref, sc_ref, loglb_ref, log1mlb_ref, whg_ref, wmla_ref, wdiff_ref, wswa_ref,
                   cos_ref, sin_ref, *rest):
    prep_consts, (ohg_ref, olf_ref, oh_ref), prep_outs = (
        rest[:N_PREP_CONSTS], rest[N_PREP_CONSTS:N_PREP_CONSTS + 3], rest[N_PREP_CONSTS + 3:])
    x = x_ref[...]
    ms = jnp.mean(x * x, axis=-1, keepdims=True)
    h = x * lax.rsqrt(ms + EPS) * g_ref[...]
    h = h * (1.0 + sc_ref[0]) + sh_ref[0]
    hb = h.astype(BF16)
    oh_ref[...] = hb

    def proj(w_ref, lo, hi):
        return _dot(hb, w_ref[:, lo:hi])

    def hg_forget():
        fr = proj(whg_ref, 512, 1024)
        ls = jnp.minimum(fr, 0.0) - jnp.log(1.0 + jnp.exp(-jnp.abs(fr)))
        a = loglb_ref[...]
        c2 = log1mlb_ref[...] + ls
        lf = jnp.maximum(a, c2) + jnp.log(1.0 + jnp.exp(-jnp.abs(a - c2)))
        olf_ref[...] = lf
        ohg_ref[:, 1536:2048] = (1.0 - jnp.exp(lf)).astype(BF16)

    def hg_query():
        ohg_ref[:, 0:512] = _silu(proj(whg_ref, 0, 512)).astype(BF16)

    def hg_input():
        ohg_ref[:, 512:1024] = proj(whg_ref, 1024, 1536).astype(BF16)

    def hg_gate():
        ohg_ref[:, 1024:1536] = proj(whg_ref, 1536, 2048).astype(BF16)

    proj.weights = (wmla_ref, wdiff_ref, wswa_ref)
    proj.interleave = (hg_forget, hg_query, hg_input, hg_gate)
    _attn_prep(proj, cos_ref, sin_ref, *prep_consts, *prep_outs)


def _inproj(x2, gain, sh, sc, cosf, sinf, p, seq):
    t, d = x2.shape
    tm = ROW_TILE
    tpb = seq // tm
    row = lambda i: (i, 0)
    per_b = lambda i: (i // tpb, 0, 0)
    const = lambda a: pl.BlockSpec(a.shape, lambda i: (0,) * a.ndim)
    weights = [p["w_hg"], p["w_mla"], p["w_diff"], p["w_swa"]]
    prep_consts = [p["qng"], p["kvg"], p["wqa"], p["wka"], p["wv"],
                   p["gq"], p["gqs"], p["gk"], p["gks"], p["dgq"], p["dgk"], p["sgq"], p["sgk"]]
    assert len(prep_consts) == N_PREP_CONSTS

    def rows_out(w, dt=BF16):
        return pl.BlockSpec((tm, w), row), jax.ShapeDtypeStruct((t, w), dt)

    def transposed_out(n, blk):
        return (pl.BlockSpec((tm // blk, n, blk), lambda i: (i, 0, 0)),
                jax.ShapeDtypeStruct((t // blk, n, blk), BF16))

    outs = [rows_out(2048), rows_out(512, F32), rows_out(d),
            rows_out(1024), rows_out(1024), transposed_out(512, ATT_BLK),
            rows_out(512), rows_out(512), transposed_out(512, ATT_BLK),
            rows_out(512), rows_out(256), transposed_out(LANE, SWA_WINDOW)]
    return pl.pallas_call(
        _inproj_kernel,
        grid=(t // tm,),
        in_specs=[pl.BlockSpec((tm, d), row), const(gain),
                  pl.BlockSpec((1, 1, d), per_b),
                  pl.BlockSpec((1, 1, d), per_b), const(p["loglb"]), const(p["log1mlb"])]
                 + [const(w) for w in weights]
                 + [pl.BlockSpec((tm, LANE), row), pl.BlockSpec((tm, LANE), row)]
                 + [const(a) for a in prep_consts],
        out_specs=[o[0] for o in outs],
        out_shape=[o[1] for o in outs],
        compiler_params=_cp(("arbitrary",)),
        name="inproj",
    )(x2, gain, sh, sc, p["loglb"], p["log1mlb"], *weights, cosf, sinf, *prep_consts)


def _segment_cumsum(x, seg):
    n = x.shape[0]
    r = lax.broadcasted_iota(jnp.int32, (n, n), 0)
    cc = lax.broadcasted_iota(jnp.int32, (n, n), 1)
    same = _same_seg(n, seg)
    tri = (same & (cc <= r)).astype(BF16)
    blk = same.astype(BF16)
    hi, lo = _split2(x)
    return _dot(tri, hi) + _dot(tri, lo), _dot(blk, hi) + _dot(blk, lo)


def _hgrn_chunk_path(i_ref, st_ref, c_s, tot_s, qs_s, kk_s, qe_s, kd_s, ke_s, dec_s, od_s):
    rows_blk = c_s.shape[0]
    tot = tot_s[...]
    rel = c_s[...] - 0.5 * tot
    half_dec = jnp.exp(0.5 * tot)
    kd = kk_s[...] * jnp.exp(-rel)
    qe_s[...] = (qs_s[...] * jnp.exp(rel)).astype(BF16)
    kd_s[...] = kd.astype(BF16)
    ke_s[...] = (kd * half_dec).astype(BF16)
    dec_s[...] = half_dec

    row = lax.broadcasted_iota(jnp.int32, (2 * rows_blk, rows_blk), 0) & (rows_blk - 1)
    col = lax.broadcasted_iota(jnp.int32, (2 * rows_blk, rows_blk), 1)
    intra = (_seg_id(row, HG_CHUNK) == _seg_id(col, HG_CHUNK)) & (col <= row)
    low_q = lax.broadcasted_iota(jnp.int32, (rows_blk, LANE), 1) < HG_DK
    low_c = lax.broadcasted_iota(jnp.int32, (HG_CHUNK, LANE), 1) < HG_DK

    for j in range(HG_W // LANE):
        cols = slice(LANE * j, LANE * (j + 1))
        qe = qe_s[:, cols]
        zero = jnp.zeros_like(qe)
        q2 = jnp.concatenate([jnp.where(low_q, qe, zero), jnp.where(low_q, zero, qe)], axis=0)
        attn = jnp.where(intra, _nt(q2, kd_s[:, cols]), 0.0).astype(BF16)
        o2 = _dot(attn, i_ref[:, cols])
        n_chunks = rows_blk // HG_CHUNK
        chunk_rows = [slice(HG_CHUNK * ch, HG_CHUNK * (ch + 1)) for ch in range(n_chunks)]
        upd = [_tn(i_ref[rows, cols], ke_s[rows, cols]) for rows in chunk_rows]
        st = st_ref[j]
        for ch, rows in enumerate(chunk_rows):
            rows_hi = slice(rows_blk + HG_CHUNK * ch, rows_blk + HG_CHUNK * (ch + 1))
            hd = dec_s[HG_CHUNK * ch:HG_CHUNK * ch + 1, cols]
            inter = _nt(jnp.concatenate([q2[rows], q2[rows_hi]], axis=0), (st * hd).astype(BF16))
            od_s[rows, cols] = jnp.where(low_c, o2[rows] + inter[:HG_CHUNK], o2[rows_hi] + inter[HG_CHUNK:])
            st = st * (hd * hd) + upd[ch]
        st_ref[j] = st


def _hgrn_exact_path(i_ref, st_ref, lf_s, c_s, qs_s, kk_s, qe_s, ke_s, dec_s, od_s, t_s, a_s):
    rows_blk = c_s.shape[0]
    n_sub = rows_blk // HG_SUB
    c, tot = _segment_cumsum(lf_s[...], HG_SUB)
    c_s[...] = c
    qe_s[...] = (qs_s[...] * jnp.exp(c)).astype(BF16)
    ke_s[...] = (kk_s[...] * jnp.exp(tot - c)).astype(BF16)
    dec_s[...] = jnp.exp(tot)

    same_head = _same_seg(LANE, HG_DK)
    head_mask = same_head.astype(F32)
    head_ones = same_head.astype(BF16)
    for j in range(HG_W // LANE):
        st_ref[j] = st_ref[j] * head_mask
    half = HG_SUB // 2
    trow = lax.broadcasted_iota(jnp.int32, (half, HG_W), 0)

    def body(i, carry):
        r0 = pl.multiple_of(i * HG_SUB, HG_SUB)
        rows = pl.ds(r0, HG_SUB)
        c_i = c_s[rows, :]
        qs_i = qs_s[rows, :]
        kk_i = kk_s[rows, :]
        v_i = i_ref[rows, :].astype(F32)
        c_lo, c_hi = c_i[:half], c_i[half:]
        q_lo, q_hi = qs_i[:half], qs_i[half:]
        for s in range(half):
            c_row, k_row = c_i[s:s + 1, :], kk_i[s:s + 1, :]
            e_lo = jnp.exp(jnp.where(trow >= s, c_lo - c_row, NEG))
            e_hi = jnp.exp(c_hi - c_row)
            both = jnp.concatenate([e_lo * q_lo, e_hi * q_hi], axis=0) * k_row
            t_s[s * HG_SUB:(s + 1) * HG_SUB, :] = both.astype(BF16)
        for s in range(half, HG_SUB, 2):
            pair = []
            for u in (s, s + 1):
                e_hi = jnp.exp(jnp.where(trow >= u - half, c_hi - c_i[u:u + 1, :], NEG))
                pair.append(e_hi * q_hi * kk_i[u:u + 1, :])
            base = half * HG_SUB + (s - half) * half
            t_s[base:base + HG_SUB, :] = jnp.concatenate(pair, axis=0).astype(BF16)
        for j in range(HG_W // LANE):
            cols = slice(LANE * j, LANE * (j + 1))
            a_s[:, cols] = _dot(t_s[:, cols], head_ones)
        acc_lo = jnp.zeros((half, HG_W), F32)
        acc_hi = jnp.zeros((half, HG_W), F32)
        for s in range(half):
            acc_lo = acc_lo + a_s[s * HG_SUB:s * HG_SUB + half, :] * v_i[s:s + 1, :]
            acc_hi = acc_hi + a_s[s * HG_SUB + half:(s + 1) * HG_SUB, :] * v_i[s:s + 1, :]
        for s in range(half, HG_SUB):
            base = half * HG_SUB + (s - half) * half
            acc_hi = acc_hi + a_s[base:base + half, :] * v_i[s:s + 1, :]
        acc = jnp.concatenate([acc_lo, acc_hi], axis=0)
        for j in range(HG_W // LANE):
            cols = slice(LANE * j, LANE * (j + 1))
            st = st_ref[j]
            o_int = _nt(qe_s[rows, cols], st.astype(BF16))
            upd = _tn(i_ref[rows, cols], ke_s[rows, cols])
            st_ref[j] = st * dec_s[pl.ds(r0, 1), cols] + upd * head_mask
            od_s[rows, cols] = acc[:, cols] + o_int
        return carry

    lax.fori_loop(0, n_sub, body, 0)


def _hgrn_kernel(q_ref, i_ref, g_ref, k_ref, lf_ref, og_ref, o_ref,
                 st_ref, lf_s, c_s, tot_s, qs_s, kk_s, qe_s, kd_s, ke_s, dec_s, od_s, t_s, a_s):
    @pl.when(pl.program_id(1) == 0)
    def _():
        st_ref[...] = jnp.zeros_like(st_ref)

    lf = lf_ref[...]
    lf_s[...] = lf
    qs_s[...] = q_ref[...].astype(F32)
    kk_s[...] = k_ref[...].astype(F32)
    c, tot = _segment_cumsum(lf, HG_CHUNK)
    c_s[...] = c
    tot_s[...] = tot
    safe = 0.5 * jnp.max(-tot) <= HG_SAFE_DECAY

    @pl.when(safe)
    def _():
        _hgrn_chunk_path(i_ref, st_ref, c_s, tot_s, qs_s, kk_s, qe_s, kd_s, ke_s, dec_s, od_s)

    @pl.when(jnp.logical_not(safe))
    def _():
        _hgrn_exact_path(i_ref, st_ref, lf_s, c_s, qs_s, kk_s, qe_s, ke_s, dec_s, od_s, t_s, a_s)

    o = od_s[...]
    ms = _seg_mean_sq(o, HG_DK)
    on = o * lax.rsqrt(ms + EPS) * og_ref[...]
    o_ref[...] = (on * _silu(g_ref[...].astype(F32))).astype(BF16)


def _hgrn(hg4, lf, ogain, batch, seq):
    t = lf.shape[0]
    rb = HG_BLOCK
    nb = seq // rb
    blk = lambda k: pl.BlockSpec((rb, HG_W), lambda b, n, k=k: (b * nb + n, k))
    vec = pl.BlockSpec((1, HG_W), lambda b, n: (0, 0))
    f32_blk = pltpu.VMEM((rb, HG_W), F32)
    bf16_blk = pltpu.VMEM((rb, HG_W), BF16)
    return pl.pallas_call(
        _hgrn_kernel,
        grid=(batch, nb),
        in_specs=[blk(0), blk(1), blk(2), blk(3), blk(0), vec],
        out_specs=blk(0),
        out_shape=jax.ShapeDtypeStruct((t, HG_W), BF16),
        scratch_shapes=[pltpu.VMEM((HG_W // LANE, LANE, LANE), F32),
                        f32_blk, f32_blk, f32_blk, f32_blk, f32_blk,
                        bf16_blk, bf16_blk, bf16_blk,
                        f32_blk, f32_blk,
                        pltpu.VMEM((HG_DIAG_ROWS, HG_W), BF16),
                        pltpu.VMEM((HG_DIAG_ROWS, HG_W), F32)],
        compiler_params=_cp(("arbitrary", "arbitrary")),
        name="hgrn2",
    )(hg4, hg4, hg4, hg4, lf, ogain)


def _store_transposed_blocks(out_ref, v):
    blk = out_ref.shape[2]
    for u in range(out_ref.shape[0]):
        out_ref[u] = v[u * blk:(u + 1) * blk, :].T.astype(BF16)


def _attn_prep(proj, cos_ref, sin_ref,
               qng_ref, kvg_ref, wqa_ref, wka_ref, wv_ref,
               gq_ref, gqs_ref, gk_ref, gks_ref, dgq_ref, dgk_ref, sgq_ref, sgk_ref,
               qm_ref, km_ref, vmt_ref, qd_ref, kd_ref, vdt_ref, qs_ref, ks_ref, vst_ref):
    wmla_ref, wdiff_ref, wswa_ref = proj.weights
    blk = proj(wmla_ref, 0, 512)
    cq = blk[:, :MLA_Q_RANK]
    rest = blk[:, MLA_Q_RANK:]
    cqn = cq * lax.rsqrt(jnp.mean(cq * cq, axis=-1, keepdims=True) + EPS) * qng_ref[...]
    lane = lax.broadcasted_iota(jnp.int32, rest.shape, 1)
    is_kv = lane < MLA_KV_RANK
    ms_kv = jnp.sum(jnp.where(is_kv, rest * rest, 0.0), axis=-1, keepdims=True) * (1.0 / MLA_KV_RANK)
    restn = jnp.where(is_kv, rest * lax.rsqrt(ms_kv + EPS) * kvg_ref[...], rest)
    cqb = cqn.astype(BF16)
    rb = restn.astype(BF16)
    qa = _dot(cqb, wqa_ref[...])
    ka = _dot(rb, wka_ref[...])
    _store_transposed_blocks(vmt_ref, _dot(rb, wv_ref[...]))
    first_half = lax.broadcasted_iota(jnp.int32, (blk.shape[0], LANE), 1) < MLA_NOPE + MLA_ROPE // 2

    def partner(x):
        return jnp.where(first_half, pltpu.roll(x, LANE - MLA_ROPE // 2, 1), pltpu.roll(x, MLA_ROPE // 2, 1))

    cosf = cos_ref[...]
    sinf = sin_ref[...]
    cq_t = cosf * gq_ref[...]
    sq_t = sinf * gqs_ref[...]
    ck_t = cosf * gk_ref[...]
    sk_t = sinf * gks_ref[...]
    inv_n = 1.0 / (MLA_NOPE + MLA_ROPE)
    scale = (MLA_NOPE + MLA_ROPE) ** -0.5 * LOG2E
    def seg_norm(x, gain, scale):
        return x * lax.rsqrt(_seg_mean_sq(x, HEAD_DIM) + EPS) * (gain * scale)

    def diff_q():
        qd_ref[...] = seg_norm(proj(wdiff_ref, 0, 512), dgq_ref[...], DIFF_QK ** -0.5 * LOG2E).astype(BF16)

    def diff_k():
        kd_ref[...] = seg_norm(proj(wdiff_ref, 512, 1024), dgk_ref[...], 1.0).astype(BF16)

    def diff_v():
        _store_transposed_blocks(vdt_ref, proj(wdiff_ref, 1024, 1536))

    def swa_q():
        qs_ref[...] = seg_norm(proj(wswa_ref, 0, 512), sgq_ref[...], HEAD_DIM ** -0.5 * LOG2E).astype(BF16)

    def swa_kv():
        skv = proj(wswa_ref, 512, 768)
        kn = seg_norm(skv[:, :LANE], sgk_ref[...], 1.0)
        low = lax.broadcasted_iota(jnp.int32, kn.shape, 1) < HEAD_DIM
        sw = pltpu.roll(kn, HEAD_DIM, 1)
        ks_ref[:, :LANE] = jnp.where(low, kn, sw).astype(BF16)
        ks_ref[:, LANE:] = jnp.where(low, sw, kn).astype(BF16)
        _store_transposed_blocks(vst_ref, skv[:, LANE:])

    pending = list(proj.interleave) + [diff_q, diff_k, diff_v, swa_q, swa_kv]
    for h in range(MLA_HEADS):
        cols = slice(LANE * h, LANE * (h + 1))
        x = qa[:, cols]
        rinv = lax.rsqrt(jnp.sum(x * x, axis=-1, keepdims=True) * inv_n + EPS)
        qm_ref[:, cols] = ((x * cq_t + partner(x) * sq_t) * (rinv * scale)).astype(BF16)
        y = ka[:, cols]
        rinv = lax.rsqrt(jnp.sum(y * y, axis=-1, keepdims=True) * inv_n + EPS)
        km_ref[:, cols] = ((y * ck_t + partner(y) * sk_t) * rinv).astype(BF16)
        if pending:
            pending.pop(0)()
    for work in pending:
        work()


def _causal_t(blk):
    key = lax.broadcasted_iota(jnp.int32, (blk, blk), 0)
    qry = lax.broadcasted_iota(jnp.int32, (blk, blk), 1)
    return key <= qry


def _two_pass_attention(n_sets, score_fn, value_fn, s_scr, acc_scr, blk):
    qi = pl.program_id(1)
    causal = _causal_t(blk)

    def scores(ki, m, masked):
        out = []
        for i in range(n_sets):
            s = score_fn(i, ki)
            if masked:
                s = jnp.where(causal, s, NEG)
            s_scr[i, ki] = s
            out.append(jnp.maximum(m[i], jnp.max(s, axis=0, keepdims=True)))
        return tuple(out)

    def blocked(n, step, carry):
        def many(k0, count, c):
            return step(tuple(k0 + u for u in range(count)), c)
        carry = lax.fori_loop(0, n // 4, lambda kp, c: many(4 * kp, 4, c), carry)
        done = (n // 4) * 4
        carry = lax.cond(n - done >= 2, lambda c: many(done, 2, c), lambda c: c, carry)
        done = (n // 2) * 2
        return lax.cond(n - done == 1, lambda c: many(done, 1, c), lambda c: c, carry)

    def scores_step(kis, m):
        for ki in kis:
            m = scores(ki, m, False)
        return m

    m = tuple(jnp.full((1, blk), NEG, F32) for _ in range(n_sets))
    m = blocked(qi, scores_step, m)
    m = scores(qi, m, True)

    acc_scr[...] = jnp.zeros_like(acc_scr)

    def accumulate(kis, l):
        out = []
        for i in range(n_sets):
            li, pv = l[i], None
            for ki in kis:
                p = jnp.exp2(s_scr[i, ki] - m[i])
                li = li + jnp.sum(p, axis=0, keepdims=True)
                term = _dot(value_fn(i, ki), p.astype(BF16))
                pv = term if pv is None else pv + term
            out.append(li)
            acc_scr[i] += pv
        return tuple(out)

    l = tuple(jnp.zeros((1, blk), F32) for _ in range(n_sets))
    return blocked(qi + 1, accumulate, l)


def _mla_attn_kernel(q_ref, k_ref, vt_ref, o_ref, s_scr, acc_scr):
    blk = q_ref.shape[0]

    def score_fn(h, ki):
        rows = pl.ds(pl.multiple_of(ki * blk, blk), blk)
        cols = slice(LANE * h, LANE * (h + 1))
        return _nt(k_ref[rows, cols], q_ref[:, cols])

    def value_fn(h, ki):
        return vt_ref[ki, MLA_V * h:MLA_V * (h + 1), :]

    l = _two_pass_attention(MLA_HEADS, score_fn, value_fn, s_scr, acc_scr, blk)
    for j in range(MLA_HEADS // 2):
        o_t = jnp.concatenate([acc_scr[2 * j] / l[2 * j], acc_scr[2 * j + 1] / l[2 * j + 1]], axis=0)
        o_ref[:, LANE * j:LANE * (j + 1)] = o_t.T.astype(BF16)


def _mla_attn(qm, km, vmt, batch, seq):
    t = qm.shape[0]
    nq = seq // ATT_BLK
    return pl.pallas_call(
        _mla_attn_kernel,
        grid=(batch, nq),
        in_specs=[pl.BlockSpec((ATT_BLK, 1024), lambda b, i: (b * nq + i, 0)),
                  pl.BlockSpec((seq, 1024), lambda b, i: (b, 0)),
                  pl.BlockSpec((nq, 512, ATT_BLK), lambda b, i: (b, 0, 0))],
        out_specs=pl.BlockSpec((ATT_BLK, 512), lambda b, i: (b * nq + i, 0)),
        out_shape=jax.ShapeDtypeStruct((t, 512), BF16),
        scratch_shapes=[pltpu.VMEM((MLA_HEADS, nq, ATT_BLK, ATT_BLK), F32),
                        pltpu.VMEM((MLA_HEADS, MLA_V, ATT_BLK), F32)],
        compiler_params=_cp(("arbitrary", "arbitrary")),
        name="mla_attn",
    )(qm, km, vmt)


def _diff_attn_kernel(q_ref, k_ref, vt_ref, lam_ref, og_ref, o_ref, s_scr, acc_scr, qm_scr, *, lam_init):
    blk = q_ref.shape[0]
    low = lax.broadcasted_iota(jnp.int32, (blk, LANE), 1) < DIFF_QK
    lp = lam_ref[...]
    lam = (jnp.exp(jnp.sum(lp[0:1] * lp[1:2], axis=-1, keepdims=True))
           - jnp.exp(jnp.sum(lp[2:3] * lp[3:4], axis=-1, keepdims=True)) + lam_init)

    for h in range(DIFF_HEADS):
        qt = q_ref[:, LANE * h:LANE * (h + 1)]
        zero = jnp.zeros_like(qt)
        qm_scr[2 * h] = jnp.where(low, qt, zero)
        qm_scr[2 * h + 1] = jnp.where(low, zero, qt)

    def score_fn(i, ki):
        rows = pl.ds(pl.multiple_of(ki * blk, blk), blk)
        h = i // 2
        return _nt(k_ref[rows, LANE * h:LANE * (h + 1)], qm_scr[i])

    def value_fn(i, ki):
        h = i // 2
        return vt_ref[ki, DIFF_V * h:DIFF_V * (h + 1), :]

    l = _two_pass_attention(2 * DIFF_HEADS, score_fn, value_fn, s_scr, acc_scr, blk)
    for h in range(DIFF_HEADS):
        o_t = acc_scr[2 * h] / l[2 * h] - lam * (acc_scr[2 * h + 1] / l[2 * h + 1])
        on_t = o_t * lax.rsqrt(jnp.mean(o_t * o_t, axis=0, keepdims=True) + EPS)
        o_ref[:, LANE * h:LANE * (h + 1)] = (on_t.T * (og_ref[...] * (1.0 - lam_init))).astype(BF16)


def _diff_attn(qd, kd, vdt, lam_p, og, lam_init, batch, seq):
    t = qd.shape[0]
    nq = seq // ATT_BLK
    return pl.pallas_call(
        functools.partial(_diff_attn_kernel, lam_init=lam_init),
        grid=(batch, nq),
        in_specs=[pl.BlockSpec((ATT_BLK, 512), lambda b, i: (b * nq + i, 0)),
                  pl.BlockSpec((seq, 512), lambda b, i: (b, 0)),
                  pl.BlockSpec((nq, 512, ATT_BLK), lambda b, i: (b, 0, 0)),
                  pl.BlockSpec(lam_p.shape, lambda b, i: (0, 0)),
                  pl.BlockSpec(og.shape, lambda b, i: (0, 0))],
        out_specs=pl.BlockSpec((ATT_BLK, 512), lambda b, i: (b * nq + i, 0)),
        out_shape=jax.ShapeDtypeStruct((t, 512), BF16),
        scratch_shapes=[pltpu.VMEM((2 * DIFF_HEADS, nq, ATT_BLK, ATT_BLK), F32),
                        pltpu.VMEM((2 * DIFF_HEADS, DIFF_V, ATT_BLK), F32),
                        pltpu.VMEM((2 * DIFF_HEADS, ATT_BLK, LANE), BF16)],
        compiler_params=_cp(("arbitrary", "arbitrary")),
        name="diff_attn",
    )(qd, kd, vdt, lam_p, og)


def _swa_kernel(q_ref, kp_ref, kc_ref, vtp_ref, vtc_ref, sink_ref, o_ref):
    w = SWA_WINDOW
    grp = SWA_Q_HEADS // SWA_KV_HEADS
    n = pl.program_id(1)
    key = lax.broadcasted_iota(jnp.int32, (2 * w, grp * w), 0)
    qry = lax.broadcasted_iota(jnp.int32, (2 * w, grp * w), 1) & (w - 1)
    cur_ok = (key >= w) & (key - w <= qry)
    prev_ok = (key < w) & (key > qry)
    low = lax.broadcasted_iota(jnp.int32, (w, LANE), 1) < HEAD_DIM

    for t in range(q_ref.shape[0] // w):
        rows = slice(t * w, (t + 1) * w)
        if t == 0:
            kp, vtp = kp_ref[...], vtp_ref[0]
            valid = cur_ok | (prev_ok & (n > 0))
        else:
            kp, vtp = kc_ref[(t - 1) * w:t * w, :], vtc_ref[t - 1]
            valid = cur_ok | prev_ok
        kc, vtc = kc_ref[rows, :], vtc_ref[t]
        for kv in range(SWA_KV_HEADS):
            kcols = slice(LANE * kv, LANE * (kv + 1))
            vrows = slice(HEAD_DIM * kv, HEAD_DIM * (kv + 1))
            k_win = jnp.concatenate([kp[:, kcols], kc[:, kcols]], axis=0)
            parts = []
            for u in range(2):
                qt = q_ref[rows, LANE * (2 * kv + u):LANE * (2 * kv + u + 1)]
                zero = jnp.zeros_like(qt)
                parts += [jnp.where(low, qt, zero), jnp.where(low, zero, qt)]
            s = jnp.where(valid, _nt(k_win, jnp.concatenate(parts, axis=0)), NEG)
            sink = sink_ref[:, grp * w * kv:grp * w * (kv + 1)]
            m = jnp.maximum(jnp.max(s, axis=0, keepdims=True), sink)
            p = jnp.exp2(s - m)
            den = jnp.sum(p, axis=0, keepdims=True) + jnp.exp2(sink - m)
            vt_win = jnp.concatenate([vtp[vrows, :], vtc[vrows, :]], axis=1)
            o_t = _dot(vt_win, p.astype(BF16)) / den
            for u in range(2):
                pair = jnp.concatenate([o_t[:, 2 * u * w:(2 * u + 1) * w],
                                        o_t[:, (2 * u + 1) * w:(2 * u + 2) * w]], axis=0)
                o_ref[rows, LANE * (2 * kv + u):LANE * (2 * kv + u + 1)] = pair.T.astype(BF16)


def _swa(qs, ks, vst, sink_row, batch, seq):
    t = qs.shape[0]
    w = SWA_WINDOW
    nb = seq // w
    ns = nb // SWA_QB
    cur = lambda b, n: (b * ns + n, 0)
    cur3 = lambda b, n: (b * ns + n, 0, 0)
    prev = lambda b, n: (b * nb + jnp.maximum(n * SWA_QB - 1, 0), 0)
    prev3 = lambda b, n: (b * nb + jnp.maximum(n * SWA_QB - 1, 0), 0, 0)
    return pl.pallas_call(
        _swa_kernel,
        grid=(batch, ns),
        in_specs=[pl.BlockSpec((SWA_QB * w, 512), cur),
                  pl.BlockSpec((w, 256), prev), pl.BlockSpec((SWA_QB * w, 256), cur),
                  pl.BlockSpec((1, LANE, w), prev3), pl.BlockSpec((SWA_QB, LANE, w), cur3),
                  pl.BlockSpec(sink_row.shape, lambda b, n: (0, 0))],
        out_specs=pl.BlockSpec((SWA_QB * w, 512), cur),
        out_shape=jax.ShapeDtypeStruct((t, 512), BF16),
        compiler_params=_cp(("arbitrary", "arbitrary")),
        name="swa_attn",
    )(qs, ks, ks, vst, vst, sink_row)


def _merge_kernel(*refs):
    n_sub = refs[0].shape[0] // MOE_TILE
    h2s = [_merge_mix(k, *refs) for k in range(n_sub)]
    for k in range(n_sub):
        _merge_route(k, h2s[k], *refs)


def _merge_mix(k, h_ref, ya_ref, yb_ref, yc_ref, yd_ref, x_ref, gt1_ref, wg_ref, wb_ref, wo_ref,
               g2_ref, sh2_ref, sc2_ref, rw_ref, rb_ref, xo_ref, h2_ref, comb_ref, dest_ref, meta_ref):
    rows = slice(MOE_TILE * k, MOE_TILE * (k + 1))
    h = h_ref[rows, :]
    d = x_ref.shape[1]
    merged = None
    for b, y_ref in enumerate((ya_ref, yb_ref, yc_ref, yd_ref)):
        gate = jax.nn.sigmoid(_dot(h, wg_ref[:, d * b:d * (b + 1)]))
        term = gate * _dot(y_ref[rows, :], wb_ref[b])
        merged = term if merged is None else merged + term
    xn = x_ref[rows, :] + gt1_ref[0] * _dot(merged.astype(BF16), wo_ref[...])
    xo_ref[rows, :] = xn
    ms = jnp.mean(xn * xn, axis=-1, keepdims=True)
    h2 = xn * lax.rsqrt(ms + EPS) * g2_ref[...]
    h2 = h2 * (1.0 + sc2_ref[0]) + sh2_ref[0]
    h2_ref[rows, :] = h2.astype(BF16)
    return h2


def _merge_route(k, h2, h_ref, ya_ref, yb_ref, yc_ref, yd_ref, x_ref, gt1_ref, wg_ref, wb_ref, wo_ref,
                 g2_ref, sh2_ref, sc2_ref, rw_ref, rb_ref, xo_ref, h2_ref, comb_ref, dest_ref, meta_ref):
    rows = slice(MOE_TILE * k, MOE_TILE * (k + 1))
    hh, hm = _split2(h2)
    wh = rw_ref[...].astype(BF16)
    logits = _dot(hh, wh) + _dot(hm, wh)
    lt = logits.T
    scores = jax.nn.sigmoid(lt[0:N_EXPERTS, :])
    sel = scores + rb_ref[...]
    per = N_EXPERTS // N_GROUPS
    srow = [sel[e:e + 1, :] for e in range(N_EXPERTS)]
    gsum = []
    for g in range(N_GROUPS):
        a, b_, c, e_ = srow[per * g:per * (g + 1)]
        gsum.append(jnp.maximum(jnp.maximum(jnp.maximum(a + b_, a + c), jnp.maximum(a + e_, b_ + c)),
                                jnp.maximum(b_ + e_, c + e_)))
    best = jnp.maximum(jnp.maximum(gsum[0], gsum[1]), jnp.maximum(gsum[2], gsum[3]))
    taken = None
    weights = []
    picks = []
    for g in range(N_GROUPS):
        hit = gsum[g] == best
        pick = hit if taken is None else hit & jnp.logical_not(taken)
        taken = hit if taken is None else taken | hit
        picks.append(pick.astype(F32))
        for e in range(per * g, per * (g + 1)):
            rank = jnp.zeros_like(best)
            for o in range(per * g, per * (g + 1)):
                if o == e:
                    continue
                ahead = (srow[o] > srow[e]) | ((srow[o] == srow[e]) & (o < e))
                rank = rank + ahead.astype(F32)
            weights.append(jnp.where(pick & (rank < 1.5), scores[e:e + 1, :], 0.0))
    wsum = weights[0]
    for r_ in weights[1:]:
        wsum = wsum + r_
    inv = 1.0 / wsum
    rid = lax.broadcasted_iota(jnp.int32, scores.shape, 0)
    comb_e = jnp.zeros_like(scores)
    for e, r_ in enumerate(weights):
        comb_e = jnp.where(rid == e, r_ * inv, comb_e)

    tm = lt.shape[1]
    gid = lax.broadcasted_iota(jnp.int32, (8, tm), 0)
    onehot = jnp.zeros((8, tm), F32)
    for g in range(N_GROUPS):
        onehot = jnp.where(gid == g, picks[g], onehot)
    before = (lax.broadcasted_iota(jnp.int32, (tm, tm), 0) < lax.broadcasted_iota(jnp.int32, (tm, tm), 1))
    rank = _dot(onehot.astype(BF16), before.astype(BF16))
    dest = jnp.zeros((1, tm), F32)
    off = jnp.zeros((1, 1), F32)
    meta = jnp.zeros((8, LANE), F32)
    mrow = lax.broadcasted_iota(jnp.int32, (8, LANE), 0)
    for g in range(N_GROUPS):
        cnt = jnp.sum(picks[g], axis=-1, keepdims=True)
        dest = dest + picks[g] * (off + rank[g:g + 1, :])
        meta = jnp.where(mrow == g, cnt, meta)
        meta = jnp.where(mrow == N_GROUPS + g, off, meta)
        off = off + jnp.ceil(cnt * (1.0 / MOE_ALIGN)) * MOE_ALIGN
    dest_ref[k] = dest.astype(jnp.int32)
    meta_ref[k] = meta.astype(jnp.int32)
    pad = jnp.zeros((LANE - N_EXPERTS - 8, tm), F32)
    comb_t = jnp.concatenate([comb_e, jnp.where(gid == 0, dest, 0.0), pad], axis=0)
    comb_ref[rows, :] = comb_t.T


def _merge(h, ys, x2, gt1, wg, wb, wo, g2, sh2, sc2, rw, rb, seq):
    t, d = x2.shape
    sub = MERGE_SUBTILES
    tm = sub * MOE_TILE
    tpb = seq // tm
    row = lambda i: (i, 0)
    per_b = lambda i: (i // tpb, 0, 0)
    c2 = lambda i: (0, 0)
    once = pl.Buffered(1)
    return pl.pallas_call(
        _merge_kernel,
        grid=(t // tm,),
        in_specs=[pl.BlockSpec((tm, d), row)] + [pl.BlockSpec((tm, 512), row)] * 4
                 + [pl.BlockSpec((tm, d), row), pl.BlockSpec((1, 1, d), per_b),
                    pl.BlockSpec(wg.shape, c2, pipeline_mode=once),
                    pl.BlockSpec(wb.shape, lambda i: (0, 0, 0), pipeline_mode=once),
                    pl.BlockSpec(wo.shape, c2, pipeline_mode=once), pl.BlockSpec((1, d), c2),
                    pl.BlockSpec((1, 1, d), per_b), pl.BlockSpec((1, 1, d), per_b),
                    pl.BlockSpec(rw.shape, c2), pl.BlockSpec(rb.shape, c2)],
        out_specs=[pl.BlockSpec((tm, d), row), pl.BlockSpec((tm, d), row), pl.BlockSpec((tm, LANE), row),
                   pl.BlockSpec((sub, 1, MOE_TILE), lambda i: (i, 0, 0)),
                   pl.BlockSpec((sub, 8, LANE), lambda i: (i, 0, 0))],
        out_shape=[jax.ShapeDtypeStruct((t, d), F32), jax.ShapeDtypeStruct((t, d), BF16),
                   jax.ShapeDtypeStruct((t, LANE), F32),
                   jax.ShapeDtypeStruct((t // MOE_TILE, 1, MOE_TILE), jnp.int32),
                   jax.ShapeDtypeStruct((t // MOE_TILE, 8, LANE), jnp.int32)],
        compiler_params=_cp(("arbitrary",)),
        name="merge_router",
    )(h, *ys, x2, gt1, wg, wb, wo, g2, sh2, sc2, rw, rb)


def _moe_kernel(meta_ref, h2_ref, comb_ref, dest_ref, x_ref, gt2_ref, wg_ref, wu_ref, wd_ref, o_ref,
                sorted_s, csort_s, out_s):
    i = pl.program_id(0)
    tm = h2_ref.shape[0]
    n_rows = sorted_s.shape[0]
    per = N_EXPERTS // N_GROUPS
    comb = comb_ref[...]

    place = (lax.broadcasted_iota(jnp.int32, (n_rows, tm), 0) == dest_ref[0]).astype(BF16)
    sorted_s[...] = _dot(place, h2_ref[...]).astype(BF16)
    c_hi, c_lo = _split2(comb)
    csort_s[...] = _dot(place, c_hi) + _dot(place, c_lo)
    out_s[...] = jnp.zeros_like(out_s)
    def run_experts(g, start, size):
        rows = pl.ds(pl.multiple_of(start, MOE_ALIGN), size)
        xk = sorted_s[rows, :]
        cw = csort_s[rows, :]
        lane = lax.broadcasted_iota(jnp.int32, (size, LANE), 1)
        acc = None
        for j in range(per):
            e = per * g + j
            hid = _silu(_dot(xk, wg_ref[0, e])) * _dot(xk, wu_ref[0, e])
            ce = jnp.sum(jnp.where(lane == e, cw, 0.0), axis=-1, keepdims=True)
            term = _dot((hid * ce).astype(BF16), wd_ref[0, e])
            acc = term if acc is None else acc + term
        out_s[rows, :] += acc

    def group(g, carry):
        cnt = meta_ref[i, g]
        off = meta_ref[i, N_GROUPS + g]
        big = MOE_CHUNKS[-1]
        n_big = cnt // big

        def big_chunk(c, inner):
            run_experts(g, off + c * big, big)
            return inner

        lax.fori_loop(0, n_big, big_chunk, 0)
        rest = cnt - n_big * big
        lower = 0
        for size in MOE_CHUNKS:
            @pl.when((rest > lower) & (rest <= size))
            def _(size=size):
                run_experts(g, off + n_big * big, size)
            lower = size
        return carry

    lax.fori_loop(0, N_GROUPS, group, 0)

    dest_col = comb[:, MOE_DEST_LANE:MOE_DEST_LANE + 1].astype(jnp.int32)
    back = (lax.broadcasted_iota(jnp.int32, (tm, n_rows), 1) == dest_col).astype(BF16)
    o_ref[...] = x_ref[...] + gt2_ref[0] * _dot(back, out_s[...].astype(BF16))


def _moe(h2, comb, dest, meta, x2, gt2, wg, wu, wd, layer, seq):
    t, d = x2.shape
    tm = MOE_TILE
    tpb = seq // tm
    n_rows = MOE_SORT_ROWS
    overhang = max(b - a for a, b in zip((0,) + MOE_CHUNKS, MOE_CHUNKS)) - 1
    assert tm + N_GROUPS * (MOE_ALIGN - 1) + overhang <= n_rows
    row = lambda i, m: (i, 0)
    whole = lambda a: pl.BlockSpec((1,) + a.shape[1:], lambda i, m: (layer, 0, 0, 0),
                                   pipeline_mode=pl.Buffered(1))
    grid_spec = pltpu.PrefetchScalarGridSpec(
        num_scalar_prefetch=1,
        grid=(t // tm,),
        in_specs=[pl.BlockSpec((tm, d), row), pl.BlockSpec((tm, LANE), row),
                  pl.BlockSpec((1, 1, tm), lambda i, m: (i, 0, 0)),
                  pl.BlockSpec((tm, d), row),
                  pl.BlockSpec((1, 1, d), lambda i, m: (i // tpb, 0, 0)),
                  whole(wg), whole(wu), whole(wd)],
        out_specs=pl.BlockSpec((tm, d), row),
        scratch_shapes=[pltpu.VMEM((n_rows, d), BF16), pltpu.VMEM((n_rows, LANE), F32),
                        pltpu.VMEM((n_rows, d), F32)])
    return pl.pallas_call(
        _moe_kernel,
        grid_spec=grid_spec,
        out_shape=jax.ShapeDtypeStruct((t, d), F32),
        compiler_params=_cp(("arbitrary",)),
        name="moe",
    )(meta, h2, comb, dest, x2, gt2, wg, wu, wd)


def _layer_params(l, w_in, hg_onorm, mla_q_norm, mla_kv_norm, mla_w_uq, mla_w_ukv, mla_qk_norm,
                  diff_qk_norm, swa_qk_norm, swa_sinks, lb_all):
    ends = [sum(IN_SPLITS[:i]) for i in range(len(IN_SPLITS) + 1)]
    cols = lambda a, b: w_in[l, :, ends[a]:ends[b]].astype(BF16)
    mla_pad = 512 - (ends[7] - ends[4])
    p = {"w_hg": cols(0, 4),
         "w_mla": jnp.pad(cols(4, 7), ((0, 0), (0, mla_pad))),
         "w_diff": cols(7, 10), "w_swa": cols(10, 13), "wg": cols(13, 14)}

    lb = lb_all[l]
    p["loglb"] = jnp.log(lb)[None, :]
    p["log1mlb"] = jnp.log1p(-lb)[None, :]
    p["ogain"] = jnp.tile(hg_onorm[l], HG_HEADS)[None, :]

    hd = MLA_NOPE + MLA_ROPE
    half = MLA_ROPE // 2
    wq = mla_w_uq[l].reshape(MLA_Q_RANK, MLA_HEADS, hd)
    z = lambda r, n: jnp.zeros((r, MLA_HEADS, n), F32)
    nope, rope = wq[:, :, :MLA_NOPE], wq[:, :, MLA_NOPE:]
    p["wqa"] = jnp.concatenate([nope, rope, z(MLA_Q_RANK, 32)], -1).reshape(MLA_Q_RANK, -1).astype(BF16)
    wkv = mla_w_ukv[l].reshape(MLA_KV_RANK, MLA_HEADS, MLA_NOPE + MLA_V)
    knope, vproj = wkv[:, :, :MLA_NOPE], wkv[:, :, MLA_NOPE:]
    eye = jnp.eye(MLA_ROPE, dtype=F32)
    place = lambda m: jnp.broadcast_to(
        jnp.concatenate([jnp.zeros((MLA_ROPE, MLA_NOPE), F32), m, jnp.zeros((MLA_ROPE, 32), F32)], -1)[:, None, :],
        (MLA_ROPE, MLA_HEADS, LANE))
    pad_rows = 256 - MLA_KV_RANK - MLA_ROPE
    p["wka"] = jnp.concatenate([jnp.concatenate([knope, z(MLA_KV_RANK, 64)], -1), place(eye),
                                z(pad_rows, LANE)], 0).reshape(256, -1).astype(BF16)
    p["wv"] = jnp.concatenate([vproj.reshape(MLA_KV_RANK, -1),
                               jnp.zeros((256 - MLA_KV_RANK, MLA_HEADS * MLA_V), F32)], 0).astype(BF16)
    p["qng"] = mla_q_norm[l][None, :]
    p["kvg"] = jnp.concatenate([mla_kv_norm[l], jnp.ones((256 - MLA_KV_RANK,), F32)])[None, :]

    def rope_gains(g):
        base = jnp.concatenate([g, jnp.zeros((LANE - hd,), F32)])
        part = jnp.concatenate([jnp.zeros((MLA_NOPE,), F32), g[MLA_NOPE + half:], g[MLA_NOPE:MLA_NOPE + half],
                                jnp.zeros((LANE - hd,), F32)])
        return base[None, :], part[None, :]

    p["gq"], p["gqs"] = rope_gains(mla_qk_norm[l, 0])
    p["gk"], p["gks"] = rope_gains(mla_qk_norm[l, 1])
    p["dgq"] = jnp.tile(diff_qk_norm[l, 0], 8)[None, :]
    p["dgk"] = jnp.tile(diff_qk_norm[l, 1], 8)[None, :]
    p["sgq"] = jnp.tile(swa_qk_norm[l, 0], 8)[None, :]
    p["sgk"] = jnp.tile(swa_qk_norm[l, 1], 2)[None, :]
    p["sinks"] = jnp.repeat(swa_sinks[l].astype(F32) * LOG2E, SWA_WINDOW)[None, :]
    return p


def _rope_tables(positions):
    inv_freq = ROPE_BASE ** (-jnp.arange(0, MLA_ROPE, 2, dtype=F32) / MLA_ROPE)
    zeros = lambda n: jnp.zeros((n,), F32)
    half = MLA_ROPE // 2
    pad = LANE - MLA_NOPE - MLA_ROPE
    freq = jnp.concatenate([zeros(MLA_NOPE), inv_freq, inv_freq, zeros(pad)])
    keep = jnp.concatenate([jnp.ones((MLA_NOPE + MLA_ROPE,), F32), zeros(pad)])
    sign = jnp.concatenate([zeros(MLA_NOPE), -jnp.ones((half,), F32), jnp.ones((half,), F32), zeros(pad)])
    ang = positions.astype(F32).reshape(-1)[:, None] * freq
    return jnp.cos(ang) * keep, jnp.sin(ang) * sign


def kernel(x, c, positions, ada_w, ada_b, norm_mix, norm_ffn, w_in, hg_lb_logits, hg_onorm, mla_q_norm, mla_kv_norm, mla_w_uq, mla_w_ukv, mla_qk_norm, diff_qk_norm, diff_lam, diff_onorm, swa_qk_norm, swa_sinks, w_branch, w_out, router_w, router_bias, moe_w_gate, moe_w_up, moe_w_down):
    batch, seq, d = x.shape
    x2 = x.reshape(batch * seq, d)
    cosf, sinf = _rope_tables(positions)
    lb_all = jnp.cumsum(jax.nn.softmax(hg_lb_logits.astype(F32), axis=0), axis=0)
    lb_all = lb_all - lb_all[0]
    mod = _modulation(c, ada_w, ada_b)
    rw = jnp.concatenate([router_w, jnp.zeros((d, LANE - N_EXPERTS), F32)], axis=1)
    rb = router_bias.astype(F32)[:, None]
    moe_w = (moe_w_gate.astype(BF16), moe_w_up.astype(BF16), moe_w_down.astype(BF16))

    for l in range(DEPTH):
        sh1, sc1, gt1, sh2, sc2, gt2 = [mod[l, :, d * k:d * (k + 1)][:, None, :] for k in range(6)]
        p = _layer_params(l, w_in, hg_onorm, mla_q_norm, mla_kv_norm, mla_w_uq, mla_w_ukv, mla_qk_norm,
                          diff_qk_norm, swa_qk_norm, swa_sinks, lb_all)
        hg4, lf, h, qm, km, vmt, qd, kd, vdt, qs, ks, vst = _inproj(
            x2, norm_mix[l][None, :], sh1, sc1, cosf, sinf, p, seq)
        y_a = _hgrn(hg4, lf, p["ogain"], batch, seq)
        y_b = _mla_attn(qm, km, vmt, batch, seq)
        lam_init = 0.8 - 0.6 * math.exp(-0.3 * l)
        y_c = _diff_attn(qd, kd, vdt, diff_lam[l], diff_onorm[l][None, :], lam_init, batch, seq)
        y_d = _swa(qs, ks, vst, p["sinks"], batch, seq)
        x2, h2, comb, dest, meta = _merge(h, (y_a, y_b, y_c, y_d), x2, gt1, p["wg"], w_branch[l].astype(BF16),
                                          w_out[l].astype(BF16), norm_ffn[l][None, :], sh2, sc2, rw, rb, seq)
        x2 = _moe(h2, comb, dest, meta[:, :, 0], x2, gt2, *moe_w, l, seq)
    return x2.reshape(batch, seq, d)
```

```python
import functools
import math

import jax
import jax.numpy as jnp
from jax import lax
from jax.experimental import pallas as pl
from jax.experimental.pallas import tpu as pltpu

F32 = jnp.float32
BF16 = jnp.bfloat16

D_MODEL = 1024
DEPTH = 2
EPS = 1e-6
N_BRANCH = 4
HG_HEADS = 8
HG_DK = 64
HG_W = HG_HEADS * HG_DK
HG_SUB = 16
HG_CHUNK = 64
HG_SAFE_DECAY = 80.0
HG_DIAG_ROWS =(HG_SUB // 2) * HG_SUB + (HG_SUB // 2) ** 2
MLA_HEADS = 8
MLA_Q_RANK = 256
MLA_KV_RANK = 128
MLA_NOPE = 64
MLA_ROPE = 32
MLA_V = 64
ROPE_BASE = 10000.0
DIFF_HEADS = 4
DIFF_QK = 64
DIFF_V = 128
SWA_Q_HEADS = 8
SWA_KV_HEADS = 2
SWA_WINDOW = 128
HEAD_DIM = 64
N_EXPERTS = 16
N_GROUPS = 4
D_FF_EXPERT = 256
IN_SPLITS = (512, 512, 512, 512, 256, 128, 32, 512, 512, 512, 512, 128, 128, 4096)

MOD_COLS = 1536
ROW_TILE = 512
HG_BLOCK = 256
MOE_TILE = 512
MERGE_SUBTILES = 2
MOE_ALIGN = 16
MOE_CHUNKS = (64, 128, 192, 256)
MOE_SORT_ROWS = 640
MOE_DEST_LANE = N_EXPERTS
LANE = 128
ATT_BLK = 256
SWA_QB = 16
LOG2E = 1.4426950408889634
NEG = -1e30
VMEM_LIMIT = 56 * 1024 * 1024


def _cp(sem, vmem=VMEM_LIMIT):
    return pltpu.CompilerParams(dimension_semantics=sem, vmem_limit_bytes=vmem)


def _nt(a, b):
    return lax.dot_general(a, b, (((1,), (1,)), ((), ())), preferred_element_type=F32)


def _tn(a, b):
    return lax.dot_general(a, b, (((0,), (0,)), ((), ())), preferred_element_type=F32)


def _dot(a, b):
    return jnp.dot(a, b, preferred_element_type=F32)


def _split2(x):
    hi = x.astype(BF16)
    lo = (x - hi.astype(F32)).astype(BF16)
    return hi, lo


def _seg_id(idx, seg):
    shift = seg.bit_length() - 1
    assert 1 << shift == seg
    return lax.shift_right_logical(idx, shift)


def _same_seg(n, seg):
    r = lax.broadcasted_iota(jnp.int32, (n, n), 0)
    c = lax.broadcasted_iota(jnp.int32, (n, n), 1)
    return _seg_id(r, seg) == _seg_id(c, seg)


def _seg_ones(n, seg):
    return _same_seg(n, seg).astype(BF16)


def _seg_mean_sq(x, seg):
    n = x.shape[-1]
    return _dot((x * x).astype(BF16), _seg_ones(n, seg)) * (1.0 / seg)


def _silu(x):
    return x * jax.nn.sigmoid(x)


def _mod_kernel(c_ref, w_ref, b_ref, o_ref):
    hi, lo = _split2(_silu(c_ref[...]))
    w = w_ref[0].astype(BF16)
    o_ref[0] = _dot(hi, w) + _dot(lo, w) + b_ref[0]


def _modulation(c, ada_w, ada_b):
    nl, d, n6 = ada_w.shape
    b = c.shape[0]
    tn = MOD_COLS
    return pl.pallas_call(
        _mod_kernel,
        grid=(nl, n6 // tn),
        in_specs=[pl.BlockSpec((b, d), lambda l, j: (0, 0)),
                  pl.BlockSpec((1, d, tn), lambda l, j: (l, 0, j)),
                  pl.BlockSpec((1, 1, tn), lambda l, j: (l, 0, j))],
        out_specs=pl.BlockSpec((1, b, tn), lambda l, j: (l, 0, j)),
        out_shape=jax.ShapeDtypeStruct((nl, b, n6), F32),
        compiler_params=_cp(("arbitrary", "arbitrary")),
        name="modulation",
    )(c, ada_w, ada_b.reshape(nl, 1, n6))


N_PREP_CONSTS = 13


def _inproj_kernel(x_ref, g_ref, sh_ref, sc_ref, loglb_ref, log1mlb_ref, whg_ref, wmla_ref, wdiff_ref, wswa_ref,
                   cos_ref, sin_ref, *rest):
    prep_consts, (ohg_ref, olf_ref, oh_ref), prep_outs = (
        rest[:N_PREP_CONSTS], rest[N_PREP_CONSTS:N_PREP_CONSTS + 3], rest[N_PREP_CONSTS + 3:])
    x = x_ref[...]
    ms = jnp.mean(x * x, axis=-1, keepdims=True)
    h = x * lax.rsqrt(ms + EPS) * g_ref[...]
    h = h * (1.0 + sc_ref[0]) + sh_ref[0]
    hb = h.astype(BF16)
    oh_ref[...] = hb

    def proj(w_ref, lo, hi):
        return _dot(hb, w_ref[:, lo:hi])

    def hg_forget():
        fr = proj(whg_ref, 512, 1024)
        ls = jnp.minimum(fr, 0.0) - jnp.log(1.0 + jnp.exp(-jnp.abs(fr)))
        a = loglb_ref[...]
        c2 = log1mlb_ref[...] + ls
        lf = jnp.maximum(a, c2) + jnp.log(1.0 + jnp.exp(-jnp.abs(a - c2)))
        olf_ref[...] = lf
        ohg_ref[:, 1536:2048] = (1.0 - jnp.exp(lf)).astype(BF16)

    def hg_query():
        ohg_ref[:, 0:512] = _silu(proj(whg_ref, 0, 512)).astype(BF16)

    def hg_input():
        ohg_ref[:, 512:1024] = proj(whg_ref, 1024, 1536).astype(BF16)

    def hg_gate():
        ohg_ref[:, 1024:1536] = proj(whg_ref, 1536, 2048).astype(BF16)

    proj.weights = (wmla_ref, wdiff_ref, wswa_ref)
    proj.interleave = (hg_forget, hg_query, hg_input, hg_gate)
    _attn_prep(proj, cos_ref, sin_ref, *prep_consts, *prep_outs)


def _inproj(x2, gain, sh, sc, cosf, sinf, p, seq):
    t, d = x2.shape
    tm = ROW_TILE
    tpb = seq // tm
    row = lambda i: (i, 0)
    per_b = lambda i: (i // tpb, 0, 0)
    const = lambda a: pl.BlockSpec(a.shape, lambda i: (0,) * a.ndim)
    weights = [p["w_hg"], p["w_mla"], p["w_diff"], p["w_swa"]]
    prep_consts = [p["qng"], p["kvg"], p["wqa"], p["wka"], p["wv"],
                   p["gq"], p["gqs"], p["gk"], p["gks"], p["dgq"], p["dgk"], p["sgq"], p["sgk"]]
    assert len(prep_consts) == N_PREP_CONSTS

    def rows_out(w, dt=BF16):
        return pl.BlockSpec((tm, w), row), jax.ShapeDtypeStruct((t, w), dt)

    def transposed_out(n, blk):
        return (pl.BlockSpec((tm // blk, n, blk), lambda i: (i, 0, 0)),
                jax.ShapeDtypeStruct((t // blk, n, blk), BF16))

    outs = [rows_out(2048), rows_out(512, F32), rows_out(d),
            rows_out(1024), rows_out(1024), transposed_out(512, ATT_BLK),
            rows_out(512), rows_out(512), transposed_out(512, ATT_BLK),
            rows_out(512), rows_out(256), transposed_out(LANE, SWA_WINDOW)]
    return pl.pallas_call(
        _inproj_kernel,
        grid=(t // tm,),
        in_specs=[pl.BlockSpec((tm, d), row), const(gain),
                  pl.BlockSpec((1, 1, d), per_b),
                  pl.BlockSpec((1, 1, d), per_b), const(p["loglb"]), const(p["log1mlb"])]
                 + [const(w) for w in weights]
                 + [pl.BlockSpec((tm, LANE), row), pl.BlockSpec((tm, LANE), row)]
                 + [const(a) for a in prep_consts],
        out_specs=[o[0] for o in outs],
        out_shape=[o[1] for o in outs],
        compiler_params=_cp(("arbitrary",)),
        name="inproj",
    )(x2, gain, sh, sc, p["loglb"], p["log1mlb"], *weights, cosf, sinf, *prep_consts)


def _segment_cumsum(x, seg):
    n = x.shape[0]
    r = lax.broadcasted_iota(jnp.int32, (n, n), 0)
    cc = lax.broadcasted_iota(jnp.int32, (n, n), 1)
    same = _same_seg(n, seg)
    tri = (same & (cc <= r)).astype(BF16)
    blk = same.astype(BF16)
    hi, lo = _split2(x)
    return _dot(tri, hi) + _dot(tri, lo), _dot(blk, hi) + _dot(blk, lo)


def _hgrn_chunk_path(i_ref, st_ref, c_s, tot_s, qs_s, kk_s, qe_s, kd_s, ke_s, dec_s, od_s):
    rows_blk = c_s.shape[0]
    tot = tot_s[...]
    rel = c_s[...] - 0.5 * tot
    half_dec = jnp.exp(0.5 * tot)
    kd = kk_s[...] * jnp.exp(-rel)
    qe_s[...] = (qs_s[...] * jnp.exp(rel)).astype(BF16)
    kd_s[...] = kd.astype(BF16)
    ke_s[...] = (kd * half_dec).astype(BF16)
    dec_s[...] = half_dec

    row = lax.broadcasted_iota(jnp.int32, (2 * rows_blk, rows_blk), 0) & (rows_blk - 1)
    col = lax.broadcasted_iota(jnp.int32, (2 * rows_blk, rows_blk), 1)
    intra = (_seg_id(row, HG_CHUNK) == _seg_id(col, HG_CHUNK)) & (col <= row)
    low_q = lax.broadcasted_iota(jnp.int32, (rows_blk, LANE), 1) < HG_DK
    low_c = lax.broadcasted_iota(jnp.int32, (HG_CHUNK, LANE), 1) < HG_DK

    for j in range(HG_W // LANE):
        cols = slice(LANE * j, LANE * (j + 1))
        qe = qe_s[:, cols]
        zero = jnp.zeros_like(qe)
        q2 = jnp.concatenate([jnp.where(low_q, qe, zero), jnp.where(low_q, zero, qe)], axis=0)
        attn = jnp.where(intra, _nt(q2, kd_s[:, cols]), 0.0).astype(BF16)
        o2 = _dot(attn, i_ref[:, cols])
        n_chunks = rows_blk // HG_CHUNK
        chunk_rows = [slice(HG_CHUNK * ch, HG_CHUNK * (ch + 1)) for ch in range(n_chunks)]
        upd = [_tn(i_ref[rows, cols], ke_s[rows, cols]) for rows in chunk_rows]
        st = st_ref[j]
        for ch, rows in enumerate(chunk_rows):
            rows_hi = slice(rows_blk + HG_CHUNK * ch, rows_blk + HG_CHUNK * (ch + 1))
            hd = dec_s[HG_CHUNK * ch:HG_CHUNK * ch + 1, cols]
            inter = _nt(jnp.concatenate([q2[rows], q2[rows_hi]], axis=0), (st * hd).astype(BF16))
            od_s[rows, cols] = jnp.where(low_c, o2[rows] + inter[:HG_CHUNK], o2[rows_hi] + inter[HG_CHUNK:])
            st = st * (hd * hd) + upd[ch]
        st_ref[j] = st


def _hgrn_exact_path(i_ref, st_ref, lf_s, c_s, qs_s, kk_s, qe_s, ke_s, dec_s, od_s, t_s, a_s):
    rows_blk = c_s.shape[0]
    n_sub = rows_blk // HG_SUB
    c, tot = _segment_cumsum(lf_s[...], HG_SUB)
    c_s[...] = c
    qe_s[...] = (qs_s[...] * jnp.exp(c)).astype(BF16)
    ke_s[...] = (kk_s[...] * jnp.exp(tot - c)).astype(BF16)
    dec_s[...] = jnp.exp(tot)

    same_head = _same_seg(LANE, HG_DK)
    head_mask = same_head.astype(F32)
    head_ones = same_head.astype(BF16)
    for j in range(HG_W // LANE):
        st_ref[j] = st_ref[j] * head_mask
    half = HG_SUB // 2
    trow = lax.broadcasted_iota(jnp.int32, (half, HG_W), 0)

    def body(i, carry):
        r0 = pl.multiple_of(i * HG_SUB, HG_SUB)
        rows = pl.ds(r0, HG_SUB)
        c_i = c_s[rows, :]
        qs_i = qs_s[rows, :]
        kk_i = kk_s[rows, :]
        v_i = i_ref[rows, :].astype(F32)
        c_lo, c_hi = c_i[:half], c_i[half:]
        q_lo, q_hi = qs_i[:half], qs_i[half:]
        for s in range(half):
            c_row, k_row = c_i[s:s + 1, :], kk_i[s:s + 1, :]
            e_lo = jnp.exp(jnp.where(trow >= s, c_lo - c_row, NEG))
            e_hi = jnp.exp(c_hi - c_row)
            both = jnp.concatenate([e_lo * q_lo, e_hi * q_hi], axis=0) * k_row
            t_s[s * HG_SUB:(s + 1) * HG_SUB, :] = both.astype(BF16)
        for s in range(half, HG_SUB, 2):
            pair = []
            for u in (s, s + 1):
                e_hi = jnp.exp(jnp.where(trow >= u - half, c_hi - c_i[u:u + 1, :], NEG))
                pair.append(e_hi * q_hi * kk_i[u:u + 1, :])
            base = half * HG_SUB + (s - half) * half
            t_s[base:base + HG_SUB, :] = jnp.concatenate(pair, axis=0).astype(BF16)
        for j in range(HG_W // LANE):
            cols = slice(LANE * j, LANE * (j + 1))
            a_s[:, cols] = _dot(t_s[:, cols], head_ones)
        acc_lo = jnp.zeros((half, HG_W), F32)
        acc_hi = jnp.zeros((half, HG_W), F32)
        for s in range(half):
            acc_lo = acc_lo + a_s[s * HG_SUB:s * HG_SUB + half, :] * v_i[s:s + 1, :]
            acc_hi = acc_hi + a_s[s * HG_SUB + half:(s + 1) * HG_SUB, :] * v_i[s:s + 1, :]
        for s in range(half, HG_SUB):
            base = half * HG_SUB + (s - half) * half
            acc_hi = acc_hi + a_s[base:base + half, :] * v_i[s:s + 1, :]
        acc = jnp.concatenate([acc_lo, acc_hi], axis=0)
        for j in range(HG_W // LANE):
            cols = slice(LANE * j, LANE * (j + 1))
            st = st_ref[j]
            o_int = _nt(qe_s[rows, cols], st.astype(BF16))
            upd = _tn(i_ref[rows, cols], ke_s[rows, cols])
            st_ref[j] = st * dec_s[pl.ds(r0, 1), cols] + upd * head_mask
            od_s[rows, cols] = acc[:, cols] + o_int
        return carry

    lax.fori_loop(0, n_sub, body, 0)


def _hgrn_kernel(q_ref, i_ref, g_ref, k_ref, lf_ref, og_ref, o_ref,
                 st_ref, lf_s, c_s, tot_s, qs_s, kk_s, qe_s, kd_s, ke_s, dec_s, od_s, t_s, a_s):
    @pl.when(pl.program_id(1) == 0)
    def _():
        st_ref[...] = jnp.zeros_like(st_ref)

    lf = lf_ref[...]
    lf_s[...] = lf
    qs_s[...] = q_ref[...].astype(F32)
    kk_s[...] = k_ref[...].astype(F32)
    c, tot = _segment_cumsum(lf, HG_CHUNK)
    c_s[...] = c
    tot_s[...] = tot
    safe = 0.5 * jnp.max(-tot) <= HG_SAFE_DECAY

    @pl.when(safe)
    def _():
        _hgrn_chunk_path(i_ref, st_ref, c_s, tot_s, qs_s, kk_s, qe_s, kd_s, ke_s, dec_s, od_s)

    @pl.when(jnp.logical_not(safe))
    def _():
        _hgrn_exact_path(i_ref, st_ref, lf_s, c_s, qs_s, kk_s, qe_s, ke_s, dec_s, od_s, t_s, a_s)

    o = od_s[...]
    ms = _seg_mean_sq(o, HG_DK)
    on = o * lax.rsqrt(ms + EPS) * og_ref[...]
    o_ref[...] = (on * _silu(g_ref[...].astype(F32))).astype(BF16)


def _hgrn(hg4, lf, ogain, batch, seq):
    t = lf.shape[0]
    rb = HG_BLOCK
    nb = seq // rb
    blk = lambda k: pl.BlockSpec((rb, HG_W), lambda b, n, k=k: (b * nb + n, k))
    vec = pl.BlockSpec((1, HG_W), lambda b, n: (0, 0))
    f32_blk = pltpu.VMEM((rb, HG_W), F32)
    bf16_blk = pltpu.VMEM((rb, HG_W), BF16)
    return pl.pallas_call(
        _hgrn_kernel,
        grid=(batch, nb),
        in_specs=[blk(0), blk(1), blk(2), blk(3), blk(0), vec],
        out_specs=blk(0),
        out_shape=jax.ShapeDtypeStruct((t, HG_W), BF16),
        scratch_shapes=[pltpu.VMEM((HG_W // LANE, LANE, LANE), F32),
                        f32_blk, f32_blk, f32_blk, f32_blk, f32_blk,
                        bf16_blk, bf16_blk, bf16_blk,
                        f32_blk, f32_blk,
                        pltpu.VMEM((HG_DIAG_ROWS, HG_W), BF16),
                        pltpu.VMEM((HG_DIAG_ROWS, HG_W), F32)],
        compiler_params=_cp(("arbitrary", "arbitrary")),
        name="hgrn2",
    )(hg4, hg4, hg4, hg4, lf, ogain)


def _store_transposed_blocks(out_ref, v):
    blk = out_ref.shape[2]
    for u in range(out_ref.shape[0]):
        out_ref[u] = v[u * blk:(u + 1) * blk, :].T.astype(BF16)


def _attn_prep(proj, cos_ref, sin_ref,
               qng_ref, kvg_ref, wqa_ref, wka_ref, wv_ref,
               gq_ref, gqs_ref, gk_ref, gks_ref, dgq_ref, dgk_ref, sgq_ref, sgk_ref,
               qm_ref, km_ref, vmt_ref, qd_ref, kd_ref, vdt_ref, qs_ref, ks_ref, vst_ref):
    wmla_ref, wdiff_ref, wswa_ref = proj.weights
    blk = proj(wmla_ref, 0, 512)
    cq = blk[:, :MLA_Q_RANK]
    rest = blk[:, MLA_Q_RANK:]
    cqn = cq * lax.rsqrt(jnp.mean(cq * cq, axis=-1, keepdims=True) + EPS) * qng_ref[...]
    lane = lax.broadcasted_iota(jnp.int32, rest.shape, 1)
    is_kv = lane < MLA_KV_RANK
    ms_kv = jnp.sum(jnp.where(is_kv, rest * rest, 0.0), axis=-1, keepdims=True) * (1.0 / MLA_KV_RANK)
    restn = jnp.where(is_kv, rest * lax.rsqrt(ms_kv + EPS) * kvg_ref[...], rest)
    cqb = cqn.astype(BF16)
    rb = restn.astype(BF16)
    qa = _dot(cqb, wqa_ref[...])
    ka = _dot(rb, wka_ref[...])
    _store_transposed_blocks(vmt_ref, _dot(rb, wv_ref[...]))
    first_half = lax.broadcasted_iota(jnp.int32, (blk.shape[0], LANE), 1) < MLA_NOPE + MLA_ROPE // 2

    def partner(x):
        return jnp.where(first_half, pltpu.roll(x, LANE - MLA_ROPE // 2, 1), pltpu.roll(x, MLA_ROPE // 2, 1))

    cosf = cos_ref[...]
    sinf = sin_ref[...]
    cq_t = cosf * gq_ref[...]
    sq_t = sinf * gqs_ref[...]
    ck_t = cosf * gk_ref[...]
    sk_t = sinf * gks_ref[...]
    inv_n = 1.0 / (MLA_NOPE + MLA_ROPE)
    scale = (MLA_NOPE + MLA_ROPE) ** -0.5 * LOG2E
    def seg_norm(x, gain, scale):
        return x * lax.rsqrt(_seg_mean_sq(x, HEAD_DIM) + EPS) * (gain * scale)

    def diff_q():
        qd_ref[...] = seg_norm(proj(wdiff_ref, 0, 512), dgq_ref[...], DIFF_QK ** -0.5 * LOG2E).astype(BF16)

    def diff_k():
        kd_ref[...] = seg_norm(proj(wdiff_ref, 512, 1024), dgk_ref[...], 1.0).astype(BF16)

    def diff_v():
        _store_transposed_blocks(vdt_ref, proj(wdiff_ref, 1024, 1536))

    def swa_q():
        qs_ref[...] = seg_norm(proj(wswa_ref, 0, 512), sgq_ref[...], HEAD_DIM ** -0.5 * LOG2E).astype(BF16)

    def swa_kv():
        skv = proj(wswa_ref, 512, 768)
        kn = seg_norm(skv[:, :LANE], sgk_ref[...], 1.0)
        low = lax.broadcasted_iota(jnp.int32, kn.shape, 1) < HEAD_DIM
        sw = pltpu.roll(kn, HEAD_DIM, 1)
        ks_ref[:, :LANE] = jnp.where(low, kn, sw).astype(BF16)
        ks_ref[:, LANE:] = jnp.where(low, sw, kn).astype(BF16)
        _store_transposed_blocks(vst_ref, skv[:, LANE:])

    pending = list(proj.interleave) + [diff_q, diff_k, diff_v, swa_q, swa_kv]
    for h in range(MLA_HEADS):
        cols = slice(LANE * h, LANE * (h + 1))
        x = qa[:, cols]
        rinv = lax.rsqrt(jnp.sum(x * x, axis=-1, keepdims=True) * inv_n + EPS)
        qm_ref[:, cols] = ((x * cq_t + partner(x) * sq_t) * (rinv * scale)).astype(BF16)
        y = ka[:, cols]
        rinv = lax.rsqrt(jnp.sum(y * y, axis=-1, keepdims=True) * inv_n + EPS)
        km_ref[:, cols] = ((y * ck_t + partner(y) * sk_t) * rinv).astype(BF16)
        if pending:
            pending.pop(0)()
    for work in pending:
        work()


def _causal_t(blk):
    key = lax.broadcasted_iota(jnp.int32, (blk, blk), 0)
    qry = lax.broadcasted_iota(jnp.int32, (blk, blk), 1)
    return key <= qry


def _two_pass_attention(n_sets, score_fn, value_fn, s_scr, acc_scr, blk):
    qi = pl.program_id(1)
    causal = _causal_t(blk)

    def scores(ki, m, masked):
        out = []
        for i in range(n_sets):
            s = score_fn(i, ki)
            if masked:
                s = jnp.where(causal, s, NEG)
            s_scr[i, ki] = s
            out.append(jnp.maximum(m[i], jnp.max(s, axis=0, keepdims=True)))
        return tuple(out)

    def blocked(n, step, carry):
        def many(k0, count, c):
            return step(tuple(k0 + u for u in range(count)), c)
        carry = lax.fori_loop(0, n // 4, lambda kp, c: many(4 * kp, 4, c), carry)
        done = (n // 4) * 4
        carry = lax.cond(n - done >= 2, lambda c: many(done, 2, c), lambda c: c, carry)
        done = (n // 2) * 2
        return lax.cond(n - done == 1, lambda c: many(done, 1, c), lambda c: c, carry)

    def scores_step(kis, m):
        for ki in kis:
            m = scores(ki, m, False)
        return m

    m = tuple(jnp.full((1, blk), NEG, F32) for _ in range(n_sets))
    m = blocked(qi, scores_step, m)
    m = scores(qi, m, True)

    acc_scr[...] = jnp.zeros_like(acc_scr)

    def accumulate(kis, l):
        out = []
        for i in range(n_sets):
            li, pv = l[i], None
            for ki in kis:
                p = jnp.exp2(s_scr[i, ki] - m[i])
                li = li + jnp.sum(p, axis=0, keepdims=True)
                term = _dot(value_fn(i, ki), p.astype(BF16))
                pv = term if pv is None else pv + term
            out.append(li)
            acc_scr[i] += pv
        return tuple(out)

    l = tuple(jnp.zeros((1, blk), F32) for _ in range(n_sets))
    return blocked(qi + 1, accumulate, l)


def _mla_attn_kernel(q_ref, k_ref, vt_ref, o_ref, s_scr, acc_scr):
    blk = q_ref.shape[0]

    def score_fn(h, ki):
        rows = pl.ds(pl.multiple_of(ki * blk, blk), blk)
        cols = slice(LANE * h, LANE * (h + 1))
        return _nt(k_ref[rows, cols], q_ref[:, cols])

    def value_fn(h, ki):
        return vt_ref[ki, MLA_V * h:MLA_V * (h + 1), :]

    l = _two_pass_attention(MLA_HEADS, score_fn, value_fn, s_scr, acc_scr, blk)
    for j in range(MLA_HEADS // 2):
        o_t = jnp.concatenate([acc_scr[2 * j] / l[2 * j], acc_scr[2 * j + 1] / l[2 * j + 1]], axis=0)
        o_ref[:, LANE * j:LANE * (j + 1)] = o_t.T.astype(BF16)


def _mla_attn(qm, km, vmt, batch, seq):
    t = qm.shape[0]
    nq = seq // ATT_BLK
    return pl.pallas_call(
        _mla_attn_kernel,
        grid=(batch, nq),
        in_specs=[pl.BlockSpec((ATT_BLK, 1024), lambda b, i: (b * nq + i, 0)),
                  pl.BlockSpec((seq, 1024), lambda b, i: (b, 0)),
                  pl.BlockSpec((nq, 512, ATT_BLK), lambda b, i: (b, 0, 0))],
        out_specs=pl.BlockSpec((ATT_BLK, 512), lambda b, i: (b * nq + i, 0)),
        out_shape=jax.ShapeDtypeStruct((t, 512), BF16),
        scratch_shapes=[pltpu.VMEM((MLA_HEADS, nq, ATT_BLK, ATT_BLK), F32),
                        pltpu.VMEM((MLA_HEADS, MLA_V, ATT_BLK), F32)],
        compiler_params=_cp(("arbitrary", "arbitrary")),
        name="mla_attn",
    )(qm, km, vmt)


def _diff_attn_kernel(q_ref, k_ref, vt_ref, lam_ref, og_ref, o_ref, s_scr, acc_scr, qm_scr, *, lam_init):
    blk = q_ref.shape[0]
    low = lax.broadcasted_iota(jnp.int32, (blk, LANE), 1) < DIFF_QK
    lp = lam_ref[...]
    lam = (jnp.exp(jnp.sum(lp[0:1] * lp[1:2], axis=-1, keepdims=True))
           - jnp.exp(jnp.sum(lp[2:3] * lp[3:4], axis=-1, keepdims=True)) + lam_init)

    for h in range(DIFF_HEADS):
        qt = q_ref[:, LANE * h:LANE * (h + 1)]
        zero = jnp.zeros_like(qt)
        qm_scr[2 * h] = jnp.where(low, qt, zero)
        qm_scr[2 * h + 1] = jnp.where(low, zero, qt)

    def score_fn(i, ki):
        rows = pl.ds(pl.multiple_of(ki * blk, blk), blk)
        h = i // 2
        return _nt(k_ref[rows, LANE * h:LANE * (h + 1)], qm_scr[i])

    def value_fn(i, ki):
        h = i // 2
        return vt_ref[ki, DIFF_V * h:DIFF_V * (h + 1), :]

    l = _two_pass_attention(2 * DIFF_HEADS, score_fn, value_fn, s_scr, acc_scr, blk)
    for h in range(DIFF_HEADS):
        o_t = acc_scr[2 * h] / l[2 * h] - lam * (acc_scr[2 * h + 1] / l[2 * h + 1])
        on_t = o_t * lax.rsqrt(jnp.mean(o_t * o_t, axis=0, keepdims=True) + EPS)
        o_ref[:, LANE * h:LANE * (h + 1)] = (on_t.T * (og_ref[...] * (1.0 - lam_init))).astype(BF16)


def _diff_attn(qd, kd, vdt, lam_p, og, lam_init, batch, seq):
    t = qd.shape[0]
    nq = seq // ATT_BLK
    return pl.pallas_call(
        functools.partial(_diff_attn_kernel, lam_init=lam_init),
        grid=(batch, nq),
        in_specs=[pl.BlockSpec((ATT_BLK, 512), lambda b, i: (b * nq + i, 0)),
                  pl.BlockSpec((seq, 512), lambda b, i: (b, 0)),
                  pl.BlockSpec((nq, 512, ATT_BLK), lambda b, i: (b, 0, 0)),
                  pl.BlockSpec(lam_p.shape, lambda b, i: (0, 0)),
                  pl.BlockSpec(og.shape, lambda b, i: (0, 0))],
        out_specs=pl.BlockSpec((ATT_BLK, 512), lambda b, i: (b * nq + i, 0)),
        out_shape=jax.ShapeDtypeStruct((t, 512), BF16),
        scratch_shapes=[pltpu.VMEM((2 * DIFF_HEADS, nq, ATT_BLK, ATT_BLK), F32),
                        pltpu.VMEM((2 * DIFF_HEADS, DIFF_V, ATT_BLK), F32),
                        pltpu.VMEM((2 * DIFF_HEADS, ATT_BLK, LANE), BF16)],
        compiler_params=_cp(("arbitrary", "arbitrary")),
        name="diff_attn",
    )(qd, kd, vdt, lam_p, og)


def _swa_kernel(q_ref, kp_ref, kc_ref, vtp_ref, vtc_ref, sink_ref, o_ref):
    w = SWA_WINDOW
    grp = SWA_Q_HEADS // SWA_KV_HEADS
    n = pl.program_id(1)
    key = lax.broadcasted_iota(jnp.int32, (2 * w, grp * w), 0)
    qry = lax.broadcasted_iota(jnp.int32, (2 * w, grp * w), 1) & (w - 1)
    cur_ok = (key >= w) & (key - w <= qry)
    prev_ok = (key < w) & (key > qry)
    low = lax.broadcasted_iota(jnp.int32, (w, LANE), 1) < HEAD_DIM

    for t in range(q_ref.shape[0] // w):
        rows = slice(t * w, (t + 1) * w)
        if t == 0:
            kp, vtp = kp_ref[...], vtp_ref[0]
            valid = cur_ok | (prev_ok & (n > 0))
        else:
            kp, vtp = kc_ref[(t - 1) * w:t * w, :], vtc_ref[t - 1]
            valid = cur_ok | prev_ok
        kc, vtc = kc_ref[rows, :], vtc_ref[t]
        for kv in range(SWA_KV_HEADS):
            kcols = slice(LANE * kv, LANE * (kv + 1))
            vrows = slice(HEAD_DIM * kv, HEAD_DIM * (kv + 1))
            k_win = jnp.concatenate([kp[:, kcols], kc[:, kcols]], axis=0)
            parts = []
            for u in range(2):
                qt = q_ref[rows, LANE * (2 * kv + u):LANE * (2 * kv + u + 1)]
                zero = jnp.zeros_like(qt)
                parts += [jnp.where(low, qt, zero), jnp.where(low, zero, qt)]
            s = jnp.where(valid, _nt(k_win, jnp.concatenate(parts, axis=0)), NEG)
            sink = sink_ref[:, grp * w * kv:grp * w * (kv + 1)]
            m = jnp.maximum(jnp.max(s, axis=0, keepdims=True), sink)
            p = jnp.exp2(s - m)
            den = jnp.sum(p, axis=0, keepdims=True) + jnp.exp2(sink - m)
            vt_win = jnp.concatenate([vtp[vrows, :], vtc[vrows, :]], axis=1)
            o_t = _dot(vt_win, p.astype(BF16)) / den
            for u in range(2):
                pair = jnp.concatenate([o_t[:, 2 * u * w:(2 * u + 1) * w],
                                        o_t[:, (2 * u + 1) * w:(2 * u + 2) * w]], axis=0)
                o_ref[rows, LANE * (2 * kv + u):LANE * (2 * kv + u + 1)] = pair.T.astype(BF16)


def _swa(qs, ks, vst, sink_row, batch, seq):
    t = qs.shape[0]
    w = SWA_WINDOW
    nb = seq // w
    qb = math.gcd(SWA_QB, nb)
    ns = nb // qb
    cur = lambda b, n: (b * ns + n, 0)
    cur3 = lambda b, n: (b * ns + n, 0, 0)
    prev = lambda b, n: (b * nb + jnp.maximum(n * qb - 1, 0), 0)
    prev3 = lambda b, n: (b * nb + jnp.maximum(n * qb - 1, 0), 0, 0)
    return pl.pallas_call(
        _swa_kernel,
        grid=(batch, ns),
        in_specs=[pl.BlockSpec((qb * w, 512), cur),
                  pl.BlockSpec((w, 256), prev), pl.BlockSpec((qb * w, 256), cur),
                  pl.BlockSpec((1, LANE, w), prev3), pl.BlockSpec((qb, LANE, w), cur3),
                  pl.BlockSpec(sink_row.shape, lambda b, n: (0, 0))],
        out_specs=pl.BlockSpec((qb * w, 512), cur),
        out_shape=jax.ShapeDtypeStruct((t, 512), BF16),
        compiler_params=_cp(("arbitrary", "arbitrary")),
        name="swa_attn",
    )(qs, ks, ks, vst, vst, sink_row)


def _merge_kernel(*refs):
    n_sub = refs[0].shape[0] // MOE_TILE
    route_prev = None
    for k in range(n_sub):
        h2 = _merge_mix(k, route_prev, *refs)
        route_prev = functools.partial(_merge_route, k, h2, *refs)
    route_prev()


def _merge_mix(k, between, h_ref, ya_ref, yb_ref, yc_ref, yd_ref, x_ref, gt1_ref, wg_ref, wb_ref, wo_ref,
               g2_ref, sh2_ref, sc2_ref, rw_ref, rb_ref, xo_ref, h2_ref, comb_ref, dest_ref, meta_ref):
    rows = slice(MOE_TILE * k, MOE_TILE * (k + 1))
    h = h_ref[rows, :]
    d = x_ref.shape[1]
    merged = None
    for b, y_ref in enumerate((ya_ref, yb_ref, yc_ref, yd_ref)):
        gate = jax.nn.sigmoid(_dot(h, wg_ref[:, d * b:d * (b + 1)]))
        term = gate * _dot(y_ref[rows, :], wb_ref[b])
        merged = term if merged is None else merged + term
        if b == 0 and between is not None:
            between()
    xn = x_ref[rows, :] + gt1_ref[0] * _dot(merged.astype(BF16), wo_ref[...])
    xo_ref[rows, :] = xn
    ms = jnp.mean(xn * xn, axis=-1, keepdims=True)
    h2 = xn * lax.rsqrt(ms + EPS) * g2_ref[...]
    h2 = h2 * (1.0 + sc2_ref[0]) + sh2_ref[0]
    h2_ref[rows, :] = h2.astype(BF16)
    return h2


def _merge_route(k, h2, h_ref, ya_ref, yb_ref, yc_ref, yd_ref, x_ref, gt1_ref, wg_ref, wb_ref, wo_ref,
                 g2_ref, sh2_ref, sc2_ref, rw_ref, rb_ref, xo_ref, h2_ref, comb_ref, dest_ref, meta_ref):
    rows = slice(MOE_TILE * k, MOE_TILE * (k + 1))
    hh, hm = _split2(h2)
    wh = rw_ref[...].astype(BF16)
    logits = _dot(hh, wh) + _dot(hm, wh)
    lt = logits.T
    scores = jax.nn.sigmoid(lt[0:N_EXPERTS, :])
    sel = scores + rb_ref[...]
    per = N_EXPERTS // N_GROUPS
    srow = [sel[e:e + 1, :] for e in range(N_EXPERTS)]
    gsum = []
    for g in range(N_GROUPS):
        a, b_, c, e_ = srow[per * g:per * (g + 1)]
        gsum.append(jnp.maximum(jnp.maximum(jnp.maximum(a + b_, a + c), jnp.maximum(a + e_, b_ + c)),
                                jnp.maximum(b_ + e_, c + e_)))
    best = jnp.maximum(jnp.maximum(gsum[0], gsum[1]), jnp.maximum(gsum[2], gsum[3]))
    taken = None
    weights = []
    picks = []
    for g in range(N_GROUPS):
        hit = gsum[g] == best
        pick = hit if taken is None else hit & jnp.logical_not(taken)
        taken = hit if taken is None else taken | hit
        picks.append(pick.astype(F32))
        for e in range(per * g, per * (g + 1)):
            rank = jnp.zeros_like(best)
            for o in range(per * g, per * (g + 1)):
                if o == e:
                    continue
                ahead = (srow[o] > srow[e]) | ((srow[o] == srow[e]) & (o < e))
                rank = rank + ahead.astype(F32)
            weights.append(jnp.where(pick & (rank < 1.5), scores[e:e + 1, :], 0.0))
    wsum = weights[0]
    for r_ in weights[1:]:
        wsum = wsum + r_
    inv = 1.0 / wsum
    rid = lax.broadcasted_iota(jnp.int32, scores.shape, 0)
    comb_e = jnp.zeros_like(scores)
    for e, r_ in enumerate(weights):
        comb_e = jnp.where(rid == e, r_ * inv, comb_e)

    tm = lt.shape[1]
    gid = lax.broadcasted_iota(jnp.int32, (8, tm), 0)
    onehot = jnp.zeros((8, tm), F32)
    for g in range(N_GROUPS):
        onehot = jnp.where(gid == g, picks[g], onehot)
    before = (lax.broadcasted_iota(jnp.int32, (tm, tm), 0) < lax.broadcasted_iota(jnp.int32, (tm, tm), 1))
    rank = _dot(onehot.astype(BF16), before.astype(BF16))
    dest = jnp.zeros((1, tm), F32)
    off = jnp.zeros((1, 1), F32)
    meta = jnp.zeros((8, LANE), F32)
    mrow = lax.broadcasted_iota(jnp.int32, (8, LANE), 0)
    for g in range(N_GROUPS):
        cnt = jnp.sum(picks[g], axis=-1, keepdims=True)
        dest = dest + picks[g] * (off + rank[g:g + 1, :])
        meta = jnp.where(mrow == g, cnt, meta)
        meta = jnp.where(mrow == N_GROUPS + g, off, meta)
        off = off + jnp.ceil(cnt * (1.0 / MOE_ALIGN)) * MOE_ALIGN
    dest_ref[k] = dest.astype(jnp.int32)
    meta_ref[k] = meta.astype(jnp.int32)
    pad = jnp.zeros((LANE - N_EXPERTS - 8, tm), F32)
    comb_t = jnp.concatenate([comb_e, jnp.where(gid == 0, dest, 0.0), pad], axis=0)
    comb_ref[rows, :] = comb_t.T


def _merge(h, ys, x2, gt1, wg, wb, wo, g2, sh2, sc2, rw, rb, seq):
    t, d = x2.shape
    sub = MERGE_SUBTILES
    tm = sub * MOE_TILE
    tpb = seq // tm
    row = lambda i: (i, 0)
    per_b = lambda i: (i // tpb, 0, 0)
    c2 = lambda i: (0, 0)
    once = pl.Buffered(1)
    return pl.pallas_call(
        _merge_kernel,
        grid=(t // tm,),
        in_specs=[pl.BlockSpec((tm, d), row)] + [pl.BlockSpec((tm, 512), row)] * 4
                 + [pl.BlockSpec((tm, d), row), pl.BlockSpec((1, 1, d), per_b),
                    pl.BlockSpec(wg.shape, c2, pipeline_mode=once),
                    pl.BlockSpec(wb.shape, lambda i: (0, 0, 0), pipeline_mode=once),
                    pl.BlockSpec(wo.shape, c2, pipeline_mode=once), pl.BlockSpec((1, d), c2),
                    pl.BlockSpec((1, 1, d), per_b), pl.BlockSpec((1, 1, d), per_b),
                    pl.BlockSpec(rw.shape, c2), pl.BlockSpec(rb.shape, c2)],
        out_specs=[pl.BlockSpec((tm, d), row), pl.BlockSpec((tm, d), row), pl.BlockSpec((tm, LANE), row),
                   pl.BlockSpec((sub, 1, MOE_TILE), lambda i: (i, 0, 0)),
                   pl.BlockSpec((sub, 8, LANE), lambda i: (i, 0, 0))],
        out_shape=[jax.ShapeDtypeStruct((t, d), F32), jax.ShapeDtypeStruct((t, d), BF16),
                   jax.ShapeDtypeStruct((t, LANE), F32),
                   jax.ShapeDtypeStruct((t // MOE_TILE, 1, MOE_TILE), jnp.int32),
                   jax.ShapeDtypeStruct((t // MOE_TILE, 8, LANE), jnp.int32)],
        compiler_params=_cp(("arbitrary",)),
        name="merge_router",
    )(h, *ys, x2, gt1, wg, wb, wo, g2, sh2, sc2, rw, rb)


def _moe_kernel(meta_ref, h2_ref, comb_ref, dest_ref, x_ref, gt2_ref, wg_ref, wu_ref, wd_ref, o_ref,
                sorted_s, csort_s, out_s):
    i = pl.program_id(0)
    tm = h2_ref.shape[0]
    n_rows = sorted_s.shape[0]
    per = N_EXPERTS // N_GROUPS
    comb = comb_ref[...]

    place = (lax.broadcasted_iota(jnp.int32, (n_rows, tm), 0) == dest_ref[0]).astype(BF16)
    sorted_s[...] = _dot(place, h2_ref[...]).astype(BF16)
    c_hi, c_lo = _split2(comb)
    csort_s[...] = _dot(place, c_hi) + _dot(place, c_lo)
    out_s[...] = jnp.zeros_like(out_s)
    def run_experts(g, start, size):
        rows = pl.ds(pl.multiple_of(start, MOE_ALIGN), size)
        xk = sorted_s[rows, :]
        cw = csort_s[rows, :]
        lane = lax.broadcasted_iota(jnp.int32, (size, LANE), 1)
        acc = None
        for j in range(per):
            e = per * g + j
            hid = _silu(_dot(xk, wg_ref[0, e])) * _dot(xk, wu_ref[0, e])
            ce = jnp.sum(jnp.where(lane == e, cw, 0.0), axis=-1, keepdims=True)
            term = _dot((hid * ce).astype(BF16), wd_ref[0, e])
            acc = term if acc is None else acc + term
        out_s[rows, :] += acc

    def group(g, carry):
        cnt = meta_ref[i, g]
        off = meta_ref[i, N_GROUPS + g]
        big = MOE_CHUNKS[-1]
        n_big = cnt // big

        def big_chunk(c, inner):
            run_experts(g, off + c * big, big)
            return inner

        lax.fori_loop(0, n_big, big_chunk, 0)
        rest = cnt - n_big * big
        lower = 0
        for size in MOE_CHUNKS:
            @pl.when((rest > lower) & (rest <= size))
            def _(size=size):
                run_experts(g, off + n_big * big, size)
            lower = size
        return carry

    lax.fori_loop(0, N_GROUPS, group, 0)

    dest_col = comb[:, MOE_DEST_LANE:MOE_DEST_LANE + 1].astype(jnp.int32)
    back = (lax.broadcasted_iota(jnp.int32, (tm, n_rows), 1) == dest_col).astype(BF16)
    o_ref[...] = x_ref[...] + gt2_ref[0] * _dot(back, out_s[...].astype(BF16))


def _moe(h2, comb, dest, meta, x2, gt2, wg, wu, wd, layer, seq):
    t, d = x2.shape
    tm = MOE_TILE
    tpb = seq // tm
    n_rows = MOE_SORT_ROWS
    overhang = max(b - a for a, b in zip((0,) + MOE_CHUNKS, MOE_CHUNKS)) - 1
    assert tm + N_GROUPS * (MOE_ALIGN - 1) + overhang <= n_rows
    row = lambda i, m: (i, 0)
    whole = lambda a: pl.BlockSpec((1,) + a.shape[1:], lambda i, m: (layer, 0, 0, 0),
                                   pipeline_mode=pl.Buffered(1))
    grid_spec = pltpu.PrefetchScalarGridSpec(
        num_scalar_prefetch=1,
        grid=(t // tm,),
        in_specs=[pl.BlockSpec((tm, d), row), pl.BlockSpec((tm, LANE), row),
                  pl.BlockSpec((1, 1, tm), lambda i, m: (i, 0, 0)),
                  pl.BlockSpec((tm, d), row),
                  pl.BlockSpec((1, 1, d), lambda i, m: (i // tpb, 0, 0)),
                  whole(wg), whole(wu), whole(wd)],
        out_specs=pl.BlockSpec((tm, d), row),
        scratch_shapes=[pltpu.VMEM((n_rows, d), BF16), pltpu.VMEM((n_rows, LANE), F32),
                        pltpu.VMEM((n_rows, d), F32)])
    return pl.pallas_call(
        _moe_kernel,
        grid_spec=grid_spec,
        out_shape=jax.ShapeDtypeStruct((t, d), F32),
        compiler_params=_cp(("arbitrary",)),
        name="moe",
    )(meta, h2, comb, dest, x2, gt2, wg, wu, wd)


def _layer_params(l, w_in, hg_onorm, mla_q_norm, mla_kv_norm, mla_w_uq, mla_w_ukv, mla_qk_norm,
                  diff_qk_norm, swa_qk_norm, swa_sinks, lb_all):
    ends = [sum(IN_SPLITS[:i]) for i in range(len(IN_SPLITS) + 1)]
    cols = lambda a, b: w_in[l, :, ends[a]:ends[b]].astype(BF16)
    mla_pad = 512 - (ends[7] - ends[4])
    p = {"w_hg": cols(0, 4),
         "w_mla": jnp.pad(cols(4, 7), ((0, 0), (0, mla_pad))),
         "w_diff": cols(7, 10), "w_swa": cols(10, 13), "wg": cols(13, 14)}

    lb = lb_all[l]
    p["loglb"] = jnp.log(lb)[None, :]
    p["log1mlb"] = jnp.log1p(-lb)[None, :]
    p["ogain"] = jnp.tile(hg_onorm[l], HG_HEADS)[None, :]

    hd = MLA_NOPE + MLA_ROPE
    half = MLA_ROPE // 2
    wq = mla_w_uq[l].reshape(MLA_Q_RANK, MLA_HEADS, hd)
    z = lambda r, n: jnp.zeros((r, MLA_HEADS, n), F32)
    nope, rope = wq[:, :, :MLA_NOPE], wq[:, :, MLA_NOPE:]
    p["wqa"] = jnp.concatenate([nope, rope, z(MLA_Q_RANK, 32)], -1).reshape(MLA_Q_RANK, -1).astype(BF16)
    wkv = mla_w_ukv[l].reshape(MLA_KV_RANK, MLA_HEADS, MLA_NOPE + MLA_V)
    knope, vproj = wkv[:, :, :MLA_NOPE], wkv[:, :, MLA_NOPE:]
    eye = jnp.eye(MLA_ROPE, dtype=F32)
    place = lambda m: jnp.broadcast_to(
        jnp.concatenate([jnp.zeros((MLA_ROPE, MLA_NOPE), F32), m, jnp.zeros((MLA_ROPE, 32), F32)], -1)[:, None, :],
        (MLA_ROPE, MLA_HEADS, LANE))
    pad_rows = 256 - MLA_KV_RANK - MLA_ROPE
    p["wka"] = jnp.concatenate([jnp.concatenate([knope, z(MLA_KV_RANK, 64)], -1), place(eye),
                                z(pad_rows, LANE)], 0).reshape(256, -1).astype(BF16)
    p["wv"] = jnp.concatenate([vproj.reshape(MLA_KV_RANK, -1),
                               jnp.zeros((256 - MLA_KV_RANK, MLA_HEADS * MLA_V), F32)], 0).astype(BF16)
    p["qng"] = mla_q_norm[l][None, :]
    p["kvg"] = jnp.concatenate([mla_kv_norm[l], jnp.ones((256 - MLA_KV_RANK,), F32)])[None, :]

    def rope_gains(g):
        base = jnp.concatenate([g, jnp.zeros((LANE - hd,), F32)])
        part = jnp.concatenate([jnp.zeros((MLA_NOPE,), F32), g[MLA_NOPE + half:], g[MLA_NOPE:MLA_NOPE + half],
                                jnp.zeros((LANE - hd,), F32)])
        return base[None, :], part[None, :]

    p["gq"], p["gqs"] = rope_gains(mla_qk_norm[l, 0])
    p["gk"], p["gks"] = rope_gains(mla_qk_norm[l, 1])
    p["dgq"] = jnp.tile(diff_qk_norm[l, 0], 8)[None, :]
    p["dgk"] = jnp.tile(diff_qk_norm[l, 1], 8)[None, :]
    p["sgq"] = jnp.tile(swa_qk_norm[l, 0], 8)[None, :]
    p["sgk"] = jnp.tile(swa_qk_norm[l, 1], 2)[None, :]
    p["sinks"] = jnp.repeat(swa_sinks[l].astype(F32) * LOG2E, SWA_WINDOW)[None, :]
    return p


def _rope_tables(positions):
    inv_freq = ROPE_BASE ** (-jnp.arange(0, MLA_ROPE, 2, dtype=F32) / MLA_ROPE)
    zeros = lambda n: jnp.zeros((n,), F32)
    half = MLA_ROPE // 2
    pad = LANE - MLA_NOPE - MLA_ROPE
    freq = jnp.concatenate([zeros(MLA_NOPE), inv_freq, inv_freq, zeros(pad)])
    keep = jnp.concatenate([jnp.ones((MLA_NOPE + MLA_ROPE,), F32), zeros(pad)])
    sign = jnp.concatenate([zeros(MLA_NOPE), -jnp.ones((half,), F32), jnp.ones((half,), F32), zeros(pad)])
    ang = positions.astype(F32).reshape(-1)[:, None] * freq
    return jnp.cos(ang) * keep, jnp.sin(ang) * sign


def kernel(x, c, positions, ada_w, ada_b, norm_mix, norm_ffn, w_in, hg_lb_logits, hg_onorm, mla_q_norm, mla_kv_norm, mla_w_uq, mla_w_ukv, mla_qk_norm, diff_qk_norm, diff_lam, diff_onorm, swa_qk_norm, swa_sinks, w_branch, w_out, router_w, router_bias, moe_w_gate, moe_w_up, moe_w_down):
    batch, seq, d = x.shape
    x2 = x.reshape(batch * seq, d)
    cosf, sinf = _rope_tables(positions)
    lb_all = jnp.cumsum(jax.nn.softmax(hg_lb_logits.astype(F32), axis=0), axis=0)
    lb_all = lb_all - lb_all[0]
    mod = _modulation(c, ada_w, ada_b)
    rw = jnp.concatenate([router_w, jnp.zeros((d, LANE - N_EXPERTS), F32)], axis=1)
    rb = router_bias.astype(F32)[:, None]
    moe_w = (moe_w_gate.astype(BF16), moe_w_up.astype(BF16), moe_w_down.astype(BF16))

    for l in range(DEPTH):
        sh1, sc1, gt1, sh2, sc2, gt2 = [mod[l, :, d * k:d * (k + 1)][:, None, :] for k in range(6)]
        p = _layer_params(l, w_in, hg_onorm, mla_q_norm, mla_kv_norm, mla_w_uq, mla_w_ukv, mla_qk_norm,
                          diff_qk_norm, swa_qk_norm, swa_sinks, lb_all)
        hg4, lf, h, qm, km, vmt, qd, kd, vdt, qs, ks, vst = _inproj(
            x2, norm_mix[l][None, :], sh1, sc1, cosf, sinf, p, seq)
        y_a = _hgrn(hg4, lf, p["ogain"], batch, seq)
        y_b = _mla_attn(qm, km, vmt, batch, seq)
        lam_init = 0.8 - 0.6 * math.exp(-0.3 * l)
        y_c = _diff_attn(qd, kd, vdt, diff_lam[l], diff_onorm[l][None, :], lam_init, batch, seq)
        y_d = _swa(qs, ks, vst, p["sinks"], batch, seq)
        x2, h2, comb, dest, meta = _merge(h, (y_a, y_b, y_c, y_d), x2, gt1, p["wg"], w_branch[l].astype(BF16),
                                          w_out[l].astype(BF16), norm_ffn[l][None, :], sh2, sc2, rw, rb, seq)
        x2 = _moe(h2, comb, dest, meta[:, :, 0], x2, gt2, *moe_w, l, seq)
    return x2.reshape(batch, seq, d)
```

```python
import functools
import math

import jax
import jax.numpy as jnp
from jax import lax
from jax.experimental import pallas as pl
from jax.experimental.pallas import tpu as pltpu

F32 = jnp.float32
BF16 = jnp.bfloat16

D_MODEL = 1024
DEPTH = 2
EPS = 1e-6
N_BRANCH = 4
HG_HEADS = 8
HG_DK = 64
HG_W = HG_HEADS * HG_DK
HG_SUB = 16
HG_CHUNK = 64
HG_SAFE_DECAY = 80.0
HG_DIAG_ROWS =(HG_SUB // 2) * HG_SUB + (HG_SUB // 2) ** 2
MLA_HEADS = 8
MLA_Q_RANK = 256
MLA_KV_RANK = 128
MLA_NOPE = 64
MLA_ROPE = 32
MLA_V = 64
ROPE_BASE = 10000.0
DIFF_HEADS = 4
DIFF_QK = 64
DIFF_V = 128
SWA_Q_HEADS = 8
SWA_KV_HEADS = 2
SWA_WINDOW = 128
HEAD_DIM = 64
N_EXPERTS = 16
N_GROUPS = 4
D_FF_EXPERT = 256
IN_SPLITS = (512, 512, 512, 512, 256, 128, 32, 512, 512, 512, 512, 128, 128, 4096)

MOD_COLS = 1536
ROW_TILE = 512
HG_BLOCK = 256
MOE_TILE = 512
MERGE_SUBTILES = 2
MOE_ALIGN = 16
MOE_CHUNKS = (64, 96, 128, 160, 192, 224, 256)
MOE_SORT_ROWS = 640
MOE_DEST_LANE = N_EXPERTS
LANE = 128
ATT_BLK = 256
SWA_QB = 16
LOG2E = 1.4426950408889634
NEG = -1e30
VMEM_LIMIT = 56 * 1024 * 1024


def _cp(sem, vmem=VMEM_LIMIT):
    return pltpu.CompilerParams(dimension_semantics=sem, vmem_limit_bytes=vmem)


def _nt(a, b):
    return lax.dot_general(a, b, (((1,), (1,)), ((), ())), preferred_element_type=F32)


def _tn(a, b):
    return lax.dot_general(a, b, (((0,), (0,)), ((), ())), preferred_element_type=F32)


def _dot(a, b):
    return jnp.dot(a, b, preferred_element_type=F32)


def _split2(x):
    hi = x.astype(BF16)
    lo = (x - hi.astype(F32)).astype(BF16)
    return hi, lo


def _seg_id(idx, seg):
    shift = seg.bit_length() - 1
    assert 1 << shift == seg
    return lax.shift_right_logical(idx, shift)


def _same_seg(n, seg):
    r = lax.broadcasted_iota(jnp.int32, (n, n), 0)
    c = lax.broadcasted_iota(jnp.int32, (n, n), 1)
    return _seg_id(r, seg) == _seg_id(c, seg)


def _seg_ones(n, seg):
    return _same_seg(n, seg).astype(BF16)


def _seg_mean_sq(x, seg):
    n = x.shape[-1]
    return _dot((x * x).astype(BF16), _seg_ones(n, seg)) * (1.0 / seg)


def _silu(x):
    return x * jax.nn.sigmoid(x)


def _mod_kernel(c_ref, w_ref, b_ref, o_ref):
    hi, lo = _split2(_silu(c_ref[...]))
    w = w_ref[0].astype(BF16)
    o_ref[0] = _dot(hi, w) + _dot(lo, w) + b_ref[0]


def _modulation(c, ada_w, ada_b):
    nl, d, n6 = ada_w.shape
    b = c.shape[0]
    tn = MOD_COLS
    return pl.pallas_call(
        _mod_kernel,
        grid=(nl, n6 // tn),
        in_specs=[pl.BlockSpec((b, d), lambda l, j: (0, 0)),
                  pl.BlockSpec((1, d, tn), lambda l, j: (l, 0, j)),
                  pl.BlockSpec((1, 1, tn), lambda l, j: (l, 0, j))],
        out_specs=pl.BlockSpec((1, b, tn), lambda l, j: (l, 0, j)),
        out_shape=jax.ShapeDtypeStruct((nl, b, n6), F32),
        compiler_params=_cp(("arbitrary", "arbitrary")),
        name="modulation",
    )(c, ada_w, ada_b.reshape(nl, 1, n6))


N_PREP_CONSTS = 13


def _inproj_kernel(x_ref, g_ref, sh_ref, sc_ref, loglb_ref, log1mlb_ref, whg_ref, wmla_ref, wdiff_ref, wswa_ref,
                   cos_ref, sin_ref, *rest):
    prep_consts, (ohg_ref, olf_ref, oh_ref), prep_outs = (
        rest[:N_PREP_CONSTS], rest[N_PREP_CONSTS:N_PREP_CONSTS + 3], rest[N_PREP_CONSTS + 3:])
    x = x_ref[...]
    ms = jnp.mean(x * x, axis=-1, keepdims=True)
    h = x * lax.rsqrt(ms + EPS) * g_ref[...]
    h = h * (1.0 + sc_ref[0]) + sh_ref[0]
    hb = h.astype(BF16)
    oh_ref[...] = hb

    def proj(w_ref, lo, hi):
        return _dot(hb, w_ref[:, lo:hi])

    def hg_forget():
        fr = proj(whg_ref, 512, 1024)
        ls = jnp.minimum(fr, 0.0) - jnp.log(1.0 + jnp.exp(-jnp.abs(fr)))
        a = loglb_ref[...]
        c2 = log1mlb_ref[...] + ls
        lf = jnp.maximum(a, c2) + jnp.log(1.0 + jnp.exp(-jnp.abs(a - c2)))
        olf_ref[...] = lf
        ohg_ref[:, 1536:2048] = (1.0 - jnp.exp(lf)).astype(BF16)

    def hg_query():
        ohg_ref[:, 0:512] = _silu(proj(whg_ref, 0, 512)).astype(BF16)

    def hg_input():
        ohg_ref[:, 512:1024] = proj(whg_ref, 1024, 1536).astype(BF16)

    def hg_gate():
        ohg_ref[:, 1024:1536] = proj(whg_ref, 1536, 2048).astype(BF16)

    proj.weights = (wmla_ref, wdiff_ref, wswa_ref)
    proj.interleave = (hg_forget, hg_query, hg_input, hg_gate)
    _attn_prep(proj, cos_ref, sin_ref, *prep_consts, *prep_outs)


def _inproj(x2, gain, sh, sc, cosf, sinf, p, seq):
    t, d = x2.shape
    tm = ROW_TILE
    tpb = seq // tm
    row = lambda i: (i, 0)
    per_b = lambda i: (i // tpb, 0, 0)
    const = lambda a: pl.BlockSpec(a.shape, lambda i: (0,) * a.ndim)
    weights = [p["w_hg"], p["w_mla"], p["w_diff"], p["w_swa"]]
    prep_consts = [p["qng"], p["kvg"], p["wqa"], p["wka"], p["wv"],
                   p["gq"], p["gqs"], p["gk"], p["gks"], p["dgq"], p["dgk"], p["sgq"], p["sgk"]]
    assert len(prep_consts) == N_PREP_CONSTS

    def rows_out(w, dt=BF16):
        return pl.BlockSpec((tm, w), row), jax.ShapeDtypeStruct((t, w), dt)

    def transposed_out(n, blk):
        return (pl.BlockSpec((tm // blk, n, blk), lambda i: (i, 0, 0)),
                jax.ShapeDtypeStruct((t // blk, n, blk), BF16))

    outs = [rows_out(2048), rows_out(512, F32), rows_out(d),
            rows_out(1024), rows_out(1024), transposed_out(512, ATT_BLK),
            rows_out(512), rows_out(512), transposed_out(512, ATT_BLK),
            rows_out(512), rows_out(256), transposed_out(LANE, SWA_WINDOW)]
    return pl.pallas_call(
        _inproj_kernel,
        grid=(t // tm,),
        in_specs=[pl.BlockSpec((tm, d), row), const(gain),
                  pl.BlockSpec((1, 1, d), per_b),
                  pl.BlockSpec((1, 1, d), per_b), const(p["loglb"]), const(p["log1mlb"])]
                 + [const(w) for w in weights]
                 + [pl.BlockSpec((tm, LANE), row), pl.BlockSpec((tm, LANE), row)]
                 + [const(a) for a in prep_consts],
        out_specs=[o[0] for o in outs],
        out_shape=[o[1] for o in outs],
        compiler_params=_cp(("arbitrary",)),
        name="inproj",
    )(x2, gain, sh, sc, p["loglb"], p["log1mlb"], *weights, cosf, sinf, *prep_consts)


def _segment_cumsum(x, seg):
    n = x.shape[0]
    r = lax.broadcasted_iota(jnp.int32, (n, n), 0)
    cc = lax.broadcasted_iota(jnp.int32, (n, n), 1)
    same = _same_seg(n, seg)
    tri = (same & (cc <= r)).astype(BF16)
    blk = same.astype(BF16)
    hi, lo = _split2(x)
    return _dot(tri, hi) + _dot(tri, lo), _dot(blk, hi) + _dot(blk, lo)


def _hgrn_chunk_path(i_ref, st_ref, c_s, tot_s, qs_s, kk_s, qe_s, kd_s, ke_s, dec_s, od_s):
    rows_blk = c_s.shape[0]
    tot = tot_s[...]
    rel = c_s[...] - 0.5 * tot
    half_dec = jnp.exp(0.5 * tot)
    kd = kk_s[...] * jnp.exp(-rel)
    qe_s[...] = (qs_s[...] * jnp.exp(rel)).astype(BF16)
    kd_s[...] = kd.astype(BF16)
    ke_s[...] = (kd * half_dec).astype(BF16)
    dec_s[...] = half_dec

    row = lax.broadcasted_iota(jnp.int32, (2 * rows_blk, rows_blk), 0) & (rows_blk - 1)
    col = lax.broadcasted_iota(jnp.int32, (2 * rows_blk, rows_blk), 1)
    intra = (_seg_id(row, HG_CHUNK) == _seg_id(col, HG_CHUNK)) & (col <= row)
    low_q = lax.broadcasted_iota(jnp.int32, (rows_blk, LANE), 1) < HG_DK
    low_c = lax.broadcasted_iota(jnp.int32, (HG_CHUNK, LANE), 1) < HG_DK

    for j in range(HG_W // LANE):
        cols = slice(LANE * j, LANE * (j + 1))
        qe = qe_s[:, cols]
        zero = jnp.zeros_like(qe)
        q2 = jnp.concatenate([jnp.where(low_q, qe, zero), jnp.where(low_q, zero, qe)], axis=0)
        attn = jnp.where(intra, _nt(q2, kd_s[:, cols]), 0.0).astype(BF16)
        o2 = _dot(attn, i_ref[:, cols])
        n_chunks = rows_blk // HG_CHUNK
        chunk_rows = [slice(HG_CHUNK * ch, HG_CHUNK * (ch + 1)) for ch in range(n_chunks)]
        upd = [_tn(i_ref[rows, cols], ke_s[rows, cols]) for rows in chunk_rows]
        st = st_ref[j]
        for ch, rows in enumerate(chunk_rows):
            rows_hi = slice(rows_blk + HG_CHUNK * ch, rows_blk + HG_CHUNK * (ch + 1))
            hd = dec_s[HG_CHUNK * ch:HG_CHUNK * ch + 1, cols]
            inter = _nt(jnp.concatenate([q2[rows], q2[rows_hi]], axis=0), (st * hd).astype(BF16))
            od_s[rows, cols] = jnp.where(low_c, o2[rows] + inter[:HG_CHUNK], o2[rows_hi] + inter[HG_CHUNK:])
            st = st * (hd * hd) + upd[ch]
        st_ref[j] = st


def _hgrn_exact_path(i_ref, st_ref, lf_s, c_s, qs_s, kk_s, qe_s, ke_s, dec_s, od_s, t_s, a_s):
    rows_blk = c_s.shape[0]
    n_sub = rows_blk // HG_SUB
    c, tot = _segment_cumsum(lf_s[...], HG_SUB)
    c_s[...] = c
    qe_s[...] = (qs_s[...] * jnp.exp(c)).astype(BF16)
    ke_s[...] = (kk_s[...] * jnp.exp(tot - c)).astype(BF16)
    dec_s[...] = jnp.exp(tot)

    same_head = _same_seg(LANE, HG_DK)
    head_mask = same_head.astype(F32)
    head_ones = same_head.astype(BF16)
    for j in range(HG_W // LANE):
        st_ref[j] = st_ref[j] * head_mask
    half = HG_SUB // 2
    trow = lax.broadcasted_iota(jnp.int32, (half, HG_W), 0)

    def body(i, carry):
        r0 = pl.multiple_of(i * HG_SUB, HG_SUB)
        rows = pl.ds(r0, HG_SUB)
        c_i = c_s[rows, :]
        qs_i = qs_s[rows, :]
        kk_i = kk_s[rows, :]
        v_i = i_ref[rows, :].astype(F32)
        c_lo, c_hi = c_i[:half], c_i[half:]
        q_lo, q_hi = qs_i[:half], qs_i[half:]
        for s in range(half):
            c_row, k_row = c_i[s:s + 1, :], kk_i[s:s + 1, :]
            e_lo = jnp.exp(jnp.where(trow >= s, c_lo - c_row, NEG))
            e_hi = jnp.exp(c_hi - c_row)
            both = jnp.concatenate([e_lo * q_lo, e_hi * q_hi], axis=0) * k_row
            t_s[s * HG_SUB:(s + 1) * HG_SUB, :] = both.astype(BF16)
        for s in range(half, HG_SUB, 2):
            pair = []
            for u in (s, s + 1):
                e_hi = jnp.exp(jnp.where(trow >= u - half, c_hi - c_i[u:u + 1, :], NEG))
                pair.append(e_hi * q_hi * kk_i[u:u + 1, :])
            base = half * HG_SUB + (s - half) * half
            t_s[base:base + HG_SUB, :] = jnp.concatenate(pair, axis=0).astype(BF16)
        for j in range(HG_W // LANE):
            cols = slice(LANE * j, LANE * (j + 1))
            a_s[:, cols] = _dot(t_s[:, cols], head_ones)
        acc_lo = jnp.zeros((half, HG_W), F32)
        acc_hi = jnp.zeros((half, HG_W), F32)
        for s in range(half):
            acc_lo = acc_lo + a_s[s * HG_SUB:s * HG_SUB + half, :] * v_i[s:s + 1, :]
            acc_hi = acc_hi + a_s[s * HG_SUB + half:(s + 1) * HG_SUB, :] * v_i[s:s + 1, :]
        for s in range(half, HG_SUB):
            base = half * HG_SUB + (s - half) * half
            acc_hi = acc_hi + a_s[base:base + half, :] * v_i[s:s + 1, :]
        acc = jnp.concatenate([acc_lo, acc_hi], axis=0)
        for j in range(HG_W // LANE):
            cols = slice(LANE * j, LANE * (j + 1))
            st = st_ref[j]
            o_int = _nt(qe_s[rows, cols], st.astype(BF16))
            upd = _tn(i_ref[rows, cols], ke_s[rows, cols])
            st_ref[j] = st * dec_s[pl.ds(r0, 1), cols] + upd * head_mask
            od_s[rows, cols] = acc[:, cols] + o_int
        return carry

    lax.fori_loop(0, n_sub, body, 0)


def _hgrn_kernel(q_ref, i_ref, g_ref, k_ref, lf_ref, og_ref, o_ref,
                 st_ref, lf_s, c_s, tot_s, qs_s, kk_s, qe_s, kd_s, ke_s, dec_s, od_s, t_s, a_s):
    @pl.when(pl.program_id(1) == 0)
    def _():
        st_ref[...] = jnp.zeros_like(st_ref)

    lf = lf_ref[...]
    lf_s[...] = lf
    qs_s[...] = q_ref[...].astype(F32)
    kk_s[...] = k_ref[...].astype(F32)
    c, tot = _segment_cumsum(lf, HG_CHUNK)
    c_s[...] = c
    tot_s[...] = tot
    safe = 0.5 * jnp.max(-tot) <= HG_SAFE_DECAY

    @pl.when(safe)
    def _():
        _hgrn_chunk_path(i_ref, st_ref, c_s, tot_s, qs_s, kk_s, qe_s, kd_s, ke_s, dec_s, od_s)

    @pl.when(jnp.logical_not(safe))
    def _():
        _hgrn_exact_path(i_ref, st_ref, lf_s, c_s, qs_s, kk_s, qe_s, ke_s, dec_s, od_s, t_s, a_s)

    o = od_s[...]
    ms = _seg_mean_sq(o, HG_DK)
    on = o * lax.rsqrt(ms + EPS) * og_ref[...]
    o_ref[...] = (on * _silu(g_ref[...].astype(F32))).astype(BF16)


def _hgrn(hg4, lf, ogain, batch, seq):
    t = lf.shape[0]
    rb = HG_BLOCK
    nb = seq // rb
    blk = lambda k: pl.BlockSpec((rb, HG_W), lambda b, n, k=k: (b * nb + n, k))
    vec = pl.BlockSpec((1, HG_W), lambda b, n: (0, 0))
    f32_blk = pltpu.VMEM((rb, HG_W), F32)
    bf16_blk = pltpu.VMEM((rb, HG_W), BF16)
    return pl.pallas_call(
        _hgrn_kernel,
        grid=(batch, nb),
        in_specs=[blk(0), blk(1), blk(2), blk(3), blk(0), vec],
        out_specs=blk(0),
        out_shape=jax.ShapeDtypeStruct((t, HG_W), BF16),
        scratch_shapes=[pltpu.VMEM((HG_W // LANE, LANE, LANE), F32),
                        f32_blk, f32_blk, f32_blk, f32_blk, f32_blk,
                        bf16_blk, bf16_blk, bf16_blk,
                        f32_blk, f32_blk,
                        pltpu.VMEM((HG_DIAG_ROWS, HG_W), BF16),
                        pltpu.VMEM((HG_DIAG_ROWS, HG_W), F32)],
        compiler_params=_cp(("arbitrary", "arbitrary")),
        name="hgrn2",
    )(hg4, hg4, hg4, hg4, lf, ogain)


def _store_transposed_blocks(out_ref, v):
    blk = out_ref.shape[2]
    for u in range(out_ref.shape[0]):
        out_ref[u] = v[u * blk:(u + 1) * blk, :].T.astype(BF16)


def _attn_prep(proj, cos_ref, sin_ref,
               qng_ref, kvg_ref, wqa_ref, wka_ref, wv_ref,
               gq_ref, gqs_ref, gk_ref, gks_ref, dgq_ref, dgk_ref, sgq_ref, sgk_ref,
               qm_ref, km_ref, vmt_ref, qd_ref, kd_ref, vdt_ref, qs_ref, ks_ref, vst_ref):
    wmla_ref, wdiff_ref, wswa_ref = proj.weights
    blk = proj(wmla_ref, 0, 512)
    cq = blk[:, :MLA_Q_RANK]
    rest = blk[:, MLA_Q_RANK:]
    cqn = cq * lax.rsqrt(jnp.mean(cq * cq, axis=-1, keepdims=True) + EPS) * qng_ref[...]
    lane = lax.broadcasted_iota(jnp.int32, rest.shape, 1)
    is_kv = lane < MLA_KV_RANK
    ms_kv = jnp.sum(jnp.where(is_kv, rest * rest, 0.0), axis=-1, keepdims=True) * (1.0 / MLA_KV_RANK)
    restn = jnp.where(is_kv, rest * lax.rsqrt(ms_kv + EPS) * kvg_ref[...], rest)
    cqb = cqn.astype(BF16)
    rb = restn.astype(BF16)
    qa = _dot(cqb, wqa_ref[...])
    ka = _dot(rb, wka_ref[...])
    _store_transposed_blocks(vmt_ref, _dot(rb, wv_ref[...]))
    first_half = lax.broadcasted_iota(jnp.int32, (blk.shape[0], LANE), 1) < MLA_NOPE + MLA_ROPE // 2

    def partner(x):
        return jnp.where(first_half, pltpu.roll(x, LANE - MLA_ROPE // 2, 1), pltpu.roll(x, MLA_ROPE // 2, 1))

    cosf = cos_ref[...]
    sinf = sin_ref[...]
    cq_t = cosf * gq_ref[...]
    sq_t = sinf * gqs_ref[...]
    ck_t = cosf * gk_ref[...]
    sk_t = sinf * gks_ref[...]
    inv_n = 1.0 / (MLA_NOPE + MLA_ROPE)
    scale = (MLA_NOPE + MLA_ROPE) ** -0.5 * LOG2E
    def seg_norm(x, gain, scale):
        return x * lax.rsqrt(_seg_mean_sq(x, HEAD_DIM) + EPS) * (gain * scale)

    def diff_q():
        qd_ref[...] = seg_norm(proj(wdiff_ref, 0, 512), dgq_ref[...], DIFF_QK ** -0.5 * LOG2E).astype(BF16)

    def diff_k():
        kd_ref[...] = seg_norm(proj(wdiff_ref, 512, 1024), dgk_ref[...], 1.0).astype(BF16)

    def diff_v():
        _store_transposed_blocks(vdt_ref, proj(wdiff_ref, 1024, 1536))

    def swa_q():
        qs_ref[...] = seg_norm(proj(wswa_ref, 0, 512), sgq_ref[...], HEAD_DIM ** -0.5 * LOG2E).astype(BF16)

    def swa_kv():
        skv = proj(wswa_ref, 512, 768)
        kn = seg_norm(skv[:, :LANE], sgk_ref[...], 1.0)
        low = lax.broadcasted_iota(jnp.int32, kn.shape, 1) < HEAD_DIM
        sw = pltpu.roll(kn, HEAD_DIM, 1)
        ks_ref[:, :LANE] = jnp.where(low, kn, sw).astype(BF16)
        ks_ref[:, LANE:] = jnp.where(low, sw, kn).astype(BF16)
        _store_transposed_blocks(vst_ref, skv[:, LANE:])

    pending = list(proj.interleave) + [diff_q, diff_k, diff_v, swa_q, swa_kv]
    for h in range(MLA_HEADS):
        cols = slice(LANE * h, LANE * (h + 1))
        x = qa[:, cols]
        rinv = lax.rsqrt(jnp.sum(x * x, axis=-1, keepdims=True) * inv_n + EPS)
        qm_ref[:, cols] = ((x * cq_t + partner(x) * sq_t) * (rinv * scale)).astype(BF16)
        y = ka[:, cols]
        rinv = lax.rsqrt(jnp.sum(y * y, axis=-1, keepdims=True) * inv_n + EPS)
        km_ref[:, cols] = ((y * ck_t + partner(y) * sk_t) * rinv).astype(BF16)
        if pending:
            pending.pop(0)()
    for work in pending:
        work()


def _causal_t(blk):
    key = lax.broadcasted_iota(jnp.int32, (blk, blk), 0)
    qry = lax.broadcasted_iota(jnp.int32, (blk, blk), 1)
    return key <= qry


def _two_pass_attention(n_sets, score_fn, value_fn, s_scr, acc_scr, blk):
    qi = pl.program_id(1)
    causal = _causal_t(blk)

    def scores(ki, m, masked):
        out = []
        for i in range(n_sets):
            s = score_fn(i, ki)
            if masked:
                s = jnp.where(causal, s, NEG)
            s_scr[i, ki] = s
            out.append(jnp.maximum(m[i], jnp.max(s, axis=0, keepdims=True)))
        return tuple(out)

    def blocked(n, step, carry):
        def many(k0, count, c):
            return step(tuple(k0 + u for u in range(count)), c)
        carry = lax.fori_loop(0, n // 4, lambda kp, c: many(4 * kp, 4, c), carry)
        done = (n // 4) * 4
        carry = lax.cond(n - done >= 2, lambda c: many(done, 2, c), lambda c: c, carry)
        done = (n // 2) * 2
        return lax.cond(n - done == 1, lambda c: many(done, 1, c), lambda c: c, carry)

    def scores_step(kis, m):
        for ki in kis:
            m = scores(ki, m, False)
        return m

    m = tuple(jnp.full((1, blk), NEG, F32) for _ in range(n_sets))
    m = blocked(qi, scores_step, m)
    m = scores(qi, m, True)

    acc_scr[...] = jnp.zeros_like(acc_scr)

    def accumulate(kis, l):
        out = []
        for i in range(n_sets):
            li, pv = l[i], None
            for ki in kis:
                p = jnp.exp2(s_scr[i, ki] - m[i])
                li = li + jnp.sum(p, axis=0, keepdims=True)
                term = _dot(value_fn(i, ki), p.astype(BF16))
                pv = term if pv is None else pv + term
            out.append(li)
            acc_scr[i] += pv
        return tuple(out)

    l = tuple(jnp.zeros((1, blk), F32) for _ in range(n_sets))
    return blocked(qi + 1, accumulate, l)


def _mla_attn_kernel(q_ref, k_ref, vt_ref, o_ref, s_scr, acc_scr):
    blk = q_ref.shape[0]

    def score_fn(h, ki):
        rows = pl.ds(pl.multiple_of(ki * blk, blk), blk)
        cols = slice(LANE * h, LANE * (h + 1))
        return _nt(k_ref[rows, cols], q_ref[:, cols])

    def value_fn(h, ki):
        return vt_ref[ki, MLA_V * h:MLA_V * (h + 1), :]

    l = _two_pass_attention(MLA_HEADS, score_fn, value_fn, s_scr, acc_scr, blk)
    for j in range(MLA_HEADS // 2):
        o_t = jnp.concatenate([acc_scr[2 * j] / l[2 * j], acc_scr[2 * j + 1] / l[2 * j + 1]], axis=0)
        o_ref[:, LANE * j:LANE * (j + 1)] = o_t.T.astype(BF16)


def _mla_attn(qm, km, vmt, batch, seq):
    t = qm.shape[0]
    nq = seq // ATT_BLK
    return pl.pallas_call(
        _mla_attn_kernel,
        grid=(batch, nq),
        in_specs=[pl.BlockSpec((ATT_BLK, 1024), lambda b, i: (b * nq + i, 0)),
                  pl.BlockSpec((seq, 1024), lambda b, i: (b, 0)),
                  pl.BlockSpec((nq, 512, ATT_BLK), lambda b, i: (b, 0, 0))],
        out_specs=pl.BlockSpec((ATT_BLK, 512), lambda b, i: (b * nq + i, 0)),
        out_shape=jax.ShapeDtypeStruct((t, 512), BF16),
        scratch_shapes=[pltpu.VMEM((MLA_HEADS, nq, ATT_BLK, ATT_BLK), F32),
                        pltpu.VMEM((MLA_HEADS, MLA_V, ATT_BLK), F32)],
        compiler_params=_cp(("arbitrary", "arbitrary")),
        name="mla_attn",
    )(qm, km, vmt)


def _diff_attn_kernel(q_ref, k_ref, vt_ref, lam_ref, og_ref, o_ref, s_scr, acc_scr, qm_scr, *, lam_init):
    blk = q_ref.shape[0]
    low = lax.broadcasted_iota(jnp.int32, (blk, LANE), 1) < DIFF_QK
    lp = lam_ref[...]
    lam = (jnp.exp(jnp.sum(lp[0:1] * lp[1:2], axis=-1, keepdims=True))
           - jnp.exp(jnp.sum(lp[2:3] * lp[3:4], axis=-1, keepdims=True)) + lam_init)

    for h in range(DIFF_HEADS):
        qt = q_ref[:, LANE * h:LANE * (h + 1)]
        zero = jnp.zeros_like(qt)
        qm_scr[2 * h] = jnp.where(low, qt, zero)
        qm_scr[2 * h + 1] = jnp.where(low, zero, qt)

    def score_fn(i, ki):
        rows = pl.ds(pl.multiple_of(ki * blk, blk), blk)
        h = i // 2
        return _nt(k_ref[rows, LANE * h:LANE * (h + 1)], qm_scr[i])

    def value_fn(i, ki):
        h = i // 2
        return vt_ref[ki, DIFF_V * h:DIFF_V * (h + 1), :]

    l = _two_pass_attention(2 * DIFF_HEADS, score_fn, value_fn, s_scr, acc_scr, blk)
    for h in range(DIFF_HEADS):
        o_t = acc_scr[2 * h] / l[2 * h] - lam * (acc_scr[2 * h + 1] / l[2 * h + 1])
        on_t = o_t * lax.rsqrt(jnp.mean(o_t * o_t, axis=0, keepdims=True) + EPS)
        o_ref[:, LANE * h:LANE * (h + 1)] = (on_t.T * (og_ref[...] * (1.0 - lam_init))).astype(BF16)


def _diff_attn(qd, kd, vdt, lam_p, og, lam_init, batch, seq):
    t = qd.shape[0]
    nq = seq // ATT_BLK
    return pl.pallas_call(
        functools.partial(_diff_attn_kernel, lam_init=lam_init),
        grid=(batch, nq),
        in_specs=[pl.BlockSpec((ATT_BLK, 512), lambda b, i: (b * nq + i, 0)),
                  pl.BlockSpec((seq, 512), lambda b, i: (b, 0)),
                  pl.BlockSpec((nq, 512, ATT_BLK), lambda b, i: (b, 0, 0)),
                  pl.BlockSpec(lam_p.shape, lambda b, i: (0, 0)),
                  pl.BlockSpec(og.shape, lambda b, i: (0, 0))],
        out_specs=pl.BlockSpec((ATT_BLK, 512), lambda b, i: (b * nq + i, 0)),
        out_shape=jax.ShapeDtypeStruct((t, 512), BF16),
        scratch_shapes=[pltpu.VMEM((2 * DIFF_HEADS, nq, ATT_BLK, ATT_BLK), F32),
                        pltpu.VMEM((2 * DIFF_HEADS, DIFF_V, ATT_BLK), F32),
                        pltpu.VMEM((2 * DIFF_HEADS, ATT_BLK, LANE), BF16)],
        compiler_params=_cp(("arbitrary", "arbitrary")),
        name="diff_attn",
    )(qd, kd, vdt, lam_p, og)


def _swa_kernel(q_ref, kp_ref, kc_ref, vtp_ref, vtc_ref, sink_ref, o_ref):
    w = SWA_WINDOW
    grp = SWA_Q_HEADS // SWA_KV_HEADS
    n = pl.program_id(1)
    key = lax.broadcasted_iota(jnp.int32, (2 * w, grp * w), 0)
    qry = lax.broadcasted_iota(jnp.int32, (2 * w, grp * w), 1) & (w - 1)
    cur_ok = (key >= w) & (key - w <= qry)
    prev_ok = (key < w) & (key > qry)
    low = lax.broadcasted_iota(jnp.int32, (w, LANE), 1) < HEAD_DIM

    for t in range(q_ref.shape[0] // w):
        rows = slice(t * w, (t + 1) * w)
        if t == 0:
            kp, vtp = kp_ref[...], vtp_ref[0]
            valid = cur_ok | (prev_ok & (n > 0))
        else:
            kp, vtp = kc_ref[(t - 1) * w:t * w, :], vtc_ref[t - 1]
            valid = cur_ok | prev_ok
        kc, vtc = kc_ref[rows, :], vtc_ref[t]
        for kv in range(SWA_KV_HEADS):
            kcols = slice(LANE * kv, LANE * (kv + 1))
            vrows = slice(HEAD_DIM * kv, HEAD_DIM * (kv + 1))
            k_win = jnp.concatenate([kp[:, kcols], kc[:, kcols]], axis=0)
            parts = []
            for u in range(2):
                qt = q_ref[rows, LANE * (2 * kv + u):LANE * (2 * kv + u + 1)]
                zero = jnp.zeros_like(qt)
                parts += [jnp.where(low, qt, zero), jnp.where(low, zero, qt)]
            s = jnp.where(valid, _nt(k_win, jnp.concatenate(parts, axis=0)), NEG)
            sink = sink_ref[:, grp * w * kv:grp * w * (kv + 1)]
            m = jnp.maximum(jnp.max(s, axis=0, keepdims=True), sink)
            p = jnp.exp2(s - m)
            den = jnp.sum(p, axis=0, keepdims=True) + jnp.exp2(sink - m)
            vt_win = jnp.concatenate([vtp[vrows, :], vtc[vrows, :]], axis=1)
            o_t = _dot(vt_win, p.astype(BF16)) / den
            for u in range(2):
                pair = jnp.concatenate([o_t[:, 2 * u * w:(2 * u + 1) * w],
                                        o_t[:, (2 * u + 1) * w:(2 * u + 2) * w]], axis=0)
                o_ref[rows, LANE * (2 * kv + u):LANE * (2 * kv + u + 1)] = pair.T.astype(BF16)


def _swa(qs, ks, vst, sink_row, batch, seq):
    t = qs.shape[0]
    w = SWA_WINDOW
    nb = seq // w
    qb = math.gcd(SWA_QB, nb)
    ns = nb // qb
    cur = lambda b, n: (b * ns + n, 0)
    cur3 = lambda b, n: (b * ns + n, 0, 0)
    prev = lambda b, n: (b * nb + jnp.maximum(n * qb - 1, 0), 0)
    prev3 = lambda b, n: (b * nb + jnp.maximum(n * qb - 1, 0), 0, 0)
    return pl.pallas_call(
        _swa_kernel,
        grid=(batch, ns),
        in_specs=[pl.BlockSpec((qb * w, 512), cur),
                  pl.BlockSpec((w, 256), prev), pl.BlockSpec((qb * w, 256), cur),
                  pl.BlockSpec((1, LANE, w), prev3), pl.BlockSpec((qb, LANE, w), cur3),
                  pl.BlockSpec(sink_row.shape, lambda b, n: (0, 0))],
        out_specs=pl.BlockSpec((qb * w, 512), cur),
        out_shape=jax.ShapeDtypeStruct((t, 512), BF16),
        compiler_params=_cp(("arbitrary", "arbitrary")),
        name="swa_attn",
    )(qs, ks, ks, vst, vst, sink_row)


def _merge_kernel(*refs):
    n_sub = refs[0].shape[0] // MOE_TILE
    route_prev = None
    for k in range(n_sub):
        h2 = _merge_mix(k, route_prev, *refs)
        route_prev = functools.partial(_merge_route, k, h2, *refs)
    route_prev()


def _merge_mix(k, between, h_ref, ya_ref, yb_ref, yc_ref, yd_ref, x_ref, gt1_ref, wg_ref, wb_ref, wo_ref,
               g2_ref, sh2_ref, sc2_ref, rw_ref, rb_ref, xo_ref, h2_ref, comb_ref, dest_ref, meta_ref):
    rows = slice(MOE_TILE * k, MOE_TILE * (k + 1))
    h = h_ref[rows, :]
    d = x_ref.shape[1]
    merged = None
    for b, y_ref in enumerate((ya_ref, yb_ref, yc_ref, yd_ref)):
        gate = jax.nn.sigmoid(_dot(h, wg_ref[:, d * b:d * (b + 1)]))
        term = gate * _dot(y_ref[rows, :], wb_ref[b])
        merged = term if merged is None else merged + term
        if b == 0 and between is not None:
            between()
    xn = x_ref[rows, :] + gt1_ref[0] * _dot(merged.astype(BF16), wo_ref[...])
    xo_ref[rows, :] = xn
    ms = jnp.mean(xn * xn, axis=-1, keepdims=True)
    h2 = xn * lax.rsqrt(ms + EPS) * g2_ref[...]
    h2 = h2 * (1.0 + sc2_ref[0]) + sh2_ref[0]
    h2_ref[rows, :] = h2.astype(BF16)
    return h2


def _merge_route(k, h2, h_ref, ya_ref, yb_ref, yc_ref, yd_ref, x_ref, gt1_ref, wg_ref, wb_ref, wo_ref,
                 g2_ref, sh2_ref, sc2_ref, rw_ref, rb_ref, xo_ref, h2_ref, comb_ref, dest_ref, meta_ref):
    rows = slice(MOE_TILE * k, MOE_TILE * (k + 1))
    hh, hm = _split2(h2)
    wh = rw_ref[...].astype(BF16)
    logits = _dot(hh, wh) + _dot(hm, wh)
    lt = logits.T
    scores = jax.nn.sigmoid(lt[0:N_EXPERTS, :])
    sel = scores + rb_ref[...]
    per = N_EXPERTS // N_GROUPS
    srow = [sel[e:e + 1, :] for e in range(N_EXPERTS)]
    gsum = []
    for g in range(N_GROUPS):
        a, b_, c, e_ = srow[per * g:per * (g + 1)]
        gsum.append(jnp.maximum(jnp.maximum(jnp.maximum(a + b_, a + c), jnp.maximum(a + e_, b_ + c)),
                                jnp.maximum(b_ + e_, c + e_)))
    best = jnp.maximum(jnp.maximum(gsum[0], gsum[1]), jnp.maximum(gsum[2], gsum[3]))
    taken = None
    weights = []
    picks = []
    for g in range(N_GROUPS):
        hit = gsum[g] == best
        pick = hit if taken is None else hit & jnp.logical_not(taken)
        taken = hit if taken is None else taken | hit
        picks.append(pick.astype(F32))
        for e in range(per * g, per * (g + 1)):
            rank = jnp.zeros_like(best)
            for o in range(per * g, per * (g + 1)):
                if o == e:
                    continue
                ahead = (srow[o] > srow[e]) | ((srow[o] == srow[e]) & (o < e))
                rank = rank + ahead.astype(F32)
            weights.append(jnp.where(pick & (rank < 1.5), scores[e:e + 1, :], 0.0))
    wsum = weights[0]
    for r_ in weights[1:]:
        wsum = wsum + r_
    inv = 1.0 / wsum
    rid = lax.broadcasted_iota(jnp.int32, scores.shape, 0)
    comb_e = jnp.zeros_like(scores)
    for e, r_ in enumerate(weights):
        comb_e = jnp.where(rid == e, r_ * inv, comb_e)

    tm = lt.shape[1]
    gid = lax.broadcasted_iota(jnp.int32, (8, tm), 0)
    onehot = jnp.zeros((8, tm), F32)
    for g in range(N_GROUPS):
        onehot = jnp.where(gid == g, picks[g], onehot)
    before = (lax.broadcasted_iota(jnp.int32, (tm, tm), 0) < lax.broadcasted_iota(jnp.int32, (tm, tm), 1))
    rank = _dot(onehot.astype(BF16), before.astype(BF16))
    dest = jnp.zeros((1, tm), F32)
    off = jnp.zeros((1, 1), F32)
    meta = jnp.zeros((8, LANE), F32)
    mrow = lax.broadcasted_iota(jnp.int32, (8, LANE), 0)
    for g in range(N_GROUPS):
        cnt = jnp.sum(picks[g], axis=-1, keepdims=True)
        dest = dest + picks[g] * (off + rank[g:g + 1, :])
        meta = jnp.where(mrow == g, cnt, meta)
        meta = jnp.where(mrow == N_GROUPS + g, off, meta)
        off = off + jnp.ceil(cnt * (1.0 / MOE_ALIGN)) * MOE_ALIGN
    dest_ref[k] = dest.astype(jnp.int32)
    meta_ref[k] = meta.astype(jnp.int32)
    pad = jnp.zeros((LANE - N_EXPERTS - 8, tm), F32)
    comb_t = jnp.concatenate([comb_e, jnp.where(gid == 0, dest, 0.0), pad], axis=0)
    comb_ref[rows, :] = comb_t.T


def _merge(h, ys, x2, gt1, wg, wb, wo, g2, sh2, sc2, rw, rb, seq):
    t, d = x2.shape
    sub = MERGE_SUBTILES
    tm = sub * MOE_TILE
    tpb = seq // tm
    row = lambda i: (i, 0)
    per_b = lambda i: (i // tpb, 0, 0)
    c2 = lambda i: (0, 0)
    once = pl.Buffered(1)
    return pl.pallas_call(
        _merge_kernel,
        grid=(t // tm,),
        in_specs=[pl.BlockSpec((tm, d), row)] + [pl.BlockSpec((tm, 512), row)] * 4
                 + [pl.BlockSpec((tm, d), row), pl.BlockSpec((1, 1, d), per_b),
                    pl.BlockSpec(wg.shape, c2, pipeline_mode=once),
                    pl.BlockSpec(wb.shape, lambda i: (0, 0, 0), pipeline_mode=once),
                    pl.BlockSpec(wo.shape, c2, pipeline_mode=once), pl.BlockSpec((1, d), c2),
                    pl.BlockSpec((1, 1, d), per_b), pl.BlockSpec((1, 1, d), per_b),
                    pl.BlockSpec(rw.shape, c2), pl.BlockSpec(rb.shape, c2)],
        out_specs=[pl.BlockSpec((tm, d), row), pl.BlockSpec((tm, d), row), pl.BlockSpec((tm, LANE), row),
                   pl.BlockSpec((sub, 1, MOE_TILE), lambda i: (i, 0, 0)),
                   pl.BlockSpec((sub, 8, LANE), lambda i: (i, 0, 0))],
        out_shape=[jax.ShapeDtypeStruct((t, d), F32), jax.ShapeDtypeStruct((t, d), BF16),
                   jax.ShapeDtypeStruct((t, LANE), F32),
                   jax.ShapeDtypeStruct((t // MOE_TILE, 1, MOE_TILE), jnp.int32),
                   jax.ShapeDtypeStruct((t // MOE_TILE, 8, LANE), jnp.int32)],
        compiler_params=_cp(("arbitrary",)),
        name="merge_router",
    )(h, *ys, x2, gt1, wg, wb, wo, g2, sh2, sc2, rw, rb)


def _moe_kernel(meta_ref, h2_ref, comb_ref, dest_ref, x_ref, gt2_ref, wg_ref, wu_ref, wd_ref, o_ref,
                sorted_s, csort_s, out_s):
    i = pl.program_id(0)
    tm = h2_ref.shape[0]
    n_rows = sorted_s.shape[0]
    per = N_EXPERTS // N_GROUPS
    comb = comb_ref[...]

    place = (lax.broadcasted_iota(jnp.int32, (n_rows, tm), 0) == dest_ref[0]).astype(BF16)
    sorted_s[...] = _dot(place, h2_ref[...]).astype(BF16)
    c_hi, c_lo = _split2(comb)
    csort_s[...] = _dot(place, c_hi) + _dot(place, c_lo)
    out_s[...] = jnp.zeros_like(out_s)
    def run_experts(g, start, size):
        rows = pl.ds(pl.multiple_of(start, MOE_ALIGN), size)
        xk = sorted_s[rows, :]
        cw = csort_s[rows, :]
        lane = lax.broadcasted_iota(jnp.int32, (size, LANE), 1)
        acc = None
        for j in range(per):
            e = per * g + j
            hid = _silu(_dot(xk, wg_ref[0, e])) * _dot(xk, wu_ref[0, e])
            ce = jnp.sum(jnp.where(lane == e, cw, 0.0), axis=-1, keepdims=True)
            term = _dot((hid * ce).astype(BF16), wd_ref[0, e])
            acc = term if acc is None else acc + term
        out_s[rows, :] += acc

    def group(g, carry):
        cnt = meta_ref[i, g]
        off = meta_ref[i, N_GROUPS + g]
        big = MOE_CHUNKS[-1]
        n_big = cnt // big

        def big_chunk(c, inner):
            run_experts(g, off + c * big, big)
            return inner

        lax.fori_loop(0, n_big, big_chunk, 0)
        rest = cnt - n_big * big
        lower = 0
        for size in MOE_CHUNKS:
            @pl.when((rest > lower) & (rest <= size))
            def _(size=size):
                run_experts(g, off + n_big * big, size)
            lower = size
        return carry

    lax.fori_loop(0, N_GROUPS, group, 0)

    dest_col = comb[:, MOE_DEST_LANE:MOE_DEST_LANE + 1].astype(jnp.int32)
    back = (lax.broadcasted_iota(jnp.int32, (tm, n_rows), 1) == dest_col).astype(BF16)
    o_ref[...] = x_ref[...] + gt2_ref[0] * _dot(back, out_s[...].astype(BF16))


def _moe(h2, comb, dest, meta, x2, gt2, wg, wu, wd, layer, seq):
    t, d = x2.shape
    tm = MOE_TILE
    tpb = seq // tm
    n_rows = MOE_SORT_ROWS
    overhang = max(b - a for a, b in zip((0,) + MOE_CHUNKS, MOE_CHUNKS)) - 1
    assert tm + N_GROUPS * (MOE_ALIGN - 1) + overhang <= n_rows
    row = lambda i, m: (i, 0)
    whole = lambda a: pl.BlockSpec((1,) + a.shape[1:], lambda i, m: (layer, 0, 0, 0),
                                   pipeline_mode=pl.Buffered(1))
    grid_spec = pltpu.PrefetchScalarGridSpec(
        num_scalar_prefetch=1,
        grid=(t // tm,),
        in_specs=[pl.BlockSpec((tm, d), row), pl.BlockSpec((tm, LANE), row),
                  pl.BlockSpec((1, 1, tm), lambda i, m: (i, 0, 0)),
                  pl.BlockSpec((tm, d), row),
                  pl.BlockSpec((1, 1, d), lambda i, m: (i // tpb, 0, 0)),
                  whole(wg), whole(wu), whole(wd)],
        out_specs=pl.BlockSpec((tm, d), row),
        scratch_shapes=[pltpu.VMEM((n_rows, d), BF16), pltpu.VMEM((n_rows, LANE), F32),
                        pltpu.VMEM((n_rows, d), F32)])
    return pl.pallas_call(
        _moe_kernel,
        grid_spec=grid_spec,
        out_shape=jax.ShapeDtypeStruct((t, d), F32),
        compiler_params=_cp(("arbitrary",)),
        name="moe",
    )(meta, h2, comb, dest, x2, gt2, wg, wu, wd)


def _layer_params(l, w_in, hg_onorm, mla_q_norm, mla_kv_norm, mla_w_uq, mla_w_ukv, mla_qk_norm,
                  diff_qk_norm, swa_qk_norm, swa_sinks, lb_all):
    ends = [sum(IN_SPLITS[:i]) for i in range(len(IN_SPLITS) + 1)]
    cols = lambda a, b: w_in[l, :, ends[a]:ends[b]].astype(BF16)
    mla_pad = 512 - (ends[7] - ends[4])
    p = {"w_hg": cols(0, 4),
         "w_mla": jnp.pad(cols(4, 7), ((0, 0), (0, mla_pad))),
         "w_diff": cols(7, 10), "w_swa": cols(10, 13), "wg": cols(13, 14)}

    lb = lb_all[l]
    p["loglb"] = jnp.log(lb)[None, :]
    p["log1mlb"] = jnp.log1p(-lb)[None, :]
    p["ogain"] = jnp.tile(hg_onorm[l], HG_HEADS)[None, :]

    hd = MLA_NOPE + MLA_ROPE
    half = MLA_ROPE // 2
    wq = mla_w_uq[l].reshape(MLA_Q_RANK, MLA_HEADS, hd)
    z = lambda r, n: jnp.zeros((r, MLA_HEADS, n), F32)
    nope, rope = wq[:, :, :MLA_NOPE], wq[:, :, MLA_NOPE:]
    p["wqa"] = jnp.concatenate([nope, rope, z(MLA_Q_RANK, 32)], -1).reshape(MLA_Q_RANK, -1).astype(BF16)
    wkv = mla_w_ukv[l].reshape(MLA_KV_RANK, MLA_HEADS, MLA_NOPE + MLA_V)
    knope, vproj = wkv[:, :, :MLA_NOPE], wkv[:, :, MLA_NOPE:]
    eye = jnp.eye(MLA_ROPE, dtype=F32)
    place = lambda m: jnp.broadcast_to(
        jnp.concatenate([jnp.zeros((MLA_ROPE, MLA_NOPE), F32), m, jnp.zeros((MLA_ROPE, 32), F32)], -1)[:, None, :],
        (MLA_ROPE, MLA_HEADS, LANE))
    pad_rows = 256 - MLA_KV_RANK - MLA_ROPE
    p["wka"] = jnp.concatenate([jnp.concatenate([knope, z(MLA_KV_RANK, 64)], -1), place(eye),
                                z(pad_rows, LANE)], 0).reshape(256, -1).astype(BF16)
    p["wv"] = jnp.concatenate([vproj.reshape(MLA_KV_RANK, -1),
                               jnp.zeros((256 - MLA_KV_RANK, MLA_HEADS * MLA_V), F32)], 0).astype(BF16)
    p["qng"] = mla_q_norm[l][None, :]
    p["kvg"] = jnp.concatenate([mla_kv_norm[l], jnp.ones((256 - MLA_KV_RANK,), F32)])[None, :]

    def rope_gains(g):
        base = jnp.concatenate([g, jnp.zeros((LANE - hd,), F32)])
        part = jnp.concatenate([jnp.zeros((MLA_NOPE,), F32), g[MLA_NOPE + half:], g[MLA_NOPE:MLA_NOPE + half],
                                jnp.zeros((LANE - hd,), F32)])
        return base[None, :], part[None, :]

    p["gq"], p["gqs"] = rope_gains(mla_qk_norm[l, 0])
    p["gk"], p["gks"] = rope_gains(mla_qk_norm[l, 1])
    p["dgq"] = jnp.tile(diff_qk_norm[l, 0], 8)[None, :]
    p["dgk"] = jnp.tile(diff_qk_norm[l, 1], 8)[None, :]
    p["sgq"] = jnp.tile(swa_qk_norm[l, 0], 8)[None, :]
    p["sgk"] = jnp.tile(swa_qk_norm[l, 1], 2)[None, :]
    p["sinks"] = jnp.repeat(swa_sinks[l].astype(F32) * LOG2E, SWA_WINDOW)[None, :]
    return p


def _rope_tables(positions):
    inv_freq = ROPE_BASE ** (-jnp.arange(0, MLA_ROPE, 2, dtype=F32) / MLA_ROPE)
    zeros = lambda n: jnp.zeros((n,), F32)
    half = MLA_ROPE // 2
    pad = LANE - MLA_NOPE - MLA_ROPE
    freq = jnp.concatenate([zeros(MLA_NOPE), inv_freq, inv_freq, zeros(pad)])
    keep = jnp.concatenate([jnp.ones((MLA_NOPE + MLA_ROPE,), F32), zeros(pad)])
    sign = jnp.concatenate([zeros(MLA_NOPE), -jnp.ones((half,), F32), jnp.ones((half,), F32), zeros(pad)])
    ang = positions.astype(F32).reshape(-1)[:, None] * freq
    return jnp.cos(ang) * keep, jnp.sin(ang) * sign


def kernel(x, c, positions, ada_w, ada_b, norm_mix, norm_ffn, w_in, hg_lb_logits, hg_onorm, mla_q_norm, mla_kv_norm, mla_w_uq, mla_w_ukv, mla_qk_norm, diff_qk_norm, diff_lam, diff_onorm, swa_qk_norm, swa_sinks, w_branch, w_out, router_w, router_bias, moe_w_gate, moe_w_up, moe_w_down):
    batch, seq, d = x.shape
    x2 = x.reshape(batch * seq, d)
    cosf, sinf = _rope_tables(positions)
    lb_all = jnp.cumsum(jax.nn.softmax(hg_lb_logits.astype(F32), axis=0), axis=0)
    lb_all = lb_all - lb_all[0]
    mod = _modulation(c, ada_w, ada_b)
    rw = jnp.concatenate([router_w, jnp.zeros((d, LANE - N_EXPERTS), F32)], axis=1)
    rb = router_bias.astype(F32)[:, None]
    moe_w = (moe_w_gate.astype(BF16), moe_w_up.astype(BF16), moe_w_down.astype(BF16))

    for l in range(DEPTH):
        sh1, sc1, gt1, sh2, sc2, gt2 = [mod[l, :, d * k:d * (k + 1)][:, None, :] for k in range(6)]
        p = _layer_params(l, w_in, hg_onorm, mla_q_norm, mla_kv_norm, mla_w_uq, mla_w_ukv, mla_qk_norm,
                          diff_qk_norm, swa_qk_norm, swa_sinks, lb_all)
        hg4, lf, h, qm, km, vmt, qd, kd, vdt, qs, ks, vst = _inproj(
            x2, norm_mix[l][None, :], sh1, sc1, cosf, sinf, p, seq)
        y_a = _hgrn(hg4, lf, p["ogain"], batch, seq)
        y_b = _mla_attn(qm, km, vmt, batch, seq)
        lam_init = 0.8 - 0.6 * math.exp(-0.3 * l)
        y_c = _diff_attn(qd, kd, vdt, diff_lam[l], diff_onorm[l][None, :], lam_init, batch, seq)
        y_d = _swa(qs, ks, vst, p["sinks"], batch, seq)
        x2, h2, comb, dest, meta = _merge(h, (y_a, y_b, y_c, y_d), x2, gt1, p["wg"], w_branch[l].astype(BF16),
                                          w_out[l].astype(BF16), norm_ffn[l][None, :], sh2, sc2, rw, rb, seq)
        x2 = _moe(h2, comb, dest, meta[:, :, 0], x2, gt2, *moe_w, l, seq)
    return x2.reshape(batch, seq, d)
```
